```python
import jax, jax.numpy as jnp
from jax import lax
import numpy as np

D_MODEL = 1024
BATCH = 2
SEQ = 16384
DEPTH = 4

EPS = 1e-6
NEG_INF = -1e30
FORCE = 1e4
HEAD_DIM = 64
D_MIX = D_MODEL
GDN_HEADS = D_MIX // 4 // HEAD_DIM
GDN_WIDTH = GDN_HEADS * HEAD_DIM
GDN_CONV = 4
GDN_CHUNK = 64
NSA_HEADS = D_MIX // 2 // HEAD_DIM
NSA_WIDTH = NSA_HEADS * HEAD_DIM
NSA_KV_HEADS = 2
NSA_GROUP = NSA_HEADS // NSA_KV_HEADS
NSA_KV_WIDTH = NSA_KV_HEADS * HEAD_DIM
CMP_STRIDE = 16
CMP_LEN = 2 * CMP_STRIDE
SLC_BLOCK = 64
N_SELECT = 16
WINDOW = 512
Q_BLOCK = 128
CONV_WIDTH = D_MIX - GDN_WIDTH - NSA_WIDTH
CONV_GROUPS = 4
CONV_K = 3
ROPE_THETA = 500000.0
ROT_DIM = HEAD_DIM // 4
D_FF = ((8 * D_MODEL + 3 * 256 - 1) // (3 * 256)) * 256
IN_SIZES = (GDN_WIDTH, GDN_WIDTH, GDN_WIDTH, GDN_WIDTH, GDN_HEADS, GDN_HEADS,
            NSA_WIDTH, NSA_KV_WIDTH, NSA_KV_WIDTH, NSA_KV_WIDTH, NSA_KV_WIDTH, NSA_KV_WIDTH, NSA_KV_WIDTH,
            3 * NSA_HEADS, CONV_WIDTH, CONV_WIDTH, CONV_WIDTH)
D_IN = sum(IN_SIZES)

kernel_name = "hymba_style_gdn_nsa_shortconv_trunk"


def rms_norm(x, gain):
    xf = x.astype(jnp.float32)
    y = xf * lax.rsqrt(jnp.mean(xf * xf, axis=-1, keepdims=True) + EPS)
    return (y * gain.astype(jnp.float32)).astype(x.dtype)


def l2_norm(x):
    return x * lax.rsqrt(jnp.sum(x * x, axis=-1, keepdims=True) + EPS)


def causal_depthwise_conv(x, w):
    k = w.shape[0]
    return lax.conv_general_dilated(x, w[:, None, :].astype(x.dtype), window_strides=(1,),
                                    padding=[(k - 1, 0)], dimension_numbers=("NWC", "WIO", "NWC"),
                                    feature_group_count=x.shape[-1])


def rope_tables(positions):
    inv = jnp.float32(ROPE_THETA) ** (-jnp.arange(0, ROT_DIM, 2, dtype=jnp.float32) / ROT_DIM)
    ang = positions.astype(jnp.float32)[..., None] * inv
    return jnp.cos(ang), jnp.sin(ang)


def apply_partial_rope(x, cos, sin):
    extra = x.ndim - 3
    shp = cos.shape[:2] + (1,) * extra + cos.shape[-1:]
    c, s = cos.reshape(shp), sin.reshape(shp)
    half = ROT_DIM // 2
    x1 = x[..., :half].astype(jnp.float32)
    x2 = x[..., half:ROT_DIM].astype(jnp.float32)
    rot = jnp.concatenate([x1 * c - x2 * s, x2 * c + x1 * s], axis=-1).astype(x.dtype)
    return jnp.concatenate([rot, x[..., ROT_DIM:]], axis=-1)


def masked_softmax(scores, mask):
    s = jnp.where(mask, scores.astype(jnp.float32), NEG_INF)
    return jnp.where(mask, jax.nn.softmax(s, axis=-1), 0.0)


def gated_delta_rule(q, k, v, g, beta):
    b, T, h, dk = q.shape
    dv = v.shape[-1]
    n = T // GDN_CHUNK
    q = l2_norm(q.astype(jnp.float32)) * (dk ** -0.5)
    k = l2_norm(k.astype(jnp.float32))
    v = v.astype(jnp.float32)

    def chunks(t):
        return t.reshape(b, n, GDN_CHUNK, h, t.shape[-1]).transpose(0, 3, 1, 2, 4)

    qc, kc, vc = chunks(q), chunks(k), chunks(v)
    gc = g.astype(jnp.float32).reshape(b, n, GDN_CHUNK, h).transpose(0, 3, 1, 2)
    bc = beta.astype(jnp.float32).reshape(b, n, GDN_CHUNK, h).transpose(0, 3, 1, 2)
    gcum = jnp.cumsum(gc, axis=-1)
    tril = jnp.tril(jnp.ones((GDN_CHUNK, GDN_CHUNK), dtype=bool))
    strict = jnp.tril(jnp.ones((GDN_CHUNK, GDN_CHUNK), dtype=bool), -1)
    diff = gcum[..., :, None] - gcum[..., None, :]
    decay = jnp.where(tril, jnp.exp(jnp.where(tril, diff, 0.0)), 0.0)
    kb = kc * bc[..., None]
    a = jnp.where(strict, jnp.einsum("bhncd,bhnsd->bhncs", kb, kc) * decay, 0.0)
    ia = a + jnp.eye(GDN_CHUNK, dtype=jnp.float32)
    rhs = jnp.concatenate([vc * bc[..., None], kb * jnp.exp(gcum)[..., None]], axis=-1)
    sol = lax.linalg.triangular_solve(ia, rhs, left_side=True, lower=True, unit_diagonal=True)
    u, w = sol[..., :dv], sol[..., dv:]
    qk = jnp.where(tril, jnp.einsum("bhncd,bhnsd->bhncs", qc, kc) * decay, 0.0)
    q_dec = qc * jnp.exp(gcum)[..., None]
    k_dec = kc * jnp.exp(gcum[..., -1:] - gcum)[..., None]
    g_last = jnp.exp(gcum[..., -1])

    def step(state, xs):
        u_i, w_i, qd_i, kd_i, qk_i, gl_i = xs
        v_new = u_i - w_i @ state
        o_i = qd_i @ state + qk_i @ v_new
        state = state * gl_i[..., None, None] + jnp.einsum("bhck,bhcv->bhkv", kd_i, v_new)
        return state, o_i

    xs = tuple(jnp.moveaxis(t, 2, 0) for t in (u, w, q_dec, k_dec, qk, g_last))
    s0 = jnp.zeros((b, h, dk, dv), jnp.float32)
    _, o = lax.scan(step, s0, xs)
    return o.transpose(1, 0, 3, 2, 4).reshape(b, T, h, dv)


def compress_blocks(t, pe, w1, w2):
    b, T, g, d = t.shape
    nb = T // CMP_STRIDE
    blk = t.reshape(b, nb, CMP_STRIDE, g, d)
    win = jnp.concatenate([blk[:, :-1], blk[:, 1:]], axis=2) + pe[:, None, :].astype(t.dtype)
    flat = win.transpose(0, 1, 3, 2, 4).reshape(b, nb - 1, g, CMP_LEN * d)
    return jax.nn.gelu(flat @ w1) @ w2


def nsa_attention(q, k_cmp, v_cmp, k_slc, v_slc, k_win, v_win, gate_logits, cos, sin,
                  q_gain, k_gain, cmp_pe, cmp_w1, cmp_w2):
    b, T = q.shape[:2]
    G, R, dh = NSA_KV_HEADS, NSA_GROUP, HEAD_DIM
    kv = lambda t: t.reshape(b, T, G, dh)
    qn = rms_norm(q.reshape(b, T, G, R, dh), q_gain)
    qr = apply_partial_rope(qn, cos, sin)
    kc = rms_norm(compress_blocks(kv(k_cmp), cmp_pe[0], cmp_w1[0], cmp_w2[0]), k_gain[0])
    vc = compress_blocks(kv(v_cmp), cmp_pe[1], cmp_w1[1], cmp_w2[1])
    ks = apply_partial_rope(rms_norm(kv(k_slc), k_gain[1]), cos, sin)
    kw = apply_partial_rope(rms_norm(kv(k_win), k_gain[2]), cos, sin)
    nslc = T // SLC_BLOCK
    n_sel = min(N_SELECT, nslc)
    ks_blk = ks.reshape(b, nslc, SLC_BLOCK, G, dh).transpose(0, 3, 1, 2, 4)
    vs_blk = kv(v_slc).reshape(b, nslc, SLC_BLOCK, G, dh).transpose(0, 3, 1, 2, 4)
    pad = ((0, 0), (WINDOW, 0), (0, 0), (0, 0))
    kw_pad, vw_pad = jnp.pad(kw, pad), jnp.pad(kv(v_win), pad)
    gates = jax.nn.sigmoid(gate_logits.astype(jnp.float32)).reshape(b, T, G, R, 3)
    nc = kc.shape[1]
    cmp_end = jnp.arange(nc) * CMP_STRIDE + CMP_LEN - 1
    ratio = SLC_BLOCK // CMP_STRIDE
    frac = jnp.minimum(CMP_LEN, SLC_BLOCK - CMP_STRIDE * jnp.arange(ratio)).astype(jnp.float32) / CMP_LEN
    scale = dh ** -0.5
    b_ix = jnp.arange(b)[:, None, None, None]
    g_ix = jnp.arange(G)[None, :, None, None]

    def block(i):
        s = i * Q_BLOCK
        tq = s + jnp.arange(Q_BLOCK)
        q_n = lax.dynamic_slice_in_dim(qn, s, Q_BLOCK, 1)
        q_r = lax.dynamic_slice_in_dim(qr, s, Q_BLOCK, 1)
        gt = lax.dynamic_slice_in_dim(gates, s, Q_BLOCK, 1)
        cmask = cmp_end[None, :] <= tq[:, None]
        p_c = masked_softmax(jnp.einsum("bqgrd,bcgd->bgrqc", q_n, kc) * scale, cmask)
        o_c = jnp.einsum("bgrqc,bcgd->bqgrd", p_c.astype(vc.dtype), vc)
        imp = jnp.pad(p_c.sum(axis=2), ((0, 0), (0, 0), (0, 0), (0, 1))).reshape(b, G, Q_BLOCK, nslc, ratio)
        own, spill = imp @ frac, imp @ (1.0 - frac)
        slc = own + jnp.pad(spill[..., :-1], ((0, 0), (0, 0), (0, 0), (1, 0)))
        cur = (tq // SLC_BLOCK)[:, None]
        j = jnp.arange(nslc)[None, :]
        forced = (j == 0) | (j == cur) | (j == cur - 1)
        score = jnp.where(forced, FORCE, jnp.where(j <= cur, slc, -FORCE))
        _, idx = lax.top_k(score, n_sel)
        k_g = ks_blk[b_ix, g_ix, idx].reshape(b, G, Q_BLOCK, n_sel * SLC_BLOCK, dh)
        v_g = vs_blk[b_ix, g_ix, idx].reshape(b, G, Q_BLOCK, n_sel * SLC_BLOCK, dh)
        kpos = (idx[..., None] * SLC_BLOCK + jnp.arange(SLC_BLOCK)).reshape(b, G, Q_BLOCK, -1)
        smask = (kpos <= tq[:, None])[:, :, None]
        p_s = masked_softmax(jnp.einsum("bqgrd,bgqkd->bgrqk", q_r, k_g) * scale, smask)
        o_s = jnp.einsum("bgrqk,bgqkd->bqgrd", p_s.astype(v_g.dtype), v_g)
        kb = lax.dynamic_slice_in_dim(kw_pad, s, Q_BLOCK + WINDOW, 1)
        vb = lax.dynamic_slice_in_dim(vw_pad, s, Q_BLOCK + WINDOW, 1)
        kpos_w = s - WINDOW + jnp.arange(Q_BLOCK + WINDOW)
        dist = tq[:, None] - kpos_w[None, :]
        wmask = (dist >= 0) & (dist < WINDOW) & (kpos_w >= 0)[None, :]
        p_w = masked_softmax(jnp.einsum("bqgrd,bkgd->bgrqk", q_r, kb) * scale, wmask)
        o_w = jnp.einsum("bgrqk,bkgd->bqgrd", p_w.astype(vb.dtype), vb)
        return (gt[..., 0:1] * o_c + gt[..., 1:2] * o_s + gt[..., 2:3] * o_w).astype(q.dtype)

    out = lax.map(block, jnp.arange(T // Q_BLOCK))
    return out.transpose(1, 0, 2, 3, 4, 5).reshape(b, T, NSA_WIDTH)


def setup_inputs(seed: int = 0) -> dict:
    key = jax.random.key(seed)
    ks = jax.random.split(key, 20)
    f32 = jnp.float32
    nrm = lambda k, shp, sc: jax.random.normal(k, shp, f32) * sc
    gain = lambda k, shp: 1.0 + 0.02 * jax.random.normal(k, shp, f32)
    dt = jnp.exp(jax.random.uniform(ks[5], (DEPTH, GDN_HEADS), f32) * (jnp.log(0.1) - jnp.log(0.001)) + jnp.log(0.001))
    return {
        "x": nrm(ks[0], (BATCH, SEQ, D_MODEL), 1.0),
        "positions": jnp.tile(jnp.arange(SEQ, dtype=jnp.int32)[None, :], (BATCH, 1)),
        "attn_norm": gain(ks[1], (DEPTH, D_MODEL)),
        "w_in": nrm(ks[2], (DEPTH, D_MODEL, D_IN), D_MODEL ** -0.5),
        "gdn_conv_w": nrm(ks[3], (DEPTH, GDN_CONV, 3 * GDN_WIDTH), GDN_CONV ** -0.5),
        "gdn_a_log": jnp.log(jax.random.uniform(ks[4], (DEPTH, GDN_HEADS), f32, 1.0, 16.0)),
        "gdn_dt_bias": dt + jnp.log(-jnp.expm1(-dt)),
        "gdn_norm": gain(ks[6], (DEPTH, HEAD_DIM)),
        "nsa_q_norm": gain(ks[7], (DEPTH, HEAD_DIM)),
        "nsa_k_norm": gain(ks[8], (DEPTH, 3, HEAD_DIM)),
        "nsa_cmp_pe": nrm(ks[9], (DEPTH, 2, CMP_LEN, HEAD_DIM), 0.1),
        "nsa_cmp_w1": nrm(ks[10], (DEPTH, 2, CMP_LEN * HEAD_DIM, HEAD_DIM), (CMP_LEN * HEAD_DIM) ** -0.5),
        "nsa_cmp_w2": nrm(ks[11], (DEPTH, 2, HEAD_DIM, HEAD_DIM), HEAD_DIM ** -0.5),
        "conv_w": nrm(ks[12], (DEPTH, CONV_K, CONV_WIDTH), CONV_K ** -0.5),
        "w_out": nrm(ks[13], (DEPTH, D_MIX, D_MODEL), D_MIX ** -0.5),
        "ffn_norm": gain(ks[14], (DEPTH, D_MODEL)),
        "w_gate_up": nrm(ks[15], (DEPTH, D_MODEL, 2 * D_FF), D_MODEL ** -0.5),
        "w_down": nrm(ks[16], (DEPTH, D_FF, D_MODEL), D_FF ** -0.5),
    }


def reference(x, positions, attn_norm, w_in, gdn_conv_w, gdn_a_log, gdn_dt_bias, gdn_norm,
              nsa_q_norm, nsa_k_norm, nsa_cmp_pe, nsa_cmp_w1, nsa_cmp_w2, conv_w, w_out,
              ffn_norm, w_gate_up, w_down):
    b, T, _ = x.shape
    cos, sin = rope_tables(positions)
    split_at = [int(v) for v in np.cumsum(IN_SIZES)[:-1]]
    for l in range(DEPTH):
        h = rms_norm(x, attn_norm[l])
        proj = h @ w_in[l]
        (gq, gk, gv, gz, gb, ga, nq, kcmp, vcmp, kslc, vslc, kwin, vwin, ngate,
         cb, cc, cx) = jnp.split(proj, split_at, axis=-1)
        qkv = jax.nn.silu(causal_depthwise_conv(jnp.concatenate([gq, gk, gv], axis=-1), gdn_conv_w[l]))
        hq, hk, hv = jnp.split(qkv, 3, axis=-1)
        heads = lambda t: t.reshape(b, T, GDN_HEADS, HEAD_DIM)
        g = -jnp.exp(gdn_a_log[l].astype(jnp.float32)) * jax.nn.softplus(
            ga.astype(jnp.float32) + gdn_dt_bias[l].astype(jnp.float32))
        beta = jax.nn.sigmoid(gb.astype(jnp.float32))
        o_gdn = gated_delta_rule(heads(hq), heads(hk), heads(hv), g, beta)
        o_gdn = (rms_norm(o_gdn, gdn_norm[l]) * jax.nn.silu(heads(gz).astype(jnp.float32)))
        o_gdn = o_gdn.reshape(b, T, GDN_WIDTH).astype(x.dtype)
        o_nsa = nsa_attention(nq, kcmp, vcmp, kslc, vslc, kwin, vwin, ngate, cos, sin,
                              nsa_q_norm[l], nsa_k_norm[l], nsa_cmp_pe[l], nsa_cmp_w1[l], nsa_cmp_w2[l])
        o_conv = cb * causal_depthwise_conv(cc * cx, conv_w[l])
        mix = jnp.concatenate([o_gdn, o_nsa.astype(x.dtype), o_conv.astype(x.dtype)], axis=-1)
        x = x + mix @ w_out[l]
        h2 = rms_norm(x, ffn_norm[l])
        gate, up = jnp.split(h2 @ w_gate_up[l], 2, axis=-1)
        x = x + (jax.nn.silu(gate) * up) @ w_down[l]
    return x
```

```python
import functools

import numpy as np
import jax
import jax.numpy as jnp
from jax import lax
from jax.experimental import pallas as pl
from jax.experimental.pallas import tpu as pltpu

F32 = jnp.float32
MXU_DTYPE = jnp.bfloat16

D_MODEL = 1024
HEAD = 64
EPS = 1e-6
NEG_INF = -1e30
GDN_HEADS = 4
GDN_WIDTH = 256
GDN_CONV = 4
GDN_CHUNK = 64
NSA_HEADS = 8
NSA_WIDTH = 512
NSA_KV_HEADS = 2
NSA_GROUP = 4
NSA_KV_WIDTH = 128
CMP_STRIDE = 16
CMP_LEN = 32
SLC_BLOCK = 64
N_SELECT = 16
WINDOW = 512
CONV_WIDTH = 256
CONV_K = 3
ROPE_THETA = 500000.0
ROT_DIM = 16
D_FF = 2816
IN_SIZES = (256, 256, 256, 256, 4, 4, 512, 128, 128, 128, 128, 128, 128, 24, 256, 256, 256)
D_IN = sum(IN_SIZES)

LANES = 128
SUBLANES = 8
VMEM_LIMIT = 56 * 1024 * 1024

G_W = 1024 + LANES
N_W = 512 + 4 * 128 + LANES
C_W = 3 * CONV_WIDTH
P_W = 2 * NSA_KV_WIDTH
G_OFF, N_OFF, C_OFF, P_OFF = 0, G_W, G_W + N_W, G_W + N_W + C_W
PROJ_W = G_W + N_W + C_W + P_W
Q_SLOT_HEADS = (0, 4, 1, 5, 2, 6, 3, 7)

SEL_BIG = 16384.0
FORCE_LANES = 8

TM_PROJ = 512
TM_FFN = 512
T_GDN = 256
T_PREP = 512
TQ = 128
KC = 512
FF_CHUNK = 256


def _dot(a, b):
    return jnp.dot(a.astype(MXU_DTYPE), b.astype(MXU_DTYPE), preferred_element_type=F32)


def _dot_nt(a, b):
    return lax.dot_general(a.astype(MXU_DTYPE), b.astype(MXU_DTYPE), (((1,), (1,)), ((), ())),
                           preferred_element_type=F32)


def _dot_tn(a, b):
    return lax.dot_general(a.astype(MXU_DTYPE), b.astype(MXU_DTYPE), (((0,), (0,)), ((), ())),
                           preferred_element_type=F32)


def _split3(x):
    a = x.astype(MXU_DTYPE)
    r = x - a.astype(F32)
    b = r.astype(MXU_DTYPE)
    c = (r - b.astype(F32)).astype(MXU_DTYPE)
    return a, b, c


def _dot_hi_l(x, m):
    a, b, c = _split3(x)
    f = lambda t: jnp.dot(t, m, preferred_element_type=F32)
    return f(a) + f(b) + f(c)


def _dot_hi(x, y):
    xa, xb, xc = _split3(x)
    ya, yb, yc = _split3(y)
    f = lambda s, t: jnp.dot(s, t, preferred_element_type=F32)
    return f(xa, ya) + (f(xa, yb) + f(xb, ya)) + (f(xa, yc) + f(xb, yb) + f(xc, ya))


def _silu(x):
    return x * jax.nn.sigmoid(x)


def _shift_rows(x, halo, s, rows):
    y = pltpu.roll(x, s, 0)
    for r in range(s):
        y = jnp.where(rows == r, halo[SUBLANES - s + r:SUBLANES - s + r + 1, :], y)
    return y


def _in_proj_kernel(x_ref, g_ref, w_ref, o_ref):
    x = x_ref[...]
    ms = jnp.mean(x * x, axis=-1, keepdims=True)
    h = (x * lax.rsqrt(ms + EPS) * g_ref[...]).astype(MXU_DTYPE)
    for c in range(0, PROJ_W, 256):
        o_ref[:, c:c + 256] = jnp.dot(h, w_ref[:, c:c + 256], preferred_element_type=F32)


def _in_proj(x2, gain, w):
    m = x2.shape[0]
    tm = TM_PROJ
    return pl.pallas_call(
        _in_proj_kernel,
        grid=(m // tm,),
        in_specs=[pl.BlockSpec((tm, D_MODEL), lambda i: (i, 0)),
                  pl.BlockSpec((1, D_MODEL), lambda i: (0, 0)),
                  pl.BlockSpec((D_MODEL, PROJ_W), lambda i: (0, 0))],
        out_specs=pl.BlockSpec((tm, PROJ_W), lambda i: (i, 0)),
        out_shape=jax.ShapeDtypeStruct((m, PROJ_W), F32),
        compiler_params=pltpu.CompilerParams(dimension_semantics=("parallel",),
                                             vmem_limit_bytes=VMEM_LIMIT),
        name="in_proj",
    )(x2, gain, w)


def _gdn_kernel(g_ref, gh_ref, cw_ref, alog_ref, dt_ref, gn_ref, bd_ref, o_ref, s_ref, *, tt):
    t_idx = pl.program_id(1)

    @pl.when(t_idx == 0)
    def _():
        s_ref[...] = jnp.zeros_like(s_ref)

    keep = jnp.where(t_idx == 0, 0.0, 1.0)
    rows = lax.broadcasted_iota(jnp.int32, (tt, 1), 0)
    x = g_ref[:, 0:3 * GDN_WIDTH]
    hx = gh_ref[:, 0:3 * GDN_WIDTH] * keep
    w = cw_ref[...]
    y = w[GDN_CONV - 1:GDN_CONV, :] * x
    for s in range(1, GDN_CONV):
        y = y + w[GDN_CONV - 1 - s:GDN_CONV - s, :] * _shift_rows(x, hx, s, rows)
    y = _silu(y)
    q = y[:, 0:GDN_WIDTH]
    k = y[:, GDN_WIDTH:2 * GDN_WIDTH]
    v = y[:, 2 * GDN_WIDTH:3 * GDN_WIDTH]
    bd = bd_ref[...]
    q = q * lax.rsqrt(_dot_hi_l(q * q, bd) + EPS) * (HEAD ** -0.5)
    k = k * lax.rsqrt(_dot_hi_l(k * k, bd) + EPS)

    gg = g_ref[:, 4 * GDN_WIDTH:4 * GDN_WIDTH + LANES]
    lane = lax.broadcasted_iota(jnp.int32, (1, LANES), 1)
    xa = gg + dt_ref[...]
    softplus = jnp.maximum(xa, 0.0) + jnp.log1p(jnp.exp(-jnp.abs(xa)))
    g2 = jnp.where(lane < GDN_HEADS, jax.nn.sigmoid(gg), -jnp.exp(alog_ref[...]) * softplus)
    cs = g2.T
    lane_t = lax.broadcasted_iota(jnp.int32, (1, tt), 1) % GDN_CHUNK
    step = 1
    while step < GDN_CHUNK:
        cs = cs + jnp.where(lane_t >= step, pltpu.roll(cs, step, 1), 0.0)
        step *= 2
    gcum_t = cs
    gcum = cs.T
    egc = jnp.exp(gcum)

    ri = lax.broadcasted_iota(jnp.int32, (GDN_CHUNK, GDN_CHUNK), 0)
    ci = lax.broadcasted_iota(jnp.int32, (GDN_CHUNK, GDN_CHUNK), 1)
    tril = ri >= ci
    strict = ri > ci
    eye = jnp.where(ri == ci, 1.0, 0.0)
    gn = gn_ref[...]

    states = [s_ref[h] for h in range(GDN_HEADS)]
    for n in range(tt // GDN_CHUNK):
        r0 = n * GDN_CHUNK
        rs = slice(r0, r0 + GDN_CHUNK)
        for h in range(GDN_HEADS):
            hs = slice(h * HEAD, (h + 1) * HEAD)
            gl_lane = GDN_HEADS + h
            gc = gcum[rs, gl_lane:gl_lane + 1]
            gr = gcum_t[gl_lane:gl_lane + 1, rs]
            decay = jnp.where(tril, jnp.exp(jnp.where(tril, gc - gr, 0.0)), 0.0)
            beta = g2[rs, h:h + 1]
            eg = egc[rs, gl_lane:gl_lane + 1]
            qh, kh, vh = q[rs, hs], k[rs, hs], v[rs, hs]
            kb = kh * beta
            a = jnp.where(strict, _dot_nt(kb, kh) * decay, 0.0)
            xm = -a
            tinv = eye + xm
            pw = xm
            for _ in range(5):
                pw = _dot_hi(pw, pw)
                tinv = tinv + _dot_hi(tinv, pw)
            rhs = jnp.concatenate([vh * beta, kb * eg], axis=1)
            sol = _dot_hi(tinv, rhs)
            u = sol[:, 0:HEAD]
            wm = sol[:, HEAD:2 * HEAD]
            qk = jnp.where(tril, _dot_nt(qh, kh) * decay, 0.0)
            g_last = gcum[r0 + GDN_CHUNK - 1:r0 + GDN_CHUNK, gl_lane:gl_lane + 1]
            q_dec = qh * eg
            k_dec = kh * jnp.exp(g_last - gc)
            st = states[h]
            v_new = u - _dot(wm, st)
            o = _dot(q_dec, st) + _dot(qk, v_new)
            states[h] = st * jnp.exp(g_last) + _dot_tn(k_dec, v_new)
            on = o * lax.rsqrt(jnp.mean(o * o, axis=-1, keepdims=True) + EPS) * gn
            z = g_ref[rs, 3 * GDN_WIDTH + h * HEAD:3 * GDN_WIDTH + (h + 1) * HEAD]
            o_ref[rs, hs] = (on * _silu(z)).astype(o_ref.dtype)
    for h in range(GDN_HEADS):
        s_ref[h] = states[h]


def _gdn(proj, conv_w, alog_pad, dt_pad, gnorm, bd256, batch, seq):
    tt = T_GDN
    nt = seq // tt
    hb = tt // SUBLANES
    return pl.pallas_call(
        functools.partial(_gdn_kernel, tt=tt),
        grid=(batch, nt),
        in_specs=[pl.BlockSpec((tt, G_W), lambda b, t: (b * nt + t, 0)),
                  pl.BlockSpec((SUBLANES, G_W), lambda b, t: (jnp.maximum((b * nt + t) * hb - 1, 0), 0)),
                  pl.BlockSpec((GDN_CONV, 3 * GDN_WIDTH), lambda b, t: (0, 0)),
                  pl.BlockSpec((1, LANES), lambda b, t: (0, 0)),
                  pl.BlockSpec((1, LANES), lambda b, t: (0, 0)),
                  pl.BlockSpec((1, HEAD), lambda b, t: (0, 0)),
                  pl.BlockSpec((GDN_WIDTH, GDN_WIDTH), lambda b, t: (0, 0))],
        out_specs=pl.BlockSpec((tt, GDN_WIDTH), lambda b, t: (b * nt + t, 0)),
        out_shape=jax.ShapeDtypeStruct((batch * seq, GDN_WIDTH), MXU_DTYPE),
        scratch_shapes=[pltpu.VMEM((GDN_HEADS, HEAD, HEAD), F32)],
        compiler_params=pltpu.CompilerParams(dimension_semantics=("parallel", "arbitrary"),
                                             vmem_limit_bytes=VMEM_LIMIT),
        name="gdn",
    )(proj, proj, conv_w, alog_pad, dt_pad, gnorm, bd256)


def _rope(x, c, s1, s2):
    wdt = x.shape[1]
    return x * c + pltpu.roll(x, wdt - ROT_DIM // 2, 1) * s1 + pltpu.roll(x, ROT_DIM // 2, 1) * s2


def _nsa_prep_kernel(n_ref, c_ref, s1_ref, s2_ref, qg_ref, kg_ref, bdq_ref, bdk_ref,
                     qn_ref, qr_ref, ksa_ref, kw_ref, vs_ref, vw_ref, gt_ref, *, tt, seq):
    c1, s1, s2 = c_ref[...], s1_ref[...], s2_ref[...]
    c4 = jnp.concatenate([c1] * 4, axis=1)
    s14 = jnp.concatenate([s1] * 4, axis=1)
    s24 = jnp.concatenate([s2] * 4, axis=1)
    q = n_ref[:, 0:NSA_WIDTH]
    qn = q * lax.rsqrt(_dot_hi_l(q * q, bdq_ref[...]) * (1.0 / HEAD) + EPS) * qg_ref[...]
    qn = qn * (HEAD ** -0.5)
    qn_ref[...] = qn.astype(qn_ref.dtype)
    qr_ref[...] = _rope(qn, c4, s14, s24).astype(qr_ref.dtype)
    bdk = bdk_ref[...]
    ks = n_ref[:, 512:640]
    kw = n_ref[:, 640:768]
    ks = ks * lax.rsqrt(_dot_hi_l(ks * ks, bdk) * (1.0 / HEAD) + EPS) * kg_ref[0:1, :]
    kw = kw * lax.rsqrt(_dot_hi_l(kw * kw, bdk) * (1.0 / HEAD) + EPS) * kg_ref[1:2, :]
    ksa_ref[:, 0:LANES] = _rope(ks, c1, s1, s2).astype(ksa_ref.dtype)
    rows = (lax.broadcasted_iota(jnp.int32, (tt, LANES), 0) + pl.program_id(0) * tt) % seq
    lane = lax.broadcasted_iota(jnp.int32, (tt, LANES), 1)
    ksa_ref[:, LANES:2 * LANES] = jnp.where((rows // SLC_BLOCK) % LANES == lane, 1.0, 0.0).astype(ksa_ref.dtype)
    kw_ref[...] = _rope(kw, c1, s1, s2).astype(kw_ref.dtype)
    vs_ref[...] = n_ref[:, 768:896].astype(vs_ref.dtype)
    vw_ref[...] = n_ref[:, 896:1024].astype(vw_ref.dtype)
    gt_ref[...] = jax.nn.sigmoid(n_ref[:, 1024:1152])


def _nsa_prep(proj, rc, rs1, rs2, qg, kg, bd512, bd128, seq):
    m = proj.shape[0]
    tt = T_PREP
    row = lambda w: pl.BlockSpec((tt, w), lambda i: (i, 0))
    full = lambda a, b: pl.BlockSpec((a, b), lambda i: (0, 0))
    sds = lambda w, dt: jax.ShapeDtypeStruct((m, w), dt)
    return pl.pallas_call(
        functools.partial(_nsa_prep_kernel, tt=tt, seq=seq),
        grid=(m // tt,),
        in_specs=[pl.BlockSpec((tt, N_W), lambda i: (i, 1)), row(LANES), row(LANES), row(LANES),
                  full(1, NSA_WIDTH), full(2, LANES), full(NSA_WIDTH, NSA_WIDTH), full(LANES, LANES)],
        out_specs=[row(NSA_WIDTH), row(NSA_WIDTH), row(2 * LANES), row(LANES), row(LANES), row(LANES),
                   row(LANES)],
        out_shape=[sds(NSA_WIDTH, MXU_DTYPE), sds(NSA_WIDTH, MXU_DTYPE), sds(2 * LANES, MXU_DTYPE),
                   sds(LANES, MXU_DTYPE), sds(LANES, MXU_DTYPE), sds(LANES, MXU_DTYPE), sds(LANES, F32)],
        compiler_params=pltpu.CompilerParams(dimension_semantics=("parallel",),
                                             vmem_limit_bytes=VMEM_LIMIT),
        name="nsa_prep",
    )(proj, rc, rs1, rs2, qg, kg, bd512, bd128)


def _compress_kernel(x_ref, w1c_ref, pe_ref, w1_ref, w2_ref, kg_ref, o_ref, *, nb):
    is_k = pl.program_id(0) == 0
    w1c = w1c_ref[0]
    pe_term = _dot(jnp.broadcast_to(pe_ref[0], (SUBLANES, CMP_LEN * HEAD)), w1_ref[0])[0:1, :]
    rows = lax.broadcasted_iota(jnp.int32, (nb, 1), 0)
    outs = []
    for g in range(NSA_KV_HEADS):
        p = _dot(x_ref[0, 0, g], w1c)
        pre = p[:, 0:HEAD] + pltpu.roll(p[:, HEAD:2 * HEAD], nb - 1, 0) + pe_term
        y = _dot(jax.nn.gelu(pre), w2_ref[0])
        yk = y * lax.rsqrt(jnp.mean(y * y, axis=-1, keepdims=True) + EPS) * kg_ref[...]
        y = jnp.where(is_k, yk, y)
        outs.append(jnp.where(rows < nb - 1, y, 0.0))
    o_ref[0, 0] = jnp.concatenate(outs, axis=1).astype(o_ref.dtype)


def _compress(xc, w1c, pe, w1, w2, kg0, batch, nb):
    blk = CMP_STRIDE * HEAD
    return pl.pallas_call(
        functools.partial(_compress_kernel, nb=nb),
        grid=(2, batch),
        in_specs=[pl.BlockSpec((1, 1, NSA_KV_HEADS, nb, blk), lambda c, b: (c, b, 0, 0, 0)),
                  pl.BlockSpec((1, blk, 2 * HEAD), lambda c, b: (c, 0, 0)),
                  pl.BlockSpec((1, 1, CMP_LEN * HEAD), lambda c, b: (c, 0, 0)),
                  pl.BlockSpec((1, CMP_LEN * HEAD, HEAD), lambda c, b: (c, 0, 0)),
                  pl.BlockSpec((1, HEAD, HEAD), lambda c, b: (c, 0, 0)),
                  pl.BlockSpec((1, HEAD), lambda c, b: (0, 0))],
        out_specs=pl.BlockSpec((1, 1, nb, 2 * HEAD), lambda c, b: (c, b, 0, 0)),
        out_shape=jax.ShapeDtypeStruct((2, batch, nb, 2 * HEAD), MXU_DTYPE),
        compiler_params=pltpu.CompilerParams(dimension_semantics=("parallel", "parallel"),
                                             vmem_limit_bytes=VMEM_LIMIT),
        name="compress",
    )(xc, w1c, pe, w1, w2, kg0)


def _nsa_attn_kernel(*refs, tq, seq, nwb, kc, ncp, nslp):
    qn_ref, qr_ref, gt_ref, kcmp_ref, vcmp_ref, ksa_ref, vs_ref = refs[:7]
    kw_refs = refs[7:7 + nwb]
    vw_refs = refs[7 + nwb:7 + 2 * nwb]
    msel_ref = refs[7 + 2 * nwb]
    o_ref = refs[8 + 2 * nwb]
    bias_scr, m_scr, l_scr, acc_scr = refs[9 + 2 * nwb:]
    nslot = NSA_HEADS
    i = pl.program_id(1)
    lane = lax.broadcasted_iota(jnp.int32, (1, LANES), 1)
    lo = lane < HEAD

    def stack(q_ref):
        sl = [q_ref[:, j * LANES:(j + 1) * LANES] for j in range(NSA_GROUP)]
        zero = jnp.zeros_like(sl[0])
        return jnp.concatenate([jnp.where(lo, s, zero) for s in sl] + [jnp.where(lo, zero, s) for s in sl], axis=0)

    tpos = i * tq + lax.broadcasted_iota(jnp.int32, (tq, 1), 0)
    tpos_st = i * tq + lax.broadcasted_iota(jnp.int32, (nslot * tq, 1), 0) % tq
    gates = gt_ref[...]

    qn_st = stack(qn_ref)
    ckey = lax.broadcasted_iota(jnp.int32, (1, ncp), 1)
    cmask = (ckey * CMP_STRIDE + (CMP_LEN - 1)) <= tpos
    s_all = _dot_nt(qn_st, kcmp_ref[0, 0])
    vcm = vcmp_ref[0, 0]
    o_c = []
    psum = [None, None]
    for slot in range(nslot):
        s = jnp.where(cmask, s_all[slot * tq:(slot + 1) * tq], NEG_INF)
        e = jnp.where(cmask, jnp.exp(s - jnp.max(s, axis=-1, keepdims=True)), 0.0)
        l = jnp.sum(e, axis=-1, keepdims=True)
        p = e / jnp.where(l > 0.0, l, 1.0)
        o_c.append(_dot(p, vcm))
        g = slot // NSA_GROUP
        psum[g] = p if psum[g] is None else psum[g] + p

    jl = lax.broadcasted_iota(jnp.int32, (1, nslp), 1)
    jlf = jl.astype(F32)
    cur = tpos // SLC_BLOCK
    forced = (jl == 0) | (jl == cur) | (jl == cur - 1)
    causal = jl <= cur
    msel = msel_ref[...]
    for g in range(NSA_KV_HEADS):
        slc = _dot_hi_l(psum[g], msel)
        cand = jnp.where(causal & jnp.logical_not(forced), slc, -1.0)
        sel = jnp.where(forced, 1.0, 0.0)
        for _ in range(N_SELECT - 3):
            mx = jnp.max(cand, axis=-1, keepdims=True)
            first = jnp.min(jnp.where(cand == mx, jlf, float(nslp)), axis=-1, keepdims=True)
            hit = jlf == first
            sel = jnp.where(hit, 1.0, sel)
            cand = jnp.where(hit, -2.0, cand)
        sel = jnp.where(causal, sel, 0.0)
        bias_scr[g] = ((sel - 1.0) * SEL_BIG).astype(bias_scr.dtype)

    qr_st = stack(qr_ref)
    m_scr[...] = jnp.full(m_scr.shape, NEG_INF, F32)
    l_scr[...] = jnp.zeros(l_scr.shape, F32)
    acc_scr[...] = jnp.zeros(acc_scr.shape, F32)
    blocks_per_chunk = kc // SLC_BLOCK

    def chunk_step(c, diag):
        grp = (c * blocks_per_chunk) // LANES
        off = pl.multiple_of(grp * LANES, LANES)
        b0 = bias_scr[0, :, pl.ds(off, LANES)]
        b1 = bias_scr[1, :, pl.ds(off, LANES)]
        bst = jnp.concatenate([b0] * NSA_GROUP + [b1] * NSA_GROUP, axis=0)
        q_aug = jnp.concatenate([qr_st, bst], axis=1)
        k0 = pl.multiple_of(c * kc, kc)
        s = _dot_nt(q_aug, ksa_ref[pl.ds(k0, kc), :])
        if diag:
            kpos = c * kc + lax.broadcasted_iota(jnp.int32, (1, kc), 1)
            s = jnp.where(kpos <= tpos_st, s, -SEL_BIG)
        m_old = m_scr[...]
        m_new = jnp.maximum(m_old, jnp.max(s, axis=-1, keepdims=True))
        alpha = jnp.exp(m_old - m_new)
        p = jnp.exp(s - m_new[:, 0:1])
        l_scr[...] = alpha * l_scr[...] + jnp.sum(p, axis=-1, keepdims=True)
        acc_scr[...] = alpha * acc_scr[...] + _dot(p, vs_ref[pl.ds(k0, kc), :])
        m_scr[...] = m_new

    n_full = (i * tq) // kc

    def body(c, carry):
        chunk_step(c, False)
        return carry

    lax.fori_loop(0, n_full, body, 0)
    chunk_step(n_full, True)
    o_s = acc_scr[...] / l_scr[...]

    kwc = jnp.concatenate([r[...] for r in kw_refs], axis=0)
    vwc = jnp.concatenate([r[...] for r in vw_refs], axis=0)
    kpos_w = (i - (nwb - 1)) * tq + lax.broadcasted_iota(jnp.int32, (1, nwb * tq), 1)
    dist = tpos_st - kpos_w
    wmask = (dist >= 0) & (dist < WINDOW) & (kpos_w >= 0)
    sw = jnp.where(wmask, _dot_nt(qr_st, kwc), NEG_INF)
    ew = jnp.exp(sw - jnp.max(sw, axis=-1, keepdims=True))
    o_w = _dot(ew, vwc) / jnp.sum(ew, axis=-1, keepdims=True)

    for j in range(NSA_GROUP):
        halves = []
        for g in range(NSA_KV_HEADS):
            slot = g * NSA_GROUP + j
            hh = g * NSA_GROUP + j
            rs = slice(slot * tq, (slot + 1) * tq)
            gc = gates[:, 3 * hh:3 * hh + 1]
            gs = gates[:, 3 * hh + 1:3 * hh + 2]
            gw = gates[:, 3 * hh + 2:3 * hh + 3]
            halves.append(gc * o_c[slot] + gs * o_s[rs] + gw * o_w[rs])
        o_ref[:, j * LANES:(j + 1) * LANES] = jnp.where(lo, halves[0], halves[1]).astype(o_ref.dtype)


def _nsa_attn(qn, qr, gt, kvc, ksa, vs, kw, vw, msel, batch, seq):
    tq, kc = TQ, KC
    nq = seq // tq
    nwb = WINDOW // tq + 1
    ncp = seq // CMP_STRIDE
    nslp = msel.shape[1]
    row = lambda w: pl.BlockSpec((tq, w), lambda b, i: (b * nq + i, 0))
    win = lambda jb: pl.BlockSpec((tq, LANES), lambda b, i: (b * nq + jnp.maximum(i - (nwb - 1) + jb, 0), 0))
    in_specs = ([row(NSA_WIDTH), row(NSA_WIDTH), row(LANES),
                 pl.BlockSpec((1, 1, ncp, LANES), lambda b, i: (0, b, 0, 0)),
                 pl.BlockSpec((1, 1, ncp, LANES), lambda b, i: (1, b, 0, 0)),
                 pl.BlockSpec((seq, 2 * LANES), lambda b, i: (b, 0)),
                 pl.BlockSpec((seq, LANES), lambda b, i: (b, 0))]
                + [win(jb) for jb in range(nwb)] + [win(jb) for jb in range(nwb)]
                + [pl.BlockSpec((ncp, nslp), lambda b, i: (0, 0))])
    return pl.pallas_call(
        functools.partial(_nsa_attn_kernel, tq=tq, seq=seq, nwb=nwb, kc=kc, ncp=ncp, nslp=nslp),
        grid=(batch, nq),
        in_specs=in_specs,
        out_specs=row(NSA_WIDTH),
        out_shape=jax.ShapeDtypeStruct((batch * seq, NSA_WIDTH), MXU_DTYPE),
        scratch_shapes=[pltpu.VMEM((NSA_KV_HEADS, tq, nslp), MXU_DTYPE),
                        pltpu.VMEM((NSA_HEADS * tq, LANES), F32),
                        pltpu.VMEM((NSA_HEADS * tq, LANES), F32),
                        pltpu.VMEM((NSA_HEADS * tq, LANES), F32)],
        compiler_params=pltpu.CompilerParams(dimension_semantics=("parallel", "arbitrary"),
                                             vmem_limit_bytes=VMEM_LIMIT),
        name="nsa_attn",
    )(qn, qr, gt, kvc, kvc, ksa, vs, *([kw] * nwb), *([vw] * nwb), msel)


def _out_ffn_kernel(x_ref, og_ref, on_ref, c_ref, ch_ref, cw_ref, wo_ref, fg_ref, wgu_ref, wd_ref, o_ref,
                    *, tm, seq):
    i = pl.program_id(0)
    keep = jnp.where((i * tm) % seq == 0, 0.0, 1.0)
    rows = lax.broadcasted_iota(jnp.int32, (tm, 1), 0)
    u = c_ref[:, CONV_WIDTH:2 * CONV_WIDTH] * c_ref[:, 2 * CONV_WIDTH:3 * CONV_WIDTH]
    hu = ch_ref[:, CONV_WIDTH:2 * CONV_WIDTH] * ch_ref[:, 2 * CONV_WIDTH:3 * CONV_WIDTH] * keep
    w = cw_ref[...]
    conv = w[CONV_K - 1:CONV_K, :] * u
    for s in range(1, CONV_K):
        conv = conv + w[CONV_K - 1 - s:CONV_K - s, :] * _shift_rows(u, hu, s, rows)
    oc = c_ref[:, 0:CONV_WIDTH] * conv
    x1 = (x_ref[...] + _dot(og_ref[...], wo_ref[0:GDN_WIDTH, :])
          + _dot(on_ref[...], wo_ref[GDN_WIDTH:GDN_WIDTH + NSA_WIDTH, :])
          + _dot(oc, wo_ref[GDN_WIDTH + NSA_WIDTH:, :]))
    ms = jnp.mean(x1 * x1, axis=-1, keepdims=True)
    h2 = (x1 * lax.rsqrt(ms + EPS) * fg_ref[...]).astype(MXU_DTYPE)
    o_ref[...] = x1
    for c0 in range(0, D_FF, FF_CHUNK):
        gate = jnp.dot(h2, wgu_ref[:, c0:c0 + FF_CHUNK], preferred_element_type=F32)
        up = jnp.dot(h2, wgu_ref[:, D_FF + c0:D_FF + c0 + FF_CHUNK], preferred_element_type=F32)
        o_ref[...] += _dot(_silu(gate) * up, wd_ref[c0:c0 + FF_CHUNK, :])


def _out_ffn(x2, o_gdn, o_nsa, proj, conv_w, w_out, fgain, wgu, wd, seq):
    m = x2.shape[0]
    tm = TM_FFN
    hb = tm // SUBLANES
    full = lambda a, b: pl.BlockSpec((a, b), lambda i: (0, 0), pipeline_mode=pl.Buffered(1))
    return pl.pallas_call(
        functools.partial(_out_ffn_kernel, tm=tm, seq=seq),
        grid=(m // tm,),
        in_specs=[pl.BlockSpec((tm, D_MODEL), lambda i: (i, 0)),
                  pl.BlockSpec((tm, GDN_WIDTH), lambda i: (i, 0)),
                  pl.BlockSpec((tm, NSA_WIDTH), lambda i: (i, 0)),
                  pl.BlockSpec((tm, C_W), lambda i: (i, C_OFF // C_W)),
                  pl.BlockSpec((SUBLANES, C_W), lambda i: (jnp.maximum(i * hb - 1, 0), C_OFF // C_W)),
                  full(CONV_K, CONV_WIDTH), full(D_MODEL, D_MODEL), full(1, D_MODEL),
                  full(D_MODEL, 2 * D_FF), full(D_FF, D_MODEL)],
        out_specs=pl.BlockSpec((tm, D_MODEL), lambda i: (i, 0)),
        out_shape=jax.ShapeDtypeStruct((m, D_MODEL), F32),
        compiler_params=pltpu.CompilerParams(dimension_semantics=("parallel",),
                                             vmem_limit_bytes=VMEM_LIMIT),
        name="out_ffn",
    )(x2, o_gdn, o_nsa, proj, proj, conv_w, w_out, fgain, wgu, wd)


def _proj_column_map():
    offs = np.concatenate([[0], np.cumsum(IN_SIZES)])
    seg = lambda k: np.arange(offs[k], offs[k + 1])
    pad = lambda n: -np.ones(n, np.int64)
    nq = seg(6).reshape(NSA_HEADS, HEAD)[list(Q_SLOT_HEADS)].reshape(-1)
    cols = np.concatenate([
        seg(0), seg(1), seg(2), seg(3), seg(4), seg(5), pad(LANES - 2 * GDN_HEADS),
        nq, seg(9), seg(11), seg(10), seg(12), seg(13), pad(LANES - 3 * NSA_HEADS),
        seg(14), seg(15), seg(16),
        seg(7), seg(8)])
    assert cols.shape[0] == PROJ_W
    return cols


def _block_diag_ones(n):
    idx = np.arange(n) // HEAD
    return jnp.asarray(idx[:, None] == idx[None, :], MXU_DTYPE)


def _rope_lane_tables(positions):
    inv = jnp.float32(ROPE_THETA) ** (-jnp.arange(0, ROT_DIM, 2, dtype=jnp.float32) / ROT_DIM)
    ang = positions.astype(jnp.float32).reshape(-1)[:, None] * inv
    cos, sin = jnp.cos(ang), jnp.sin(ang)
    m = ang.shape[0]
    half = ROT_DIM // 2
    one = jnp.ones((m, HEAD - ROT_DIM), F32)
    zero_h = jnp.zeros((m, half), F32)
    zero_r = jnp.zeros((m, HEAD - ROT_DIM), F32)
    c = jnp.concatenate([cos, cos, one], axis=1)
    s1 = jnp.concatenate([-sin, zero_h, zero_r], axis=1)
    s2 = jnp.concatenate([zero_h, sin, zero_r], axis=1)
    tile2 = lambda t: jnp.concatenate([t, t], axis=1)
    return tile2(c), tile2(s1), tile2(s2)


def _selection_matrix(ncp, nslp):
    ratio = SLC_BLOCK // CMP_STRIDE
    frac = np.minimum(CMP_LEN, SLC_BLOCK - CMP_STRIDE * np.arange(ratio)).astype(np.float64) / CMP_LEN
    mat = np.zeros((ncp, nslp), np.float32)
    c = np.arange(ncp)
    mat[c, c // ratio] = frac[c % ratio]
    nxt = c // ratio + 1
    ok = nxt < nslp
    mat[c[ok], nxt[ok]] += (1.0 - frac[c % ratio])[ok]
    return jnp.asarray(mat, MXU_DTYPE)


def kernel(x, positions, attn_norm, w_in, gdn_conv_w, gdn_a_log, gdn_dt_bias, gdn_norm, nsa_q_norm,
           nsa_k_norm, nsa_cmp_pe, nsa_cmp_w1, nsa_cmp_w2, conv_w, w_out, ffn_norm, w_gate_up, w_down):
    batch, seq, _ = x.shape
    depth = w_in.shape[0]
    m = batch * seq
    assert seq % max(TM_PROJ, TM_FFN, T_PREP, KC) == 0 and (seq // CMP_STRIDE) % LANES == 0
    nb = seq // CMP_STRIDE
    nslp = -(-(seq // SLC_BLOCK) // LANES) * LANES

    cols = _proj_column_map()
    take = jnp.asarray(np.maximum(cols, 0), jnp.int32)
    valid = jnp.asarray(cols >= 0)
    o_rows = np.arange(NSA_WIDTH).reshape(NSA_HEADS, HEAD)[list(Q_SLOT_HEADS)].reshape(-1) + GDN_WIDTH
    wo_rows = jnp.asarray(np.concatenate([np.arange(GDN_WIDTH), o_rows,
                                          np.arange(GDN_WIDTH + NSA_WIDTH, D_MODEL)]), jnp.int32)
    bd128, bd256, bd512 = _block_diag_ones(128), _block_diag_ones(256), _block_diag_ones(512)
    rc, rs1, rs2 = _rope_lane_tables(positions)
    msel = _selection_matrix(nb, nslp)
    lane_pad = lambda v: jnp.zeros((1, LANES), F32).at[0, GDN_HEADS:2 * GDN_HEADS].set(v.astype(F32))

    x2 = x.reshape(m, D_MODEL)
    for l in range(depth):
        w_l = jnp.where(valid[None, :], jnp.take(w_in[l], take, axis=1), 0.0).astype(MXU_DTYPE)
        proj = _in_proj(x2, attn_norm[l].reshape(1, D_MODEL), w_l)

        o_gdn = _gdn(proj, gdn_conv_w[l], lane_pad(gdn_a_log[l]), lane_pad(gdn_dt_bias[l]),
                     gdn_norm[l].reshape(1, HEAD), bd256, batch, seq)

        qg = jnp.tile(nsa_q_norm[l], NSA_HEADS).reshape(1, NSA_WIDTH)
        kg = jnp.tile(nsa_k_norm[l, 1:3], (1, NSA_KV_HEADS))
        qn, qr, ksa, kw, vs, vw, gt = _nsa_prep(proj, rc, rs1, rs2, qg, kg, bd512, bd128, seq)

        xc = proj[:, P_OFF:P_OFF + P_W].reshape(batch, nb, CMP_STRIDE, 2, NSA_KV_HEADS, HEAD)
        xc = xc.transpose(3, 0, 4, 1, 2, 5).reshape(2, batch, NSA_KV_HEADS, nb, CMP_STRIDE * HEAD)
        w1 = nsa_cmp_w1[l]
        half = CMP_STRIDE * HEAD
        w1c = jnp.concatenate([w1[:, :half], w1[:, half:]], axis=2).astype(MXU_DTYPE)
        kvc = _compress(xc, w1c, nsa_cmp_pe[l].reshape(2, 1, CMP_LEN * HEAD), w1.astype(MXU_DTYPE),
                        nsa_cmp_w2[l].astype(MXU_DTYPE), nsa_k_norm[l, 0].reshape(1, HEAD), batch, nb)

        o_nsa = _nsa_attn(qn, qr, gt, kvc, ksa, vs, kw, vw, msel, batch, seq)

        x2 = _out_ffn(x2, o_gdn, o_nsa, proj, conv_w[l], jnp.take(w_out[l], wo_rows, axis=0).astype(MXU_DTYPE),
                      ffn_norm[l].reshape(1, D_MODEL), w_gate_up[l].astype(MXU_DTYPE),
                      w_down[l].astype(MXU_DTYPE), seq)
    return x2.reshape(batch, seq, D_MODEL)
```

```python
import functools

import numpy as np
import jax
import jax.numpy as jnp
from jax import lax
from jax.experimental import pallas as pl
from jax.experimental.pallas import tpu as pltpu

F32 = jnp.float32
MXU_DTYPE = jnp.bfloat16

D_MODEL = 1024
HEAD = 64
EPS = 1e-6
NEG_INF = -1e30
GDN_HEADS = 4
GDN_WIDTH = 256
GDN_CONV = 4
GDN_CHUNK = 64
NSA_HEADS = 8
NSA_WIDTH = 512
NSA_KV_HEADS = 2
NSA_GROUP = 4
NSA_KV_WIDTH = 128
CMP_STRIDE = 16
CMP_LEN = 32
SLC_BLOCK = 64
N_SELECT = 16
WINDOW = 512
CONV_WIDTH = 256
CONV_K = 3
ROPE_THETA = 500000.0
ROT_DIM = 16
D_FF = 2816
IN_SIZES = (256, 256, 256, 256, 4, 4, 512, 128, 128, 128, 128, 128, 128, 24, 256, 256, 256)
D_IN = sum(IN_SIZES)

LANES = 128
SUBLANES = 8
VMEM_LIMIT = 56 * 1024 * 1024

G_W = 1024 + LANES
N_W = 512 + 4 * 128 + LANES
C_W = 3 * CONV_WIDTH
P_W = 2 * NSA_KV_WIDTH
G_OFF, N_OFF, C_OFF, P_OFF = 0, G_W, G_W + N_W, G_W + N_W + C_W
PROJ_W = G_W + N_W + C_W + P_W
Q_SLOT_HEADS = (0, 4, 1, 5, 2, 6, 3, 7)

SEL_BIG = 16384.0
MAX_SOFTMAX_SHIFT = 40.0

TM_PROJ = 512
TM_FFN = 512
T_GDN = 256
T_PREP = 512
TQ = 128
KC = 512
FF_CHUNK = 256


def _dot(a, b):
    return jnp.dot(a.astype(MXU_DTYPE), b.astype(MXU_DTYPE), preferred_element_type=F32)


def _dot_nt(a, b):
    return lax.dot_general(a.astype(MXU_DTYPE), b.astype(MXU_DTYPE), (((1,), (1,)), ((), ())),
                           preferred_element_type=F32)


def _dot_tn(a, b):
    return lax.dot_general(a.astype(MXU_DTYPE), b.astype(MXU_DTYPE), (((0,), (0,)), ((), ())),
                           preferred_element_type=F32)


def _split3(x):
    a = x.astype(MXU_DTYPE)
    r = x - a.astype(F32)
    b = r.astype(MXU_DTYPE)
    c = (r - b.astype(F32)).astype(MXU_DTYPE)
    return a, b, c


def _dot_hi_l(x, m):
    a, b, c = _split3(x)
    f = lambda t: jnp.dot(t, m, preferred_element_type=F32)
    return f(a) + f(b) + f(c)


def _dot_hi(x, y):
    xa, xb, xc = _split3(x)
    ya, yb, yc = _split3(y)
    f = lambda s, t: jnp.dot(s, t, preferred_element_type=F32)
    return f(xa, ya) + (f(xa, yb) + f(xb, ya)) + (f(xa, yc) + f(xb, yb) + f(xc, ya))


def _silu(x):
    return x * jax.nn.sigmoid(x)


def _shift_rows(x, halo, s, rows):
    y = pltpu.roll(x, s, 0)
    for r in range(s):
        y = jnp.where(rows == r, halo[SUBLANES - s + r:SUBLANES - s + r + 1, :], y)
    return y


def _in_proj_kernel(x_ref, g_ref, w_ref, o_ref):
    x = x_ref[...]
    ms = jnp.mean(x * x, axis=-1, keepdims=True)
    h = (x * lax.rsqrt(ms + EPS) * g_ref[...]).astype(MXU_DTYPE)
    for c in range(0, PROJ_W, 256):
        o_ref[:, c:c + 256] = jnp.dot(h, w_ref[:, c:c + 256], preferred_element_type=F32)


def _in_proj(x2, gain, w):
    m = x2.shape[0]
    tm = TM_PROJ
    return pl.pallas_call(
        _in_proj_kernel,
        grid=(m // tm,),
        in_specs=[pl.BlockSpec((tm, D_MODEL), lambda i: (i, 0)),
                  pl.BlockSpec((1, D_MODEL), lambda i: (0, 0)),
                  pl.BlockSpec((D_MODEL, PROJ_W), lambda i: (0, 0))],
        out_specs=pl.BlockSpec((tm, PROJ_W), lambda i: (i, 0)),
        out_shape=jax.ShapeDtypeStruct((m, PROJ_W), F32),
        compiler_params=pltpu.CompilerParams(dimension_semantics=("parallel",),
                                             vmem_limit_bytes=VMEM_LIMIT),
        name="in_proj",
    )(x2, gain, w)


def _gdn_kernel(g_ref, gh_ref, cw_ref, alog_ref, dt_ref, gn_ref, bd_ref, o_ref, s_ref, *, tt):
    t_idx = pl.program_id(1)

    @pl.when(t_idx == 0)
    def _():
        s_ref[...] = jnp.zeros_like(s_ref)

    keep = jnp.where(t_idx == 0, 0.0, 1.0)
    rows = lax.broadcasted_iota(jnp.int32, (tt, 1), 0)
    x = g_ref[:, 0:3 * GDN_WIDTH]
    hx = gh_ref[:, 0:3 * GDN_WIDTH] * keep
    w = cw_ref[...]
    y = w[GDN_CONV - 1:GDN_CONV, :] * x
    for s in range(1, GDN_CONV):
        y = y + w[GDN_CONV - 1 - s:GDN_CONV - s, :] * _shift_rows(x, hx, s, rows)
    y = _silu(y)
    q = y[:, 0:GDN_WIDTH]
    k = y[:, GDN_WIDTH:2 * GDN_WIDTH]
    v = y[:, 2 * GDN_WIDTH:3 * GDN_WIDTH]
    bd = bd_ref[...]
    q = q * lax.rsqrt(_dot_hi_l(q * q, bd) + EPS) * (HEAD ** -0.5)
    k = k * lax.rsqrt(_dot_hi_l(k * k, bd) + EPS)

    gg = g_ref[:, 4 * GDN_WIDTH:4 * GDN_WIDTH + LANES]
    lane = lax.broadcasted_iota(jnp.int32, (1, LANES), 1)
    xa = gg + dt_ref[...]
    softplus = jnp.maximum(xa, 0.0) + jnp.log1p(jnp.exp(-jnp.abs(xa)))
    g2 = jnp.where(lane < GDN_HEADS, jax.nn.sigmoid(gg), -jnp.exp(alog_ref[...]) * softplus)
    cs = g2.T
    lane_t = lax.broadcasted_iota(jnp.int32, (1, tt), 1) % GDN_CHUNK
    step = 1
    while step < GDN_CHUNK:
        cs = cs + jnp.where(lane_t >= step, pltpu.roll(cs, step, 1), 0.0)
        step *= 2
    gcum_t = cs
    gcum = cs.T
    egc = jnp.exp(gcum)

    ri = lax.broadcasted_iota(jnp.int32, (GDN_CHUNK, GDN_CHUNK), 0)
    ci = lax.broadcasted_iota(jnp.int32, (GDN_CHUNK, GDN_CHUNK), 1)
    tril = ri >= ci
    strict = ri > ci
    eye = jnp.where(ri == ci, 1.0, 0.0)
    gn = gn_ref[...]

    states = [s_ref[h] for h in range(GDN_HEADS)]
    for n in range(tt // GDN_CHUNK):
        r0 = n * GDN_CHUNK
        rs = slice(r0, r0 + GDN_CHUNK)
        for h in range(GDN_HEADS):
            hs = slice(h * HEAD, (h + 1) * HEAD)
            gl_lane = GDN_HEADS + h
            gc = gcum[rs, gl_lane:gl_lane + 1]
            gr = gcum_t[gl_lane:gl_lane + 1, rs]
            decay = jnp.where(tril, jnp.exp(jnp.where(tril, gc - gr, 0.0)), 0.0)
            beta = g2[rs, h:h + 1]
            eg = egc[rs, gl_lane:gl_lane + 1]
            qh, kh, vh = q[rs, hs], k[rs, hs], v[rs, hs]
            kb = kh * beta
            a = jnp.where(strict, _dot_nt(kb, kh) * decay, 0.0)
            xm = -a
            tinv = eye + xm
            pw = xm
            for _ in range(5):
                pw = _dot_hi(pw, pw)
                tinv = tinv + _dot_hi(tinv, pw)
            rhs = jnp.concatenate([vh * beta, kb * eg], axis=1)
            sol = _dot_hi(tinv, rhs)
            u = sol[:, 0:HEAD]
            wm = sol[:, HEAD:2 * HEAD]
            qk = jnp.where(tril, _dot_nt(qh, kh) * decay, 0.0)
            g_last = gcum[r0 + GDN_CHUNK - 1:r0 + GDN_CHUNK, gl_lane:gl_lane + 1]
            q_dec = qh * eg
            k_dec = kh * jnp.exp(g_last - gc)
            st = states[h]
            v_new = u - _dot(wm, st)
            o = _dot(q_dec, st) + _dot(qk, v_new)
            states[h] = st * jnp.exp(g_last) + _dot_tn(k_dec, v_new)
            on = o * lax.rsqrt(jnp.mean(o * o, axis=-1, keepdims=True) + EPS) * gn
            z = g_ref[rs, 3 * GDN_WIDTH + h * HEAD:3 * GDN_WIDTH + (h + 1) * HEAD]
            o_ref[rs, hs] = (on * _silu(z)).astype(o_ref.dtype)
    for h in range(GDN_HEADS):
        s_ref[h] = states[h]


def _gdn(proj, conv_w, alog_pad, dt_pad, gnorm, bd256, batch, seq):
    tt = T_GDN
    nt = seq // tt
    hb = tt // SUBLANES
    return pl.pallas_call(
        functools.partial(_gdn_kernel, tt=tt),
        grid=(batch, nt),
        in_specs=[pl.BlockSpec((tt, G_W), lambda b, t: (b * nt + t, 0)),
                  pl.BlockSpec((SUBLANES, G_W), lambda b, t: (jnp.maximum((b * nt + t) * hb - 1, 0), 0)),
                  pl.BlockSpec((GDN_CONV, 3 * GDN_WIDTH), lambda b, t: (0, 0)),
                  pl.BlockSpec((1, LANES), lambda b, t: (0, 0)),
                  pl.BlockSpec((1, LANES), lambda b, t: (0, 0)),
                  pl.BlockSpec((1, HEAD), lambda b, t: (0, 0)),
                  pl.BlockSpec((GDN_WIDTH, GDN_WIDTH), lambda b, t: (0, 0))],
        out_specs=pl.BlockSpec((tt, GDN_WIDTH), lambda b, t: (b * nt + t, 0)),
        out_shape=jax.ShapeDtypeStruct((batch * seq, GDN_WIDTH), MXU_DTYPE),
        scratch_shapes=[pltpu.VMEM((GDN_HEADS, HEAD, HEAD), F32)],
        compiler_params=pltpu.CompilerParams(dimension_semantics=("parallel", "arbitrary"),
                                             vmem_limit_bytes=VMEM_LIMIT),
        name="gdn",
    )(proj, proj, conv_w, alog_pad, dt_pad, gnorm, bd256)


def _rope(x, c, s1, s2):
    wdt = x.shape[1]
    return x * c + pltpu.roll(x, wdt - ROT_DIM // 2, 1) * s1 + pltpu.roll(x, ROT_DIM // 2, 1) * s2


def _nsa_prep_kernel(n_ref, c_ref, s1_ref, s2_ref, qg_ref, kg_ref, bdq_ref, bdk_ref,
                     qn_ref, qr_ref, ksa_ref, kw_ref, vs_ref, vw_ref, gt_ref, *, tt, seq):
    c1, s1, s2 = c_ref[...], s1_ref[...], s2_ref[...]
    c4 = jnp.concatenate([c1] * 4, axis=1)
    s14 = jnp.concatenate([s1] * 4, axis=1)
    s24 = jnp.concatenate([s2] * 4, axis=1)
    q = n_ref[:, 0:NSA_WIDTH]
    qn = q * lax.rsqrt(_dot_hi_l(q * q, bdq_ref[...]) * (1.0 / HEAD) + EPS) * qg_ref[...]
    qn = qn * (HEAD ** -0.5)
    qn_ref[...] = qn.astype(qn_ref.dtype)
    qr_ref[...] = _rope(qn, c4, s14, s24).astype(qr_ref.dtype)
    bdk = bdk_ref[...]
    ks = n_ref[:, 512:640]
    kw = n_ref[:, 640:768]
    ks = ks * lax.rsqrt(_dot_hi_l(ks * ks, bdk) * (1.0 / HEAD) + EPS) * kg_ref[0:1, :]
    kw = kw * lax.rsqrt(_dot_hi_l(kw * kw, bdk) * (1.0 / HEAD) + EPS) * kg_ref[1:2, :]
    ksa_ref[:, 0:LANES] = _rope(ks, c1, s1, s2).astype(ksa_ref.dtype)
    rows = (lax.broadcasted_iota(jnp.int32, (tt, LANES), 0) + pl.program_id(0) * tt) % seq
    lane = lax.broadcasted_iota(jnp.int32, (tt, LANES), 1)
    ksa_ref[:, LANES:2 * LANES] = jnp.where((rows // SLC_BLOCK) % LANES == lane, 1.0, 0.0).astype(ksa_ref.dtype)
    kw_ref[...] = _rope(kw, c1, s1, s2).astype(kw_ref.dtype)
    vs_ref[...] = n_ref[:, 768:896].astype(vs_ref.dtype)
    vw_ref[...] = n_ref[:, 896:1024].astype(vw_ref.dtype)
    gt_ref[...] = jax.nn.sigmoid(n_ref[:, 1024:1152])


def _nsa_prep(proj, rc, rs1, rs2, qg, kg, bd512, bd128, seq):
    m = proj.shape[0]
    tt = T_PREP
    row = lambda w: pl.BlockSpec((tt, w), lambda i: (i, 0))
    full = lambda a, b: pl.BlockSpec((a, b), lambda i: (0, 0))
    sds = lambda w, dt: jax.ShapeDtypeStruct((m, w), dt)
    return pl.pallas_call(
        functools.partial(_nsa_prep_kernel, tt=tt, seq=seq),
        grid=(m // tt,),
        in_specs=[pl.BlockSpec((tt, N_W), lambda i: (i, 1)), row(LANES), row(LANES), row(LANES),
                  full(1, NSA_WIDTH), full(2, LANES), full(NSA_WIDTH, NSA_WIDTH), full(LANES, LANES)],
        out_specs=[row(NSA_WIDTH), row(NSA_WIDTH), row(2 * LANES), row(LANES), row(LANES), row(LANES),
                   row(LANES)],
        out_shape=[sds(NSA_WIDTH, MXU_DTYPE), sds(NSA_WIDTH, MXU_DTYPE), sds(2 * LANES, MXU_DTYPE),
                   sds(LANES, MXU_DTYPE), sds(LANES, MXU_DTYPE), sds(LANES, MXU_DTYPE), sds(LANES, F32)],
        compiler_params=pltpu.CompilerParams(dimension_semantics=("parallel",),
                                             vmem_limit_bytes=VMEM_LIMIT),
        name="nsa_prep",
    )(proj, rc, rs1, rs2, qg, kg, bd512, bd128)


def _compress_kernel(x_ref, w1c_ref, pe_ref, w1_ref, w2_ref, kg_ref, o_ref, *, nb):
    is_k = pl.program_id(0) == 0
    w1c = w1c_ref[0]
    pe_term = _dot(jnp.broadcast_to(pe_ref[0], (SUBLANES, CMP_LEN * HEAD)), w1_ref[0])[0:1, :]
    rows = lax.broadcasted_iota(jnp.int32, (nb, 1), 0)
    outs = []
    for g in range(NSA_KV_HEADS):
        p = _dot(x_ref[0, 0, g], w1c)
        pre = p[:, 0:HEAD] + pltpu.roll(p[:, HEAD:2 * HEAD], nb - 1, 0) + pe_term
        y = _dot(jax.nn.gelu(pre), w2_ref[0])
        yk = y * lax.rsqrt(jnp.mean(y * y, axis=-1, keepdims=True) + EPS) * kg_ref[...]
        y = jnp.where(is_k, yk, y)
        outs.append(jnp.where(rows < nb - 1, y, 0.0))
    o_ref[0, 0] = jnp.concatenate(outs, axis=1).astype(o_ref.dtype)


def _compress(xc, w1c, pe, w1, w2, kg0, batch, nb):
    blk = CMP_STRIDE * HEAD
    return pl.pallas_call(
        functools.partial(_compress_kernel, nb=nb),
        grid=(2, batch),
        in_specs=[pl.BlockSpec((1, 1, NSA_KV_HEADS, nb, blk), lambda c, b: (c, b, 0, 0, 0)),
                  pl.BlockSpec((1, blk, 2 * HEAD), lambda c, b: (c, 0, 0)),
                  pl.BlockSpec((1, 1, CMP_LEN * HEAD), lambda c, b: (c, 0, 0)),
                  pl.BlockSpec((1, CMP_LEN * HEAD, HEAD), lambda c, b: (c, 0, 0)),
                  pl.BlockSpec((1, HEAD, HEAD), lambda c, b: (c, 0, 0)),
                  pl.BlockSpec((1, HEAD), lambda c, b: (0, 0))],
        out_specs=pl.BlockSpec((1, 1, nb, 2 * HEAD), lambda c, b: (c, b, 0, 0)),
        out_shape=jax.ShapeDtypeStruct((2, batch, nb, 2 * HEAD), MXU_DTYPE),
        compiler_params=pltpu.CompilerParams(dimension_semantics=("parallel", "parallel"),
                                             vmem_limit_bytes=VMEM_LIMIT),
        name="compress",
    )(xc, w1c, pe, w1, w2, kg0)


def _nsa_attn_kernel(*refs, tq, seq, nwb, kc, ncp, nslp, shifted):
    qn_ref, qr_ref, gt_ref, kcmp_ref, vcmp_ref, ksa_ref, vs_ref = refs[:7]
    kw_refs = refs[7:7 + nwb]
    vw_refs = refs[7 + nwb:7 + 2 * nwb]
    msel_ref, cs_ref = refs[7 + 2 * nwb:9 + 2 * nwb]
    o_ref = refs[9 + 2 * nwb]
    bias_scr, qaug_scr, m_scr, l_scr, acc_scr = refs[10 + 2 * nwb:]
    nslot = NSA_HEADS
    i = pl.program_id(1)
    c_cmp, c_slc, c_win = cs_ref[0:1, 0:1], cs_ref[0:1, 1:2], cs_ref[0:1, 2:3]
    lane = lax.broadcasted_iota(jnp.int32, (1, LANES), 1)
    lo = lane < HEAD

    def stack(q_ref):
        sl = [q_ref[:, j * LANES:(j + 1) * LANES] for j in range(NSA_GROUP)]
        zero = jnp.zeros_like(sl[0])
        return jnp.concatenate([jnp.where(lo, s, zero) for s in sl] + [jnp.where(lo, zero, s) for s in sl], axis=0)

    tpos = i * tq + lax.broadcasted_iota(jnp.int32, (tq, 1), 0)
    tpos_st = i * tq + lax.broadcasted_iota(jnp.int32, (nslot * tq, 1), 0) % tq
    gates = gt_ref[...]

    qn_st = stack(qn_ref)
    ckey = lax.broadcasted_iota(jnp.int32, (1, ncp), 1)
    cmask = (ckey * CMP_STRIDE + (CMP_LEN - 1)) <= tpos
    s_all = _dot_nt(qn_st, kcmp_ref[0, 0])
    vcm = vcmp_ref[0, 0]
    o_c = []
    psum = [None, None]
    for slot in range(nslot):
        s = s_all[slot * tq:(slot + 1) * tq]
        if shifted:
            e = jnp.where(cmask, jnp.exp(s - c_cmp), 0.0)
        else:
            s = jnp.where(cmask, s, NEG_INF)
            e = jnp.where(cmask, jnp.exp(s - jnp.max(s, axis=-1, keepdims=True)), 0.0)
        l = jnp.sum(e, axis=-1, keepdims=True)
        p = e * (1.0 / jnp.where(l > 0.0, l, 1.0))
        o_c.append(_dot(p, vcm))
        g = slot // NSA_GROUP
        psum[g] = p if psum[g] is None else psum[g] + p

    jrow = lax.broadcasted_iota(jnp.int32, (nslp, 1), 0)
    jrowf = jrow.astype(F32)
    cur_t = (i * tq + lax.broadcasted_iota(jnp.int32, (1, tq), 1)) // SLC_BLOCK
    forced = (jrow == 0) | (jrow == cur_t) | (jrow == cur_t - 1)
    causal = jrow <= cur_t
    msel = msel_ref[...]
    shift_s = c_slc if shifted else 0.0
    for g in range(NSA_KV_HEADS):
        slc = _dot_hi_l(psum[g], msel).T
        cand = jnp.where(causal & jnp.logical_not(forced), slc, -1.0)
        sel = jnp.where(forced, 1.0, 0.0)
        for _ in range(N_SELECT - 3):
            mx = jnp.max(cand, axis=0, keepdims=True)
            first = jnp.min(jnp.where(cand == mx, jrowf, float(nslp)), axis=0, keepdims=True)
            hit = jrowf == first
            sel = jnp.where(hit, 1.0, sel)
            cand = jnp.where(hit, -2.0, cand)
        sel = jnp.where(causal, sel, 0.0)
        bias_scr[g] = ((sel - 1.0) * SEL_BIG - shift_s).T.astype(bias_scr.dtype)

    qr_st = stack(qr_ref)
    qaug_scr[:, 0:LANES] = qr_st
    if not shifted:
        m_scr[...] = jnp.full(m_scr.shape, NEG_INF, F32)
    l_scr[...] = jnp.zeros(l_scr.shape, F32)
    acc_scr[...] = jnp.zeros(acc_scr.shape, F32)
    blocks_per_chunk = kc // SLC_BLOCK
    chunks_per_group = LANES // blocks_per_chunk

    def chunk_step(c, diag):
        @pl.when(c % chunks_per_group == 0)
        def _():
            off = pl.multiple_of((c // chunks_per_group) * LANES, LANES)
            b0 = bias_scr[0, :, pl.ds(off, LANES)]
            b1 = bias_scr[1, :, pl.ds(off, LANES)]
            qaug_scr[:, LANES:2 * LANES] = jnp.concatenate([b0] * NSA_GROUP + [b1] * NSA_GROUP, axis=0)

        k0 = pl.multiple_of(c * kc, kc)
        s = _dot_nt(qaug_scr[...], ksa_ref[pl.ds(k0, kc), :])
        if diag:
            kpos = c * kc + lax.broadcasted_iota(jnp.int32, (1, kc), 1)
            s = jnp.where(kpos <= tpos_st, s, -SEL_BIG)
        if shifted:
            p = jnp.exp(s)
            psum_l = p[:, 0:LANES]
            for t in range(1, kc // LANES):
                psum_l = psum_l + p[:, t * LANES:(t + 1) * LANES]
            l_scr[...] += psum_l
            acc_scr[...] += _dot(p, vs_ref[pl.ds(k0, kc), :])
        else:
            m_old = m_scr[...]
            m_new = jnp.maximum(m_old, jnp.max(s, axis=-1, keepdims=True))
            alpha = jnp.exp(m_old - m_new)
            p = jnp.exp(s - m_new[:, 0:1])
            l_scr[...] = alpha * l_scr[...] + jnp.sum(p, axis=-1, keepdims=True)
            acc_scr[...] = alpha * acc_scr[...] + _dot(p, vs_ref[pl.ds(k0, kc), :])
            m_scr[...] = m_new

    n_full = (i * tq) // kc

    def body(c, carry):
        chunk_step(c, False)
        return carry

    lax.fori_loop(0, n_full, body, 0)
    chunk_step(n_full, True)
    if shifted:
        o_s = acc_scr[...] * (1.0 / jnp.sum(l_scr[...], axis=-1, keepdims=True))
    else:
        o_s = acc_scr[...] / l_scr[...]

    kwc = jnp.concatenate([r[...] for r in kw_refs], axis=0)
    vwc = jnp.concatenate([r[...] for r in vw_refs], axis=0)
    kpos_w = (i - (nwb - 1)) * tq + lax.broadcasted_iota(jnp.int32, (1, nwb * tq), 1)
    dist = tpos_st - kpos_w
    wmask = (dist >= 0) & (dist < WINDOW) & (kpos_w >= 0)
    sw = _dot_nt(qr_st, kwc)
    if shifted:
        ew = jnp.where(wmask, jnp.exp(sw - c_win), 0.0)
    else:
        sw = jnp.where(wmask, sw, NEG_INF)
        ew = jnp.exp(sw - jnp.max(sw, axis=-1, keepdims=True))
    o_w = _dot(ew, vwc) * (1.0 / jnp.sum(ew, axis=-1, keepdims=True))

    for j in range(NSA_GROUP):
        halves = []
        for g in range(NSA_KV_HEADS):
            slot = g * NSA_GROUP + j
            hh = g * NSA_GROUP + j
            rs = slice(slot * tq, (slot + 1) * tq)
            gc = gates[:, 3 * hh:3 * hh + 1]
            gs = gates[:, 3 * hh + 1:3 * hh + 2]
            gw = gates[:, 3 * hh + 2:3 * hh + 3]
            halves.append(gc * o_c[slot] + gs * o_s[rs] + gw * o_w[rs])
        o_ref[:, j * LANES:(j + 1) * LANES] = jnp.where(lo, halves[0], halves[1]).astype(o_ref.dtype)


def _nsa_attn(qn, qr, gt, kvc, ksa, vs, kw, vw, msel, cshift, batch, seq, shifted):
    tq, kc = TQ, KC
    nq = seq // tq
    nwb = WINDOW // tq + 1
    ncp = seq // CMP_STRIDE
    nslp = msel.shape[1]
    row = lambda w: pl.BlockSpec((tq, w), lambda b, i: (b * nq + i, 0))
    win = lambda jb: pl.BlockSpec((tq, LANES), lambda b, i: (b * nq + jnp.maximum(i - (nwb - 1) + jb, 0), 0))
    in_specs = ([row(NSA_WIDTH), row(NSA_WIDTH), row(LANES),
                 pl.BlockSpec((1, 1, ncp, LANES), lambda b, i: (0, b, 0, 0)),
                 pl.BlockSpec((1, 1, ncp, LANES), lambda b, i: (1, b, 0, 0)),
                 pl.BlockSpec((seq, 2 * LANES), lambda b, i: (b, 0)),
                 pl.BlockSpec((seq, LANES), lambda b, i: (b, 0))]
                + [win(jb) for jb in range(nwb)] + [win(jb) for jb in range(nwb)]
                + [pl.BlockSpec((ncp, nslp), lambda b, i: (0, 0)),
                   pl.BlockSpec((1, LANES), lambda b, i: (0, 0))])
    return pl.pallas_call(
        functools.partial(_nsa_attn_kernel, tq=tq, seq=seq, nwb=nwb, kc=kc, ncp=ncp, nslp=nslp,
                          shifted=shifted),
        grid=(batch, nq),
        in_specs=in_specs,
        out_specs=row(NSA_WIDTH),
        out_shape=jax.ShapeDtypeStruct((batch * seq, NSA_WIDTH), MXU_DTYPE),
        scratch_shapes=[pltpu.VMEM((NSA_KV_HEADS, tq, nslp), MXU_DTYPE),
                        pltpu.VMEM((NSA_HEADS * tq, 2 * LANES), MXU_DTYPE),
                        pltpu.VMEM((NSA_HEADS * tq, LANES), F32),
                        pltpu.VMEM((NSA_HEADS * tq, LANES), F32),
                        pltpu.VMEM((NSA_HEADS * tq, LANES), F32)],
        compiler_params=pltpu.CompilerParams(dimension_semantics=("parallel", "arbitrary"),
                                             vmem_limit_bytes=VMEM_LIMIT),
        name="nsa_attn" if shifted else "nsa_attn_runmax",
    )(qn, qr, gt, kvc, kvc, ksa, vs, *([kw] * nwb), *([vw] * nwb), msel, cshift)


def _out_ffn_kernel(x_ref, og_ref, on_ref, c_ref, ch_ref, cw_ref, wo_ref, fg_ref, wgu_ref, wd_ref, o_ref,
                    *, tm, seq):
    i = pl.program_id(0)
    keep = jnp.where((i * tm) % seq == 0, 0.0, 1.0)
    rows = lax.broadcasted_iota(jnp.int32, (tm, 1), 0)
    u = c_ref[:, CONV_WIDTH:2 * CONV_WIDTH] * c_ref[:, 2 * CONV_WIDTH:3 * CONV_WIDTH]
    hu = ch_ref[:, CONV_WIDTH:2 * CONV_WIDTH] * ch_ref[:, 2 * CONV_WIDTH:3 * CONV_WIDTH] * keep
    w = cw_ref[...]
    conv = w[CONV_K - 1:CONV_K, :] * u
    for s in range(1, CONV_K):
        conv = conv + w[CONV_K - 1 - s:CONV_K - s, :] * _shift_rows(u, hu, s, rows)
    oc = c_ref[:, 0:CONV_WIDTH] * conv
    x1 = (x_ref[...] + _dot(og_ref[...], wo_ref[0:GDN_WIDTH, :])
          + _dot(on_ref[...], wo_ref[GDN_WIDTH:GDN_WIDTH + NSA_WIDTH, :])
          + _dot(oc, wo_ref[GDN_WIDTH + NSA_WIDTH:, :]))
    ms = jnp.mean(x1 * x1, axis=-1, keepdims=True)
    h2 = (x1 * lax.rsqrt(ms + EPS) * fg_ref[...]).astype(MXU_DTYPE)
    o_ref[...] = x1
    for c0 in range(0, D_FF, FF_CHUNK):
        gate = jnp.dot(h2, wgu_ref[:, c0:c0 + FF_CHUNK], preferred_element_type=F32)
        up = jnp.dot(h2, wgu_ref[:, D_FF + c0:D_FF + c0 + FF_CHUNK], preferred_element_type=F32)
        o_ref[...] += _dot(_silu(gate) * up, wd_ref[c0:c0 + FF_CHUNK, :])


def _out_ffn(x2, o_gdn, o_nsa, proj, conv_w, w_out, fgain, wgu, wd, seq):
    m = x2.shape[0]
    tm = TM_FFN
    hb = tm // SUBLANES
    full = lambda a, b: pl.BlockSpec((a, b), lambda i: (0, 0), pipeline_mode=pl.Buffered(1))
    return pl.pallas_call(
        functools.partial(_out_ffn_kernel, tm=tm, seq=seq),
        grid=(m // tm,),
        in_specs=[pl.BlockSpec((tm, D_MODEL), lambda i: (i, 0)),
                  pl.BlockSpec((tm, GDN_WIDTH), lambda i: (i, 0)),
                  pl.BlockSpec((tm, NSA_WIDTH), lambda i: (i, 0)),
                  pl.BlockSpec((tm, C_W), lambda i: (i, C_OFF // C_W)),
                  pl.BlockSpec((SUBLANES, C_W), lambda i: (jnp.maximum(i * hb - 1, 0), C_OFF // C_W)),
                  full(CONV_K, CONV_WIDTH), full(D_MODEL, D_MODEL), full(1, D_MODEL),
                  full(D_MODEL, 2 * D_FF), full(D_FF, D_MODEL)],
        out_specs=pl.BlockSpec((tm, D_MODEL), lambda i: (i, 0)),
        out_shape=jax.ShapeDtypeStruct((m, D_MODEL), F32),
        compiler_params=pltpu.CompilerParams(dimension_semantics=("parallel",),
                                             vmem_limit_bytes=VMEM_LIMIT),
        name="out_ffn",
    )(x2, o_gdn, o_nsa, proj, proj, conv_w, w_out, fgain, wgu, wd)


def _proj_column_map():
    offs = np.concatenate([[0], np.cumsum(IN_SIZES)])
    seg = lambda k: np.arange(offs[k], offs[k + 1])
    pad = lambda n: -np.ones(n, np.int64)
    nq = seg(6).reshape(NSA_HEADS, HEAD)[list(Q_SLOT_HEADS)].reshape(-1)
    cols = np.concatenate([
        seg(0), seg(1), seg(2), seg(3), seg(4), seg(5), pad(LANES - 2 * GDN_HEADS),
        nq, seg(9), seg(11), seg(10), seg(12), seg(13), pad(LANES - 3 * NSA_HEADS),
        seg(14), seg(15), seg(16),
        seg(7), seg(8)])
    assert cols.shape[0] == PROJ_W
    return cols


def _block_diag_ones(n):
    idx = np.arange(n) // HEAD
    return jnp.asarray(idx[:, None] == idx[None, :], MXU_DTYPE)


def _rope_lane_tables(positions):
    inv = jnp.float32(ROPE_THETA) ** (-jnp.arange(0, ROT_DIM, 2, dtype=jnp.float32) / ROT_DIM)
    ang = positions.astype(jnp.float32).reshape(-1)[:, None] * inv
    cos, sin = jnp.cos(ang), jnp.sin(ang)
    m = ang.shape[0]
    half = ROT_DIM // 2
    one = jnp.ones((m, HEAD - ROT_DIM), F32)
    zero_h = jnp.zeros((m, half), F32)
    zero_r = jnp.zeros((m, HEAD - ROT_DIM), F32)
    c = jnp.concatenate([cos, cos, one], axis=1)
    s1 = jnp.concatenate([-sin, zero_h, zero_r], axis=1)
    s2 = jnp.concatenate([zero_h, sin, zero_r], axis=1)
    tile2 = lambda t: jnp.concatenate([t, t], axis=1)
    return tile2(c), tile2(s1), tile2(s2)


def _selection_matrix(ncp, nslp):
    ratio = SLC_BLOCK // CMP_STRIDE
    frac = np.minimum(CMP_LEN, SLC_BLOCK - CMP_STRIDE * np.arange(ratio)).astype(np.float64) / CMP_LEN
    mat = np.zeros((ncp, nslp), np.float32)
    c = np.arange(ncp)
    mat[c, c // ratio] = frac[c % ratio]
    nxt = c // ratio + 1
    ok = nxt < nslp
    mat[c[ok], nxt[ok]] += (1.0 - frac[c % ratio])[ok]
    return jnp.asarray(mat, MXU_DTYPE)


def kernel(x, positions, attn_norm, w_in, gdn_conv_w, gdn_a_log, gdn_dt_bias, gdn_norm, nsa_q_norm,
           nsa_k_norm, nsa_cmp_pe, nsa_cmp_w1, nsa_cmp_w2, conv_w, w_out, ffn_norm, w_gate_up, w_down):
    batch, seq, _ = x.shape
    depth = w_in.shape[0]
    m = batch * seq
    assert seq % max(TM_PROJ, TM_FFN, T_PREP, KC) == 0 and (seq // CMP_STRIDE) % LANES == 0
    nb = seq // CMP_STRIDE
    nslp = -(-(seq // SLC_BLOCK) // LANES) * LANES

    cols = _proj_column_map()
    take = jnp.asarray(np.maximum(cols, 0), jnp.int32)
    valid = jnp.asarray(cols >= 0)
    o_rows = np.arange(NSA_WIDTH).reshape(NSA_HEADS, HEAD)[list(Q_SLOT_HEADS)].reshape(-1) + GDN_WIDTH
    wo_rows = jnp.asarray(np.concatenate([np.arange(GDN_WIDTH), o_rows,
                                          np.arange(GDN_WIDTH + NSA_WIDTH, D_MODEL)]), jnp.int32)
    bd128, bd256, bd512 = _block_diag_ones(128), _block_diag_ones(256), _block_diag_ones(512)
    rc, rs1, rs2 = _rope_lane_tables(positions)
    msel = _selection_matrix(nb, nslp)
    lane_pad = lambda v: jnp.zeros((1, LANES), F32).at[0, GDN_HEADS:2 * GDN_HEADS].set(v.astype(F32))

    x2 = x.reshape(m, D_MODEL)
    for l in range(depth):
        w_l = jnp.where(valid[None, :], jnp.take(w_in[l], take, axis=1), 0.0).astype(MXU_DTYPE)
        proj = _in_proj(x2, attn_norm[l].reshape(1, D_MODEL), w_l)

        o_gdn = _gdn(proj, gdn_conv_w[l], lane_pad(gdn_a_log[l]), lane_pad(gdn_dt_bias[l]),
                     gdn_norm[l].reshape(1, HEAD), bd256, batch, seq)

        qg = jnp.tile(nsa_q_norm[l], NSA_HEADS).reshape(1, NSA_WIDTH)
        kg = jnp.tile(nsa_k_norm[l, 1:3], (1, NSA_KV_HEADS))
        qn, qr, ksa, kw, vs, vw, gt = _nsa_prep(proj, rc, rs1, rs2, qg, kg, bd512, bd128, seq)

        xc = proj[:, P_OFF:P_OFF + P_W].reshape(batch, nb, CMP_STRIDE, 2, NSA_KV_HEADS, HEAD)
        xc = xc.transpose(3, 0, 4, 1, 2, 5).reshape(2, batch, NSA_KV_HEADS, nb, CMP_STRIDE * HEAD)
        w1 = nsa_cmp_w1[l]
        half = CMP_STRIDE * HEAD
        w1c = jnp.concatenate([w1[:, :half], w1[:, half:]], axis=2).astype(MXU_DTYPE)
        kvc = _compress(xc, w1c, nsa_cmp_pe[l].reshape(2, 1, CMP_LEN * HEAD), w1.astype(MXU_DTYPE),
                        nsa_cmp_w2[l].astype(MXU_DTYPE), nsa_k_norm[l, 0].reshape(1, HEAD), batch, nb)

        bound = (HEAD ** 0.5) * jnp.max(jnp.abs(nsa_q_norm[l])) * jnp.max(jnp.abs(nsa_k_norm[l]), axis=1)
        cshift = jnp.zeros((1, LANES), F32).at[0, 0:3].set(bound.astype(F32))
        attn_args = (qn, qr, gt, kvc, ksa, vs, kw, vw, msel, cshift)
        o_nsa = lax.cond(
            jnp.max(bound) <= MAX_SOFTMAX_SHIFT,
            lambda a: _nsa_attn(*a, batch, seq, True),
            lambda a: _nsa_attn(*a, batch, seq, False),
            attn_args)

        x2 = _out_ffn(x2, o_gdn, o_nsa, proj, conv_w[l], jnp.take(w_out[l], wo_rows, axis=0).astype(MXU_DTYPE),
                      ffn_norm[l].reshape(1, D_MODEL), w_gate_up[l].astype(MXU_DTYPE),
                      w_down[l].astype(MXU_DTYPE), seq)
    return x2.reshape(batch, seq, D_MODEL)
```

```python
import functools

import numpy as np
import jax
import jax.numpy as jnp
from jax import lax
from jax.experimental import pallas as pl
from jax.experimental.pallas import tpu as pltpu

F32 = jnp.float32
MXU_DTYPE = jnp.bfloat16

D_MODEL = 1024
HEAD = 64
EPS = 1e-6
NEG_INF = -1e30
GDN_HEADS = 4
GDN_WIDTH = 256
GDN_CONV = 4
GDN_CHUNK = 64
NSA_HEADS = 8
NSA_WIDTH = 512
NSA_KV_HEADS = 2
NSA_GROUP = 4
NSA_KV_WIDTH = 128
CMP_STRIDE = 16
CMP_LEN = 32
SLC_BLOCK = 64
N_SELECT = 16
WINDOW = 512
CONV_WIDTH = 256
CONV_K = 3
ROPE_THETA = 500000.0
ROT_DIM = 16
D_FF = 2816
IN_SIZES = (256, 256, 256, 256, 4, 4, 512, 128, 128, 128, 128, 128, 128, 24, 256, 256, 256)
D_IN = sum(IN_SIZES)

LANES = 128
SUBLANES = 8
VMEM_LIMIT = 56 * 1024 * 1024

G_W = 1024 + LANES
N_W = 512 + 4 * 128 + LANES
C_W = 3 * CONV_WIDTH
P_W = 2 * NSA_KV_WIDTH
G_OFF, N_OFF, C_OFF, P_OFF = 0, G_W, G_W + N_W, G_W + N_W + C_W
PROJ_W = G_W + N_W + C_W + P_W
Q_SLOT_HEADS = (0, 4, 1, 5, 2, 6, 3, 7)

SEL_BIG = 16384.0
MAX_SOFTMAX_SHIFT = 40.0

TM_PROJ = 512
TM_FFN = 512
T_GDN = 512
T_PREP = 512
TQ = 128
KC = 1024
FF_CHUNK = 256


def _dot(a, b):
    return jnp.dot(a.astype(MXU_DTYPE), b.astype(MXU_DTYPE), preferred_element_type=F32)


def _dot_nt(a, b):
    return lax.dot_general(a.astype(MXU_DTYPE), b.astype(MXU_DTYPE), (((1,), (1,)), ((), ())),
                           preferred_element_type=F32)


def _dot_tn(a, b):
    return lax.dot_general(a.astype(MXU_DTYPE), b.astype(MXU_DTYPE), (((0,), (0,)), ((), ())),
                           preferred_element_type=F32)


def _split3(x):
    a = x.astype(MXU_DTYPE)
    r = x - a.astype(F32)
    b = r.astype(MXU_DTYPE)
    c = (r - b.astype(F32)).astype(MXU_DTYPE)
    return a, b, c


def _dot_hi_l(x, m):
    a, b, c = _split3(x)
    f = lambda t: jnp.dot(t, m, preferred_element_type=F32)
    return f(a) + f(b) + f(c)


def _dot_hi(x, y):
    xa, xb, xc = _split3(x)
    ya, yb, yc = _split3(y)
    f = lambda s, t: jnp.dot(s, t, preferred_element_type=F32)
    return f(xa, ya) + (f(xa, yb) + f(xb, ya)) + (f(xa, yc) + f(xb, yb) + f(xc, ya))


def _silu(x):
    return x * jax.nn.sigmoid(x)


def _shift_rows(x, halo, s, rows):
    y = pltpu.roll(x, s, 0)
    for r in range(s):
        y = jnp.where(rows == r, halo[SUBLANES - s + r:SUBLANES - s + r + 1, :], y)
    return y


def _in_proj_kernel(x_ref, g_ref, w_ref, o_ref):
    x = x_ref[...]
    ms = jnp.mean(x * x, axis=-1, keepdims=True)
    h = (x * lax.rsqrt(ms + EPS) * g_ref[...]).astype(MXU_DTYPE)
    for c in range(0, PROJ_W, 256):
        o_ref[:, c:c + 256] = jnp.dot(h, w_ref[:, c:c + 256], preferred_element_type=F32)


def _in_proj(x2, gain, w):
    m = x2.shape[0]
    tm = TM_PROJ
    return pl.pallas_call(
        _in_proj_kernel,
        grid=(m // tm,),
        in_specs=[pl.BlockSpec((tm, D_MODEL), lambda i: (i, 0)),
                  pl.BlockSpec((1, D_MODEL), lambda i: (0, 0)),
                  pl.BlockSpec((D_MODEL, PROJ_W), lambda i: (0, 0))],
        out_specs=pl.BlockSpec((tm, PROJ_W), lambda i: (i, 0)),
        out_shape=jax.ShapeDtypeStruct((m, PROJ_W), F32),
        compiler_params=pltpu.CompilerParams(dimension_semantics=("parallel",),
                                             vmem_limit_bytes=VMEM_LIMIT),
        name="in_proj",
    )(x2, gain, w)


def _gdn_kernel(g_ref, gh_ref, cw_ref, alog_ref, dt_ref, gn_ref, bd_ref, eb_ref, eg_ref, o_ref, s_ref, *, tt):
    t_idx = pl.program_id(1)

    @pl.when(t_idx == 0)
    def _():
        s_ref[...] = jnp.zeros_like(s_ref)

    keep = jnp.where(t_idx == 0, 0.0, 1.0)
    rows = lax.broadcasted_iota(jnp.int32, (tt, 1), 0)
    x = g_ref[:, 0:3 * GDN_WIDTH]
    hx = gh_ref[:, 0:3 * GDN_WIDTH] * keep
    w = cw_ref[...]
    y = w[GDN_CONV - 1:GDN_CONV, :] * x
    for s in range(1, GDN_CONV):
        y = y + w[GDN_CONV - 1 - s:GDN_CONV - s, :] * _shift_rows(x, hx, s, rows)
    y = _silu(y)
    q = y[:, 0:GDN_WIDTH]
    k = y[:, GDN_WIDTH:2 * GDN_WIDTH]
    v = y[:, 2 * GDN_WIDTH:3 * GDN_WIDTH]
    bd = bd_ref[...]
    q = q * lax.rsqrt(_dot_hi_l(q * q, bd) + EPS) * (HEAD ** -0.5)
    k = k * lax.rsqrt(_dot_hi_l(k * k, bd) + EPS)

    gg = g_ref[:, 4 * GDN_WIDTH:4 * GDN_WIDTH + LANES]
    lane = lax.broadcasted_iota(jnp.int32, (1, LANES), 1)
    xa = gg + dt_ref[...]
    softplus = jnp.maximum(xa, 0.0) + jnp.log1p(jnp.exp(-jnp.abs(xa)))
    g2 = jnp.where(lane < GDN_HEADS, jax.nn.sigmoid(gg), -jnp.exp(alog_ref[...]) * softplus)
    cs = g2.T
    lane_t = lax.broadcasted_iota(jnp.int32, (1, tt), 1) % GDN_CHUNK
    step = 1
    while step < GDN_CHUNK:
        cs = cs + jnp.where(lane_t >= step, pltpu.roll(cs, step, 1), 0.0)
        step *= 2
    gcum_t = cs
    gcum = cs.T
    beta_e = _dot_hi_l(g2, eb_ref[...])
    gcum_e = _dot_hi_l(gcum, eg_ref[...])
    eg_e = jnp.exp(gcum_e)
    kb = k * beta_e
    rv = v * beta_e
    rk = kb * eg_e
    qd = q * eg_e

    nst = GDN_HEADS * GDN_CHUNK
    ri = lax.broadcasted_iota(jnp.int32, (nst, nst), 0)
    ci = lax.broadcasted_iota(jnp.int32, (nst, nst), 1)
    same = (ri // GDN_CHUNK) == (ci // GDN_CHUNK)
    m_tril = same & ((ri % GDN_CHUNK) >= (ci % GDN_CHUNK))
    m_strict = same & ((ri % GDN_CHUNK) > (ci % GDN_CHUNK))
    tile4 = lambda t: jnp.concatenate([t] * GDN_HEADS, axis=0)
    expand = lambda t: jnp.where(same, tile4(t), 0.0)

    nchunk = tt // GDN_CHUNK
    rss = [slice(n * GDN_CHUNK, (n + 1) * GDN_CHUNK) for n in range(nchunk)]
    g_last = [gcum_e[(n + 1) * GDN_CHUNK - 1:(n + 1) * GDN_CHUNK, :] for n in range(nchunk)]
    rmat, pw, qk = [], [], []
    for n, rs in enumerate(rss):
        g_row = jnp.concatenate([gcum_t[GDN_HEADS + h:GDN_HEADS + h + 1, rs] for h in range(GDN_HEADS)], axis=1)
        decay = jnp.where(m_tril, jnp.exp(jnp.where(m_tril, tile4(gcum_e[rs]) - g_row, 0.0)), 0.0)
        k4 = tile4(k[rs])
        a = jnp.where(m_strict, _dot_nt(expand(kb[rs]), k4) * decay, 0.0)
        qk.append(jnp.where(m_tril, _dot_nt(expand(q[rs]), k4) * decay, 0.0))
        rmat.append(-a)
        pw.append(-a)
    for _ in range(5):
        pw = [_dot(t, t) for t in pw]
        rmat = [r + t + _dot(r, t) for r, t in zip(rmat, pw)]
    u, wm = [], []
    for n, rs in enumerate(rss):
        rv_x = expand(rv[rs])
        rk_x = expand(rk[rs])
        u.append(rv_x + _dot(rmat[n], rv_x))
        wm.append(rk_x + _dot(rmat[n], rk_x))

    st = s_ref[...]
    outs = []
    for n, rs in enumerate(rss):
        kd = k[rs] * jnp.exp(g_last[n] - gcum_e[rs])
        v_new = u[n] - _dot(wm[n], st)
        o_x = _dot(expand(qd[rs]), st) + _dot(qk[n], v_new)
        st = st * jnp.exp(g_last[n]) + _dot_tn(expand(kd), v_new)
        o = o_x[0:GDN_CHUNK]
        for h in range(1, GDN_HEADS):
            o = o + o_x[h * GDN_CHUNK:(h + 1) * GDN_CHUNK]
        outs.append(o)
    s_ref[...] = st
    o_all = jnp.concatenate(outs, axis=0)
    on = o_all * lax.rsqrt(_dot_hi_l(o_all * o_all, bd) * (1.0 / HEAD) + EPS) * gn_ref[...]
    o_ref[...] = (on * _silu(g_ref[:, 3 * GDN_WIDTH:4 * GDN_WIDTH])).astype(o_ref.dtype)


def _gdn(proj, conv_w, alog_pad, dt_pad, gnorm, bd256, eb, eg, batch, seq):
    tt = T_GDN
    nt = seq // tt
    hb = tt // SUBLANES
    return pl.pallas_call(
        functools.partial(_gdn_kernel, tt=tt),
        grid=(batch, nt),
        in_specs=[pl.BlockSpec((tt, G_W), lambda b, t: (b * nt + t, 0)),
                  pl.BlockSpec((SUBLANES, G_W), lambda b, t: (jnp.maximum((b * nt + t) * hb - 1, 0), 0)),
                  pl.BlockSpec((GDN_CONV, 3 * GDN_WIDTH), lambda b, t: (0, 0)),
                  pl.BlockSpec((1, LANES), lambda b, t: (0, 0)),
                  pl.BlockSpec((1, LANES), lambda b, t: (0, 0)),
                  pl.BlockSpec((1, GDN_WIDTH), lambda b, t: (0, 0)),
                  pl.BlockSpec((GDN_WIDTH, GDN_WIDTH), lambda b, t: (0, 0)),
                  pl.BlockSpec((LANES, GDN_WIDTH), lambda b, t: (0, 0)),
                  pl.BlockSpec((LANES, GDN_WIDTH), lambda b, t: (0, 0))],
        out_specs=pl.BlockSpec((tt, GDN_WIDTH), lambda b, t: (b * nt + t, 0)),
        out_shape=jax.ShapeDtypeStruct((batch * seq, GDN_WIDTH), MXU_DTYPE),
        scratch_shapes=[pltpu.VMEM((GDN_WIDTH, GDN_WIDTH), F32)],
        compiler_params=pltpu.CompilerParams(dimension_semantics=("parallel", "arbitrary"),
                                             vmem_limit_bytes=VMEM_LIMIT),
        name="gdn",
    )(proj, proj, conv_w, alog_pad, dt_pad, gnorm, bd256, eb, eg)


def _rope(x, c, s1, s2):
    wdt = x.shape[1]
    return x * c + pltpu.roll(x, wdt - ROT_DIM // 2, 1) * s1 + pltpu.roll(x, ROT_DIM // 2, 1) * s2


def _nsa_prep_kernel(n_ref, c_ref, s1_ref, s2_ref, qg_ref, kg_ref, bdq_ref, bdk_ref,
                     qn_ref, qr_ref, ksa_ref, kw_ref, vs_ref, vw_ref, gt_ref, *, tt, seq):
    c1, s1, s2 = c_ref[...], s1_ref[...], s2_ref[...]
    c4 = jnp.concatenate([c1] * 4, axis=1)
    s14 = jnp.concatenate([s1] * 4, axis=1)
    s24 = jnp.concatenate([s2] * 4, axis=1)
    q = n_ref[:, 0:NSA_WIDTH]
    qn = q * lax.rsqrt(_dot_hi_l(q * q, bdq_ref[...]) * (1.0 / HEAD) + EPS) * qg_ref[...]
    qn = qn * (HEAD ** -0.5)
    qn_ref[...] = qn.astype(qn_ref.dtype)
    qr_ref[...] = _rope(qn, c4, s14, s24).astype(qr_ref.dtype)
    bdk = bdk_ref[...]
    ks = n_ref[:, 512:640]
    kw = n_ref[:, 640:768]
    ks = ks * lax.rsqrt(_dot_hi_l(ks * ks, bdk) * (1.0 / HEAD) + EPS) * kg_ref[0:1, :]
    kw = kw * lax.rsqrt(_dot_hi_l(kw * kw, bdk) * (1.0 / HEAD) + EPS) * kg_ref[1:2, :]
    ksa_ref[:, 0:LANES] = _rope(ks, c1, s1, s2).astype(ksa_ref.dtype)
    rows = (lax.broadcasted_iota(jnp.int32, (tt, LANES), 0) + pl.program_id(0) * tt) % seq
    lane = lax.broadcasted_iota(jnp.int32, (tt, LANES), 1)
    ksa_ref[:, LANES:2 * LANES] = jnp.where((rows // SLC_BLOCK) % LANES == lane, 1.0, 0.0).astype(ksa_ref.dtype)
    kw_ref[...] = _rope(kw, c1, s1, s2).astype(kw_ref.dtype)
    vs_ref[...] = n_ref[:, 768:896].astype(vs_ref.dtype)
    vw_ref[...] = n_ref[:, 896:1024].astype(vw_ref.dtype)
    gt_ref[...] = jax.nn.sigmoid(n_ref[:, 1024:1152])


def _nsa_prep(proj, rc, rs1, rs2, qg, kg, bd512, bd128, seq):
    m = proj.shape[0]
    tt = T_PREP
    row = lambda w: pl.BlockSpec((tt, w), lambda i: (i, 0))
    full = lambda a, b: pl.BlockSpec((a, b), lambda i: (0, 0))
    sds = lambda w, dt: jax.ShapeDtypeStruct((m, w), dt)
    return pl.pallas_call(
        functools.partial(_nsa_prep_kernel, tt=tt, seq=seq),
        grid=(m // tt,),
        in_specs=[pl.BlockSpec((tt, N_W), lambda i: (i, 1)), row(LANES), row(LANES), row(LANES),
                  full(1, NSA_WIDTH), full(2, LANES), full(NSA_WIDTH, NSA_WIDTH), full(LANES, LANES)],
        out_specs=[row(NSA_WIDTH), row(NSA_WIDTH), row(2 * LANES), row(LANES), row(LANES), row(LANES),
                   row(LANES)],
        out_shape=[sds(NSA_WIDTH, MXU_DTYPE), sds(NSA_WIDTH, MXU_DTYPE), sds(2 * LANES, MXU_DTYPE),
                   sds(LANES, MXU_DTYPE), sds(LANES, MXU_DTYPE), sds(LANES, MXU_DTYPE), sds(LANES, F32)],
        compiler_params=pltpu.CompilerParams(dimension_semantics=("parallel",),
                                             vmem_limit_bytes=VMEM_LIMIT),
        name="nsa_prep",
    )(proj, rc, rs1, rs2, qg, kg, bd512, bd128)


def _compress_kernel(x_ref, w1c_ref, pe_ref, w1_ref, w2_ref, kg_ref, o_ref, *, nb):
    is_k = pl.program_id(0) == 0
    w1c = w1c_ref[0]
    pe_term = _dot(jnp.broadcast_to(pe_ref[0], (SUBLANES, CMP_LEN * HEAD)), w1_ref[0])[0:1, :]
    rows = lax.broadcasted_iota(jnp.int32, (nb, 1), 0)
    outs = []
    for g in range(NSA_KV_HEADS):
        p = _dot(x_ref[0, 0, g], w1c)
        pre = p[:, 0:HEAD] + pltpu.roll(p[:, HEAD:2 * HEAD], nb - 1, 0) + pe_term
        y = _dot(jax.nn.gelu(pre), w2_ref[0])
        yk = y * lax.rsqrt(jnp.mean(y * y, axis=-1, keepdims=True) + EPS) * kg_ref[...]
        y = jnp.where(is_k, yk, y)
        outs.append(jnp.where(rows < nb - 1, y, 0.0))
    o_ref[0, 0] = jnp.concatenate(outs, axis=1).astype(o_ref.dtype)


def _compress(xc, w1c, pe, w1, w2, kg0, batch, nb):
    blk = CMP_STRIDE * HEAD
    return pl.pallas_call(
        functools.partial(_compress_kernel, nb=nb),
        grid=(2, batch),
        in_specs=[pl.BlockSpec((1, 1, NSA_KV_HEADS, nb, blk), lambda c, b: (c, b, 0, 0, 0)),
                  pl.BlockSpec((1, blk, 2 * HEAD), lambda c, b: (c, 0, 0)),
                  pl.BlockSpec((1, 1, CMP_LEN * HEAD), lambda c, b: (c, 0, 0)),
                  pl.BlockSpec((1, CMP_LEN * HEAD, HEAD), lambda c, b: (c, 0, 0)),
                  pl.BlockSpec((1, HEAD, HEAD), lambda c, b: (c, 0, 0)),
                  pl.BlockSpec((1, HEAD), lambda c, b: (0, 0))],
        out_specs=pl.BlockSpec((1, 1, nb, 2 * HEAD), lambda c, b: (c, b, 0, 0)),
        out_shape=jax.ShapeDtypeStruct((2, batch, nb, 2 * HEAD), MXU_DTYPE),
        compiler_params=pltpu.CompilerParams(dimension_semantics=("parallel", "parallel"),
                                             vmem_limit_bytes=VMEM_LIMIT),
        name="compress",
    )(xc, w1c, pe, w1, w2, kg0)


def _nsa_attn_kernel(*refs, tq, seq, nwb, kc, ncp, nslp, shifted):
    qn_ref, qr_ref, gt_ref, kcmp_ref, vcmp_ref, ksa_ref, vs_ref = refs[:7]
    kw_refs = refs[7:7 + nwb]
    vw_refs = refs[7 + nwb:7 + 2 * nwb]
    msel_ref, cs_ref = refs[7 + 2 * nwb:9 + 2 * nwb]
    o_ref = refs[9 + 2 * nwb]
    bias_scr, qaug_scr, m_scr, l_scr, acc_scr, ow_scr = refs[10 + 2 * nwb:]
    nslot = NSA_HEADS
    i = pl.program_id(1)
    c_cmp, c_slc, c_win = cs_ref[0:1, 0:1], cs_ref[0:1, 1:2], cs_ref[0:1, 2:3]
    lane = lax.broadcasted_iota(jnp.int32, (1, LANES), 1)
    lo = lane < HEAD

    def stack(q_ref):
        sl = [q_ref[:, j * LANES:(j + 1) * LANES] for j in range(NSA_GROUP)]
        zero = jnp.zeros_like(sl[0])
        return jnp.concatenate([jnp.where(lo, s, zero) for s in sl] + [jnp.where(lo, zero, s) for s in sl], axis=0)

    tpos = i * tq + lax.broadcasted_iota(jnp.int32, (tq, 1), 0)
    tpos_st = i * tq + lax.broadcasted_iota(jnp.int32, (nslot * tq, 1), 0) % tq
    gates = gt_ref[...]

    qn_st = stack(qn_ref)
    ckey = lax.broadcasted_iota(jnp.int32, (1, ncp), 1)
    cmask = (ckey * CMP_STRIDE + (CMP_LEN - 1)) <= tpos
    s_all = _dot_nt(qn_st, kcmp_ref[0, 0])
    vcm = vcmp_ref[0, 0]
    o_c = []
    psum = [None, None]
    for slot in range(nslot):
        s = s_all[slot * tq:(slot + 1) * tq]
        if shifted:
            e = jnp.where(cmask, jnp.exp(s - c_cmp), 0.0)
        else:
            s = jnp.where(cmask, s, NEG_INF)
            e = jnp.where(cmask, jnp.exp(s - jnp.max(s, axis=-1, keepdims=True)), 0.0)
        l = jnp.sum(e, axis=-1, keepdims=True)
        p = e * (1.0 / jnp.where(l > 0.0, l, 1.0))
        o_c.append(_dot(p, vcm))
        g = slot // NSA_GROUP
        psum[g] = p if psum[g] is None else psum[g] + p

    qr_st = stack(qr_ref)
    kwc = jnp.concatenate([r[...] for r in kw_refs], axis=0)
    vwc = jnp.concatenate([r[...] for r in vw_refs], axis=0)
    sw = _dot_nt(qr_st, kwc)
    if shifted:
        ew_all = jnp.exp(sw - c_win)
        rr = lax.broadcasted_iota(jnp.int32, (tq, tq), 0)
        cc = lax.broadcasted_iota(jnp.int32, (tq, tq), 1)
        parts = []
        for jb in range(nwb):
            blk = ew_all[:, jb * tq:(jb + 1) * tq]
            if jb == 0 or jb == nwb - 1:
                vis = (cc > rr) if jb == 0 else (cc <= rr)
                blk = jnp.concatenate([jnp.where(vis, blk[s * tq:(s + 1) * tq], 0.0) for s in range(nslot)],
                                      axis=0)
            if jb < nwb - 1:
                blk = blk * jnp.where(i - (nwb - 1) + jb >= 0, 1.0, 0.0)
            parts.append(blk)
        ew = jnp.concatenate(parts, axis=1)
    else:
        kpos_w = (i - (nwb - 1)) * tq + lax.broadcasted_iota(jnp.int32, (1, nwb * tq), 1)
        dist = tpos_st - kpos_w
        wmask = (dist >= 0) & (dist < WINDOW) & (kpos_w >= 0)
        sw = jnp.where(wmask, sw, NEG_INF)
        ew = jnp.exp(sw - jnp.max(sw, axis=-1, keepdims=True))
    ow_scr[...] = _dot(ew, vwc) * (1.0 / jnp.sum(ew, axis=-1, keepdims=True))

    jrow = lax.broadcasted_iota(jnp.int32, (nslp, 1), 0)
    jrowf = jrow.astype(F32)
    cur_t = (i * tq + lax.broadcasted_iota(jnp.int32, (1, tq), 1)) // SLC_BLOCK
    forced = (jrow == 0) | (jrow == cur_t) | (jrow == cur_t - 1)
    causal = jrow <= cur_t
    msel = msel_ref[...]
    shift_s = c_slc if shifted else 0.0
    for g in range(NSA_KV_HEADS):
        slc = _dot_hi_l(psum[g], msel).T
        cand = jnp.where(causal & jnp.logical_not(forced), slc, -1.0)
        sel = jnp.where(forced, 1.0, 0.0)
        for _ in range(N_SELECT - 3):
            mx = jnp.max(cand, axis=0, keepdims=True)
            first = jnp.min(jnp.where(cand == mx, jrowf, float(nslp)), axis=0, keepdims=True)
            hit = jrowf == first
            sel = jnp.where(hit, 1.0, sel)
            cand = jnp.where(hit, -2.0, cand)
        sel = jnp.where(causal, sel, 0.0)
        bias_scr[g] = ((sel - 1.0) * SEL_BIG - shift_s).T.astype(bias_scr.dtype)

    qaug_scr[:, 0:LANES] = qr_st
    if not shifted:
        m_scr[...] = jnp.full(m_scr.shape, NEG_INF, F32)
    l_scr[...] = jnp.zeros(l_scr.shape, F32)
    acc_scr[...] = jnp.zeros(acc_scr.shape, F32)
    blocks_per_chunk = kc // SLC_BLOCK
    chunks_per_group = LANES // blocks_per_chunk

    def chunk_step(c, diag):
        @pl.when(c % chunks_per_group == 0)
        def _():
            off = pl.multiple_of((c // chunks_per_group) * LANES, LANES)
            b0 = bias_scr[0, :, pl.ds(off, LANES)]
            b1 = bias_scr[1, :, pl.ds(off, LANES)]
            qaug_scr[:, LANES:2 * LANES] = jnp.concatenate([b0] * NSA_GROUP + [b1] * NSA_GROUP, axis=0)

        k0 = pl.multiple_of(c * kc, kc)
        s = _dot_nt(qaug_scr[...], ksa_ref[pl.ds(k0, kc), :])
        if diag:
            kpos = c * kc + lax.broadcasted_iota(jnp.int32, (1, kc), 1)
            s = jnp.where(kpos <= tpos_st, s, -SEL_BIG)
        if shifted:
            p = jnp.exp(s)
            psum_l = p[:, 0:LANES]
            for t in range(1, kc // LANES):
                psum_l = psum_l + p[:, t * LANES:(t + 1) * LANES]
            l_scr[...] += psum_l
            acc_scr[...] += _dot(p, vs_ref[pl.ds(k0, kc), :])
        else:
            m_old = m_scr[...]
            m_new = jnp.maximum(m_old, jnp.max(s, axis=-1, keepdims=True))
            alpha = jnp.exp(m_old - m_new)
            p = jnp.exp(s - m_new[:, 0:1])
            l_scr[...] = alpha * l_scr[...] + jnp.sum(p, axis=-1, keepdims=True)
            acc_scr[...] = alpha * acc_scr[...] + _dot(p, vs_ref[pl.ds(k0, kc), :])
            m_scr[...] = m_new

    n_full = (i * tq) // kc

    def body(c, carry):
        chunk_step(c, False)
        return carry

    lax.fori_loop(0, n_full, body, 0)
    chunk_step(n_full, True)
    if shifted:
        o_s = acc_scr[...] * (1.0 / jnp.sum(l_scr[...], axis=-1, keepdims=True))
    else:
        o_s = acc_scr[...] / l_scr[...]
    o_w = ow_scr[...]

    for j in range(NSA_GROUP):
        halves = []
        for g in range(NSA_KV_HEADS):
            slot = g * NSA_GROUP + j
            hh = g * NSA_GROUP + j
            rs = slice(slot * tq, (slot + 1) * tq)
            gc = gates[:, 3 * hh:3 * hh + 1]
            gs = gates[:, 3 * hh + 1:3 * hh + 2]
            gw = gates[:, 3 * hh + 2:3 * hh + 3]
            halves.append(gc * o_c[slot] + gs * o_s[rs] + gw * o_w[rs])
        o_ref[:, j * LANES:(j + 1) * LANES] = jnp.where(lo, halves[0], halves[1]).astype(o_ref.dtype)


def _nsa_attn(qn, qr, gt, kvc, ksa, vs, kw, vw, msel, cshift, batch, seq, shifted):
    tq, kc = TQ, KC
    nq = seq // tq
    nwb = WINDOW // tq + 1
    ncp = seq // CMP_STRIDE
    nslp = msel.shape[1]
    row = lambda w: pl.BlockSpec((tq, w), lambda b, i: (b * nq + i, 0))
    win = lambda jb: pl.BlockSpec((tq, LANES), lambda b, i: (b * nq + jnp.maximum(i - (nwb - 1) + jb, 0), 0))
    in_specs = ([row(NSA_WIDTH), row(NSA_WIDTH), row(LANES),
                 pl.BlockSpec((1, 1, ncp, LANES), lambda b, i: (0, b, 0, 0)),
                 pl.BlockSpec((1, 1, ncp, LANES), lambda b, i: (1, b, 0, 0)),
                 pl.BlockSpec((seq, 2 * LANES), lambda b, i: (b, 0)),
                 pl.BlockSpec((seq, LANES), lambda b, i: (b, 0))]
                + [win(jb) for jb in range(nwb)] + [win(jb) for jb in range(nwb)]
                + [pl.BlockSpec((ncp, nslp), lambda b, i: (0, 0)),
                   pl.BlockSpec((1, LANES), lambda b, i: (0, 0))])
    return pl.pallas_call(
        functools.partial(_nsa_attn_kernel, tq=tq, seq=seq, nwb=nwb, kc=kc, ncp=ncp, nslp=nslp,
                          shifted=shifted),
        grid=(batch, nq),
        in_specs=in_specs,
        out_specs=row(NSA_WIDTH),
        out_shape=jax.ShapeDtypeStruct((batch * seq, NSA_WIDTH), MXU_DTYPE),
        scratch_shapes=[pltpu.VMEM((NSA_KV_HEADS, tq, nslp), MXU_DTYPE),
                        pltpu.VMEM((NSA_HEADS * tq, 2 * LANES), MXU_DTYPE),
                        pltpu.VMEM((NSA_HEADS * tq, LANES), F32),
                        pltpu.VMEM((NSA_HEADS * tq, LANES), F32),
                        pltpu.VMEM((NSA_HEADS * tq, LANES), F32),
                        pltpu.VMEM((NSA_HEADS * tq, LANES), F32)],
        compiler_params=pltpu.CompilerParams(dimension_semantics=("parallel", "arbitrary"),
                                             vmem_limit_bytes=VMEM_LIMIT),
        name="nsa_attn" if shifted else "nsa_attn_runmax",
    )(qn, qr, gt, kvc, kvc, ksa, vs, *([kw] * nwb), *([vw] * nwb), msel, cshift)


def _out_ffn_kernel(x_ref, og_ref, on_ref, c_ref, ch_ref, cw_ref, wo_ref, fg_ref, wgu_ref, wd_ref, o_ref,
                    *, tm, seq):
    i = pl.program_id(0)
    keep = jnp.where((i * tm) % seq == 0, 0.0, 1.0)
    rows = lax.broadcasted_iota(jnp.int32, (tm, 1), 0)
    u = c_ref[:, CONV_WIDTH:2 * CONV_WIDTH] * c_ref[:, 2 * CONV_WIDTH:3 * CONV_WIDTH]
    hu = ch_ref[:, CONV_WIDTH:2 * CONV_WIDTH] * ch_ref[:, 2 * CONV_WIDTH:3 * CONV_WIDTH] * keep
    w = cw_ref[...]
    conv = w[CONV_K - 1:CONV_K, :] * u
    for s in range(1, CONV_K):
        conv = conv + w[CONV_K - 1 - s:CONV_K - s, :] * _shift_rows(u, hu, s, rows)
    oc = c_ref[:, 0:CONV_WIDTH] * conv
    x1 = (x_ref[...] + _dot(og_ref[...], wo_ref[0:GDN_WIDTH, :])
          + _dot(on_ref[...], wo_ref[GDN_WIDTH:GDN_WIDTH + NSA_WIDTH, :])
          + _dot(oc, wo_ref[GDN_WIDTH + NSA_WIDTH:, :]))
    ms = jnp.mean(x1 * x1, axis=-1, keepdims=True)
    h2 = (x1 * lax.rsqrt(ms + EPS) * fg_ref[...]).astype(MXU_DTYPE)
    o_ref[...] = x1
    for c0 in range(0, D_FF, FF_CHUNK):
        gate = jnp.dot(h2, wgu_ref[:, c0:c0 + FF_CHUNK], preferred_element_type=F32)
        up = jnp.dot(h2, wgu_ref[:, D_FF + c0:D_FF + c0 + FF_CHUNK], preferred_element_type=F32)
        o_ref[...] += _dot(_silu(gate) * up, wd_ref[c0:c0 + FF_CHUNK, :])


def _out_ffn(x2, o_gdn, o_nsa, proj, conv_w, w_out, fgain, wgu, wd, seq):
    m = x2.shape[0]
    tm = TM_FFN
    hb = tm // SUBLANES
    full = lambda a, b: pl.BlockSpec((a, b), lambda i: (0, 0), pipeline_mode=pl.Buffered(1))
    return pl.pallas_call(
        functools.partial(_out_ffn_kernel, tm=tm, seq=seq),
        grid=(m // tm,),
        in_specs=[pl.BlockSpec((tm, D_MODEL), lambda i: (i, 0)),
                  pl.BlockSpec((tm, GDN_WIDTH), lambda i: (i, 0)),
                  pl.BlockSpec((tm, NSA_WIDTH), lambda i: (i, 0)),
                  pl.BlockSpec((tm, C_W), lambda i: (i, C_OFF // C_W)),
                  pl.BlockSpec((SUBLANES, C_W), lambda i: (jnp.maximum(i * hb - 1, 0), C_OFF // C_W)),
                  full(CONV_K, CONV_WIDTH), full(D_MODEL, D_MODEL), full(1, D_MODEL),
                  full(D_MODEL, 2 * D_FF), full(D_FF, D_MODEL)],
        out_specs=pl.BlockSpec((tm, D_MODEL), lambda i: (i, 0)),
        out_shape=jax.ShapeDtypeStruct((m, D_MODEL), F32),
        compiler_params=pltpu.CompilerParams(dimension_semantics=("parallel",),
                                             vmem_limit_bytes=VMEM_LIMIT),
        name="out_ffn",
    )(x2, o_gdn, o_nsa, proj, proj, conv_w, w_out, fgain, wgu, wd)


def _proj_column_map():
    offs = np.concatenate([[0], np.cumsum(IN_SIZES)])
    seg = lambda k: np.arange(offs[k], offs[k + 1])
    pad = lambda n: -np.ones(n, np.int64)
    nq = seg(6).reshape(NSA_HEADS, HEAD)[list(Q_SLOT_HEADS)].reshape(-1)
    cols = np.concatenate([
        seg(0), seg(1), seg(2), seg(3), seg(4), seg(5), pad(LANES - 2 * GDN_HEADS),
        nq, seg(9), seg(11), seg(10), seg(12), seg(13), pad(LANES - 3 * NSA_HEADS),
        seg(14), seg(15), seg(16),
        seg(7), seg(8)])
    assert cols.shape[0] == PROJ_W
    return cols


def _block_diag_ones(n):
    idx = np.arange(n) // HEAD
    return jnp.asarray(idx[:, None] == idx[None, :], MXU_DTYPE)


def _head_expander(first_lane):
    mat = np.zeros((LANES, GDN_WIDTH), np.float32)
    for h in range(GDN_HEADS):
        mat[first_lane + h, h * HEAD:(h + 1) * HEAD] = 1.0
    return jnp.asarray(mat, MXU_DTYPE)


def _rope_lane_tables(positions):
    inv = jnp.float32(ROPE_THETA) ** (-jnp.arange(0, ROT_DIM, 2, dtype=jnp.float32) / ROT_DIM)
    ang = positions.astype(jnp.float32).reshape(-1)[:, None] * inv
    cos, sin = jnp.cos(ang), jnp.sin(ang)
    m = ang.shape[0]
    half = ROT_DIM // 2
    one = jnp.ones((m, HEAD - ROT_DIM), F32)
    zero_h = jnp.zeros((m, half), F32)
    zero_r = jnp.zeros((m, HEAD - ROT_DIM), F32)
    c = jnp.concatenate([cos, cos, one], axis=1)
    s1 = jnp.concatenate([-sin, zero_h, zero_r], axis=1)
    s2 = jnp.concatenate([zero_h, sin, zero_r], axis=1)
    tile2 = lambda t: jnp.concatenate([t, t], axis=1)
    return tile2(c), tile2(s1), tile2(s2)


def _selection_matrix(ncp, nslp):
    ratio = SLC_BLOCK // CMP_STRIDE
    frac = np.minimum(CMP_LEN, SLC_BLOCK - CMP_STRIDE * np.arange(ratio)).astype(np.float64) / CMP_LEN
    mat = np.zeros((ncp, nslp), np.float32)
    c = np.arange(ncp)
    mat[c, c // ratio] = frac[c % ratio]
    nxt = c // ratio + 1
    ok = nxt < nslp
    mat[c[ok], nxt[ok]] += (1.0 - frac[c % ratio])[ok]
    return jnp.asarray(mat, MXU_DTYPE)


def kernel(x, positions, attn_norm, w_in, gdn_conv_w, gdn_a_log, gdn_dt_bias, gdn_norm, nsa_q_norm,
           nsa_k_norm, nsa_cmp_pe, nsa_cmp_w1, nsa_cmp_w2, conv_w, w_out, ffn_norm, w_gate_up, w_down):
    batch, seq, _ = x.shape
    depth = w_in.shape[0]
    m = batch * seq
    assert seq % max(TM_PROJ, TM_FFN, T_PREP, KC) == 0 and (seq // CMP_STRIDE) % LANES == 0
    nb = seq // CMP_STRIDE
    nslp = -(-(seq // SLC_BLOCK) // LANES) * LANES

    cols = _proj_column_map()
    take = jnp.asarray(np.maximum(cols, 0), jnp.int32)
    valid = jnp.asarray(cols >= 0)
    o_rows = np.arange(NSA_WIDTH).reshape(NSA_HEADS, HEAD)[list(Q_SLOT_HEADS)].reshape(-1) + GDN_WIDTH
    wo_rows = jnp.asarray(np.concatenate([np.arange(GDN_WIDTH), o_rows,
                                          np.arange(GDN_WIDTH + NSA_WIDTH, D_MODEL)]), jnp.int32)
    bd128, bd256, bd512 = _block_diag_ones(128), _block_diag_ones(256), _block_diag_ones(512)
    rc, rs1, rs2 = _rope_lane_tables(positions)
    msel = _selection_matrix(nb, nslp)
    lane_pad = lambda v: jnp.zeros((1, LANES), F32).at[0, GDN_HEADS:2 * GDN_HEADS].set(v.astype(F32))

    x2 = x.reshape(m, D_MODEL)
    for l in range(depth):
        w_l = jnp.where(valid[None, :], jnp.take(w_in[l], take, axis=1), 0.0).astype(MXU_DTYPE)
        proj = _in_proj(x2, attn_norm[l].reshape(1, D_MODEL), w_l)

        o_gdn = _gdn(proj, gdn_conv_w[l], lane_pad(gdn_a_log[l]), lane_pad(gdn_dt_bias[l]),
                     jnp.tile(gdn_norm[l], GDN_HEADS).reshape(1, GDN_WIDTH), bd256,
                     _head_expander(0), _head_expander(GDN_HEADS), batch, seq)

        qg = jnp.tile(nsa_q_norm[l], NSA_HEADS).reshape(1, NSA_WIDTH)
        kg = jnp.tile(nsa_k_norm[l, 1:3], (1, NSA_KV_HEADS))
        qn, qr, ksa, kw, vs, vw, gt = _nsa_prep(proj, rc, rs1, rs2, qg, kg, bd512, bd128, seq)

        xc = proj[:, P_OFF:P_OFF + P_W].reshape(batch, nb, CMP_STRIDE, 2, NSA_KV_HEADS, HEAD)
        xc = xc.transpose(3, 0, 4, 1, 2, 5).reshape(2, batch, NSA_KV_HEADS, nb, CMP_STRIDE * HEAD)
        w1 = nsa_cmp_w1[l]
        half = CMP_STRIDE * HEAD
        w1c = jnp.concatenate([w1[:, :half], w1[:, half:]], axis=2).astype(MXU_DTYPE)
        kvc = _compress(xc, w1c, nsa_cmp_pe[l].reshape(2, 1, CMP_LEN * HEAD), w1.astype(MXU_DTYPE),
                        nsa_cmp_w2[l].astype(MXU_DTYPE), nsa_k_norm[l, 0].reshape(1, HEAD), batch, nb)

        bound = (HEAD ** 0.5) * jnp.max(jnp.abs(nsa_q_norm[l])) * jnp.max(jnp.abs(nsa_k_norm[l]), axis=1)
        cshift = jnp.zeros((1, LANES), F32).at[0, 0:3].set(bound.astype(F32))
        attn_args = (qn, qr, gt, kvc, ksa, vs, kw, vw, msel, cshift)
        o_nsa = lax.cond(
            jnp.max(bound) <= MAX_SOFTMAX_SHIFT,
            lambda a: _nsa_attn(*a, batch, seq, True),
            lambda a: _nsa_attn(*a, batch, seq, False),
            attn_args)

        x2 = _out_ffn(x2, o_gdn, o_nsa, proj, conv_w[l], jnp.take(w_out[l], wo_rows, axis=0).astype(MXU_DTYPE),
                      ffn_norm[l].reshape(1, D_MODEL), w_gate_up[l].astype(MXU_DTYPE),
                      w_down[l].astype(MXU_DTYPE), seq)
    return x2.reshape(batch, seq, D_MODEL)
```

```python
import functools

import numpy as np
import jax
import jax.numpy as jnp
from jax import lax
from jax.experimental import pallas as pl
from jax.experimental.pallas import tpu as pltpu

F32 = jnp.float32
MXU_DTYPE = jnp.bfloat16

D_MODEL = 1024
HEAD = 64
EPS = 1e-6
NEG_INF = -1e30
GDN_HEADS = 4
GDN_WIDTH = 256
GDN_CONV = 4
GDN_CHUNK = 64
NSA_HEADS = 8
NSA_WIDTH = 512
NSA_KV_HEADS = 2
NSA_GROUP = 4
NSA_KV_WIDTH = 128
CMP_STRIDE = 16
CMP_LEN = 32
SLC_BLOCK = 64
N_SELECT = 16
WINDOW = 512
CONV_WIDTH = 256
CONV_K = 3
ROPE_THETA = 500000.0
ROT_DIM = 16
D_FF = 2816
IN_SIZES = (256, 256, 256, 256, 4, 4, 512, 128, 128, 128, 128, 128, 128, 24, 256, 256, 256)
D_IN = sum(IN_SIZES)

LANES = 128
SUBLANES = 8
VMEM_LIMIT = 56 * 1024 * 1024

G_W = 1024 + LANES
N_W = 512 + 4 * 128 + LANES
C_W = 3 * CONV_WIDTH
P_W = 2 * NSA_KV_WIDTH
G_OFF, N_OFF, C_OFF, P_OFF = 0, G_W, G_W + N_W, G_W + N_W + C_W
PROJ_W = G_W + N_W + C_W + P_W
Q_SLOT_HEADS = (0, 4, 1, 5, 2, 6, 3, 7)

SEL_BIG = 16384.0
MAX_SOFTMAX_SHIFT = 40.0

TM_PROJ = 512
TM_FFN = 512
T_GDN = 512
T_PREP = 512
TQ = 256
TQ_RUNMAX = 128
KC = 1024
FF_CHUNK = 256


def _dot(a, b):
    return jnp.dot(a.astype(MXU_DTYPE), b.astype(MXU_DTYPE), preferred_element_type=F32)


def _dot_nt(a, b):
    return lax.dot_general(a.astype(MXU_DTYPE), b.astype(MXU_DTYPE), (((1,), (1,)), ((), ())),
                           preferred_element_type=F32)


def _dot_tn(a, b):
    return lax.dot_general(a.astype(MXU_DTYPE), b.astype(MXU_DTYPE), (((0,), (0,)), ((), ())),
                           preferred_element_type=F32)


def _split3(x):
    a = x.astype(MXU_DTYPE)
    r = x - a.astype(F32)
    b = r.astype(MXU_DTYPE)
    c = (r - b.astype(F32)).astype(MXU_DTYPE)
    return a, b, c


def _dot_hi_l(x, m):
    a, b, c = _split3(x)
    f = lambda t: jnp.dot(t, m, preferred_element_type=F32)
    return f(a) + f(b) + f(c)


def _dot_hi2_l(x, m):
    a = x.astype(MXU_DTYPE)
    b = (x - a.astype(F32)).astype(MXU_DTYPE)
    return jnp.dot(a, m, preferred_element_type=F32) + jnp.dot(b, m, preferred_element_type=F32)


def _dot_hi(x, y):
    xa, xb, xc = _split3(x)
    ya, yb, yc = _split3(y)
    f = lambda s, t: jnp.dot(s, t, preferred_element_type=F32)
    return f(xa, ya) + (f(xa, yb) + f(xb, ya)) + (f(xa, yc) + f(xb, yb) + f(xc, ya))


def _silu(x):
    return x * jax.nn.sigmoid(x)


def _shift_rows(x, halo, s, rows):
    y = pltpu.roll(x, s, 0)
    for r in range(s):
        y = jnp.where(rows == r, halo[SUBLANES - s + r:SUBLANES - s + r + 1, :], y)
    return y


def _in_proj_kernel(x_ref, g_ref, w_ref, o_ref):
    x = x_ref[...]
    ms = jnp.mean(x * x, axis=-1, keepdims=True)
    h = (x * lax.rsqrt(ms + EPS) * g_ref[...]).astype(MXU_DTYPE)
    for c in range(0, PROJ_W, 256):
        o_ref[:, c:c + 256] = jnp.dot(h, w_ref[:, c:c + 256], preferred_element_type=F32)


def _in_proj(x2, gain, w):
    m = x2.shape[0]
    tm = TM_PROJ
    return pl.pallas_call(
        _in_proj_kernel,
        grid=(m // tm,),
        in_specs=[pl.BlockSpec((tm, D_MODEL), lambda i: (i, 0)),
                  pl.BlockSpec((1, D_MODEL), lambda i: (0, 0)),
                  pl.BlockSpec((D_MODEL, PROJ_W), lambda i: (0, 0))],
        out_specs=pl.BlockSpec((tm, PROJ_W), lambda i: (i, 0)),
        out_shape=jax.ShapeDtypeStruct((m, PROJ_W), F32),
        compiler_params=pltpu.CompilerParams(dimension_semantics=("parallel",),
                                             vmem_limit_bytes=VMEM_LIMIT),
        name="in_proj",
    )(x2, gain, w)


def _gdn_kernel(g_ref, gh_ref, cw_ref, alog_ref, dt_ref, gn_ref, bd_ref, eb_ref, eg_ref, o_ref, s_ref, *, tt):
    t_idx = pl.program_id(1)

    @pl.when(t_idx == 0)
    def _():
        s_ref[...] = jnp.zeros_like(s_ref)

    keep = jnp.where(t_idx == 0, 0.0, 1.0)
    rows = lax.broadcasted_iota(jnp.int32, (tt, 1), 0)
    x = g_ref[:, 0:3 * GDN_WIDTH]
    hx = gh_ref[:, 0:3 * GDN_WIDTH] * keep
    w = cw_ref[...]
    y = w[GDN_CONV - 1:GDN_CONV, :] * x
    for s in range(1, GDN_CONV):
        y = y + w[GDN_CONV - 1 - s:GDN_CONV - s, :] * _shift_rows(x, hx, s, rows)
    y = _silu(y)
    q = y[:, 0:GDN_WIDTH]
    k = y[:, GDN_WIDTH:2 * GDN_WIDTH]
    v = y[:, 2 * GDN_WIDTH:3 * GDN_WIDTH]
    bd = bd_ref[...]
    q = q * lax.rsqrt(_dot_hi2_l(q * q, bd) + EPS) * (HEAD ** -0.5)
    k = k * lax.rsqrt(_dot_hi2_l(k * k, bd) + EPS)

    gg = g_ref[:, 4 * GDN_WIDTH:4 * GDN_WIDTH + LANES]
    lane = lax.broadcasted_iota(jnp.int32, (1, LANES), 1)
    xa = gg + dt_ref[...]
    softplus = jnp.maximum(xa, 0.0) + jnp.log1p(jnp.exp(-jnp.abs(xa)))
    g2 = jnp.where(lane < GDN_HEADS, jax.nn.sigmoid(gg), -jnp.exp(alog_ref[...]) * softplus)
    cs = g2.T
    lane_t = lax.broadcasted_iota(jnp.int32, (1, tt), 1) % GDN_CHUNK
    step = 1
    while step < GDN_CHUNK:
        cs = cs + jnp.where(lane_t >= step, pltpu.roll(cs, step, 1), 0.0)
        step *= 2
    gcum_t = cs
    gcum = cs.T
    beta_e = _dot_hi_l(g2, eb_ref[...])
    gcum_e = _dot_hi_l(gcum, eg_ref[...])
    eg_e = jnp.exp(gcum_e)
    kb = k * beta_e
    rv = v * beta_e
    rk = kb * eg_e
    qd = q * eg_e

    nst = GDN_HEADS * GDN_CHUNK
    ri = lax.broadcasted_iota(jnp.int32, (nst, nst), 0)
    ci = lax.broadcasted_iota(jnp.int32, (nst, nst), 1)
    same = (ri // GDN_CHUNK) == (ci // GDN_CHUNK)
    m_tril = same & ((ri % GDN_CHUNK) >= (ci % GDN_CHUNK))
    m_strict = same & ((ri % GDN_CHUNK) > (ci % GDN_CHUNK))
    tile4 = lambda t: jnp.concatenate([t] * GDN_HEADS, axis=0)
    expand = lambda t: jnp.where(same, tile4(t), 0.0)

    nchunk = tt // GDN_CHUNK
    rss = [slice(n * GDN_CHUNK, (n + 1) * GDN_CHUNK) for n in range(nchunk)]
    g_last = [gcum_e[(n + 1) * GDN_CHUNK - 1:(n + 1) * GDN_CHUNK, :] for n in range(nchunk)]
    rmat, pw, qk = [], [], []
    for n, rs in enumerate(rss):
        g_row = jnp.concatenate([gcum_t[GDN_HEADS + h:GDN_HEADS + h + 1, rs] for h in range(GDN_HEADS)], axis=1)
        decay = jnp.where(m_tril, jnp.exp(jnp.where(m_tril, tile4(gcum_e[rs]) - g_row, 0.0)), 0.0)
        k4 = tile4(k[rs])
        a = jnp.where(m_strict, _dot_nt(expand(kb[rs]), k4) * decay, 0.0)
        qk.append(jnp.where(m_tril, _dot_nt(expand(q[rs]), k4) * decay, 0.0))
        rmat.append(-a)
        pw.append(-a)
    for _ in range(5):
        pw = [_dot(t, t) for t in pw]
        rmat = [r + t + _dot(r, t) for r, t in zip(rmat, pw)]
    u, wm = [], []
    for n, rs in enumerate(rss):
        rv_x = expand(rv[rs])
        rk_x = expand(rk[rs])
        u.append(rv_x + _dot(rmat[n], rv_x))
        wm.append(rk_x + _dot(rmat[n], rk_x))

    st = s_ref[...]
    outs = []
    for n, rs in enumerate(rss):
        kd = k[rs] * jnp.exp(g_last[n] - gcum_e[rs])
        v_new = u[n] - _dot(wm[n], st)
        o_x = _dot(expand(qd[rs]), st) + _dot(qk[n], v_new)
        st = st * jnp.exp(g_last[n]) + _dot_tn(expand(kd), v_new)
        o = o_x[0:GDN_CHUNK]
        for h in range(1, GDN_HEADS):
            o = o + o_x[h * GDN_CHUNK:(h + 1) * GDN_CHUNK]
        outs.append(o)
    s_ref[...] = st
    o_all = jnp.concatenate(outs, axis=0)
    on = o_all * lax.rsqrt(_dot_hi2_l(o_all * o_all, bd) * (1.0 / HEAD) + EPS) * gn_ref[...]
    o_ref[...] = (on * _silu(g_ref[:, 3 * GDN_WIDTH:4 * GDN_WIDTH])).astype(o_ref.dtype)


def _gdn(proj, conv_w, alog_pad, dt_pad, gnorm, bd256, eb, eg, batch, seq):
    tt = T_GDN
    nt = seq // tt
    hb = tt // SUBLANES
    return pl.pallas_call(
        functools.partial(_gdn_kernel, tt=tt),
        grid=(batch, nt),
        in_specs=[pl.BlockSpec((tt, G_W), lambda b, t: (b * nt + t, 0)),
                  pl.BlockSpec((SUBLANES, G_W), lambda b, t: (jnp.maximum((b * nt + t) * hb - 1, 0), 0)),
                  pl.BlockSpec((GDN_CONV, 3 * GDN_WIDTH), lambda b, t: (0, 0)),
                  pl.BlockSpec((1, LANES), lambda b, t: (0, 0)),
                  pl.BlockSpec((1, LANES), lambda b, t: (0, 0)),
                  pl.BlockSpec((1, GDN_WIDTH), lambda b, t: (0, 0)),
                  pl.BlockSpec((GDN_WIDTH, GDN_WIDTH), lambda b, t: (0, 0)),
                  pl.BlockSpec((LANES, GDN_WIDTH), lambda b, t: (0, 0)),
                  pl.BlockSpec((LANES, GDN_WIDTH), lambda b, t: (0, 0))],
        out_specs=pl.BlockSpec((tt, GDN_WIDTH), lambda b, t: (b * nt + t, 0)),
        out_shape=jax.ShapeDtypeStruct((batch * seq, GDN_WIDTH), MXU_DTYPE),
        scratch_shapes=[pltpu.VMEM((GDN_WIDTH, GDN_WIDTH), F32)],
        compiler_params=pltpu.CompilerParams(dimension_semantics=("parallel", "arbitrary"),
                                             vmem_limit_bytes=VMEM_LIMIT),
        name="gdn",
    )(proj, proj, conv_w, alog_pad, dt_pad, gnorm, bd256, eb, eg)


def _rope(x, c, s1, s2):
    wdt = x.shape[1]
    return x * c + pltpu.roll(x, wdt - ROT_DIM // 2, 1) * s1 + pltpu.roll(x, ROT_DIM // 2, 1) * s2


def _nsa_prep_kernel(n_ref, c_ref, s1_ref, s2_ref, qg_ref, kg_ref, bdq_ref, bdk_ref,
                     qn_ref, qr_ref, ksa_ref, kw_ref, vs_ref, vw_ref, gt_ref, *, tt, seq):
    c1, s1, s2 = c_ref[...], s1_ref[...], s2_ref[...]
    c4 = jnp.concatenate([c1] * 4, axis=1)
    s14 = jnp.concatenate([s1] * 4, axis=1)
    s24 = jnp.concatenate([s2] * 4, axis=1)
    q = n_ref[:, 0:NSA_WIDTH]
    qn = q * lax.rsqrt(_dot_hi2_l(q * q, bdq_ref[...]) * (1.0 / HEAD) + EPS) * qg_ref[...]
    qn = qn * (HEAD ** -0.5)
    qn_ref[...] = qn.astype(qn_ref.dtype)
    qr_ref[...] = _rope(qn, c4, s14, s24).astype(qr_ref.dtype)
    bdk = bdk_ref[...]
    ks = n_ref[:, 512:640]
    kw = n_ref[:, 640:768]
    ks = ks * lax.rsqrt(_dot_hi2_l(ks * ks, bdk) * (1.0 / HEAD) + EPS) * kg_ref[0:1, :]
    kw = kw * lax.rsqrt(_dot_hi2_l(kw * kw, bdk) * (1.0 / HEAD) + EPS) * kg_ref[1:2, :]
    ksa_ref[:, 0:LANES] = _rope(ks, c1, s1, s2).astype(ksa_ref.dtype)
    rows = (lax.broadcasted_iota(jnp.int32, (tt, LANES), 0) + pl.program_id(0) * tt) % seq
    lane = lax.broadcasted_iota(jnp.int32, (tt, LANES), 1)
    ksa_ref[:, LANES:2 * LANES] = jnp.where((rows // SLC_BLOCK) % LANES == lane, 1.0, 0.0).astype(ksa_ref.dtype)
    kw_ref[...] = _rope(kw, c1, s1, s2).astype(kw_ref.dtype)
    vs_ref[...] = n_ref[:, 768:896].astype(vs_ref.dtype)
    vw_ref[...] = n_ref[:, 896:1024].astype(vw_ref.dtype)
    gt_ref[...] = jax.nn.sigmoid(n_ref[:, 1024:1152])


def _nsa_prep(proj, rc, rs1, rs2, qg, kg, bd512, bd128, seq):
    m = proj.shape[0]
    tt = T_PREP
    row = lambda w: pl.BlockSpec((tt, w), lambda i: (i, 0))
    full = lambda a, b: pl.BlockSpec((a, b), lambda i: (0, 0))
    sds = lambda w, dt: jax.ShapeDtypeStruct((m, w), dt)
    return pl.pallas_call(
        functools.partial(_nsa_prep_kernel, tt=tt, seq=seq),
        grid=(m // tt,),
        in_specs=[pl.BlockSpec((tt, N_W), lambda i: (i, 1)), row(LANES), row(LANES), row(LANES),
                  full(1, NSA_WIDTH), full(2, LANES), full(NSA_WIDTH, NSA_WIDTH), full(LANES, LANES)],
        out_specs=[row(NSA_WIDTH), row(NSA_WIDTH), row(2 * LANES), row(LANES), row(LANES), row(LANES),
                   row(LANES)],
        out_shape=[sds(NSA_WIDTH, MXU_DTYPE), sds(NSA_WIDTH, MXU_DTYPE), sds(2 * LANES, MXU_DTYPE),
                   sds(LANES, MXU_DTYPE), sds(LANES, MXU_DTYPE), sds(LANES, MXU_DTYPE), sds(LANES, F32)],
        compiler_params=pltpu.CompilerParams(dimension_semantics=("parallel",),
                                             vmem_limit_bytes=VMEM_LIMIT),
        name="nsa_prep",
    )(proj, rc, rs1, rs2, qg, kg, bd512, bd128)


def _compress_kernel(xk_ref, xv_ref, w1bd_ref, pe_ref, w1_ref, w2bd_ref, kg_ref, bd_ref, o_ref, *, nb):
    acc = None
    for t in range(CMP_STRIDE):
        xt = jnp.concatenate([xk_ref[pl.ds(t, nb, stride=CMP_STRIDE), :],
                              xv_ref[pl.ds(t, nb, stride=CMP_STRIDE), :]], axis=1)
        part = _dot(xt, w1bd_ref[t])
        acc = part if acc is None else acc + part
    pe_terms = [_dot(jnp.broadcast_to(pe_ref[c:c + 1, :], (SUBLANES, CMP_LEN * HEAD)), w1_ref[c])[0:1, :]
                for c in range(2)]
    pe_all = jnp.concatenate([pe_terms[0]] * NSA_KV_HEADS + [pe_terms[1]] * NSA_KV_HEADS, axis=1)
    half = 2 * NSA_KV_WIDTH
    pre = acc[:, 0:half] + pltpu.roll(acc[:, half:2 * half], nb - 1, 0) + pe_all
    y = _dot(jax.nn.gelu(pre), w2bd_ref[...])
    rows = lax.broadcasted_iota(jnp.int32, (nb, 1), 0)
    y = jnp.where(rows < nb - 1, y, 0.0)
    yk = y[:, 0:NSA_KV_WIDTH]
    yk = yk * lax.rsqrt(_dot_hi2_l(yk * yk, bd_ref[...]) * (1.0 / HEAD) + EPS) * kg_ref[...]
    o_ref[0, 0] = yk.astype(o_ref.dtype)
    o_ref[1, 0] = y[:, NSA_KV_WIDTH:2 * NSA_KV_WIDTH].astype(o_ref.dtype)


def _compress(proj, w1bd, pe, w1, w2bd, kg0, bd128, batch, seq):
    nb = seq // CMP_STRIDE
    full = lambda *s: pl.BlockSpec(s, lambda b: (0,) * len(s))
    return pl.pallas_call(
        functools.partial(_compress_kernel, nb=nb),
        grid=(batch,),
        in_specs=[pl.BlockSpec((seq, NSA_KV_WIDTH), lambda b: (b, P_OFF // NSA_KV_WIDTH)),
                  pl.BlockSpec((seq, NSA_KV_WIDTH), lambda b: (b, P_OFF // NSA_KV_WIDTH + 1)),
                  full(CMP_STRIDE, P_W, 2 * P_W), full(2, CMP_LEN * HEAD), full(2, CMP_LEN * HEAD, HEAD),
                  full(P_W, P_W), full(1, NSA_KV_WIDTH), full(LANES, LANES)],
        out_specs=pl.BlockSpec((2, 1, nb, NSA_KV_WIDTH), lambda b: (0, b, 0, 0)),
        out_shape=jax.ShapeDtypeStruct((2, batch, nb, NSA_KV_WIDTH), MXU_DTYPE),
        compiler_params=pltpu.CompilerParams(dimension_semantics=("parallel",),
                                             vmem_limit_bytes=VMEM_LIMIT),
        name="compress",
    )(proj, proj, w1bd, pe, w1, w2bd, kg0, bd128)


def _nsa_attn_kernel(*refs, tq, seq, nwb, kc, ncp, nslp, shifted):
    qn_ref, qr_ref, gt_ref, kcmp_ref, vcmp_ref, ksa_ref, vs_ref = refs[:7]
    kw_refs = refs[7:7 + nwb]
    vw_refs = refs[7 + nwb:7 + 2 * nwb]
    msel_ref, cs_ref = refs[7 + 2 * nwb:9 + 2 * nwb]
    o_ref = refs[9 + 2 * nwb]
    bias_scr, qaug_scr, m_scr, l_scr, acc_scr, ow_scr = refs[10 + 2 * nwb:]
    nslot = NSA_HEADS
    i = pl.program_id(1)
    c_cmp, c_slc, c_win = cs_ref[0:1, 0:1], cs_ref[0:1, 1:2], cs_ref[0:1, 2:3]
    lane = lax.broadcasted_iota(jnp.int32, (1, LANES), 1)
    lo = lane < HEAD

    def stack(q_ref):
        sl = [q_ref[:, j * LANES:(j + 1) * LANES] for j in range(NSA_GROUP)]
        zero = jnp.zeros_like(sl[0])
        return jnp.concatenate([jnp.where(lo, s, zero) for s in sl] + [jnp.where(lo, zero, s) for s in sl], axis=0)

    tpos = i * tq + lax.broadcasted_iota(jnp.int32, (tq, 1), 0)
    tpos_st = i * tq + lax.broadcasted_iota(jnp.int32, (nslot * tq, 1), 0) % tq
    gates = gt_ref[...]

    qn_st = stack(qn_ref)
    ckey = lax.broadcasted_iota(jnp.int32, (1, ncp), 1)
    cmask = (ckey * CMP_STRIDE + (CMP_LEN - 1)) <= tpos
    s_all = _dot_nt(qn_st, kcmp_ref[0, 0])
    vcm = vcmp_ref[0, 0]
    o_c = []
    psum = [None, None]
    if shifted:
        cbias = jnp.where(cmask, -c_cmp, -SEL_BIG)
    for slot in range(nslot):
        s = s_all[slot * tq:(slot + 1) * tq]
        if shifted:
            e = jnp.exp(s + cbias)
        else:
            s = jnp.where(cmask, s, NEG_INF)
            e = jnp.where(cmask, jnp.exp(s - jnp.max(s, axis=-1, keepdims=True)), 0.0)
        l = jnp.sum(e, axis=-1, keepdims=True)
        rinv = 1.0 / jnp.where(l > 0.0, l, 1.0)
        o_c.append(_dot(e, vcm) * rinv)
        p = e * rinv
        g = slot // NSA_GROUP
        psum[g] = p if psum[g] is None else psum[g] + p

    qr_st = stack(qr_ref)
    kwc = jnp.concatenate([r[...] for r in kw_refs], axis=0)
    vwc = jnp.concatenate([r[...] for r in vw_refs], axis=0)
    sw = _dot_nt(qr_st, kwc)
    if shifted:
        ew_all = jnp.exp(sw - c_win)
        rr = lax.broadcasted_iota(jnp.int32, (tq, tq), 0)
        cc = lax.broadcasted_iota(jnp.int32, (tq, tq), 1)
        parts = []
        for jb in range(nwb):
            blk = ew_all[:, jb * tq:(jb + 1) * tq]
            if jb == 0 or jb == nwb - 1:
                vis = (cc > rr) if jb == 0 else (cc <= rr)
                blk = jnp.concatenate([jnp.where(vis, blk[s * tq:(s + 1) * tq], 0.0) for s in range(nslot)],
                                      axis=0)
            if jb < nwb - 1:
                blk = blk * jnp.where(i - (nwb - 1) + jb >= 0, 1.0, 0.0)
            parts.append(blk)
        ew = jnp.concatenate(parts, axis=1)
    else:
        kpos_w = (i - (nwb - 1)) * tq + lax.broadcasted_iota(jnp.int32, (1, nwb * tq), 1)
        dist = tpos_st - kpos_w
        wmask = (dist >= 0) & (dist < WINDOW) & (kpos_w >= 0)
        sw = jnp.where(wmask, sw, NEG_INF)
        ew = jnp.exp(sw - jnp.max(sw, axis=-1, keepdims=True))
    ow_scr[...] = _dot(ew, vwc) * (1.0 / jnp.sum(ew, axis=-1, keepdims=True))

    jrow = lax.broadcasted_iota(jnp.int32, (nslp, 1), 0)
    jrowf = jrow.astype(F32)
    cur_t = (i * tq + lax.broadcasted_iota(jnp.int32, (1, tq), 1)) // SLC_BLOCK
    forced = (jrow == 0) | (jrow == cur_t) | (jrow == cur_t - 1)
    causal = jrow <= cur_t
    msel = msel_ref[...]
    shift_s = c_slc if shifted else 0.0
    for g in range(NSA_KV_HEADS):
        slc = _dot_hi_l(psum[g], msel).T
        cand = jnp.where(causal & jnp.logical_not(forced), slc, -1.0)
        sel = jnp.where(forced, 1.0, 0.0)
        for _ in range(N_SELECT - 3):
            mx = jnp.max(cand, axis=0, keepdims=True)
            first = jnp.min(jnp.where(cand == mx, jrowf, float(nslp)), axis=0, keepdims=True)
            hit = jrowf == first
            sel = jnp.where(hit, 1.0, sel)
            cand = jnp.where(hit, -2.0, cand)
        sel = jnp.where(causal, sel, 0.0)
        bias_scr[g] = ((sel - 1.0) * SEL_BIG - shift_s).T.astype(bias_scr.dtype)

    qaug_scr[:, 0:LANES] = qr_st
    if not shifted:
        m_scr[...] = jnp.full(m_scr.shape, NEG_INF, F32)
    l_scr[...] = jnp.zeros(l_scr.shape, F32)
    acc_scr[...] = jnp.zeros(acc_scr.shape, F32)
    blocks_per_chunk = kc // SLC_BLOCK
    chunks_per_group = LANES // blocks_per_chunk

    def chunk_step(c, diag):
        @pl.when(c % chunks_per_group == 0)
        def _():
            off = pl.multiple_of((c // chunks_per_group) * LANES, LANES)
            b0 = bias_scr[0, :, pl.ds(off, LANES)]
            b1 = bias_scr[1, :, pl.ds(off, LANES)]
            qaug_scr[:, LANES:2 * LANES] = jnp.concatenate([b0] * NSA_GROUP + [b1] * NSA_GROUP, axis=0)

        k0 = pl.multiple_of(c * kc, kc)
        s = _dot_nt(qaug_scr[...], ksa_ref[pl.ds(k0, kc), :])
        if diag:
            kpos = c * kc + lax.broadcasted_iota(jnp.int32, (1, kc), 1)
            s = jnp.where(kpos <= tpos_st, s, -SEL_BIG)
        if shifted:
            p = jnp.exp(s)
            psum_l = p[:, 0:LANES]
            for t in range(1, kc // LANES):
                psum_l = psum_l + p[:, t * LANES:(t + 1) * LANES]
            l_scr[...] += psum_l
            acc_scr[...] += _dot(p, vs_ref[pl.ds(k0, kc), :])
        else:
            m_old = m_scr[...]
            m_new = jnp.maximum(m_old, jnp.max(s, axis=-1, keepdims=True))
            alpha = jnp.exp(m_old - m_new)
            p = jnp.exp(s - m_new[:, 0:1])
            l_scr[...] = alpha * l_scr[...] + jnp.sum(p, axis=-1, keepdims=True)
            acc_scr[...] = alpha * acc_scr[...] + _dot(p, vs_ref[pl.ds(k0, kc), :])
            m_scr[...] = m_new

    n_full = (i * tq) // kc

    def body(c, carry):
        chunk_step(c, False)
        return carry

    lax.fori_loop(0, n_full, body, 0)
    chunk_step(n_full, True)
    if shifted:
        o_s = acc_scr[...] * (1.0 / jnp.sum(l_scr[...], axis=-1, keepdims=True))
    else:
        o_s = acc_scr[...] / l_scr[...]
    o_w = ow_scr[...]

    for j in range(NSA_GROUP):
        halves = []
        for g in range(NSA_KV_HEADS):
            slot = g * NSA_GROUP + j
            hh = g * NSA_GROUP + j
            rs = slice(slot * tq, (slot + 1) * tq)
            gc = gates[:, 3 * hh:3 * hh + 1]
            gs = gates[:, 3 * hh + 1:3 * hh + 2]
            gw = gates[:, 3 * hh + 2:3 * hh + 3]
            halves.append(gc * o_c[slot] + gs * o_s[rs] + gw * o_w[rs])
        o_ref[:, j * LANES:(j + 1) * LANES] = jnp.where(lo, halves[0], halves[1]).astype(o_ref.dtype)


def _nsa_attn(qn, qr, gt, kvc, ksa, vs, kw, vw, msel, cshift, batch, seq, shifted):
    tq, kc = (TQ if shifted else TQ_RUNMAX), KC
    nq = seq // tq
    nwb = WINDOW // tq + 1
    ncp = seq // CMP_STRIDE
    nslp = msel.shape[1]
    row = lambda w: pl.BlockSpec((tq, w), lambda b, i: (b * nq + i, 0))
    win = lambda jb: pl.BlockSpec((tq, LANES), lambda b, i: (b * nq + jnp.maximum(i - (nwb - 1) + jb, 0), 0))
    in_specs = ([row(NSA_WIDTH), row(NSA_WIDTH), row(LANES),
                 pl.BlockSpec((1, 1, ncp, LANES), lambda b, i: (0, b, 0, 0)),
                 pl.BlockSpec((1, 1, ncp, LANES), lambda b, i: (1, b, 0, 0)),
                 pl.BlockSpec((seq, 2 * LANES), lambda b, i: (b, 0)),
                 pl.BlockSpec((seq, LANES), lambda b, i: (b, 0))]
                + [win(jb) for jb in range(nwb)] + [win(jb) for jb in range(nwb)]
                + [pl.BlockSpec((ncp, nslp), lambda b, i: (0, 0)),
                   pl.BlockSpec((1, LANES), lambda b, i: (0, 0))])
    return pl.pallas_call(
        functools.partial(_nsa_attn_kernel, tq=tq, seq=seq, nwb=nwb, kc=kc, ncp=ncp, nslp=nslp,
                          shifted=shifted),
        grid=(batch, nq),
        in_specs=in_specs,
        out_specs=row(NSA_WIDTH),
        out_shape=jax.ShapeDtypeStruct((batch * seq, NSA_WIDTH), MXU_DTYPE),
        scratch_shapes=[pltpu.VMEM((NSA_KV_HEADS, tq, nslp), MXU_DTYPE),
                        pltpu.VMEM((NSA_HEADS * tq, 2 * LANES), MXU_DTYPE),
                        pltpu.VMEM((NSA_HEADS * tq, LANES), F32),
                        pltpu.VMEM((NSA_HEADS * tq, LANES), F32),
                        pltpu.VMEM((NSA_HEADS * tq, LANES), F32),
                        pltpu.VMEM((NSA_HEADS * tq, LANES), F32)],
        compiler_params=pltpu.CompilerParams(dimension_semantics=("parallel", "arbitrary"),
                                             vmem_limit_bytes=VMEM_LIMIT),
        name="nsa_attn" if shifted else "nsa_attn_runmax",
    )(qn, qr, gt, kvc, kvc, ksa, vs, *([kw] * nwb), *([vw] * nwb), msel, cshift)


def _out_ffn_kernel(x_ref, og_ref, on_ref, c_ref, ch_ref, cw_ref, wo_ref, fg_ref, wgu_ref, wd_ref, o_ref,
                    *, tm, seq):
    i = pl.program_id(0)
    keep = jnp.where((i * tm) % seq == 0, 0.0, 1.0)
    rows = lax.broadcasted_iota(jnp.int32, (tm, 1), 0)
    u = c_ref[:, CONV_WIDTH:2 * CONV_WIDTH] * c_ref[:, 2 * CONV_WIDTH:3 * CONV_WIDTH]
    hu = ch_ref[:, CONV_WIDTH:2 * CONV_WIDTH] * ch_ref[:, 2 * CONV_WIDTH:3 * CONV_WIDTH] * keep
    w = cw_ref[...]
    conv = w[CONV_K - 1:CONV_K, :] * u
    for s in range(1, CONV_K):
        conv = conv + w[CONV_K - 1 - s:CONV_K - s, :] * _shift_rows(u, hu, s, rows)
    oc = c_ref[:, 0:CONV_WIDTH] * conv
    x1 = (x_ref[...] + _dot(og_ref[...], wo_ref[0:GDN_WIDTH, :])
          + _dot(on_ref[...], wo_ref[GDN_WIDTH:GDN_WIDTH + NSA_WIDTH, :])
          + _dot(oc, wo_ref[GDN_WIDTH + NSA_WIDTH:, :]))
    ms = jnp.mean(x1 * x1, axis=-1, keepdims=True)
    h2 = (x1 * lax.rsqrt(ms + EPS) * fg_ref[...]).astype(MXU_DTYPE)
    o_ref[...] = x1
    for c0 in range(0, D_FF, FF_CHUNK):
        gate = jnp.dot(h2, wgu_ref[:, c0:c0 + FF_CHUNK], preferred_element_type=F32)
        up = jnp.dot(h2, wgu_ref[:, D_FF + c0:D_FF + c0 + FF_CHUNK], preferred_element_type=F32)
        o_ref[...] += _dot(_silu(gate) * up, wd_ref[c0:c0 + FF_CHUNK, :])


def _out_ffn(x2, o_gdn, o_nsa, proj, conv_w, w_out, fgain, wgu, wd, seq):
    m = x2.shape[0]
    tm = TM_FFN
    hb = tm // SUBLANES
    full = lambda a, b: pl.BlockSpec((a, b), lambda i: (0, 0), pipeline_mode=pl.Buffered(1))
    return pl.pallas_call(
        functools.partial(_out_ffn_kernel, tm=tm, seq=seq),
        grid=(m // tm,),
        in_specs=[pl.BlockSpec((tm, D_MODEL), lambda i: (i, 0)),
                  pl.BlockSpec((tm, GDN_WIDTH), lambda i: (i, 0)),
                  pl.BlockSpec((tm, NSA_WIDTH), lambda i: (i, 0)),
                  pl.BlockSpec((tm, C_W), lambda i: (i, C_OFF // C_W)),
                  pl.BlockSpec((SUBLANES, C_W), lambda i: (jnp.maximum(i * hb - 1, 0), C_OFF // C_W)),
                  full(CONV_K, CONV_WIDTH), full(D_MODEL, D_MODEL), full(1, D_MODEL),
                  full(D_MODEL, 2 * D_FF), full(D_FF, D_MODEL)],
        out_specs=pl.BlockSpec((tm, D_MODEL), lambda i: (i, 0)),
        out_shape=jax.ShapeDtypeStruct((m, D_MODEL), F32),
        compiler_params=pltpu.CompilerParams(dimension_semantics=("parallel",),
                                             vmem_limit_bytes=VMEM_LIMIT),
        name="out_ffn",
    )(x2, o_gdn, o_nsa, proj, proj, conv_w, w_out, fgain, wgu, wd)


def _proj_column_map():
    offs = np.concatenate([[0], np.cumsum(IN_SIZES)])
    seg = lambda k: np.arange(offs[k], offs[k + 1])
    pad = lambda n: -np.ones(n, np.int64)
    nq = seg(6).reshape(NSA_HEADS, HEAD)[list(Q_SLOT_HEADS)].reshape(-1)
    cols = np.concatenate([
        seg(0), seg(1), seg(2), seg(3), seg(4), seg(5), pad(LANES - 2 * GDN_HEADS),
        nq, seg(9), seg(11), seg(10), seg(12), seg(13), pad(LANES - 3 * NSA_HEADS),
        seg(14), seg(15), seg(16),
        seg(7), seg(8)])
    assert cols.shape[0] == PROJ_W
    return cols


def _block_diag_ones(n):
    idx = np.arange(n) // HEAD
    return jnp.asarray(idx[:, None] == idx[None, :], MXU_DTYPE)


def _head_expander(first_lane):
    mat = np.zeros((LANES, GDN_WIDTH), np.float32)
    for h in range(GDN_HEADS):
        mat[first_lane + h, h * HEAD:(h + 1) * HEAD] = 1.0
    return jnp.asarray(mat, MXU_DTYPE)


def _compress_weights(w1, w2):
    nslab = 2 * NSA_KV_HEADS
    w1r = w1.reshape(2, 2, CMP_STRIDE, HEAD, HEAD)
    w1bd = jnp.zeros((CMP_STRIDE, nslab, HEAD, 2, nslab, HEAD), F32)
    w2bd = jnp.zeros((nslab, HEAD, nslab, HEAD), F32)
    for s in range(nslab):
        c = s // NSA_KV_HEADS
        w1bd = w1bd.at[:, s, :, :, s, :].set(w1r[c].transpose(1, 2, 0, 3))
        w2bd = w2bd.at[s, :, s, :].set(w2[c])
    return (w1bd.reshape(CMP_STRIDE, P_W, 2 * P_W).astype(MXU_DTYPE),
            w2bd.reshape(P_W, P_W).astype(MXU_DTYPE))


def _rope_lane_tables(positions):
    inv = jnp.float32(ROPE_THETA) ** (-jnp.arange(0, ROT_DIM, 2, dtype=jnp.float32) / ROT_DIM)
    ang = positions.astype(jnp.float32).reshape(-1)[:, None] * inv
    cos, sin = jnp.cos(ang), jnp.sin(ang)
    m = ang.shape[0]
    half = ROT_DIM // 2
    one = jnp.ones((m, HEAD - ROT_DIM), F32)
    zero_h = jnp.zeros((m, half), F32)
    zero_r = jnp.zeros((m, HEAD - ROT_DIM), F32)
    c = jnp.concatenate([cos, cos, one], axis=1)
    s1 = jnp.concatenate([-sin, zero_h, zero_r], axis=1)
    s2 = jnp.concatenate([zero_h, sin, zero_r], axis=1)
    tile2 = lambda t: jnp.concatenate([t, t], axis=1)
    return tile2(c), tile2(s1), tile2(s2)


def _selection_matrix(ncp, nslp):
    ratio = SLC_BLOCK // CMP_STRIDE
    frac = np.minimum(CMP_LEN, SLC_BLOCK - CMP_STRIDE * np.arange(ratio)).astype(np.float64) / CMP_LEN
    mat = np.zeros((ncp, nslp), np.float32)
    c = np.arange(ncp)
    mat[c, c // ratio] = frac[c % ratio]
    nxt = c // ratio + 1
    ok = nxt < nslp
    mat[c[ok], nxt[ok]] += (1.0 - frac[c % ratio])[ok]
    return jnp.asarray(mat, MXU_DTYPE)


def kernel(x, positions, attn_norm, w_in, gdn_conv_w, gdn_a_log, gdn_dt_bias, gdn_norm, nsa_q_norm,
           nsa_k_norm, nsa_cmp_pe, nsa_cmp_w1, nsa_cmp_w2, conv_w, w_out, ffn_norm, w_gate_up, w_down):
    batch, seq, _ = x.shape
    depth = w_in.shape[0]
    m = batch * seq
    assert seq % max(TM_PROJ, TM_FFN, T_PREP, KC) == 0 and (seq // CMP_STRIDE) % LANES == 0
    nb = seq // CMP_STRIDE
    nslp = -(-(seq // SLC_BLOCK) // LANES) * LANES

    cols = _proj_column_map()
    take = jnp.asarray(np.maximum(cols, 0), jnp.int32)
    valid = jnp.asarray(cols >= 0)
    o_rows = np.arange(NSA_WIDTH).reshape(NSA_HEADS, HEAD)[list(Q_SLOT_HEADS)].reshape(-1) + GDN_WIDTH
    wo_rows = jnp.asarray(np.concatenate([np.arange(GDN_WIDTH), o_rows,
                                          np.arange(GDN_WIDTH + NSA_WIDTH, D_MODEL)]), jnp.int32)
    bd128, bd256, bd512 = _block_diag_ones(128), _block_diag_ones(256), _block_diag_ones(512)
    rc, rs1, rs2 = _rope_lane_tables(positions)
    msel = _selection_matrix(nb, nslp)
    lane_pad = lambda v: jnp.zeros((1, LANES), F32).at[0, GDN_HEADS:2 * GDN_HEADS].set(v.astype(F32))

    x2 = x.reshape(m, D_MODEL)
    for l in range(depth):
        w_l = jnp.where(valid[None, :], jnp.take(w_in[l], take, axis=1), 0.0).astype(MXU_DTYPE)
        proj = _in_proj(x2, attn_norm[l].reshape(1, D_MODEL), w_l)

        o_gdn = _gdn(proj, gdn_conv_w[l], lane_pad(gdn_a_log[l]), lane_pad(gdn_dt_bias[l]),
                     jnp.tile(gdn_norm[l], GDN_HEADS).reshape(1, GDN_WIDTH), bd256,
                     _head_expander(0), _head_expander(GDN_HEADS), batch, seq)

        qg = jnp.tile(nsa_q_norm[l], NSA_HEADS).reshape(1, NSA_WIDTH)
        kg = jnp.tile(nsa_k_norm[l, 1:3], (1, NSA_KV_HEADS))
        qn, qr, ksa, kw, vs, vw, gt = _nsa_prep(proj, rc, rs1, rs2, qg, kg, bd512, bd128, seq)

        w1bd, w2bd = _compress_weights(nsa_cmp_w1[l], nsa_cmp_w2[l])
        kvc = _compress(proj, w1bd, nsa_cmp_pe[l].reshape(2, CMP_LEN * HEAD), nsa_cmp_w1[l].astype(MXU_DTYPE),
                        w2bd, jnp.tile(nsa_k_norm[l, 0], NSA_KV_HEADS).reshape(1, NSA_KV_WIDTH), bd128,
                        batch, seq)

        bound = (HEAD ** 0.5) * jnp.max(jnp.abs(nsa_q_norm[l])) * jnp.max(jnp.abs(nsa_k_norm[l]), axis=1)
        cshift = jnp.zeros((1, LANES), F32).at[0, 0:3].set(bound.astype(F32))
        attn_args = (qn, qr, gt, kvc, ksa, vs, kw, vw, msel, cshift)
        o_nsa = lax.cond(
            jnp.max(bound) <= MAX_SOFTMAX_SHIFT,
            lambda a: _nsa_attn(*a, batch, seq, True),
            lambda a: _nsa_attn(*a, batch, seq, False),
            attn_args)

        x2 = _out_ffn(x2, o_gdn, o_nsa, proj, conv_w[l], jnp.take(w_out[l], wo_rows, axis=0).astype(MXU_DTYPE),
                      ffn_norm[l].reshape(1, D_MODEL), w_gate_up[l].astype(MXU_DTYPE),
                      w_down[l].astype(MXU_DTYPE), seq)
    return x2.reshape(batch, seq, D_MODEL)
```

```python
import functools

import numpy as np
import jax
import jax.numpy as jnp
from jax import lax
from jax.experimental import pallas as pl
from jax.experimental.pallas import tpu as pltpu

F32 = jnp.float32
MXU_DTYPE = jnp.bfloat16

D_MODEL = 1024
HEAD = 64
EPS = 1e-6
NEG_INF = -1e30
GDN_HEADS = 4
GDN_WIDTH = 256
GDN_CONV = 4
GDN_CHUNK = 64
NSA_HEADS = 8
NSA_WIDTH = 512
NSA_KV_HEADS = 2
NSA_GROUP = 4
NSA_KV_WIDTH = 128
CMP_STRIDE = 16
CMP_LEN = 32
SLC_BLOCK = 64
N_SELECT = 16
WINDOW = 512
CONV_WIDTH = 256
CONV_K = 3
ROPE_THETA = 500000.0
ROT_DIM = 16
D_FF = 2816
IN_SIZES = (256, 256, 256, 256, 4, 4, 512, 128, 128, 128, 128, 128, 128, 24, 256, 256, 256)
D_IN = sum(IN_SIZES)

LANES = 128
SUBLANES = 8
VMEM_LIMIT = 56 * 1024 * 1024

G_W = 1024 + LANES
N_W = 512 + 4 * 128 + LANES
C_W = 3 * CONV_WIDTH
P_W = 2 * NSA_KV_WIDTH
G_OFF, N_OFF, C_OFF, P_OFF = 0, G_W, G_W + N_W, G_W + N_W + C_W
PROJ_W = G_W + N_W + C_W + P_W
Q_SLOT_HEADS = (0, 4, 1, 5, 2, 6, 3, 7)

SEL_BIG = 16384.0
MAX_SOFTMAX_SHIFT = 40.0

TM_PROJ = 512
TM_FFN = 512
T_GDN = 512
T_PREP = 512
TQ = 256
TQ_RUNMAX = 128
KC = 1024
FF_CHUNK = 256


def _dot(a, b):
    return jnp.dot(a.astype(MXU_DTYPE), b.astype(MXU_DTYPE), preferred_element_type=F32)


def _dot_nt(a, b):
    return lax.dot_general(a.astype(MXU_DTYPE), b.astype(MXU_DTYPE), (((1,), (1,)), ((), ())),
                           preferred_element_type=F32)


def _dot_tn(a, b):
    return lax.dot_general(a.astype(MXU_DTYPE), b.astype(MXU_DTYPE), (((0,), (0,)), ((), ())),
                           preferred_element_type=F32)


def _split3(x):
    a = x.astype(MXU_DTYPE)
    r = x - a.astype(F32)
    b = r.astype(MXU_DTYPE)
    c = (r - b.astype(F32)).astype(MXU_DTYPE)
    return a, b, c


def _dot_hi_l(x, m):
    a, b, c = _split3(x)
    f = lambda t: jnp.dot(t, m, preferred_element_type=F32)
    return f(a) + f(b) + f(c)


def _dot_hi2_l(x, m):
    a = x.astype(MXU_DTYPE)
    b = (x - a.astype(F32)).astype(MXU_DTYPE)
    return jnp.dot(a, m, preferred_element_type=F32) + jnp.dot(b, m, preferred_element_type=F32)


def _dot_hi(x, y):
    xa, xb, xc = _split3(x)
    ya, yb, yc = _split3(y)
    f = lambda s, t: jnp.dot(s, t, preferred_element_type=F32)
    return f(xa, ya) + (f(xa, yb) + f(xb, ya)) + (f(xa, yc) + f(xb, yb) + f(xc, ya))


def _silu(x):
    return x * jax.nn.sigmoid(x)


def _shift_rows(x, halo, s, rows):
    y = pltpu.roll(x, s, 0)
    for r in range(s):
        y = jnp.where(rows == r, halo[SUBLANES - s + r:SUBLANES - s + r + 1, :], y)
    return y


def _in_proj_kernel(x_ref, g_ref, w_ref, o_ref):
    x = x_ref[...]
    ms = jnp.mean(x * x, axis=-1, keepdims=True)
    h = (x * lax.rsqrt(ms + EPS) * g_ref[...]).astype(MXU_DTYPE)
    for c in range(0, PROJ_W, 256):
        o_ref[:, c:c + 256] = jnp.dot(h, w_ref[:, c:c + 256], preferred_element_type=F32)


def _in_proj(x2, gain, w):
    m = x2.shape[0]
    tm = TM_PROJ
    return pl.pallas_call(
        _in_proj_kernel,
        grid=(m // tm,),
        in_specs=[pl.BlockSpec((tm, D_MODEL), lambda i: (i, 0)),
                  pl.BlockSpec((1, D_MODEL), lambda i: (0, 0)),
                  pl.BlockSpec((D_MODEL, PROJ_W), lambda i: (0, 0))],
        out_specs=pl.BlockSpec((tm, PROJ_W), lambda i: (i, 0)),
        out_shape=jax.ShapeDtypeStruct((m, PROJ_W), F32),
        compiler_params=pltpu.CompilerParams(dimension_semantics=("parallel",),
                                             vmem_limit_bytes=VMEM_LIMIT),
        name="in_proj",
    )(x2, gain, w)


def _gdn_kernel(g_ref, gh_ref, cw_ref, alog_ref, dt_ref, gn_ref, bd_ref, eb_ref, eg_ref, o_ref, s_ref, *, tt):
    t_idx = pl.program_id(1)

    @pl.when(t_idx == 0)
    def _():
        s_ref[...] = jnp.zeros_like(s_ref)

    keep = jnp.where(t_idx == 0, 0.0, 1.0)
    rows = lax.broadcasted_iota(jnp.int32, (tt, 1), 0)
    x = g_ref[:, 0:3 * GDN_WIDTH]
    hx = gh_ref[:, 0:3 * GDN_WIDTH] * keep
    w = cw_ref[...]
    y = w[GDN_CONV - 1:GDN_CONV, :] * x
    for s in range(1, GDN_CONV):
        y = y + w[GDN_CONV - 1 - s:GDN_CONV - s, :] * _shift_rows(x, hx, s, rows)
    y = _silu(y)
    q = y[:, 0:GDN_WIDTH]
    k = y[:, GDN_WIDTH:2 * GDN_WIDTH]
    v = y[:, 2 * GDN_WIDTH:3 * GDN_WIDTH]
    bd = bd_ref[...]
    q = q * lax.rsqrt(_dot_hi2_l(q * q, bd) + EPS) * (HEAD ** -0.5)
    k = k * lax.rsqrt(_dot_hi2_l(k * k, bd) + EPS)

    gg = g_ref[:, 4 * GDN_WIDTH:4 * GDN_WIDTH + LANES]
    lane = lax.broadcasted_iota(jnp.int32, (1, LANES), 1)
    xa = gg + dt_ref[...]
    softplus = jnp.maximum(xa, 0.0) + jnp.log1p(jnp.exp(-jnp.abs(xa)))
    g2 = jnp.where(lane < GDN_HEADS, jax.nn.sigmoid(gg), -jnp.exp(alog_ref[...]) * softplus)
    cs = g2.T
    lane_t = lax.broadcasted_iota(jnp.int32, (1, tt), 1) % GDN_CHUNK
    step = 1
    while step < GDN_CHUNK:
        cs = cs + jnp.where(lane_t >= step, pltpu.roll(cs, step, 1), 0.0)
        step *= 2
    gcum_t = cs
    gcum = cs.T
    beta_e = _dot_hi_l(g2, eb_ref[...])
    gcum_e = _dot_hi_l(gcum, eg_ref[...])
    eg_e = jnp.exp(gcum_e)
    kb = k * beta_e
    rv = v * beta_e
    rk = kb * eg_e
    qd = q * eg_e

    nst = GDN_HEADS * GDN_CHUNK
    ri = lax.broadcasted_iota(jnp.int32, (nst, nst), 0)
    ci = lax.broadcasted_iota(jnp.int32, (nst, nst), 1)
    same = (ri // GDN_CHUNK) == (ci // GDN_CHUNK)
    m_tril = same & ((ri % GDN_CHUNK) >= (ci % GDN_CHUNK))
    m_strict = same & ((ri % GDN_CHUNK) > (ci % GDN_CHUNK))
    tile4 = lambda t: jnp.concatenate([t] * GDN_HEADS, axis=0)
    expand = lambda t: jnp.where(same, tile4(t), 0.0)

    nchunk = tt // GDN_CHUNK
    rss = [slice(n * GDN_CHUNK, (n + 1) * GDN_CHUNK) for n in range(nchunk)]
    g_last = [gcum_e[(n + 1) * GDN_CHUNK - 1:(n + 1) * GDN_CHUNK, :] for n in range(nchunk)]
    rmat, pw, qk = [], [], []
    for n, rs in enumerate(rss):
        g_row = jnp.concatenate([gcum_t[GDN_HEADS + h:GDN_HEADS + h + 1, rs] for h in range(GDN_HEADS)], axis=1)
        decay = jnp.where(m_tril, jnp.exp(jnp.where(m_tril, tile4(gcum_e[rs]) - g_row, 0.0)), 0.0)
        k4 = tile4(k[rs])
        a = jnp.where(m_strict, _dot_nt(expand(kb[rs]), k4) * decay, 0.0)
        qk.append(jnp.where(m_tril, _dot_nt(expand(q[rs]), k4) * decay, 0.0))
        rmat.append(-a)
        pw.append(-a)
    for _ in range(5):
        pw = [_dot(t, t) for t in pw]
        rmat = [r + t + _dot(r, t) for r, t in zip(rmat, pw)]
    u, wm = [], []
    for n, rs in enumerate(rss):
        rv_x = expand(rv[rs])
        rk_x = expand(rk[rs])
        u.append(rv_x + _dot(rmat[n], rv_x))
        wm.append(rk_x + _dot(rmat[n], rk_x))

    st = s_ref[...]
    outs = []
    for n, rs in enumerate(rss):
        kd = k[rs] * jnp.exp(g_last[n] - gcum_e[rs])
        v_new = u[n] - _dot(wm[n], st)
        o_x = _dot(expand(qd[rs]), st) + _dot(qk[n], v_new)
        st = st * jnp.exp(g_last[n]) + _dot_tn(expand(kd), v_new)
        o = o_x[0:GDN_CHUNK]
        for h in range(1, GDN_HEADS):
            o = o + o_x[h * GDN_CHUNK:(h + 1) * GDN_CHUNK]
        outs.append(o)
    s_ref[...] = st
    o_all = jnp.concatenate(outs, axis=0)
    on = o_all * lax.rsqrt(_dot_hi2_l(o_all * o_all, bd) * (1.0 / HEAD) + EPS) * gn_ref[...]
    o_ref[...] = (on * _silu(g_ref[:, 3 * GDN_WIDTH:4 * GDN_WIDTH])).astype(o_ref.dtype)


def _gdn(proj, conv_w, alog_pad, dt_pad, gnorm, bd256, eb, eg, batch, seq):
    tt = T_GDN
    nt = seq // tt
    hb = tt // SUBLANES
    return pl.pallas_call(
        functools.partial(_gdn_kernel, tt=tt),
        grid=(batch, nt),
        in_specs=[pl.BlockSpec((tt, G_W), lambda b, t: (b * nt + t, 0)),
                  pl.BlockSpec((SUBLANES, G_W), lambda b, t: (jnp.maximum((b * nt + t) * hb - 1, 0), 0)),
                  pl.BlockSpec((GDN_CONV, 3 * GDN_WIDTH), lambda b, t: (0, 0)),
                  pl.BlockSpec((1, LANES), lambda b, t: (0, 0)),
                  pl.BlockSpec((1, LANES), lambda b, t: (0, 0)),
                  pl.BlockSpec((1, GDN_WIDTH), lambda b, t: (0, 0)),
                  pl.BlockSpec((GDN_WIDTH, GDN_WIDTH), lambda b, t: (0, 0)),
                  pl.BlockSpec((LANES, GDN_WIDTH), lambda b, t: (0, 0)),
                  pl.BlockSpec((LANES, GDN_WIDTH), lambda b, t: (0, 0))],
        out_specs=pl.BlockSpec((tt, GDN_WIDTH), lambda b, t: (b * nt + t, 0)),
        out_shape=jax.ShapeDtypeStruct((batch * seq, GDN_WIDTH), MXU_DTYPE),
        scratch_shapes=[pltpu.VMEM((GDN_WIDTH, GDN_WIDTH), F32)],
        compiler_params=pltpu.CompilerParams(dimension_semantics=("parallel", "arbitrary"),
                                             vmem_limit_bytes=VMEM_LIMIT),
        name="gdn",
    )(proj, proj, conv_w, alog_pad, dt_pad, gnorm, bd256, eb, eg)


def _rope(x, c, s1, s2):
    wdt = x.shape[1]
    return x * c + pltpu.roll(x, wdt - ROT_DIM // 2, 1) * s1 + pltpu.roll(x, ROT_DIM // 2, 1) * s2


def _nsa_prep_kernel(n_ref, c_ref, s1_ref, s2_ref, qg_ref, kg_ref, bdq_ref, bdk_ref,
                     qn_ref, qr_ref, ksa_ref, kw_ref, vs_ref, vw_ref, gt_ref, *, tt, seq):
    c1, s1, s2 = c_ref[...], s1_ref[...], s2_ref[...]
    c4 = jnp.concatenate([c1] * 4, axis=1)
    s14 = jnp.concatenate([s1] * 4, axis=1)
    s24 = jnp.concatenate([s2] * 4, axis=1)
    q = n_ref[:, 0:NSA_WIDTH]
    qn = q * lax.rsqrt(_dot_hi2_l(q * q, bdq_ref[...]) * (1.0 / HEAD) + EPS) * qg_ref[...]
    qn = qn * (HEAD ** -0.5)
    qn_ref[...] = qn.astype(qn_ref.dtype)
    qr_ref[...] = _rope(qn, c4, s14, s24).astype(qr_ref.dtype)
    bdk = bdk_ref[...]
    ks = n_ref[:, 512:640]
    kw = n_ref[:, 640:768]
    ks = ks * lax.rsqrt(_dot_hi2_l(ks * ks, bdk) * (1.0 / HEAD) + EPS) * kg_ref[0:1, :]
    kw = kw * lax.rsqrt(_dot_hi2_l(kw * kw, bdk) * (1.0 / HEAD) + EPS) * kg_ref[1:2, :]
    ksa_ref[:, 0:LANES] = _rope(ks, c1, s1, s2).astype(ksa_ref.dtype)
    rows = (lax.broadcasted_iota(jnp.int32, (tt, LANES), 0) + pl.program_id(0) * tt) % seq
    lane = lax.broadcasted_iota(jnp.int32, (tt, LANES), 1)
    ksa_ref[:, LANES:2 * LANES] = jnp.where((rows // SLC_BLOCK) % LANES == lane, 1.0, 0.0).astype(ksa_ref.dtype)
    kw_ref[...] = _rope(kw, c1, s1, s2).astype(kw_ref.dtype)
    vs_ref[...] = n_ref[:, 768:896].astype(vs_ref.dtype)
    vw_ref[...] = n_ref[:, 896:1024].astype(vw_ref.dtype)
    gt_ref[...] = jax.nn.sigmoid(n_ref[:, 1024:1152])


def _nsa_prep(proj, rc, rs1, rs2, qg, kg, bd512, bd128, seq):
    m = proj.shape[0]
    tt = T_PREP
    row = lambda w: pl.BlockSpec((tt, w), lambda i: (i, 0))
    full = lambda a, b: pl.BlockSpec((a, b), lambda i: (0, 0))
    sds = lambda w, dt: jax.ShapeDtypeStruct((m, w), dt)
    return pl.pallas_call(
        functools.partial(_nsa_prep_kernel, tt=tt, seq=seq),
        grid=(m // tt,),
        in_specs=[pl.BlockSpec((tt, N_W), lambda i: (i, 1)), row(LANES), row(LANES), row(LANES),
                  full(1, NSA_WIDTH), full(2, LANES), full(NSA_WIDTH, NSA_WIDTH), full(LANES, LANES)],
        out_specs=[row(NSA_WIDTH), row(NSA_WIDTH), row(2 * LANES), row(LANES), row(LANES), row(LANES),
                   row(LANES)],
        out_shape=[sds(NSA_WIDTH, MXU_DTYPE), sds(NSA_WIDTH, MXU_DTYPE), sds(2 * LANES, MXU_DTYPE),
                   sds(LANES, MXU_DTYPE), sds(LANES, MXU_DTYPE), sds(LANES, MXU_DTYPE), sds(LANES, F32)],
        compiler_params=pltpu.CompilerParams(dimension_semantics=("parallel",),
                                             vmem_limit_bytes=VMEM_LIMIT),
        name="nsa_prep",
    )(proj, rc, rs1, rs2, qg, kg, bd512, bd128)


def _compress_kernel(xk_ref, xv_ref, w1bd_ref, pe_ref, w1_ref, w2bd_ref, kg_ref, bd_ref, o_ref, *, nb):
    acc = None
    for t in range(CMP_STRIDE):
        xt = jnp.concatenate([xk_ref[pl.ds(t, nb, stride=CMP_STRIDE), :],
                              xv_ref[pl.ds(t, nb, stride=CMP_STRIDE), :]], axis=1)
        part = _dot(xt, w1bd_ref[t])
        acc = part if acc is None else acc + part
    pe_terms = [_dot(jnp.broadcast_to(pe_ref[c:c + 1, :], (SUBLANES, CMP_LEN * HEAD)), w1_ref[c])[0:1, :]
                for c in range(2)]
    pe_all = jnp.concatenate([pe_terms[0]] * NSA_KV_HEADS + [pe_terms[1]] * NSA_KV_HEADS, axis=1)
    half = 2 * NSA_KV_WIDTH
    pre = acc[:, 0:half] + pltpu.roll(acc[:, half:2 * half], nb - 1, 0) + pe_all
    y = _dot(jax.nn.gelu(pre), w2bd_ref[...])
    rows = lax.broadcasted_iota(jnp.int32, (nb, 1), 0)
    y = jnp.where(rows < nb - 1, y, 0.0)
    yk = y[:, 0:NSA_KV_WIDTH]
    yk = yk * lax.rsqrt(_dot_hi2_l(yk * yk, bd_ref[...]) * (1.0 / HEAD) + EPS) * kg_ref[...]
    o_ref[0, 0] = yk.astype(o_ref.dtype)
    o_ref[1, 0] = y[:, NSA_KV_WIDTH:2 * NSA_KV_WIDTH].astype(o_ref.dtype)


def _compress(proj, w1bd, pe, w1, w2bd, kg0, bd128, batch, seq):
    nb = seq // CMP_STRIDE
    full = lambda *s: pl.BlockSpec(s, lambda b: (0,) * len(s))
    return pl.pallas_call(
        functools.partial(_compress_kernel, nb=nb),
        grid=(batch,),
        in_specs=[pl.BlockSpec((seq, NSA_KV_WIDTH), lambda b: (b, P_OFF // NSA_KV_WIDTH)),
                  pl.BlockSpec((seq, NSA_KV_WIDTH), lambda b: (b, P_OFF // NSA_KV_WIDTH + 1)),
                  full(CMP_STRIDE, P_W, 2 * P_W), full(2, CMP_LEN * HEAD), full(2, CMP_LEN * HEAD, HEAD),
                  full(P_W, P_W), full(1, NSA_KV_WIDTH), full(LANES, LANES)],
        out_specs=pl.BlockSpec((2, 1, nb, NSA_KV_WIDTH), lambda b: (0, b, 0, 0)),
        out_shape=jax.ShapeDtypeStruct((2, batch, nb, NSA_KV_WIDTH), MXU_DTYPE),
        compiler_params=pltpu.CompilerParams(dimension_semantics=("parallel",),
                                             vmem_limit_bytes=VMEM_LIMIT),
        name="compress",
    )(proj, proj, w1bd, pe, w1, w2bd, kg0, bd128)


def _nsa_attn_kernel(*refs, tq, seq, nwb, kc, ncp, nslp, shifted):
    qn_ref, qr_ref, gt_ref, kcmp_ref, vcmp_ref, ksa_ref, vs_ref = refs[:7]
    kw_refs = refs[7:7 + nwb]
    vw_refs = refs[7 + nwb:7 + 2 * nwb]
    msel_ref, cs_ref, egate_ref = refs[7 + 2 * nwb:10 + 2 * nwb]
    o_ref = refs[10 + 2 * nwb]
    bias_scr, qaug_scr, m_scr, l_scr, acc_scr, ow_scr, cand_scr = refs[11 + 2 * nwb:]
    nslot = NSA_HEADS
    i = pl.program_id(1)
    c_cmp, c_slc, c_win = cs_ref[0:1, 0:1], cs_ref[0:1, 1:2], cs_ref[0:1, 2:3]
    lane = lax.broadcasted_iota(jnp.int32, (1, LANES), 1)
    lo = lane < HEAD

    def stack(q_ref):
        sl = [q_ref[:, j * LANES:(j + 1) * LANES] for j in range(NSA_GROUP)]
        zero = jnp.zeros_like(sl[0])
        return jnp.concatenate([jnp.where(lo, s, zero) for s in sl] + [jnp.where(lo, zero, s) for s in sl], axis=0)

    tpos = i * tq + lax.broadcasted_iota(jnp.int32, (tq, 1), 0)
    tpos_st = i * tq + lax.broadcasted_iota(jnp.int32, (nslot * tq, 1), 0) % tq
    gates = gt_ref[...]

    qn_st = stack(qn_ref)
    ckey = lax.broadcasted_iota(jnp.int32, (1, ncp), 1)
    cmask = (ckey * CMP_STRIDE + (CMP_LEN - 1)) <= tpos
    s_all = _dot_nt(qn_st, kcmp_ref[0, 0])
    psum = [None, None]
    e_parts, rinv_parts = [], []
    if shifted:
        cbias = jnp.where(cmask, -c_cmp, -SEL_BIG)
    for slot in range(nslot):
        s = s_all[slot * tq:(slot + 1) * tq]
        if shifted:
            e = jnp.exp(s + cbias)
        else:
            s = jnp.where(cmask, s, NEG_INF)
            e = jnp.where(cmask, jnp.exp(s - jnp.max(s, axis=-1, keepdims=True)), 0.0)
        l = jnp.sum(e, axis=-1, keepdims=True)
        rinv = 1.0 / jnp.where(l > 0.0, l, 1.0)
        e_parts.append(e.astype(MXU_DTYPE))
        rinv_parts.append(rinv)
        p = e * rinv
        g = slot // NSA_GROUP
        psum[g] = p if psum[g] is None else psum[g] + p
    o_c_all = _dot(jnp.concatenate(e_parts, axis=0), vcmp_ref[0, 0]) * jnp.concatenate(rinv_parts, axis=0)

    qr_st = stack(qr_ref)
    kwc = jnp.concatenate([r[...] for r in kw_refs], axis=0)
    vwc = jnp.concatenate([r[...] for r in vw_refs], axis=0)
    sw = _dot_nt(qr_st, kwc)
    if shifted:
        ew_all = jnp.exp(sw - c_win)
        rr = lax.broadcasted_iota(jnp.int32, (tq, tq), 0)
        cc = lax.broadcasted_iota(jnp.int32, (tq, tq), 1)
        parts = []
        for jb in range(nwb):
            blk = ew_all[:, jb * tq:(jb + 1) * tq]
            if jb == 0 or jb == nwb - 1:
                vis = (cc > rr) if jb == 0 else (cc <= rr)
                blk = jnp.concatenate([jnp.where(vis, blk[s * tq:(s + 1) * tq], 0.0) for s in range(nslot)],
                                      axis=0)
            if jb < nwb - 1:
                blk = blk * jnp.where(i - (nwb - 1) + jb >= 0, 1.0, 0.0)
            parts.append(blk)
        ew = jnp.concatenate(parts, axis=1)
    else:
        kpos_w = (i - (nwb - 1)) * tq + lax.broadcasted_iota(jnp.int32, (1, nwb * tq), 1)
        dist = tpos_st - kpos_w
        wmask = (dist >= 0) & (dist < WINDOW) & (kpos_w >= 0)
        sw = jnp.where(wmask, sw, NEG_INF)
        ew = jnp.exp(sw - jnp.max(sw, axis=-1, keepdims=True))
    ow_scr[...] = _dot(ew, vwc) * (1.0 / jnp.sum(ew, axis=-1, keepdims=True))

    jrow = lax.broadcasted_iota(jnp.int32, (nslp, 1), 0)
    jrowf = jrow.astype(F32)
    cur_t = (i * tq + lax.broadcasted_iota(jnp.int32, (1, tq), 1)) // SLC_BLOCK
    forced = (jrow == 0) | (jrow == cur_t) | (jrow == cur_t - 1)
    causal = jrow <= cur_t
    msel = msel_ref[...]
    shift_s = c_slc if shifted else 0.0
    n_free = N_SELECT - 3
    to_bias = lambda sel: ((sel - 1.0) * SEL_BIG - shift_s).T.astype(bias_scr.dtype)
    n_bad = None
    for g in range(NSA_KV_HEADS):
        slc = _dot_hi_l(psum[g], msel).T
        cand = jnp.where(causal & jnp.logical_not(forced), slc, -1.0)
        cand_scr[g] = cand
        c = cand
        for _ in range(n_free):
            c = jnp.where(c == jnp.max(c, axis=0, keepdims=True), -2.0, c)
        picked = (c == -2.0) & (cand >= 0.0)
        n_picked = jnp.sum(jnp.where(picked, 1.0, 0.0), axis=0, keepdims=True)
        n_real = jnp.sum(jnp.where(cand >= 0.0, 1.0, 0.0), axis=0, keepdims=True)
        bad = jnp.where(n_picked == jnp.minimum(n_real, float(n_free)), 0.0, 1.0)
        n_bad = bad if n_bad is None else n_bad + bad
        bias_scr[g] = to_bias(jnp.where(forced | picked, 1.0, 0.0))

    @pl.when(jnp.max(n_bad) > 0.0)
    def _():
        for g in range(NSA_KV_HEADS):
            cand = cand_scr[g]
            sel = jnp.where(forced, 1.0, 0.0)
            for _ in range(n_free):
                mx = jnp.max(cand, axis=0, keepdims=True)
                first = jnp.min(jnp.where(cand == mx, jrowf, float(nslp)), axis=0, keepdims=True)
                hit = jrowf == first
                sel = jnp.where(hit, 1.0, sel)
                cand = jnp.where(hit, -2.0, cand)
            bias_scr[g] = to_bias(jnp.where(causal, sel, 0.0))

    qaug_scr[:, 0:LANES] = qr_st
    if not shifted:
        m_scr[...] = jnp.full(m_scr.shape, NEG_INF, F32)
    l_scr[...] = jnp.zeros(l_scr.shape, F32)
    acc_scr[...] = jnp.zeros(acc_scr.shape, F32)
    group_keys = LANES * SLC_BLOCK

    def chunk_step(key0, width, own):
        @pl.when(key0 % group_keys == 0)
        def _():
            off = pl.multiple_of((key0 // group_keys) * LANES, LANES)
            b0 = bias_scr[0, :, pl.ds(off, LANES)]
            b1 = bias_scr[1, :, pl.ds(off, LANES)]
            qaug_scr[:, LANES:2 * LANES] = jnp.concatenate([b0] * NSA_GROUP + [b1] * NSA_GROUP, axis=0)

        k0 = pl.multiple_of(key0, width)
        s = _dot_nt(qaug_scr[...], ksa_ref[pl.ds(k0, width), :])
        if own:
            col = lax.broadcasted_iota(jnp.int32, (1, width), 1)
            s = jnp.where(col <= tpos_st - i * tq, s, -SEL_BIG)
        if shifted:
            p = jnp.exp(s)
            psum_l = p[:, 0:LANES]
            for t in range(1, width // LANES):
                psum_l = psum_l + p[:, t * LANES:(t + 1) * LANES]
            l_scr[...] += psum_l
            acc_scr[...] += _dot(p, vs_ref[pl.ds(k0, width), :])
        else:
            m_old = m_scr[...]
            m_new = jnp.maximum(m_old, jnp.max(s, axis=-1, keepdims=True))
            alpha = jnp.exp(m_old - m_new)
            p = jnp.exp(s - m_new[:, 0:1])
            l_scr[...] = alpha * l_scr[...] + jnp.sum(p, axis=-1, keepdims=True)
            acc_scr[...] = alpha * acc_scr[...] + _dot(p, vs_ref[pl.ds(k0, width), :])
            m_scr[...] = m_new

    n_full = (i * tq) // kc
    n_sub = (i * tq - n_full * kc) // tq

    def body_full(c, carry):
        chunk_step(c * kc, kc, False)
        return carry

    def body_sub(j, carry):
        chunk_step(n_full * kc + j * tq, tq, False)
        return carry

    lax.fori_loop(0, n_full, body_full, 0)
    lax.fori_loop(0, n_sub, body_sub, 0)
    chunk_step(i * tq, tq, True)
    if shifted:
        o_s = acc_scr[...] * (1.0 / jnp.sum(l_scr[...], axis=-1, keepdims=True))
    else:
        o_s = acc_scr[...] / l_scr[...]
    o_w = ow_scr[...]

    gate_e = _dot_hi2_l(gates, egate_ref[...])
    for j in range(NSA_GROUP):
        r0 = slice(j * tq, (j + 1) * tq)
        r1 = slice((NSA_GROUP + j) * tq, (NSA_GROUP + j + 1) * tq)
        out = None
        for br, o_b in enumerate((o_c_all, o_s, o_w)):
            lanes = slice(br * NSA_WIDTH + j * LANES, br * NSA_WIDTH + (j + 1) * LANES)
            term = gate_e[:, lanes] * jnp.where(lo, o_b[r0], o_b[r1])
            out = term if out is None else out + term
        o_ref[:, j * LANES:(j + 1) * LANES] = out.astype(o_ref.dtype)


def _gate_expander():
    mat = np.zeros((LANES, 3 * NSA_WIDTH), np.float32)
    for br in range(3):
        for j in range(NSA_GROUP):
            for g in range(NSA_KV_HEADS):
                head = g * NSA_GROUP + j
                c0 = br * NSA_WIDTH + j * LANES + g * HEAD
                mat[3 * head + br, c0:c0 + HEAD] = 1.0
    return jnp.asarray(mat, MXU_DTYPE)


def _nsa_attn(qn, qr, gt, kvc, ksa, vs, kw, vw, msel, cshift, batch, seq, shifted):
    tq, kc = (TQ if shifted else TQ_RUNMAX), KC
    nq = seq // tq
    nwb = WINDOW // tq + 1
    ncp = seq // CMP_STRIDE
    nslp = msel.shape[1]
    row = lambda w: pl.BlockSpec((tq, w), lambda b, i: (b * nq + i, 0))
    once = pl.Buffered(1)
    win = lambda jb: pl.BlockSpec((tq, LANES), lambda b, i: (b * nq + jnp.maximum(i - (nwb - 1) + jb, 0), 0))
    in_specs = ([row(NSA_WIDTH), row(NSA_WIDTH), row(LANES),
                 pl.BlockSpec((1, 1, ncp, LANES), lambda b, i: (0, b, 0, 0), pipeline_mode=once),
                 pl.BlockSpec((1, 1, ncp, LANES), lambda b, i: (1, b, 0, 0), pipeline_mode=once),
                 pl.BlockSpec((seq, 2 * LANES), lambda b, i: (b, 0), pipeline_mode=once),
                 pl.BlockSpec((seq, LANES), lambda b, i: (b, 0), pipeline_mode=once)]
                + [win(jb) for jb in range(nwb)] + [win(jb) for jb in range(nwb)]
                + [pl.BlockSpec((ncp, nslp), lambda b, i: (0, 0), pipeline_mode=once),
                   pl.BlockSpec((1, LANES), lambda b, i: (0, 0)),
                   pl.BlockSpec((LANES, 3 * NSA_WIDTH), lambda b, i: (0, 0), pipeline_mode=once)])
    return pl.pallas_call(
        functools.partial(_nsa_attn_kernel, tq=tq, seq=seq, nwb=nwb, kc=kc, ncp=ncp, nslp=nslp,
                          shifted=shifted),
        grid=(batch, nq),
        in_specs=in_specs,
        out_specs=row(NSA_WIDTH),
        out_shape=jax.ShapeDtypeStruct((batch * seq, NSA_WIDTH), MXU_DTYPE),
        scratch_shapes=[pltpu.VMEM((NSA_KV_HEADS, tq, nslp), MXU_DTYPE),
                        pltpu.VMEM((NSA_HEADS * tq, 2 * LANES), MXU_DTYPE),
                        pltpu.VMEM((NSA_HEADS * tq, LANES), F32),
                        pltpu.VMEM((NSA_HEADS * tq, LANES), F32),
                        pltpu.VMEM((NSA_HEADS * tq, LANES), F32),
                        pltpu.VMEM((NSA_HEADS * tq, LANES), F32),
                        pltpu.VMEM((NSA_KV_HEADS, nslp, tq), F32)],
        compiler_params=pltpu.CompilerParams(dimension_semantics=("parallel", "arbitrary"),
                                             vmem_limit_bytes=VMEM_LIMIT),
        name="nsa_attn" if shifted else "nsa_attn_runmax",
    )(qn, qr, gt, kvc, kvc, ksa, vs, *([kw] * nwb), *([vw] * nwb), msel, cshift, _gate_expander())


def _out_ffn_kernel(x_ref, og_ref, on_ref, c_ref, ch_ref, cw_ref, wo_ref, fg_ref, wgu_ref, wd_ref, o_ref,
                    *, tm, seq):
    i = pl.program_id(0)
    keep = jnp.where((i * tm) % seq == 0, 0.0, 1.0)
    rows = lax.broadcasted_iota(jnp.int32, (tm, 1), 0)
    u = c_ref[:, CONV_WIDTH:2 * CONV_WIDTH] * c_ref[:, 2 * CONV_WIDTH:3 * CONV_WIDTH]
    hu = ch_ref[:, CONV_WIDTH:2 * CONV_WIDTH] * ch_ref[:, 2 * CONV_WIDTH:3 * CONV_WIDTH] * keep
    w = cw_ref[...]
    conv = w[CONV_K - 1:CONV_K, :] * u
    for s in range(1, CONV_K):
        conv = conv + w[CONV_K - 1 - s:CONV_K - s, :] * _shift_rows(u, hu, s, rows)
    oc = c_ref[:, 0:CONV_WIDTH] * conv
    x1 = (x_ref[...] + _dot(og_ref[...], wo_ref[0:GDN_WIDTH, :])
          + _dot(on_ref[...], wo_ref[GDN_WIDTH:GDN_WIDTH + NSA_WIDTH, :])
          + _dot(oc, wo_ref[GDN_WIDTH + NSA_WIDTH:, :]))
    ms = jnp.mean(x1 * x1, axis=-1, keepdims=True)
    h2 = (x1 * lax.rsqrt(ms + EPS) * fg_ref[...]).astype(MXU_DTYPE)
    o_ref[...] = x1
    for c0 in range(0, D_FF, FF_CHUNK):
        gate = jnp.dot(h2, wgu_ref[:, c0:c0 + FF_CHUNK], preferred_element_type=F32)
        up = jnp.dot(h2, wgu_ref[:, D_FF + c0:D_FF + c0 + FF_CHUNK], preferred_element_type=F32)
        o_ref[...] += _dot(_silu(gate) * up, wd_ref[c0:c0 + FF_CHUNK, :])


def _out_ffn(x2, o_gdn, o_nsa, proj, conv_w, w_out, fgain, wgu, wd, seq):
    m = x2.shape[0]
    tm = TM_FFN
    hb = tm // SUBLANES
    full = lambda a, b: pl.BlockSpec((a, b), lambda i: (0, 0), pipeline_mode=pl.Buffered(1))
    return pl.pallas_call(
        functools.partial(_out_ffn_kernel, tm=tm, seq=seq),
        grid=(m // tm,),
        in_specs=[pl.BlockSpec((tm, D_MODEL), lambda i: (i, 0)),
                  pl.BlockSpec((tm, GDN_WIDTH), lambda i: (i, 0)),
                  pl.BlockSpec((tm, NSA_WIDTH), lambda i: (i, 0)),
                  pl.BlockSpec((tm, C_W), lambda i: (i, C_OFF // C_W)),
                  pl.BlockSpec((SUBLANES, C_W), lambda i: (jnp.maximum(i * hb - 1, 0), C_OFF // C_W)),
                  full(CONV_K, CONV_WIDTH), full(D_MODEL, D_MODEL), full(1, D_MODEL),
                  full(D_MODEL, 2 * D_FF), full(D_FF, D_MODEL)],
        out_specs=pl.BlockSpec((tm, D_MODEL), lambda i: (i, 0)),
        out_shape=jax.ShapeDtypeStruct((m, D_MODEL), F32),
        compiler_params=pltpu.CompilerParams(dimension_semantics=("parallel",),
                                             vmem_limit_bytes=VMEM_LIMIT),
        name="out_ffn",
    )(x2, o_gdn, o_nsa, proj, proj, conv_w, w_out, fgain, wgu, wd)


def _proj_column_map():
    offs = np.concatenate([[0], np.cumsum(IN_SIZES)])
    seg = lambda k: np.arange(offs[k], offs[k + 1])
    pad = lambda n: -np.ones(n, np.int64)
    nq = seg(6).reshape(NSA_HEADS, HEAD)[list(Q_SLOT_HEADS)].reshape(-1)
    cols = np.concatenate([
        seg(0), seg(1), seg(2), seg(3), seg(4), seg(5), pad(LANES - 2 * GDN_HEADS),
        nq, seg(9), seg(11), seg(10), seg(12), seg(13), pad(LANES - 3 * NSA_HEADS),
        seg(14), seg(15), seg(16),
        seg(7), seg(8)])
    assert cols.shape[0] == PROJ_W
    return cols


def _block_diag_ones(n):
    idx = np.arange(n) // HEAD
    return jnp.asarray(idx[:, None] == idx[None, :], MXU_DTYPE)


def _head_expander(first_lane):
    mat = np.zeros((LANES, GDN_WIDTH), np.float32)
    for h in range(GDN_HEADS):
        mat[first_lane + h, h * HEAD:(h + 1) * HEAD] = 1.0
    return jnp.asarray(mat, MXU_DTYPE)


def _compress_weights(w1, w2):
    nslab = 2 * NSA_KV_HEADS
    w1r = w1.reshape(2, 2, CMP_STRIDE, HEAD, HEAD)
    w1bd = jnp.zeros((CMP_STRIDE, nslab, HEAD, 2, nslab, HEAD), F32)
    w2bd = jnp.zeros((nslab, HEAD, nslab, HEAD), F32)
    for s in range(nslab):
        c = s // NSA_KV_HEADS
        w1bd = w1bd.at[:, s, :, :, s, :].set(w1r[c].transpose(1, 2, 0, 3))
        w2bd = w2bd.at[s, :, s, :].set(w2[c])
    return (w1bd.reshape(CMP_STRIDE, P_W, 2 * P_W).astype(MXU_DTYPE),
            w2bd.reshape(P_W, P_W).astype(MXU_DTYPE))


def _rope_lane_tables(positions):
    inv = jnp.float32(ROPE_THETA) ** (-jnp.arange(0, ROT_DIM, 2, dtype=jnp.float32) / ROT_DIM)
    ang = positions.astype(jnp.float32).reshape(-1)[:, None] * inv
    cos, sin = jnp.cos(ang), jnp.sin(ang)
    m = ang.shape[0]
    half = ROT_DIM // 2
    one = jnp.ones((m, HEAD - ROT_DIM), F32)
    zero_h = jnp.zeros((m, half), F32)
    zero_r = jnp.zeros((m, HEAD - ROT_DIM), F32)
    c = jnp.concatenate([cos, cos, one], axis=1)
    s1 = jnp.concatenate([-sin, zero_h, zero_r], axis=1)
    s2 = jnp.concatenate([zero_h, sin, zero_r], axis=1)
    tile2 = lambda t: jnp.concatenate([t, t], axis=1)
    return tile2(c), tile2(s1), tile2(s2)


def _selection_matrix(ncp, nslp):
    ratio = SLC_BLOCK // CMP_STRIDE
    frac = np.minimum(CMP_LEN, SLC_BLOCK - CMP_STRIDE * np.arange(ratio)).astype(np.float64) / CMP_LEN
    mat = np.zeros((ncp, nslp), np.float32)
    c = np.arange(ncp)
    mat[c, c // ratio] = frac[c % ratio]
    nxt = c // ratio + 1
    ok = nxt < nslp
    mat[c[ok], nxt[ok]] += (1.0 - frac[c % ratio])[ok]
    return jnp.asarray(mat, MXU_DTYPE)


def kernel(x, positions, attn_norm, w_in, gdn_conv_w, gdn_a_log, gdn_dt_bias, gdn_norm, nsa_q_norm,
           nsa_k_norm, nsa_cmp_pe, nsa_cmp_w1, nsa_cmp_w2, conv_w, w_out, ffn_norm, w_gate_up, w_down):
    batch, seq, _ = x.shape
    depth = w_in.shape[0]
    m = batch * seq
    assert seq % max(TM_PROJ, TM_FFN, T_PREP, KC) == 0 and (seq // CMP_STRIDE) % LANES == 0
    nb = seq // CMP_STRIDE
    nslp = -(-(seq // SLC_BLOCK) // LANES) * LANES

    cols = _proj_column_map()
    take = jnp.asarray(np.maximum(cols, 0), jnp.int32)
    valid = jnp.asarray(cols >= 0)
    o_rows = np.arange(NSA_WIDTH).reshape(NSA_HEADS, HEAD)[list(Q_SLOT_HEADS)].reshape(-1) + GDN_WIDTH
    wo_rows = jnp.asarray(np.concatenate([np.arange(GDN_WIDTH), o_rows,
                                          np.arange(GDN_WIDTH + NSA_WIDTH, D_MODEL)]), jnp.int32)
    bd128, bd256, bd512 = _block_diag_ones(128), _block_diag_ones(256), _block_diag_ones(512)
    rc, rs1, rs2 = _rope_lane_tables(positions)
    msel = _selection_matrix(nb, nslp)
    lane_pad = lambda v: jnp.zeros((1, LANES), F32).at[0, GDN_HEADS:2 * GDN_HEADS].set(v.astype(F32))

    x2 = x.reshape(m, D_MODEL)
    for l in range(depth):
        w_l = jnp.where(valid[None, :], jnp.take(w_in[l], take, axis=1), 0.0).astype(MXU_DTYPE)
        proj = _in_proj(x2, attn_norm[l].reshape(1, D_MODEL), w_l)

        o_gdn = _gdn(proj, gdn_conv_w[l], lane_pad(gdn_a_log[l]), lane_pad(gdn_dt_bias[l]),
                     jnp.tile(gdn_norm[l], GDN_HEADS).reshape(1, GDN_WIDTH), bd256,
                     _head_expander(0), _head_expander(GDN_HEADS), batch, seq)

        qg = jnp.tile(nsa_q_norm[l], NSA_HEADS).reshape(1, NSA_WIDTH)
        kg = jnp.tile(nsa_k_norm[l, 1:3], (1, NSA_KV_HEADS))
        qn, qr, ksa, kw, vs, vw, gt = _nsa_prep(proj, rc, rs1, rs2, qg, kg, bd512, bd128, seq)

        w1bd, w2bd = _compress_weights(nsa_cmp_w1[l], nsa_cmp_w2[l])
        kvc = _compress(proj, w1bd, nsa_cmp_pe[l].reshape(2, CMP_LEN * HEAD), nsa_cmp_w1[l].astype(MXU_DTYPE),
                        w2bd, jnp.tile(nsa_k_norm[l, 0], NSA_KV_HEADS).reshape(1, NSA_KV_WIDTH), bd128,
                        batch, seq)

        bound = (HEAD ** 0.5) * jnp.max(jnp.abs(nsa_q_norm[l])) * jnp.max(jnp.abs(nsa_k_norm[l]), axis=1)
        cshift = jnp.zeros((1, LANES), F32).at[0, 0:3].set(bound.astype(F32))
        attn_args = (qn, qr, gt, kvc, ksa, vs, kw, vw, msel, cshift)
        o_nsa = lax.cond(
            jnp.max(bound) <= MAX_SOFTMAX_SHIFT,
            lambda a: _nsa_attn(*a, batch, seq, True),
            lambda a: _nsa_attn(*a, batch, seq, False),
            attn_args)

        x2 = _out_ffn(x2, o_gdn, o_nsa, proj, conv_w[l], jnp.take(w_out[l], wo_rows, axis=0).astype(MXU_DTYPE),
                      ffn_norm[l].reshape(1, D_MODEL), w_gate_up[l].astype(MXU_DTYPE),
                      w_down[l].astype(MXU_DTYPE), seq)
    return x2.reshape(batch, seq, D_MODEL)
```

```python
import functools

import numpy as np
import jax
import jax.numpy as jnp
from jax import lax
from jax.experimental import pallas as pl
from jax.experimental.pallas import tpu as pltpu

F32 = jnp.float32
MXU_DTYPE = jnp.bfloat16

D_MODEL = 1024
HEAD = 64
EPS = 1e-6
NEG_INF = -1e30
GDN_HEADS = 4
GDN_WIDTH = 256
GDN_CONV = 4
GDN_CHUNK = 64
NSA_HEADS = 8
NSA_WIDTH = 512
NSA_KV_HEADS = 2
NSA_GROUP = 4
NSA_KV_WIDTH = 128
CMP_STRIDE = 16
CMP_LEN = 32
SLC_BLOCK = 64
N_SELECT = 16
WINDOW = 512
CONV_WIDTH = 256
CONV_K = 3
ROPE_THETA = 500000.0
ROT_DIM = 16
D_FF = 2816
IN_SIZES = (256, 256, 256, 256, 4, 4, 512, 128, 128, 128, 128, 128, 128, 24, 256, 256, 256)
D_IN = sum(IN_SIZES)

LANES = 128
SUBLANES = 8
VMEM_LIMIT = 56 * 1024 * 1024

G_W = 1024 + LANES
N_W = 512 + 4 * 128 + LANES
C_W = 3 * CONV_WIDTH
P_W = 2 * NSA_KV_WIDTH
G_OFF, N_OFF, C_OFF, P_OFF = 0, G_W, G_W + N_W, G_W + N_W + C_W
PROJ_W = G_W + N_W + C_W + P_W
Q_SLOT_HEADS = (0, 4, 1, 5, 2, 6, 3, 7)

SEL_BIG = 16384.0
MAX_SOFTMAX_SHIFT = 40.0
LOG2E = 1.4426950408889634

TM_PROJ = 512
TM_FFN = 512
T_GDN = 512
T_PREP = 512
TQ = 256
TQ_RUNMAX = 128
KC = 1024
FF_CHUNK = 256


def _dot(a, b):
    return jnp.dot(a.astype(MXU_DTYPE), b.astype(MXU_DTYPE), preferred_element_type=F32)


def _dot_nt(a, b):
    return lax.dot_general(a.astype(MXU_DTYPE), b.astype(MXU_DTYPE), (((1,), (1,)), ((), ())),
                           preferred_element_type=F32)


def _dot_tn(a, b):
    return lax.dot_general(a.astype(MXU_DTYPE), b.astype(MXU_DTYPE), (((0,), (0,)), ((), ())),
                           preferred_element_type=F32)


def _split3(x):
    a = x.astype(MXU_DTYPE)
    r = x - a.astype(F32)
    b = r.astype(MXU_DTYPE)
    c = (r - b.astype(F32)).astype(MXU_DTYPE)
    return a, b, c


def _dot_hi_l(x, m):
    a, b, c = _split3(x)
    f = lambda t: jnp.dot(t, m, preferred_element_type=F32)
    return f(a) + f(b) + f(c)


def _dot_hi2_l(x, m):
    a = x.astype(MXU_DTYPE)
    b = (x - a.astype(F32)).astype(MXU_DTYPE)
    return jnp.dot(a, m, preferred_element_type=F32) + jnp.dot(b, m, preferred_element_type=F32)


def _dot_hi(x, y):
    xa, xb, xc = _split3(x)
    ya, yb, yc = _split3(y)
    f = lambda s, t: jnp.dot(s, t, preferred_element_type=F32)
    return f(xa, ya) + (f(xa, yb) + f(xb, ya)) + (f(xa, yc) + f(xb, yb) + f(xc, ya))


def _silu(x):
    return x * jax.nn.sigmoid(x)


def _shift_rows(x, halo, s, rows):
    y = pltpu.roll(x, s, 0)
    for r in range(s):
        y = jnp.where(rows == r, halo[SUBLANES - s + r:SUBLANES - s + r + 1, :], y)
    return y


def _in_proj_kernel(x_ref, g_ref, w_ref, o_ref):
    x = x_ref[...]
    ms = jnp.mean(x * x, axis=-1, keepdims=True)
    h = (x * lax.rsqrt(ms + EPS) * g_ref[...]).astype(MXU_DTYPE)
    for c in range(0, PROJ_W, 256):
        o_ref[:, c:c + 256] = jnp.dot(h, w_ref[:, c:c + 256], preferred_element_type=F32)


def _in_proj(x2, gain, w):
    m = x2.shape[0]
    tm = TM_PROJ
    return pl.pallas_call(
        _in_proj_kernel,
        grid=(m // tm,),
        in_specs=[pl.BlockSpec((tm, D_MODEL), lambda i: (i, 0)),
                  pl.BlockSpec((1, D_MODEL), lambda i: (0, 0)),
                  pl.BlockSpec((D_MODEL, PROJ_W), lambda i: (0, 0))],
        out_specs=pl.BlockSpec((tm, PROJ_W), lambda i: (i, 0)),
        out_shape=jax.ShapeDtypeStruct((m, PROJ_W), F32),
        compiler_params=pltpu.CompilerParams(dimension_semantics=("parallel",),
                                             vmem_limit_bytes=VMEM_LIMIT),
        name="in_proj",
    )(x2, gain, w)


def _gdn_kernel(g_ref, gh_ref, cw_ref, alog_ref, dt_ref, gn_ref, bd_ref, eb_ref, eg_ref, o_ref, s_ref, *, tt):
    t_idx = pl.program_id(1)

    @pl.when(t_idx == 0)
    def _():
        s_ref[...] = jnp.zeros_like(s_ref)

    keep = jnp.where(t_idx == 0, 0.0, 1.0)
    rows = lax.broadcasted_iota(jnp.int32, (tt, 1), 0)
    x = g_ref[:, 0:3 * GDN_WIDTH]
    hx = gh_ref[:, 0:3 * GDN_WIDTH] * keep
    w = cw_ref[...]
    y = w[GDN_CONV - 1:GDN_CONV, :] * x
    for s in range(1, GDN_CONV):
        y = y + w[GDN_CONV - 1 - s:GDN_CONV - s, :] * _shift_rows(x, hx, s, rows)
    y = _silu(y)
    q = y[:, 0:GDN_WIDTH]
    k = y[:, GDN_WIDTH:2 * GDN_WIDTH]
    v = y[:, 2 * GDN_WIDTH:3 * GDN_WIDTH]
    bd = bd_ref[...]
    q = q * lax.rsqrt(_dot_hi2_l(q * q, bd) + EPS) * (HEAD ** -0.5)
    k = k * lax.rsqrt(_dot_hi2_l(k * k, bd) + EPS)

    gg = g_ref[:, 4 * GDN_WIDTH:4 * GDN_WIDTH + LANES]
    lane = lax.broadcasted_iota(jnp.int32, (1, LANES), 1)
    xa = gg + dt_ref[...]
    softplus = jnp.maximum(xa, 0.0) + jnp.log1p(jnp.exp(-jnp.abs(xa)))
    g2 = jnp.where(lane < GDN_HEADS, jax.nn.sigmoid(gg), -jnp.exp(alog_ref[...]) * softplus)
    cs = g2.T
    lane_t = lax.broadcasted_iota(jnp.int32, (1, tt), 1) % GDN_CHUNK
    step = 1
    while step < GDN_CHUNK:
        cs = cs + jnp.where(lane_t >= step, pltpu.roll(cs, step, 1), 0.0)
        step *= 2
    gcum_t = cs
    gcum = cs.T
    beta_e = _dot_hi_l(g2, eb_ref[...])
    gcum_e = _dot_hi_l(gcum, eg_ref[...])
    eg_e = jnp.exp(gcum_e)
    kb = k * beta_e
    rv = v * beta_e
    rk = kb * eg_e
    qd = q * eg_e

    nst = GDN_HEADS * GDN_CHUNK
    ri = lax.broadcasted_iota(jnp.int32, (nst, nst), 0)
    ci = lax.broadcasted_iota(jnp.int32, (nst, nst), 1)
    same = (ri // GDN_CHUNK) == (ci // GDN_CHUNK)
    m_tril = same & ((ri % GDN_CHUNK) >= (ci % GDN_CHUNK))
    m_strict = same & ((ri % GDN_CHUNK) > (ci % GDN_CHUNK))
    tile4 = lambda t: jnp.concatenate([t] * GDN_HEADS, axis=0)
    expand = lambda t: tile4(t.astype(MXU_DTYPE)) * bd

    nchunk = tt // GDN_CHUNK
    rss = [slice(n * GDN_CHUNK, (n + 1) * GDN_CHUNK) for n in range(nchunk)]
    g_last = [gcum_e[(n + 1) * GDN_CHUNK - 1:(n + 1) * GDN_CHUNK, :] for n in range(nchunk)]
    rmat, pw, qk = [], [], []
    for n, rs in enumerate(rss):
        g_row = jnp.concatenate([gcum_t[GDN_HEADS + h:GDN_HEADS + h + 1, rs] for h in range(GDN_HEADS)], axis=1)
        decay = jnp.where(m_tril, jnp.exp(jnp.where(m_tril, tile4(gcum_e[rs]) - g_row, 0.0)), 0.0)
        k4 = tile4(k[rs])
        a = jnp.where(m_strict, _dot_nt(expand(kb[rs]), k4) * decay, 0.0)
        qk.append(jnp.where(m_tril, _dot_nt(expand(q[rs]), k4) * decay, 0.0))
        rmat.append(-a)
        pw.append(-a)
    for _ in range(5):
        pw = [_dot(t, t) for t in pw]
        rmat = [r + t + _dot(r, t) for r, t in zip(rmat, pw)]
    u, wm = [], []
    for n, rs in enumerate(rss):
        rv_x = jnp.where(same, tile4(rv[rs]), 0.0)
        rk_x = jnp.where(same, tile4(rk[rs]), 0.0)
        u.append(rv_x + _dot(rmat[n], rv_x))
        wm.append(rk_x + _dot(rmat[n], rk_x))

    st = s_ref[...]
    outs = []
    for n, rs in enumerate(rss):
        kd = k[rs] * jnp.exp(g_last[n] - gcum_e[rs])
        v_new = u[n] - _dot(wm[n], st)
        o_x = _dot(expand(qd[rs]), st) + _dot(qk[n], v_new)
        st = st * jnp.exp(g_last[n]) + _dot_tn(expand(kd), v_new)
        o = o_x[0:GDN_CHUNK]
        for h in range(1, GDN_HEADS):
            o = o + o_x[h * GDN_CHUNK:(h + 1) * GDN_CHUNK]
        outs.append(o)
    s_ref[...] = st
    o_all = jnp.concatenate(outs, axis=0)
    on = o_all * lax.rsqrt(_dot_hi2_l(o_all * o_all, bd) * (1.0 / HEAD) + EPS) * gn_ref[...]
    o_ref[...] = (on * _silu(g_ref[:, 3 * GDN_WIDTH:4 * GDN_WIDTH])).astype(o_ref.dtype)


def _gdn(proj, conv_w, alog_pad, dt_pad, gnorm, bd256, eb, eg, batch, seq):
    tt = T_GDN
    nt = seq // tt
    hb = tt // SUBLANES
    return pl.pallas_call(
        functools.partial(_gdn_kernel, tt=tt),
        grid=(batch, nt),
        in_specs=[pl.BlockSpec((tt, G_W), lambda b, t: (b * nt + t, 0)),
                  pl.BlockSpec((SUBLANES, G_W), lambda b, t: (jnp.maximum((b * nt + t) * hb - 1, 0), 0)),
                  pl.BlockSpec((GDN_CONV, 3 * GDN_WIDTH), lambda b, t: (0, 0)),
                  pl.BlockSpec((1, LANES), lambda b, t: (0, 0)),
                  pl.BlockSpec((1, LANES), lambda b, t: (0, 0)),
                  pl.BlockSpec((1, GDN_WIDTH), lambda b, t: (0, 0)),
                  pl.BlockSpec((GDN_WIDTH, GDN_WIDTH), lambda b, t: (0, 0)),
                  pl.BlockSpec((LANES, GDN_WIDTH), lambda b, t: (0, 0)),
                  pl.BlockSpec((LANES, GDN_WIDTH), lambda b, t: (0, 0))],
        out_specs=pl.BlockSpec((tt, GDN_WIDTH), lambda b, t: (b * nt + t, 0)),
        out_shape=jax.ShapeDtypeStruct((batch * seq, GDN_WIDTH), MXU_DTYPE),
        scratch_shapes=[pltpu.VMEM((GDN_WIDTH, GDN_WIDTH), F32)],
        compiler_params=pltpu.CompilerParams(dimension_semantics=("parallel", "arbitrary"),
                                             vmem_limit_bytes=VMEM_LIMIT),
        name="gdn",
    )(proj, proj, conv_w, alog_pad, dt_pad, gnorm, bd256, eb, eg)


def _rope(x, c, s1, s2):
    wdt = x.shape[1]
    return x * c + pltpu.roll(x, wdt - ROT_DIM // 2, 1) * s1 + pltpu.roll(x, ROT_DIM // 2, 1) * s2


def _nsa_prep_kernel(n_ref, c_ref, s1_ref, s2_ref, qg_ref, kg_ref, bdq_ref, bdk_ref,
                     qn_ref, qr_ref, ksa_ref, kw_ref, vs_ref, vw_ref, gt_ref, *, tt, seq):
    c1, s1, s2 = c_ref[...], s1_ref[...], s2_ref[...]
    c4 = jnp.concatenate([c1] * 4, axis=1)
    s14 = jnp.concatenate([s1] * 4, axis=1)
    s24 = jnp.concatenate([s2] * 4, axis=1)
    q = n_ref[:, 0:NSA_WIDTH]
    qn = q * lax.rsqrt(_dot_hi2_l(q * q, bdq_ref[...]) * (1.0 / HEAD) + EPS) * qg_ref[...]
    qn = qn * (HEAD ** -0.5 * LOG2E)
    qn_ref[...] = qn.astype(qn_ref.dtype)
    qr_ref[...] = _rope(qn, c4, s14, s24).astype(qr_ref.dtype)
    bdk = bdk_ref[...]
    ks = n_ref[:, 512:640]
    kw = n_ref[:, 640:768]
    ks = ks * lax.rsqrt(_dot_hi2_l(ks * ks, bdk) * (1.0 / HEAD) + EPS) * kg_ref[0:1, :]
    kw = kw * lax.rsqrt(_dot_hi2_l(kw * kw, bdk) * (1.0 / HEAD) + EPS) * kg_ref[1:2, :]
    ksa_ref[:, 0:LANES] = _rope(ks, c1, s1, s2).astype(ksa_ref.dtype)
    rows = (lax.broadcasted_iota(jnp.int32, (tt, LANES), 0) + pl.program_id(0) * tt) % seq
    lane = lax.broadcasted_iota(jnp.int32, (tt, LANES), 1)
    ksa_ref[:, LANES:2 * LANES] = jnp.where((rows // SLC_BLOCK) % LANES == lane, 1.0, 0.0).astype(ksa_ref.dtype)
    kw_ref[...] = _rope(kw, c1, s1, s2).astype(kw_ref.dtype)
    vs_ref[...] = n_ref[:, 768:896].astype(vs_ref.dtype)
    vw_ref[...] = n_ref[:, 896:1024].astype(vw_ref.dtype)
    gt_ref[...] = jax.nn.sigmoid(n_ref[:, 1024:1152])


def _nsa_prep(proj, rc, rs1, rs2, qg, kg, bd512, bd128, seq):
    m = proj.shape[0]
    tt = T_PREP
    row = lambda w: pl.BlockSpec((tt, w), lambda i: (i, 0))
    full = lambda a, b: pl.BlockSpec((a, b), lambda i: (0, 0))
    sds = lambda w, dt: jax.ShapeDtypeStruct((m, w), dt)
    return pl.pallas_call(
        functools.partial(_nsa_prep_kernel, tt=tt, seq=seq),
        grid=(m // tt,),
        in_specs=[pl.BlockSpec((tt, N_W), lambda i: (i, 1)), row(LANES), row(LANES), row(LANES),
                  full(1, NSA_WIDTH), full(2, LANES), full(NSA_WIDTH, NSA_WIDTH), full(LANES, LANES)],
        out_specs=[row(NSA_WIDTH), row(NSA_WIDTH), row(2 * LANES), row(LANES), row(LANES), row(LANES),
                   row(LANES)],
        out_shape=[sds(NSA_WIDTH, MXU_DTYPE), sds(NSA_WIDTH, MXU_DTYPE), sds(2 * LANES, MXU_DTYPE),
                   sds(LANES, MXU_DTYPE), sds(LANES, MXU_DTYPE), sds(LANES, MXU_DTYPE), sds(LANES, F32)],
        compiler_params=pltpu.CompilerParams(dimension_semantics=("parallel",),
                                             vmem_limit_bytes=VMEM_LIMIT),
        name="nsa_prep",
    )(proj, rc, rs1, rs2, qg, kg, bd512, bd128)


def _compress_kernel(xk_ref, xv_ref, w1bd_ref, pe_ref, w1_ref, w2bd_ref, kg_ref, bd_ref, o_ref, *, nb):
    acc = None
    for t in range(CMP_STRIDE):
        xt = jnp.concatenate([xk_ref[pl.ds(t, nb, stride=CMP_STRIDE), :],
                              xv_ref[pl.ds(t, nb, stride=CMP_STRIDE), :]], axis=1)
        part = _dot(xt, w1bd_ref[t])
        acc = part if acc is None else acc + part
    pe_terms = [_dot(jnp.broadcast_to(pe_ref[c:c + 1, :], (SUBLANES, CMP_LEN * HEAD)), w1_ref[c])[0:1, :]
                for c in range(2)]
    pe_all = jnp.concatenate([pe_terms[0]] * NSA_KV_HEADS + [pe_terms[1]] * NSA_KV_HEADS, axis=1)
    half = 2 * NSA_KV_WIDTH
    pre = acc[:, 0:half] + pltpu.roll(acc[:, half:2 * half], nb - 1, 0) + pe_all
    y = _dot(jax.nn.gelu(pre), w2bd_ref[...])
    rows = lax.broadcasted_iota(jnp.int32, (nb, 1), 0)
    y = jnp.where(rows < nb - 1, y, 0.0)
    yk = y[:, 0:NSA_KV_WIDTH]
    yk = yk * lax.rsqrt(_dot_hi2_l(yk * yk, bd_ref[...]) * (1.0 / HEAD) + EPS) * kg_ref[...]
    o_ref[0, 0] = yk.astype(o_ref.dtype)
    o_ref[1, 0] = y[:, NSA_KV_WIDTH:2 * NSA_KV_WIDTH].astype(o_ref.dtype)


def _compress(proj, w1bd, pe, w1, w2bd, kg0, bd128, batch, seq):
    nb = seq // CMP_STRIDE
    full = lambda *s: pl.BlockSpec(s, lambda b: (0,) * len(s))
    return pl.pallas_call(
        functools.partial(_compress_kernel, nb=nb),
        grid=(batch,),
        in_specs=[pl.BlockSpec((seq, NSA_KV_WIDTH), lambda b: (b, P_OFF // NSA_KV_WIDTH)),
                  pl.BlockSpec((seq, NSA_KV_WIDTH), lambda b: (b, P_OFF // NSA_KV_WIDTH + 1)),
                  full(CMP_STRIDE, P_W, 2 * P_W), full(2, CMP_LEN * HEAD), full(2, CMP_LEN * HEAD, HEAD),
                  full(P_W, P_W), full(1, NSA_KV_WIDTH), full(LANES, LANES)],
        out_specs=pl.BlockSpec((2, 1, nb, NSA_KV_WIDTH), lambda b: (0, b, 0, 0)),
        out_shape=jax.ShapeDtypeStruct((2, batch, nb, NSA_KV_WIDTH), MXU_DTYPE),
        compiler_params=pltpu.CompilerParams(dimension_semantics=("parallel",),
                                             vmem_limit_bytes=VMEM_LIMIT),
        name="compress",
    )(proj, proj, w1bd, pe, w1, w2bd, kg0, bd128)


def _nsa_attn_kernel(*refs, tq, seq, nwb, kc, ncp, nslp, shifted):
    qn_ref, qr_ref, gt_ref, kcmp_ref, vcmp_ref, ksa_ref, vs_ref = refs[:7]
    kw_refs = refs[7:7 + nwb]
    vw_refs = refs[7 + nwb:7 + 2 * nwb]
    msel_ref, cs_ref, egate_ref = refs[7 + 2 * nwb:10 + 2 * nwb]
    o_ref = refs[10 + 2 * nwb]
    bias_scr, qaug_scr, m_scr, l_scr, acc_scr, ow_scr, cand_scr, oc_scr, slc_scr, e_scr = refs[11 + 2 * nwb:]
    nslot = NSA_HEADS
    i = pl.program_id(1)
    c_cmp, c_slc, c_win = cs_ref[0:1, 0:1], cs_ref[0:1, 1:2], cs_ref[0:1, 2:3]
    lane = lax.broadcasted_iota(jnp.int32, (1, LANES), 1)
    lo = lane < HEAD

    def stack(q_ref):
        sl = [q_ref[:, j * LANES:(j + 1) * LANES] for j in range(NSA_GROUP)]
        zero = jnp.zeros_like(sl[0])
        return jnp.concatenate([jnp.where(lo, s, zero) for s in sl] + [jnp.where(lo, zero, s) for s in sl], axis=0)

    tpos = i * tq + lax.broadcasted_iota(jnp.int32, (tq, 1), 0)
    tpos_st = i * tq + lax.broadcasted_iota(jnp.int32, (nslot * tq, 1), 0) % tq
    gates = gt_ref[...]

    qn_st = stack(qn_ref)
    if shifted:
        ccw = min(2 * LANES, ncp)
        n_vis = (i + 1) * (tq // CMP_STRIDE) - 1
        n_cch = (n_vis + ccw - 1) // ccw
        qaug_scr[:, 0:LANES] = qn_st
        l_scr[...] = jnp.zeros(l_scr.shape, F32)
        acc_scr[...] = jnp.zeros(acc_scr.shape, F32)

        def cmp_scores(ch, carry):
            k0 = pl.multiple_of(ch * ccw, ccw)
            s = _dot_nt(qaug_scr[:, 0:LANES], kcmp_ref[0, 0, pl.ds(k0, ccw), :])
            ckey = k0 + lax.broadcasted_iota(jnp.int32, (1, ccw), 1)
            vis = (ckey * CMP_STRIDE + (CMP_LEN - 1)) <= tpos
            cbias = jnp.where(vis, -c_cmp, -SEL_BIG)
            for slot in range(nslot):
                rs = slice(slot * tq, (slot + 1) * tq)
                e = jnp.exp2(s[rs] + cbias)
                e_scr[rs, pl.ds(k0, ccw)] = e
                part = e[:, 0:LANES]
                for t in range(1, ccw // LANES):
                    part = part + e[:, t * LANES:(t + 1) * LANES]
                l_scr[rs, :] += part
            acc_scr[...] += _dot(e_scr[:, pl.ds(k0, ccw)], vcmp_ref[0, 0, pl.ds(k0, ccw), :])
            return carry

        lax.fori_loop(0, n_cch, cmp_scores, 0)
        l_c = jnp.sum(l_scr[...], axis=-1, keepdims=True)
        rinv_c = 1.0 / jnp.where(l_c > 0.0, l_c, 1.0)
        oc_scr[...] = acc_scr[...] * rinv_c
        l_scr[...] = jnp.broadcast_to(rinv_c, l_scr.shape)
        slc_scr[...] = jnp.zeros(slc_scr.shape, F32)

        def cmp_importance(ch, carry):
            k0 = pl.multiple_of(ch * ccw, ccw)
            for g in range(NSA_KV_HEADS):
                ps = None
                for r in range(NSA_GROUP):
                    rs = slice((g * NSA_GROUP + r) * tq, (g * NSA_GROUP + r + 1) * tq)
                    rinv_l = jnp.concatenate([l_scr[rs, :]] * (ccw // LANES), axis=1)
                    p = e_scr[rs, pl.ds(k0, ccw)] * rinv_l
                    ps = p if ps is None else ps + p
                slc_scr[g] += _dot_hi_l(ps, msel_ref[pl.ds(k0, ccw), :])
            return carry

        lax.fori_loop(0, n_cch, cmp_importance, 0)
        slc_rows = [slc_scr[g] for g in range(NSA_KV_HEADS)]
    else:
        ckey = lax.broadcasted_iota(jnp.int32, (1, ncp), 1)
        cmask = (ckey * CMP_STRIDE + (CMP_LEN - 1)) <= tpos
        s_all = _dot_nt(qn_st, kcmp_ref[0, 0])
        psum = [None, None]
        e_parts, rinv_parts = [], []
        for slot in range(nslot):
            s = jnp.where(cmask, s_all[slot * tq:(slot + 1) * tq], NEG_INF)
            e = jnp.where(cmask, jnp.exp2(s - jnp.max(s, axis=-1, keepdims=True)), 0.0)
            l = jnp.sum(e, axis=-1, keepdims=True)
            rinv = 1.0 / jnp.where(l > 0.0, l, 1.0)
            e_parts.append(e.astype(MXU_DTYPE))
            rinv_parts.append(rinv)
            p = e * rinv
            g = slot // NSA_GROUP
            psum[g] = p if psum[g] is None else psum[g] + p
        oc_scr[...] = (_dot(jnp.concatenate(e_parts, axis=0), vcmp_ref[0, 0])
                       * jnp.concatenate(rinv_parts, axis=0))
        slc_rows = [_dot_hi_l(psum[g], msel_ref[...]) for g in range(NSA_KV_HEADS)]

    qr_st = stack(qr_ref)
    kwc = jnp.concatenate([r[...] for r in kw_refs], axis=0)
    vwc = jnp.concatenate([r[...] for r in vw_refs], axis=0)
    sw = _dot_nt(qr_st, kwc)
    if shifted:
        rr = lax.broadcasted_iota(jnp.int32, (tq, tq), 0)
        cc = lax.broadcasted_iota(jnp.int32, (tq, tq), 1)
        parts = []
        for jb in range(nwb):
            shift_b = c_win + jnp.where(i - (nwb - 1) + jb >= 0, 0.0, SEL_BIG)
            blk = jnp.exp2(sw[:, jb * tq:(jb + 1) * tq] - shift_b)
            if jb == 0 or jb == nwb - 1:
                vis = (cc > rr) if jb == 0 else (cc <= rr)
                blk = jnp.concatenate([jnp.where(vis, blk[s * tq:(s + 1) * tq], 0.0) for s in range(nslot)],
                                      axis=0)
            parts.append(blk)
        ew = jnp.concatenate(parts, axis=1)
    else:
        kpos_w = (i - (nwb - 1)) * tq + lax.broadcasted_iota(jnp.int32, (1, nwb * tq), 1)
        dist = tpos_st - kpos_w
        wmask = (dist >= 0) & (dist < WINDOW) & (kpos_w >= 0)
        sw = jnp.where(wmask, sw, NEG_INF)
        ew = jnp.exp2(sw - jnp.max(sw, axis=-1, keepdims=True))
    ow_scr[...] = _dot(ew, vwc) * (1.0 / jnp.sum(ew, axis=-1, keepdims=True))

    jrow = lax.broadcasted_iota(jnp.int32, (nslp, 1), 0)
    jrowf = jrow.astype(F32)
    cur_t = (i * tq + lax.broadcasted_iota(jnp.int32, (1, tq), 1)) // SLC_BLOCK
    forced = (jrow == 0) | (jrow == cur_t) | (jrow == cur_t - 1)
    causal = jrow <= cur_t
    shift_s = c_slc if shifted else 0.0
    n_free = N_SELECT - 3
    to_bias = lambda sel: ((sel - 1.0) * SEL_BIG - shift_s).T.astype(bias_scr.dtype)
    n_bad = None
    for g in range(NSA_KV_HEADS):
        slc = slc_rows[g].T
        cand = jnp.where(causal & jnp.logical_not(forced), slc, -1.0)
        cand_scr[g] = cand
        c = cand
        for _ in range(n_free):
            c = jnp.where(c == jnp.max(c, axis=0, keepdims=True), -2.0, c)
        picked = (c == -2.0) & (cand >= 0.0)
        n_picked = jnp.sum(jnp.where(picked, 1.0, 0.0), axis=0, keepdims=True)
        n_real = jnp.sum(jnp.where(cand >= 0.0, 1.0, 0.0), axis=0, keepdims=True)
        bad = jnp.where(n_picked == jnp.minimum(n_real, float(n_free)), 0.0, 1.0)
        n_bad = bad if n_bad is None else n_bad + bad
        bias_scr[g] = to_bias(jnp.where(forced | picked, 1.0, 0.0))

    @pl.when(jnp.max(n_bad) > 0.0)
    def _():
        for g in range(NSA_KV_HEADS):
            cand = cand_scr[g]
            sel = jnp.where(forced, 1.0, 0.0)
            for _ in range(n_free):
                mx = jnp.max(cand, axis=0, keepdims=True)
                first = jnp.min(jnp.where(cand == mx, jrowf, float(nslp)), axis=0, keepdims=True)
                hit = jrowf == first
                sel = jnp.where(hit, 1.0, sel)
                cand = jnp.where(hit, -2.0, cand)
            bias_scr[g] = to_bias(jnp.where(causal, sel, 0.0))

    qaug_scr[:, 0:LANES] = qr_st
    if not shifted:
        m_scr[...] = jnp.full(m_scr.shape, NEG_INF, F32)
    l_scr[...] = jnp.zeros(l_scr.shape, F32)
    acc_scr[...] = jnp.zeros(acc_scr.shape, F32)
    group_keys = LANES * SLC_BLOCK

    def chunk_step(key0, width, own):
        @pl.when(key0 % group_keys == 0)
        def _():
            off = pl.multiple_of((key0 // group_keys) * LANES, LANES)
            b0 = bias_scr[0, :, pl.ds(off, LANES)]
            b1 = bias_scr[1, :, pl.ds(off, LANES)]
            qaug_scr[:, LANES:2 * LANES] = jnp.concatenate([b0] * NSA_GROUP + [b1] * NSA_GROUP, axis=0)

        k0 = pl.multiple_of(key0, width)
        s = _dot_nt(qaug_scr[...], ksa_ref[pl.ds(k0, width), :])
        if own:
            col = lax.broadcasted_iota(jnp.int32, (1, width), 1)
            s = jnp.where(col <= tpos_st - i * tq, s, -SEL_BIG)
        if shifted:
            p = jnp.exp2(s)
            psum_l = p[:, 0:LANES]
            for t in range(1, width // LANES):
                psum_l = psum_l + p[:, t * LANES:(t + 1) * LANES]
            l_scr[...] += psum_l
            acc_scr[...] += _dot(p, vs_ref[pl.ds(k0, width), :])
        else:
            m_old = m_scr[...]
            m_new = jnp.maximum(m_old, jnp.max(s, axis=-1, keepdims=True))
            alpha = jnp.exp2(m_old - m_new)
            p = jnp.exp2(s - m_new[:, 0:1])
            l_scr[...] = alpha * l_scr[...] + jnp.sum(p, axis=-1, keepdims=True)
            acc_scr[...] = alpha * acc_scr[...] + _dot(p, vs_ref[pl.ds(k0, width), :])
            m_scr[...] = m_new

    n_full = (i * tq) // kc
    n_sub = (i * tq - n_full * kc) // tq

    def body_full(c, carry):
        chunk_step(c * kc, kc, False)
        return carry

    def body_sub(j, carry):
        chunk_step(n_full * kc + j * tq, tq, False)
        return carry

    lax.fori_loop(0, n_full, body_full, 0)
    lax.fori_loop(0, n_sub, body_sub, 0)
    chunk_step(i * tq, tq, True)
    if shifted:
        o_s = acc_scr[...] * (1.0 / jnp.sum(l_scr[...], axis=-1, keepdims=True))
    else:
        o_s = acc_scr[...] / l_scr[...]
    o_w = ow_scr[...]

    gate_e = _dot_hi2_l(gates, egate_ref[...])
    for j in range(NSA_GROUP):
        r0 = slice(j * tq, (j + 1) * tq)
        r1 = slice((NSA_GROUP + j) * tq, (NSA_GROUP + j + 1) * tq)
        out = None
        for br, o_b in enumerate((oc_scr[...], o_s, o_w)):
            lanes = slice(br * NSA_WIDTH + j * LANES, br * NSA_WIDTH + (j + 1) * LANES)
            term = gate_e[:, lanes] * jnp.where(lo, o_b[r0], o_b[r1])
            out = term if out is None else out + term
        o_ref[:, j * LANES:(j + 1) * LANES] = out.astype(o_ref.dtype)


def _gate_expander():
    mat = np.zeros((LANES, 3 * NSA_WIDTH), np.float32)
    for br in range(3):
        for j in range(NSA_GROUP):
            for g in range(NSA_KV_HEADS):
                head = g * NSA_GROUP + j
                c0 = br * NSA_WIDTH + j * LANES + g * HEAD
                mat[3 * head + br, c0:c0 + HEAD] = 1.0
    return jnp.asarray(mat, MXU_DTYPE)


def _nsa_attn(qn, qr, gt, kvc, ksa, vs, kw, vw, msel, cshift, batch, seq, shifted):
    tq, kc = (TQ if shifted else TQ_RUNMAX), KC
    nq = seq // tq
    nwb = WINDOW // tq + 1
    ncp = seq // CMP_STRIDE
    nslp = msel.shape[1]
    row = lambda w: pl.BlockSpec((tq, w), lambda b, i: (b * nq + i, 0))
    once = pl.Buffered(1)
    win = lambda jb: pl.BlockSpec((tq, LANES), lambda b, i: (b * nq + jnp.maximum(i - (nwb - 1) + jb, 0), 0))
    in_specs = ([row(NSA_WIDTH), row(NSA_WIDTH), row(LANES),
                 pl.BlockSpec((1, 1, ncp, LANES), lambda b, i: (0, b, 0, 0), pipeline_mode=once),
                 pl.BlockSpec((1, 1, ncp, LANES), lambda b, i: (1, b, 0, 0), pipeline_mode=once),
                 pl.BlockSpec((seq, 2 * LANES), lambda b, i: (b, 0), pipeline_mode=once),
                 pl.BlockSpec((seq, LANES), lambda b, i: (b, 0), pipeline_mode=once)]
                + [win(jb) for jb in range(nwb)] + [win(jb) for jb in range(nwb)]
                + [pl.BlockSpec((ncp, nslp), lambda b, i: (0, 0), pipeline_mode=once),
                   pl.BlockSpec((1, LANES), lambda b, i: (0, 0)),
                   pl.BlockSpec((LANES, 3 * NSA_WIDTH), lambda b, i: (0, 0), pipeline_mode=once)])
    return pl.pallas_call(
        functools.partial(_nsa_attn_kernel, tq=tq, seq=seq, nwb=nwb, kc=kc, ncp=ncp, nslp=nslp,
                          shifted=shifted),
        grid=(batch, nq),
        in_specs=in_specs,
        out_specs=row(NSA_WIDTH),
        out_shape=jax.ShapeDtypeStruct((batch * seq, NSA_WIDTH), MXU_DTYPE),
        scratch_shapes=[pltpu.VMEM((NSA_KV_HEADS, tq, nslp), MXU_DTYPE),
                        pltpu.VMEM((NSA_HEADS * tq, 2 * LANES), MXU_DTYPE),
                        pltpu.VMEM((NSA_HEADS * tq, LANES), F32),
                        pltpu.VMEM((NSA_HEADS * tq, LANES), F32),
                        pltpu.VMEM((NSA_HEADS * tq, LANES), F32),
                        pltpu.VMEM((NSA_HEADS * tq, LANES), F32),
                        pltpu.VMEM((NSA_KV_HEADS, nslp, tq), F32),
                        pltpu.VMEM((NSA_HEADS * tq, LANES), F32),
                        pltpu.VMEM((NSA_KV_HEADS, tq, nslp), F32),
                        pltpu.VMEM((NSA_HEADS * tq, ncp) if shifted else (SUBLANES, LANES), F32)],
        compiler_params=pltpu.CompilerParams(dimension_semantics=("parallel", "arbitrary"),
                                             vmem_limit_bytes=VMEM_LIMIT),
        name="nsa_attn" if shifted else "nsa_attn_runmax",
    )(qn, qr, gt, kvc, kvc, ksa, vs, *([kw] * nwb), *([vw] * nwb), msel, cshift, _gate_expander())


def _out_ffn_kernel(x_ref, og_ref, on_ref, c_ref, ch_ref, cw_ref, wo_ref, fg_ref, wgu_ref, wd_ref, o_ref,
                    *, tm, seq):
    i = pl.program_id(0)
    keep = jnp.where((i * tm) % seq == 0, 0.0, 1.0)
    rows = lax.broadcasted_iota(jnp.int32, (tm, 1), 0)
    u = c_ref[:, CONV_WIDTH:2 * CONV_WIDTH] * c_ref[:, 2 * CONV_WIDTH:3 * CONV_WIDTH]
    hu = ch_ref[:, CONV_WIDTH:2 * CONV_WIDTH] * ch_ref[:, 2 * CONV_WIDTH:3 * CONV_WIDTH] * keep
    w = cw_ref[...]
    conv = w[CONV_K - 1:CONV_K, :] * u
    for s in range(1, CONV_K):
        conv = conv + w[CONV_K - 1 - s:CONV_K - s, :] * _shift_rows(u, hu, s, rows)
    oc = c_ref[:, 0:CONV_WIDTH] * conv
    x1 = (x_ref[...] + _dot(og_ref[...], wo_ref[0:GDN_WIDTH, :])
          + _dot(on_ref[...], wo_ref[GDN_WIDTH:GDN_WIDTH + NSA_WIDTH, :])
          + _dot(oc, wo_ref[GDN_WIDTH + NSA_WIDTH:, :]))
    ms = jnp.mean(x1 * x1, axis=-1, keepdims=True)
    h2 = (x1 * lax.rsqrt(ms + EPS) * fg_ref[...]).astype(MXU_DTYPE)
    o_ref[...] = x1
    for c0 in range(0, D_FF, FF_CHUNK):
        gate = jnp.dot(h2, wgu_ref[:, c0:c0 + FF_CHUNK], preferred_element_type=F32)
        up = jnp.dot(h2, wgu_ref[:, D_FF + c0:D_FF + c0 + FF_CHUNK], preferred_element_type=F32)
        o_ref[...] += _dot(_silu(gate) * up, wd_ref[c0:c0 + FF_CHUNK, :])


def _out_ffn(x2, o_gdn, o_nsa, proj, conv_w, w_out, fgain, wgu, wd, seq):
    m = x2.shape[0]
    tm = TM_FFN
    hb = tm // SUBLANES
    full = lambda a, b: pl.BlockSpec((a, b), lambda i: (0, 0), pipeline_mode=pl.Buffered(1))
    return pl.pallas_call(
        functools.partial(_out_ffn_kernel, tm=tm, seq=seq),
        grid=(m // tm,),
        in_specs=[pl.BlockSpec((tm, D_MODEL), lambda i: (i, 0)),
                  pl.BlockSpec((tm, GDN_WIDTH), lambda i: (i, 0)),
                  pl.BlockSpec((tm, NSA_WIDTH), lambda i: (i, 0)),
                  pl.BlockSpec((tm, C_W), lambda i: (i, C_OFF // C_W)),
                  pl.BlockSpec((SUBLANES, C_W), lambda i: (jnp.maximum(i * hb - 1, 0), C_OFF // C_W)),
                  full(CONV_K, CONV_WIDTH), full(D_MODEL, D_MODEL), full(1, D_MODEL),
                  full(D_MODEL, 2 * D_FF), full(D_FF, D_MODEL)],
        out_specs=pl.BlockSpec((tm, D_MODEL), lambda i: (i, 0)),
        out_shape=jax.ShapeDtypeStruct((m, D_MODEL), F32),
        compiler_params=pltpu.CompilerParams(dimension_semantics=("parallel",),
                                             vmem_limit_bytes=VMEM_LIMIT),
        name="out_ffn",
    )(x2, o_gdn, o_nsa, proj, proj, conv_w, w_out, fgain, wgu, wd)


def _proj_column_map():
    offs = np.concatenate([[0], np.cumsum(IN_SIZES)])
    seg = lambda k: np.arange(offs[k], offs[k + 1])
    pad = lambda n: -np.ones(n, np.int64)
    nq = seg(6).reshape(NSA_HEADS, HEAD)[list(Q_SLOT_HEADS)].reshape(-1)
    cols = np.concatenate([
        seg(0), seg(1), seg(2), seg(3), seg(4), seg(5), pad(LANES - 2 * GDN_HEADS),
        nq, seg(9), seg(11), seg(10), seg(12), seg(13), pad(LANES - 3 * NSA_HEADS),
        seg(14), seg(15), seg(16),
        seg(7), seg(8)])
    assert cols.shape[0] == PROJ_W
    return cols


def _block_diag_ones(n):
    idx = np.arange(n) // HEAD
    return jnp.asarray(idx[:, None] == idx[None, :], MXU_DTYPE)


def _head_expander(first_lane):
    mat = np.zeros((LANES, GDN_WIDTH), np.float32)
    for h in range(GDN_HEADS):
        mat[first_lane + h, h * HEAD:(h + 1) * HEAD] = 1.0
    return jnp.asarray(mat, MXU_DTYPE)


def _compress_weights(w1, w2):
    nslab = 2 * NSA_KV_HEADS
    w1r = w1.reshape(2, 2, CMP_STRIDE, HEAD, HEAD)
    w1bd = jnp.zeros((CMP_STRIDE, nslab, HEAD, 2, nslab, HEAD), F32)
    w2bd = jnp.zeros((nslab, HEAD, nslab, HEAD), F32)
    for s in range(nslab):
        c = s // NSA_KV_HEADS
        w1bd = w1bd.at[:, s, :, :, s, :].set(w1r[c].transpose(1, 2, 0, 3))
        w2bd = w2bd.at[s, :, s, :].set(w2[c])
    return (w1bd.reshape(CMP_STRIDE, P_W, 2 * P_W).astype(MXU_DTYPE),
            w2bd.reshape(P_W, P_W).astype(MXU_DTYPE))


def _rope_lane_tables(positions):
    inv = jnp.float32(ROPE_THETA) ** (-jnp.arange(0, ROT_DIM, 2, dtype=jnp.float32) / ROT_DIM)
    ang = positions.astype(jnp.float32).reshape(-1)[:, None] * inv
    cos, sin = jnp.cos(ang), jnp.sin(ang)
    m = ang.shape[0]
    half = ROT_DIM // 2
    one = jnp.ones((m, HEAD - ROT_DIM), F32)
    zero_h = jnp.zeros((m, half), F32)
    zero_r = jnp.zeros((m, HEAD - ROT_DIM), F32)
    c = jnp.concatenate([cos, cos, one], axis=1)
    s1 = jnp.concatenate([-sin, zero_h, zero_r], axis=1)
    s2 = jnp.concatenate([zero_h, sin, zero_r], axis=1)
    tile2 = lambda t: jnp.concatenate([t, t], axis=1)
    return tile2(c), tile2(s1), tile2(s2)


def _selection_matrix(ncp, nslp):
    ratio = SLC_BLOCK // CMP_STRIDE
    frac = np.minimum(CMP_LEN, SLC_BLOCK - CMP_STRIDE * np.arange(ratio)).astype(np.float64) / CMP_LEN
    mat = np.zeros((ncp, nslp), np.float32)
    c = np.arange(ncp)
    mat[c, c // ratio] = frac[c % ratio]
    nxt = c // ratio + 1
    ok = nxt < nslp
    mat[c[ok], nxt[ok]] += (1.0 - frac[c % ratio])[ok]
    return jnp.asarray(mat, MXU_DTYPE)


def kernel(x, positions, attn_norm, w_in, gdn_conv_w, gdn_a_log, gdn_dt_bias, gdn_norm, nsa_q_norm,
           nsa_k_norm, nsa_cmp_pe, nsa_cmp_w1, nsa_cmp_w2, conv_w, w_out, ffn_norm, w_gate_up, w_down):
    batch, seq, _ = x.shape
    depth = w_in.shape[0]
    m = batch * seq
    assert seq % max(TM_PROJ, TM_FFN, T_PREP, KC) == 0 and (seq // CMP_STRIDE) % LANES == 0
    nb = seq // CMP_STRIDE
    nslp = -(-(seq // SLC_BLOCK) // LANES) * LANES

    cols = _proj_column_map()
    take = jnp.asarray(np.maximum(cols, 0), jnp.int32)
    valid = jnp.asarray(cols >= 0)
    o_rows = np.arange(NSA_WIDTH).reshape(NSA_HEADS, HEAD)[list(Q_SLOT_HEADS)].reshape(-1) + GDN_WIDTH
    wo_rows = jnp.asarray(np.concatenate([np.arange(GDN_WIDTH), o_rows,
                                          np.arange(GDN_WIDTH + NSA_WIDTH, D_MODEL)]), jnp.int32)
    bd128, bd256, bd512 = _block_diag_ones(128), _block_diag_ones(256), _block_diag_ones(512)
    rc, rs1, rs2 = _rope_lane_tables(positions)
    msel = _selection_matrix(nb, nslp)
    lane_pad = lambda v: jnp.zeros((1, LANES), F32).at[0, GDN_HEADS:2 * GDN_HEADS].set(v.astype(F32))

    x2 = x.reshape(m, D_MODEL)
    for l in range(depth):
        w_l = jnp.where(valid[None, :], jnp.take(w_in[l], take, axis=1), 0.0).astype(MXU_DTYPE)
        proj = _in_proj(x2, attn_norm[l].reshape(1, D_MODEL), w_l)

        o_gdn = _gdn(proj, gdn_conv_w[l], lane_pad(gdn_a_log[l]), lane_pad(gdn_dt_bias[l]),
                     jnp.tile(gdn_norm[l], GDN_HEADS).reshape(1, GDN_WIDTH), bd256,
                     _head_expander(0), _head_expander(GDN_HEADS), batch, seq)

        qg = jnp.tile(nsa_q_norm[l], NSA_HEADS).reshape(1, NSA_WIDTH)
        kg = jnp.tile(nsa_k_norm[l, 1:3], (1, NSA_KV_HEADS))
        qn, qr, ksa, kw, vs, vw, gt = _nsa_prep(proj, rc, rs1, rs2, qg, kg, bd512, bd128, seq)

        w1bd, w2bd = _compress_weights(nsa_cmp_w1[l], nsa_cmp_w2[l])
        kvc = _compress(proj, w1bd, nsa_cmp_pe[l].reshape(2, CMP_LEN * HEAD), nsa_cmp_w1[l].astype(MXU_DTYPE),
                        w2bd, jnp.tile(nsa_k_norm[l, 0], NSA_KV_HEADS).reshape(1, NSA_KV_WIDTH), bd128,
                        batch, seq)

        bound = (HEAD ** 0.5) * jnp.max(jnp.abs(nsa_q_norm[l])) * jnp.max(jnp.abs(nsa_k_norm[l]), axis=1)
        cshift = jnp.zeros((1, LANES), F32).at[0, 0:3].set((bound * LOG2E).astype(F32))
        attn_args = (qn, qr, gt, kvc, ksa, vs, kw, vw, msel, cshift)
        o_nsa = lax.cond(
            jnp.max(bound) <= MAX_SOFTMAX_SHIFT,
            lambda a: _nsa_attn(*a, batch, seq, True),
            lambda a: _nsa_attn(*a, batch, seq, False),
            attn_args)

        x2 = _out_ffn(x2, o_gdn, o_nsa, proj, conv_w[l], jnp.take(w_out[l], wo_rows, axis=0).astype(MXU_DTYPE),
                      ffn_norm[l].reshape(1, D_MODEL), w_gate_up[l].astype(MXU_DTYPE),
                      w_down[l].astype(MXU_DTYPE), seq)
    return x2.reshape(batch, seq, D_MODEL)
```

```python
import functools

import numpy as np
import jax
import jax.numpy as jnp
from jax import lax
from jax.experimental import pallas as pl
from jax.experimental.pallas import tpu as pltpu

F32 = jnp.float32
MXU_DTYPE = jnp.bfloat16

D_MODEL = 1024
HEAD = 64
EPS = 1e-6
NEG_INF = -1e30
GDN_HEADS = 4
GDN_WIDTH = 256
GDN_CONV = 4
GDN_CHUNK = 64
NSA_HEADS = 8
NSA_WIDTH = 512
NSA_KV_HEADS = 2
NSA_GROUP = 4
NSA_KV_WIDTH = 128
CMP_STRIDE = 16
CMP_LEN = 32
SLC_BLOCK = 64
N_SELECT = 16
WINDOW = 512
CONV_WIDTH = 256
CONV_K = 3
ROPE_THETA = 500000.0
ROT_DIM = 16
D_FF = 2816
IN_SIZES = (256, 256, 256, 256, 4, 4, 512, 128, 128, 128, 128, 128, 128, 24, 256, 256, 256)
D_IN = sum(IN_SIZES)

LANES = 128
SUBLANES = 8
VMEM_LIMIT = 56 * 1024 * 1024

G_W = 1024 + LANES
N_W = 512 + 4 * 128 + LANES
C_W = 3 * CONV_WIDTH
P_W = 2 * NSA_KV_WIDTH
G_OFF, N_OFF, C_OFF, P_OFF = 0, G_W, G_W + N_W, G_W + N_W + C_W
PROJ_W = G_W + N_W + C_W + P_W
Q_SLOT_HEADS = (0, 4, 1, 5, 2, 6, 3, 7)

SEL_BIG = 16384.0
MAX_SOFTMAX_SHIFT = 40.0
LOG2E = 1.4426950408889634

TM_PROJ = 512
TM_FFN = 512
T_GDN = 512
T_PREP = 512
TQ = 256
TQ_RUNMAX = 128
KC = 1024
FF_CHUNK = 256


def _dot(a, b):
    return jnp.dot(a.astype(MXU_DTYPE), b.astype(MXU_DTYPE), preferred_element_type=F32)


def _dot_nt(a, b):
    return lax.dot_general(a.astype(MXU_DTYPE), b.astype(MXU_DTYPE), (((1,), (1,)), ((), ())),
                           preferred_element_type=F32)


def _dot_tn(a, b):
    return lax.dot_general(a.astype(MXU_DTYPE), b.astype(MXU_DTYPE), (((0,), (0,)), ((), ())),
                           preferred_element_type=F32)


def _split3(x):
    a = x.astype(MXU_DTYPE)
    r = x - a.astype(F32)
    b = r.astype(MXU_DTYPE)
    c = (r - b.astype(F32)).astype(MXU_DTYPE)
    return a, b, c


def _dot_hi_l(x, m):
    a, b, c = _split3(x)
    f = lambda t: jnp.dot(t, m, preferred_element_type=F32)
    return f(a) + f(b) + f(c)


def _dot_hi2_l(x, m):
    a = x.astype(MXU_DTYPE)
    b = (x - a.astype(F32)).astype(MXU_DTYPE)
    return jnp.dot(a, m, preferred_element_type=F32) + jnp.dot(b, m, preferred_element_type=F32)


def _silu(x):
    return x * jax.nn.sigmoid(x)


def _shift_rows(x, halo, s, rows):
    y = pltpu.roll(x, s, 0)
    for r in range(s):
        y = jnp.where(rows == r, halo[SUBLANES - s + r:SUBLANES - s + r + 1, :], y)
    return y


def _in_proj_kernel(x_ref, g_ref, w_ref, o_ref):
    x = x_ref[...]
    ms = jnp.mean(x * x, axis=-1, keepdims=True)
    h = (x * lax.rsqrt(ms + EPS) * g_ref[...]).astype(MXU_DTYPE)
    for c in range(0, PROJ_W, 256):
        o_ref[:, c:c + 256] = jnp.dot(h, w_ref[:, c:c + 256], preferred_element_type=F32)


def _in_proj(x2, gain, w):
    m = x2.shape[0]
    tm = TM_PROJ
    return pl.pallas_call(
        _in_proj_kernel,
        grid=(m // tm,),
        in_specs=[pl.BlockSpec((tm, D_MODEL), lambda i: (i, 0)),
                  pl.BlockSpec((1, D_MODEL), lambda i: (0, 0)),
                  pl.BlockSpec((D_MODEL, PROJ_W), lambda i: (0, 0))],
        out_specs=pl.BlockSpec((tm, PROJ_W), lambda i: (i, 0)),
        out_shape=jax.ShapeDtypeStruct((m, PROJ_W), F32),
        compiler_params=pltpu.CompilerParams(dimension_semantics=("parallel",),
                                             vmem_limit_bytes=VMEM_LIMIT),
        name="in_proj",
    )(x2, gain, w)


def _gdn_kernel(g_ref, gh_ref, cw_ref, alog_ref, dt_ref, gn_ref, bd_ref, eb_ref, eg_ref, o_ref, s_ref, *, tt):
    t_idx = pl.program_id(1)

    @pl.when(t_idx == 0)
    def _():
        s_ref[...] = jnp.zeros_like(s_ref)

    keep = jnp.where(t_idx == 0, 0.0, 1.0)
    rows = lax.broadcasted_iota(jnp.int32, (tt, 1), 0)
    x = g_ref[:, 0:3 * GDN_WIDTH]
    hx = gh_ref[:, 0:3 * GDN_WIDTH] * keep
    w = cw_ref[...]
    y = w[GDN_CONV - 1:GDN_CONV, :] * x
    for s in range(1, GDN_CONV):
        y = y + w[GDN_CONV - 1 - s:GDN_CONV - s, :] * _shift_rows(x, hx, s, rows)
    y = _silu(y)
    q = y[:, 0:GDN_WIDTH]
    k = y[:, GDN_WIDTH:2 * GDN_WIDTH]
    v = y[:, 2 * GDN_WIDTH:3 * GDN_WIDTH]
    bd = bd_ref[...]
    q = q * lax.rsqrt(_dot_hi2_l(q * q, bd) + EPS) * (HEAD ** -0.5)
    k = k * lax.rsqrt(_dot_hi2_l(k * k, bd) + EPS)

    gg = g_ref[:, 4 * GDN_WIDTH:4 * GDN_WIDTH + LANES]
    lane = lax.broadcasted_iota(jnp.int32, (1, LANES), 1)
    xa = gg + dt_ref[...]
    softplus = jnp.maximum(xa, 0.0) + jnp.log1p(jnp.exp(-jnp.abs(xa)))
    g2 = jnp.where(lane < GDN_HEADS, jax.nn.sigmoid(gg), -jnp.exp(alog_ref[...]) * softplus)
    cs = g2.T
    lane_t = lax.broadcasted_iota(jnp.int32, (1, tt), 1) % GDN_CHUNK
    step = 1
    while step < GDN_CHUNK:
        cs = cs + jnp.where(lane_t >= step, pltpu.roll(cs, step, 1), 0.0)
        step *= 2
    gcum_t = cs
    gcum = cs.T
    beta_e = _dot_hi_l(g2, eb_ref[...])
    gcum_e = _dot_hi_l(gcum, eg_ref[...])
    eg_e = jnp.exp(gcum_e)
    kb = k * beta_e
    rv = v * beta_e
    rk = kb * eg_e
    qd = q * eg_e

    nst = GDN_HEADS * GDN_CHUNK
    ri = lax.broadcasted_iota(jnp.int32, (nst, nst), 0)
    ci = lax.broadcasted_iota(jnp.int32, (nst, nst), 1)
    same = (ri // GDN_CHUNK) == (ci // GDN_CHUNK)
    m_tril = same & ((ri % GDN_CHUNK) >= (ci % GDN_CHUNK))
    m_strict = same & ((ri % GDN_CHUNK) > (ci % GDN_CHUNK))
    tile4 = lambda t: jnp.concatenate([t] * GDN_HEADS, axis=0)
    expand = lambda t: tile4(t.astype(MXU_DTYPE)) * bd

    nchunk = tt // GDN_CHUNK
    rss = [slice(n * GDN_CHUNK, (n + 1) * GDN_CHUNK) for n in range(nchunk)]
    g_last = [gcum_e[(n + 1) * GDN_CHUNK - 1:(n + 1) * GDN_CHUNK, :] for n in range(nchunk)]
    rmat, pw, qk = [], [], []
    for n, rs in enumerate(rss):
        g_row = jnp.concatenate([gcum_t[GDN_HEADS + h:GDN_HEADS + h + 1, rs] for h in range(GDN_HEADS)], axis=1)
        decay = jnp.where(m_tril, jnp.exp(jnp.where(m_tril, tile4(gcum_e[rs]) - g_row, 0.0)), 0.0)
        k4 = tile4(k[rs])
        a = jnp.where(m_strict, _dot_nt(expand(kb[rs]), k4) * decay, 0.0)
        qk.append(jnp.where(m_tril, _dot_nt(expand(q[rs]), k4) * decay, 0.0))
        rmat.append(-a)
        pw.append(-a)
    for _ in range(5):
        nxt_pw, nxt_r = [], []
        for r, t in zip(rmat, pw):
            t2 = _dot(t, t)
            nxt_pw.append(t2)
            nxt_r.append(r + t2 + _dot(r, t2))
        pw, rmat = nxt_pw, nxt_r
    u, wm = [], []
    for n, rs in enumerate(rss):
        rv_x = jnp.where(same, tile4(rv[rs]), 0.0)
        rk_x = jnp.where(same, tile4(rk[rs]), 0.0)
        u.append(rv_x + _dot(rmat[n], rv_x))
        wm.append(rk_x + _dot(rmat[n], rk_x))

    st = s_ref[...]
    outs = []
    for n, rs in enumerate(rss):
        kd = k[rs] * jnp.exp(g_last[n] - gcum_e[rs])
        v_new = u[n] - _dot(wm[n], st)
        o_x = _dot(expand(qd[rs]), st) + _dot(qk[n], v_new)
        st = st * jnp.exp(g_last[n]) + _dot_tn(expand(kd), v_new)
        o = o_x[0:GDN_CHUNK]
        for h in range(1, GDN_HEADS):
            o = o + o_x[h * GDN_CHUNK:(h + 1) * GDN_CHUNK]
        outs.append(o)
    s_ref[...] = st
    o_all = jnp.concatenate(outs, axis=0)
    on = o_all * lax.rsqrt(_dot_hi2_l(o_all * o_all, bd) * (1.0 / HEAD) + EPS) * gn_ref[...]
    o_ref[...] = (on * _silu(g_ref[:, 3 * GDN_WIDTH:4 * GDN_WIDTH])).astype(o_ref.dtype)


def _gdn(proj, conv_w, alog_pad, dt_pad, gnorm, bd256, eb, eg, batch, seq):
    tt = T_GDN
    nt = seq // tt
    hb = tt // SUBLANES
    return pl.pallas_call(
        functools.partial(_gdn_kernel, tt=tt),
        grid=(batch, nt),
        in_specs=[pl.BlockSpec((tt, G_W), lambda b, t: (b * nt + t, 0)),
                  pl.BlockSpec((SUBLANES, G_W), lambda b, t: (jnp.maximum((b * nt + t) * hb - 1, 0), 0)),
                  pl.BlockSpec((GDN_CONV, 3 * GDN_WIDTH), lambda b, t: (0, 0)),
                  pl.BlockSpec((1, LANES), lambda b, t: (0, 0)),
                  pl.BlockSpec((1, LANES), lambda b, t: (0, 0)),
                  pl.BlockSpec((1, GDN_WIDTH), lambda b, t: (0, 0)),
                  pl.BlockSpec((GDN_WIDTH, GDN_WIDTH), lambda b, t: (0, 0)),
                  pl.BlockSpec((LANES, GDN_WIDTH), lambda b, t: (0, 0)),
                  pl.BlockSpec((LANES, GDN_WIDTH), lambda b, t: (0, 0))],
        out_specs=pl.BlockSpec((tt, GDN_WIDTH), lambda b, t: (b * nt + t, 0)),
        out_shape=jax.ShapeDtypeStruct((batch * seq, GDN_WIDTH), MXU_DTYPE),
        scratch_shapes=[pltpu.VMEM((GDN_WIDTH, GDN_WIDTH), F32)],
        compiler_params=pltpu.CompilerParams(dimension_semantics=("parallel", "arbitrary"),
                                             vmem_limit_bytes=VMEM_LIMIT),
        name="gdn",
    )(proj, proj, conv_w, alog_pad, dt_pad, gnorm, bd256, eb, eg)


def _rope(x, c, s1, s2):
    wdt = x.shape[1]
    return x * c + pltpu.roll(x, wdt - ROT_DIM // 2, 1) * s1 + pltpu.roll(x, ROT_DIM // 2, 1) * s2


def _nsa_prep_kernel(n_ref, t_ref, ex_ref, cp_ref, qg_ref, kg_ref, bdq_ref, bdk_ref,
                     qn_ref, qr_ref, ksa_ref, kw_ref, vs_ref, vw_ref, gt_ref, *, tt, seq):
    table = t_ref[...]
    c1 = _dot_hi2_l(table, ex_ref[0]) + cp_ref[...]
    s1 = _dot_hi2_l(table, ex_ref[1])
    s2 = _dot_hi2_l(table, ex_ref[2])
    c4 = jnp.concatenate([c1] * 4, axis=1)
    s14 = jnp.concatenate([s1] * 4, axis=1)
    s24 = jnp.concatenate([s2] * 4, axis=1)
    q = n_ref[:, 0:NSA_WIDTH]
    qn = q * lax.rsqrt(_dot_hi2_l(q * q, bdq_ref[...]) * (1.0 / HEAD) + EPS) * qg_ref[...]
    qn = qn * (HEAD ** -0.5 * LOG2E)
    qn_ref[...] = qn.astype(qn_ref.dtype)
    qr_ref[...] = _rope(qn, c4, s14, s24).astype(qr_ref.dtype)
    bdk = bdk_ref[...]
    ks = n_ref[:, 512:640]
    kw = n_ref[:, 640:768]
    ks = ks * lax.rsqrt(_dot_hi2_l(ks * ks, bdk) * (1.0 / HEAD) + EPS) * kg_ref[0:1, :]
    kw = kw * lax.rsqrt(_dot_hi2_l(kw * kw, bdk) * (1.0 / HEAD) + EPS) * kg_ref[1:2, :]
    ksa_ref[:, 0:LANES] = _rope(ks, c1, s1, s2).astype(ksa_ref.dtype)
    rows = (lax.broadcasted_iota(jnp.int32, (tt, LANES), 0) + pl.program_id(0) * tt) % seq
    lane = lax.broadcasted_iota(jnp.int32, (tt, LANES), 1)
    ksa_ref[:, LANES:2 * LANES] = jnp.where((rows // SLC_BLOCK) % LANES == lane, 1.0, 0.0).astype(ksa_ref.dtype)
    kw_ref[...] = _rope(kw, c1, s1, s2).astype(kw_ref.dtype)
    vs_ref[...] = n_ref[:, 768:896].astype(vs_ref.dtype)
    vw_ref[...] = n_ref[:, 896:1024].astype(vw_ref.dtype)
    gt_ref[...] = jax.nn.sigmoid(n_ref[:, 1024:1152])


def _nsa_prep(proj, rope_t, rope_ex, rope_cp, qg, kg, bd512, bd128, seq):
    m = proj.shape[0]
    tt = T_PREP
    row = lambda w: pl.BlockSpec((tt, w), lambda i: (i, 0))
    full = lambda a, b: pl.BlockSpec((a, b), lambda i: (0, 0))
    sds = lambda w, dt: jax.ShapeDtypeStruct((m, w), dt)
    return pl.pallas_call(
        functools.partial(_nsa_prep_kernel, tt=tt, seq=seq),
        grid=(m // tt,),
        in_specs=[pl.BlockSpec((tt, N_W), lambda i: (i, 1)), row(LANES),
                  pl.BlockSpec((3, LANES, LANES), lambda i: (0, 0, 0)), full(1, LANES),
                  full(1, NSA_WIDTH), full(2, LANES), full(NSA_WIDTH, NSA_WIDTH), full(LANES, LANES)],
        out_specs=[row(NSA_WIDTH), row(NSA_WIDTH), row(2 * LANES), row(LANES), row(LANES), row(LANES),
                   row(LANES)],
        out_shape=[sds(NSA_WIDTH, MXU_DTYPE), sds(NSA_WIDTH, MXU_DTYPE), sds(2 * LANES, MXU_DTYPE),
                   sds(LANES, MXU_DTYPE), sds(LANES, MXU_DTYPE), sds(LANES, MXU_DTYPE), sds(LANES, F32)],
        compiler_params=pltpu.CompilerParams(dimension_semantics=("parallel",),
                                             vmem_limit_bytes=VMEM_LIMIT),
        name="nsa_prep",
    )(proj, rope_t, rope_ex, rope_cp, qg, kg, bd512, bd128)


def _compress_kernel(xk_ref, xv_ref, w1bd_ref, pe_ref, w1_ref, w2bd_ref, kg_ref, bd_ref, o_ref, *, nb):
    acc = None
    for t in range(CMP_STRIDE):
        xt = jnp.concatenate([xk_ref[pl.ds(t, nb, stride=CMP_STRIDE), :],
                              xv_ref[pl.ds(t, nb, stride=CMP_STRIDE), :]], axis=1)
        part = _dot(xt, w1bd_ref[t])
        acc = part if acc is None else acc + part
    pe_terms = [_dot(jnp.broadcast_to(pe_ref[c:c + 1, :], (SUBLANES, CMP_LEN * HEAD)), w1_ref[c])[0:1, :]
                for c in range(2)]
    pe_all = jnp.concatenate([pe_terms[0]] * NSA_KV_HEADS + [pe_terms[1]] * NSA_KV_HEADS, axis=1)
    half = 2 * NSA_KV_WIDTH
    pre = acc[:, 0:half] + pltpu.roll(acc[:, half:2 * half], nb - 1, 0) + pe_all
    y = _dot(jax.nn.gelu(pre), w2bd_ref[...])
    rows = lax.broadcasted_iota(jnp.int32, (nb, 1), 0)
    y = jnp.where(rows < nb - 1, y, 0.0)
    yk = y[:, 0:NSA_KV_WIDTH]
    yk = yk * lax.rsqrt(_dot_hi2_l(yk * yk, bd_ref[...]) * (1.0 / HEAD) + EPS) * kg_ref[...]
    o_ref[0, 0] = yk.astype(o_ref.dtype)
    o_ref[1, 0] = y[:, NSA_KV_WIDTH:2 * NSA_KV_WIDTH].astype(o_ref.dtype)


def _compress(proj, w1bd, pe, w1, w2bd, kg0, bd128, batch, seq):
    nb = seq // CMP_STRIDE
    full = lambda *s: pl.BlockSpec(s, lambda b: (0,) * len(s))
    return pl.pallas_call(
        functools.partial(_compress_kernel, nb=nb),
        grid=(batch,),
        in_specs=[pl.BlockSpec((seq, NSA_KV_WIDTH), lambda b: (b, P_OFF // NSA_KV_WIDTH)),
                  pl.BlockSpec((seq, NSA_KV_WIDTH), lambda b: (b, P_OFF // NSA_KV_WIDTH + 1)),
                  full(CMP_STRIDE, P_W, 2 * P_W), full(2, CMP_LEN * HEAD), full(2, CMP_LEN * HEAD, HEAD),
                  full(P_W, P_W), full(1, NSA_KV_WIDTH), full(LANES, LANES)],
        out_specs=pl.BlockSpec((2, 1, nb, NSA_KV_WIDTH), lambda b: (0, b, 0, 0)),
        out_shape=jax.ShapeDtypeStruct((2, batch, nb, NSA_KV_WIDTH), MXU_DTYPE),
        compiler_params=pltpu.CompilerParams(dimension_semantics=("parallel",),
                                             vmem_limit_bytes=VMEM_LIMIT),
        name="compress",
    )(proj, proj, w1bd, pe, w1, w2bd, kg0, bd128)


def _nsa_attn_kernel(*refs, tq, seq, nwb, kc, ncp, nslp, shifted):
    qn_ref, qr_ref, gt_ref, kcmp_ref, vcmp_ref, ksa_ref, vs_ref = refs[:7]
    kw_refs = refs[7:7 + nwb]
    vw_refs = refs[7 + nwb:7 + 2 * nwb]
    msel_ref, cs_ref, egate_ref = refs[7 + 2 * nwb:10 + 2 * nwb]
    o_ref = refs[10 + 2 * nwb]
    bias_scr, qaug_scr, m_scr, l_scr, acc_scr, ow_scr, cand_scr, oc_scr, slc_scr, e_scr = refs[11 + 2 * nwb:]
    nslot = NSA_HEADS
    i = pl.program_id(1)
    c_cmp, c_slc, c_win = cs_ref[0:1, 0:1], cs_ref[0:1, 1:2], cs_ref[0:1, 2:3]
    lane = lax.broadcasted_iota(jnp.int32, (1, LANES), 1)
    lo = lane < HEAD

    def stack(q_ref):
        sl = [q_ref[:, j * LANES:(j + 1) * LANES] for j in range(NSA_GROUP)]
        zero = jnp.zeros_like(sl[0])
        return jnp.concatenate([jnp.where(lo, s, zero) for s in sl] + [jnp.where(lo, zero, s) for s in sl], axis=0)

    tpos = i * tq + lax.broadcasted_iota(jnp.int32, (tq, 1), 0)
    tpos_st = i * tq + lax.broadcasted_iota(jnp.int32, (nslot * tq, 1), 0) % tq
    gates = gt_ref[...]

    qn_st = stack(qn_ref)
    if shifted:
        ccw = min(2 * LANES, ncp)
        n_vis = (i + 1) * (tq // CMP_STRIDE) - 1
        n_cch = (n_vis + ccw - 1) // ccw
        qaug_scr[:, 0:LANES] = qn_st
        l_scr[...] = jnp.zeros(l_scr.shape, F32)
        acc_scr[...] = jnp.zeros(acc_scr.shape, F32)

        def cmp_scores(ch, carry):
            k0 = pl.multiple_of(ch * ccw, ccw)
            s = _dot_nt(qaug_scr[:, 0:LANES], kcmp_ref[0, 0, pl.ds(k0, ccw), :])
            ckey = k0 + lax.broadcasted_iota(jnp.int32, (1, ccw), 1)
            vis = (ckey * CMP_STRIDE + (CMP_LEN - 1)) <= tpos
            cbias = jnp.where(vis, -c_cmp, -SEL_BIG)
            for slot in range(nslot):
                rs = slice(slot * tq, (slot + 1) * tq)
                e = jnp.exp2(s[rs] + cbias)
                e_scr[rs, pl.ds(k0, ccw)] = e
                part = e[:, 0:LANES]
                for t in range(1, ccw // LANES):
                    part = part + e[:, t * LANES:(t + 1) * LANES]
                l_scr[rs, :] += part
            acc_scr[...] += _dot(e_scr[:, pl.ds(k0, ccw)], vcmp_ref[0, 0, pl.ds(k0, ccw), :])
            return carry

        lax.fori_loop(0, n_cch, cmp_scores, 0)
        l_c = jnp.sum(l_scr[...], axis=-1, keepdims=True)
        rinv_c = 1.0 / jnp.where(l_c > 0.0, l_c, 1.0)
        oc_scr[...] = acc_scr[...] * rinv_c
        l_scr[...] = jnp.broadcast_to(rinv_c, l_scr.shape)
        slc_scr[...] = jnp.zeros(slc_scr.shape, F32)

        def cmp_importance(ch, carry):
            k0 = pl.multiple_of(ch * ccw, ccw)
            for g in range(NSA_KV_HEADS):
                ps = None
                for r in range(NSA_GROUP):
                    rs = slice((g * NSA_GROUP + r) * tq, (g * NSA_GROUP + r + 1) * tq)
                    rinv_l = jnp.concatenate([l_scr[rs, :]] * (ccw // LANES), axis=1)
                    p = e_scr[rs, pl.ds(k0, ccw)] * rinv_l
                    ps = p if ps is None else ps + p
                slc_scr[g] += _dot_hi_l(ps, msel_ref[pl.ds(k0, ccw), :])
            return carry

        lax.fori_loop(0, n_cch, cmp_importance, 0)
        slc_rows = [slc_scr[g] for g in range(NSA_KV_HEADS)]
    else:
        ckey = lax.broadcasted_iota(jnp.int32, (1, ncp), 1)
        cmask = (ckey * CMP_STRIDE + (CMP_LEN - 1)) <= tpos
        s_all = _dot_nt(qn_st, kcmp_ref[0, 0])
        psum = [None, None]
        e_parts, rinv_parts = [], []
        for slot in range(nslot):
            s = jnp.where(cmask, s_all[slot * tq:(slot + 1) * tq], NEG_INF)
            e = jnp.where(cmask, jnp.exp2(s - jnp.max(s, axis=-1, keepdims=True)), 0.0)
            l = jnp.sum(e, axis=-1, keepdims=True)
            rinv = 1.0 / jnp.where(l > 0.0, l, 1.0)
            e_parts.append(e.astype(MXU_DTYPE))
            rinv_parts.append(rinv)
            p = e * rinv
            g = slot // NSA_GROUP
            psum[g] = p if psum[g] is None else psum[g] + p
        oc_scr[...] = (_dot(jnp.concatenate(e_parts, axis=0), vcmp_ref[0, 0])
                       * jnp.concatenate(rinv_parts, axis=0))
        slc_rows = [_dot_hi_l(psum[g], msel_ref[...]) for g in range(NSA_KV_HEADS)]

    qr_st = stack(qr_ref)
    kwc = jnp.concatenate([r[...] for r in kw_refs], axis=0)
    vwc = jnp.concatenate([r[...] for r in vw_refs], axis=0)
    sw = _dot_nt(qr_st, kwc)
    if shifted:
        rr = lax.broadcasted_iota(jnp.int32, (tq, tq), 0)
        cc = lax.broadcasted_iota(jnp.int32, (tq, tq), 1)
        parts = []
        for jb in range(nwb):
            shift_b = c_win + jnp.where(i - (nwb - 1) + jb >= 0, 0.0, SEL_BIG)
            blk = jnp.exp2(sw[:, jb * tq:(jb + 1) * tq] - shift_b)
            if jb == 0 or jb == nwb - 1:
                vis = (cc > rr) if jb == 0 else (cc <= rr)
                blk = jnp.concatenate([jnp.where(vis, blk[s * tq:(s + 1) * tq], 0.0) for s in range(nslot)],
                                      axis=0)
            parts.append(blk)
        ew = jnp.concatenate(parts, axis=1)
    else:
        kpos_w = (i - (nwb - 1)) * tq + lax.broadcasted_iota(jnp.int32, (1, nwb * tq), 1)
        dist = tpos_st - kpos_w
        wmask = (dist >= 0) & (dist < WINDOW) & (kpos_w >= 0)
        sw = jnp.where(wmask, sw, NEG_INF)
        ew = jnp.exp2(sw - jnp.max(sw, axis=-1, keepdims=True))
    ow_scr[...] = _dot(ew, vwc) * (1.0 / jnp.sum(ew, axis=-1, keepdims=True))

    jrow = lax.broadcasted_iota(jnp.int32, (nslp, 1), 0)
    jrowf = jrow.astype(F32)
    cur_t = (i * tq + lax.broadcasted_iota(jnp.int32, (1, tq), 1)) // SLC_BLOCK
    forced = (jrow == 0) | (jrow == cur_t) | (jrow == cur_t - 1)
    causal = jrow <= cur_t
    shift_s = c_slc if shifted else 0.0
    n_free = N_SELECT - 3
    to_bias = lambda sel: ((sel - 1.0) * SEL_BIG - shift_s).T.astype(bias_scr.dtype)
    n_bad = None
    for g in range(NSA_KV_HEADS):
        slc = slc_rows[g].T
        cand = jnp.where(causal & jnp.logical_not(forced), slc, -1.0)
        cand_scr[g] = cand
        c = cand
        for _ in range(n_free):
            c = jnp.where(c == jnp.max(c, axis=0, keepdims=True), -2.0, c)
        picked = (c == -2.0) & (cand >= 0.0)
        n_picked = jnp.sum(jnp.where(picked, 1.0, 0.0), axis=0, keepdims=True)
        n_real = jnp.sum(jnp.where(cand >= 0.0, 1.0, 0.0), axis=0, keepdims=True)
        bad = jnp.where(n_picked == jnp.minimum(n_real, float(n_free)), 0.0, 1.0)
        n_bad = bad if n_bad is None else n_bad + bad
        bias_scr[g] = to_bias(jnp.where(forced | picked, 1.0, 0.0))

    @pl.when(jnp.max(n_bad) > 0.0)
    def _():
        for g in range(NSA_KV_HEADS):
            cand = cand_scr[g]
            sel = jnp.where(forced, 1.0, 0.0)
            for _ in range(n_free):
                mx = jnp.max(cand, axis=0, keepdims=True)
                first = jnp.min(jnp.where(cand == mx, jrowf, float(nslp)), axis=0, keepdims=True)
                hit = jrowf == first
                sel = jnp.where(hit, 1.0, sel)
                cand = jnp.where(hit, -2.0, cand)
            bias_scr[g] = to_bias(jnp.where(causal, sel, 0.0))

    qaug_scr[:, 0:LANES] = qr_st
    if not shifted:
        m_scr[...] = jnp.full(m_scr.shape, NEG_INF, F32)
    l_scr[...] = jnp.zeros(l_scr.shape, F32)
    acc_scr[...] = jnp.zeros(acc_scr.shape, F32)
    group_keys = LANES * SLC_BLOCK

    def scores(key0, width):
        @pl.when(key0 % group_keys == 0)
        def _():
            off = pl.multiple_of((key0 // group_keys) * LANES, LANES)
            b0 = bias_scr[0, :, pl.ds(off, LANES)]
            b1 = bias_scr[1, :, pl.ds(off, LANES)]
            qaug_scr[:, LANES:2 * LANES] = jnp.concatenate([b0] * NSA_GROUP + [b1] * NSA_GROUP, axis=0)

        return _dot_nt(qaug_scr[...], ksa_ref[pl.ds(pl.multiple_of(key0, width), width), :])

    def accumulate(s, key0, width):
        k0 = pl.multiple_of(key0, width)
        if shifted:
            p = jnp.exp2(s)
            psum_l = p[:, 0:LANES]
            for t in range(1, width // LANES):
                psum_l = psum_l + p[:, t * LANES:(t + 1) * LANES]
            l_scr[...] += psum_l
            acc_scr[...] += _dot(p, vs_ref[pl.ds(k0, width), :])
        else:
            m_old = m_scr[...]
            m_new = jnp.maximum(m_old, jnp.max(s, axis=-1, keepdims=True))
            alpha = jnp.exp2(m_old - m_new)
            p = jnp.exp2(s - m_new[:, 0:1])
            l_scr[...] = alpha * l_scr[...] + jnp.sum(p, axis=-1, keepdims=True)
            acc_scr[...] = alpha * acc_scr[...] + _dot(p, vs_ref[pl.ds(k0, width), :])
            m_scr[...] = m_new

    n_full = (i * tq) // kc
    n_sub = (i * tq - n_full * kc) // tq

    def body_full(c, carry):
        accumulate(scores(c * kc, kc), c * kc, kc)
        return carry

    lax.fori_loop(0, n_full, body_full, 0)

    def body_sub(j, carry):
        key0 = n_full * kc + j * tq
        accumulate(scores(key0, tq), key0, tq)
        return carry

    lax.fori_loop(0, n_sub, body_sub, 0)
    col = lax.broadcasted_iota(jnp.int32, (1, tq), 1)
    accumulate(jnp.where(col <= tpos_st - i * tq, scores(i * tq, tq), -SEL_BIG), i * tq, tq)
    if shifted:
        o_s = acc_scr[...] * (1.0 / jnp.sum(l_scr[...], axis=-1, keepdims=True))
    else:
        o_s = acc_scr[...] / l_scr[...]
    o_w = ow_scr[...]

    gate_e = _dot_hi2_l(gates, egate_ref[...])
    for j in range(NSA_GROUP):
        r0 = slice(j * tq, (j + 1) * tq)
        r1 = slice((NSA_GROUP + j) * tq, (NSA_GROUP + j + 1) * tq)
        out = None
        for br, o_b in enumerate((oc_scr[...], o_s, o_w)):
            lanes = slice(br * NSA_WIDTH + j * LANES, br * NSA_WIDTH + (j + 1) * LANES)
            term = gate_e[:, lanes] * jnp.where(lo, o_b[r0], o_b[r1])
            out = term if out is None else out + term
        o_ref[:, j * LANES:(j + 1) * LANES] = out.astype(o_ref.dtype)


def _gate_expander():
    mat = np.zeros((LANES, 3 * NSA_WIDTH), np.float32)
    for br in range(3):
        for j in range(NSA_GROUP):
            for g in range(NSA_KV_HEADS):
                head = g * NSA_GROUP + j
                c0 = br * NSA_WIDTH + j * LANES + g * HEAD
                mat[3 * head + br, c0:c0 + HEAD] = 1.0
    return jnp.asarray(mat, MXU_DTYPE)


def _nsa_attn(qn, qr, gt, kvc, ksa, vs, kw, vw, msel, cshift, batch, seq, shifted):
    tq, kc = (TQ if shifted else TQ_RUNMAX), KC
    nq = seq // tq
    nwb = WINDOW // tq + 1
    ncp = seq // CMP_STRIDE
    nslp = msel.shape[1]
    row = lambda w: pl.BlockSpec((tq, w), lambda b, i: (b * nq + i, 0))
    once = pl.Buffered(1)
    win = lambda jb: pl.BlockSpec((tq, LANES), lambda b, i: (b * nq + jnp.maximum(i - (nwb - 1) + jb, 0), 0))
    in_specs = ([row(NSA_WIDTH), row(NSA_WIDTH), row(LANES),
                 pl.BlockSpec((1, 1, ncp, LANES), lambda b, i: (0, b, 0, 0), pipeline_mode=once),
                 pl.BlockSpec((1, 1, ncp, LANES), lambda b, i: (1, b, 0, 0), pipeline_mode=once),
                 pl.BlockSpec((seq, 2 * LANES), lambda b, i: (b, 0), pipeline_mode=once),
                 pl.BlockSpec((seq, LANES), lambda b, i: (b, 0), pipeline_mode=once)]
                + [win(jb) for jb in range(nwb)] + [win(jb) for jb in range(nwb)]
                + [pl.BlockSpec((ncp, nslp), lambda b, i: (0, 0), pipeline_mode=once),
                   pl.BlockSpec((1, LANES), lambda b, i: (0, 0)),
                   pl.BlockSpec((LANES, 3 * NSA_WIDTH), lambda b, i: (0, 0), pipeline_mode=once)])
    return pl.pallas_call(
        functools.partial(_nsa_attn_kernel, tq=tq, seq=seq, nwb=nwb, kc=kc, ncp=ncp, nslp=nslp,
                          shifted=shifted),
        grid=(batch, nq),
        in_specs=in_specs,
        out_specs=row(NSA_WIDTH),
        out_shape=jax.ShapeDtypeStruct((batch * seq, NSA_WIDTH), MXU_DTYPE),
        scratch_shapes=[pltpu.VMEM((NSA_KV_HEADS, tq, nslp), MXU_DTYPE),
                        pltpu.VMEM((NSA_HEADS * tq, 2 * LANES), MXU_DTYPE),
                        pltpu.VMEM((NSA_HEADS * tq, LANES), F32),
                        pltpu.VMEM((NSA_HEADS * tq, LANES), F32),
                        pltpu.VMEM((NSA_HEADS * tq, LANES), F32),
                        pltpu.VMEM((NSA_HEADS * tq, LANES), F32),
                        pltpu.VMEM((NSA_KV_HEADS, nslp, tq), F32),
                        pltpu.VMEM((NSA_HEADS * tq, LANES), F32),
                        pltpu.VMEM((NSA_KV_HEADS, tq, nslp), F32),
                        pltpu.VMEM((NSA_HEADS * tq, ncp) if shifted else (SUBLANES, LANES), F32)],
        compiler_params=pltpu.CompilerParams(dimension_semantics=("parallel", "arbitrary"),
                                             vmem_limit_bytes=VMEM_LIMIT),
        name="nsa_attn" if shifted else "nsa_attn_runmax",
    )(qn, qr, gt, kvc, kvc, ksa, vs, *([kw] * nwb), *([vw] * nwb), msel, cshift, _gate_expander())


def _out_ffn_kernel(x_ref, og_ref, on_ref, c_ref, ch_ref, cw_ref, wo_ref, fg_ref, wgu_ref, wd_ref, o_ref,
                    *, tm, seq):
    i = pl.program_id(0)
    keep = jnp.where((i * tm) % seq == 0, 0.0, 1.0)
    rows = lax.broadcasted_iota(jnp.int32, (tm, 1), 0)
    u = c_ref[:, CONV_WIDTH:2 * CONV_WIDTH] * c_ref[:, 2 * CONV_WIDTH:3 * CONV_WIDTH]
    hu = ch_ref[:, CONV_WIDTH:2 * CONV_WIDTH] * ch_ref[:, 2 * CONV_WIDTH:3 * CONV_WIDTH] * keep
    w = cw_ref[...]
    conv = w[CONV_K - 1:CONV_K, :] * u
    for s in range(1, CONV_K):
        conv = conv + w[CONV_K - 1 - s:CONV_K - s, :] * _shift_rows(u, hu, s, rows)
    oc = c_ref[:, 0:CONV_WIDTH] * conv
    x1 = (x_ref[...] + _dot(og_ref[...], wo_ref[0:GDN_WIDTH, :])
          + _dot(on_ref[...], wo_ref[GDN_WIDTH:GDN_WIDTH + NSA_WIDTH, :])
          + _dot(oc, wo_ref[GDN_WIDTH + NSA_WIDTH:, :]))
    ms = jnp.mean(x1 * x1, axis=-1, keepdims=True)
    h2 = (x1 * lax.rsqrt(ms + EPS) * fg_ref[...]).astype(MXU_DTYPE)
    o_ref[...] = x1
    for c0 in range(0, D_FF, FF_CHUNK):
        gate = jnp.dot(h2, wgu_ref[:, c0:c0 + FF_CHUNK], preferred_element_type=F32)
        up = jnp.dot(h2, wgu_ref[:, D_FF + c0:D_FF + c0 + FF_CHUNK], preferred_element_type=F32)
        o_ref[...] += _dot(_silu(gate) * up, wd_ref[c0:c0 + FF_CHUNK, :])


def _out_ffn(x2, o_gdn, o_nsa, proj, conv_w, w_out, fgain, wgu, wd, seq):
    m = x2.shape[0]
    tm = TM_FFN
    hb = tm // SUBLANES
    full = lambda a, b: pl.BlockSpec((a, b), lambda i: (0, 0), pipeline_mode=pl.Buffered(1))
    return pl.pallas_call(
        functools.partial(_out_ffn_kernel, tm=tm, seq=seq),
        grid=(m // tm,),
        in_specs=[pl.BlockSpec((tm, D_MODEL), lambda i: (i, 0)),
                  pl.BlockSpec((tm, GDN_WIDTH), lambda i: (i, 0)),
                  pl.BlockSpec((tm, NSA_WIDTH), lambda i: (i, 0)),
                  pl.BlockSpec((tm, C_W), lambda i: (i, C_OFF // C_W)),
                  pl.BlockSpec((SUBLANES, C_W), lambda i: (jnp.maximum(i * hb - 1, 0), C_OFF // C_W)),
                  full(CONV_K, CONV_WIDTH), full(D_MODEL, D_MODEL), full(1, D_MODEL),
                  full(D_MODEL, 2 * D_FF), full(D_FF, D_MODEL)],
        out_specs=pl.BlockSpec((tm, D_MODEL), lambda i: (i, 0)),
        out_shape=jax.ShapeDtypeStruct((m, D_MODEL), F32),
        compiler_params=pltpu.CompilerParams(dimension_semantics=("parallel",),
                                             vmem_limit_bytes=VMEM_LIMIT),
        name="out_ffn",
    )(x2, o_gdn, o_nsa, proj, proj, conv_w, w_out, fgain, wgu, wd)


def _proj_column_map():
    offs = np.concatenate([[0], np.cumsum(IN_SIZES)])
    seg = lambda k: np.arange(offs[k], offs[k + 1])
    pad = lambda n: -np.ones(n, np.int64)
    nq = seg(6).reshape(NSA_HEADS, HEAD)[list(Q_SLOT_HEADS)].reshape(-1)
    cols = np.concatenate([
        seg(0), seg(1), seg(2), seg(3), seg(4), seg(5), pad(LANES - 2 * GDN_HEADS),
        nq, seg(9), seg(11), seg(10), seg(12), seg(13), pad(LANES - 3 * NSA_HEADS),
        seg(14), seg(15), seg(16),
        seg(7), seg(8)])
    assert cols.shape[0] == PROJ_W
    return cols


def _block_diag_ones(n):
    idx = np.arange(n) // HEAD
    return jnp.asarray(idx[:, None] == idx[None, :], MXU_DTYPE)


def _head_expander(first_lane):
    mat = np.zeros((LANES, GDN_WIDTH), np.float32)
    for h in range(GDN_HEADS):
        mat[first_lane + h, h * HEAD:(h + 1) * HEAD] = 1.0
    return jnp.asarray(mat, MXU_DTYPE)


def _compress_weights(w1, w2):
    nslab = 2 * NSA_KV_HEADS
    w1r = w1.reshape(2, 2, CMP_STRIDE, HEAD, HEAD)
    w1bd = jnp.zeros((CMP_STRIDE, nslab, HEAD, 2, nslab, HEAD), F32)
    w2bd = jnp.zeros((nslab, HEAD, nslab, HEAD), F32)
    for s in range(nslab):
        c = s // NSA_KV_HEADS
        w1bd = w1bd.at[:, s, :, :, s, :].set(w1r[c].transpose(1, 2, 0, 3))
        w2bd = w2bd.at[s, :, s, :].set(w2[c])
    return (w1bd.reshape(CMP_STRIDE, P_W, 2 * P_W).astype(MXU_DTYPE),
            w2bd.reshape(P_W, P_W).astype(MXU_DTYPE))


def _rope_table(positions):
    half = ROT_DIM // 2
    inv = jnp.float32(ROPE_THETA) ** (-jnp.arange(0, ROT_DIM, 2, dtype=jnp.float32) / ROT_DIM)
    inv_l = jnp.concatenate([inv, inv, jnp.zeros((LANES - ROT_DIM,), F32)])
    ang = positions.astype(jnp.float32).reshape(-1)[:, None] * inv_l[None, :]
    lane = jnp.arange(LANES)[None, :]
    return jnp.where(lane < half, jnp.cos(ang), jnp.where(lane < ROT_DIM, jnp.sin(ang), 0.0))


def _rope_expanders():
    half = ROT_DIM // 2
    ex = np.zeros((3, LANES, LANES), np.float32)
    cpat = np.zeros((1, LANES), np.float32)
    for j in range(LANES):
        d = j % HEAD
        if d < ROT_DIM:
            ex[0, d % half, j] = 1.0
        else:
            cpat[0, j] = 1.0
        if d < half:
            ex[1, half + d, j] = -1.0
        elif d < ROT_DIM:
            ex[2, half + (d - half), j] = 1.0
    return jnp.asarray(ex, MXU_DTYPE), jnp.asarray(cpat, F32)


def _selection_matrix(ncp, nslp):
    ratio = SLC_BLOCK // CMP_STRIDE
    frac = np.minimum(CMP_LEN, SLC_BLOCK - CMP_STRIDE * np.arange(ratio)).astype(np.float64) / CMP_LEN
    mat = np.zeros((ncp, nslp), np.float32)
    c = np.arange(ncp)
    mat[c, c // ratio] = frac[c % ratio]
    nxt = c // ratio + 1
    ok = nxt < nslp
    mat[c[ok], nxt[ok]] += (1.0 - frac[c % ratio])[ok]
    return jnp.asarray(mat, MXU_DTYPE)


def kernel(x, positions, attn_norm, w_in, gdn_conv_w, gdn_a_log, gdn_dt_bias, gdn_norm, nsa_q_norm,
           nsa_k_norm, nsa_cmp_pe, nsa_cmp_w1, nsa_cmp_w2, conv_w, w_out, ffn_norm, w_gate_up, w_down):
    batch, seq, _ = x.shape
    depth = w_in.shape[0]
    m = batch * seq
    assert seq % max(TM_PROJ, TM_FFN, T_PREP, KC) == 0 and (seq // CMP_STRIDE) % LANES == 0
    nb = seq // CMP_STRIDE
    nslp = -(-(seq // SLC_BLOCK) // LANES) * LANES

    cols = _proj_column_map()
    take = jnp.asarray(np.maximum(cols, 0), jnp.int32)
    valid = jnp.asarray(cols >= 0)
    o_rows = np.arange(NSA_WIDTH).reshape(NSA_HEADS, HEAD)[list(Q_SLOT_HEADS)].reshape(-1) + GDN_WIDTH
    wo_rows = jnp.asarray(np.concatenate([np.arange(GDN_WIDTH), o_rows,
                                          np.arange(GDN_WIDTH + NSA_WIDTH, D_MODEL)]), jnp.int32)
    bd128, bd256, bd512 = _block_diag_ones(128), _block_diag_ones(256), _block_diag_ones(512)
    rope_t = _rope_table(positions)
    rope_ex, rope_cp = _rope_expanders()
    msel = _selection_matrix(nb, nslp)
    lane_pad = lambda v: jnp.zeros((1, LANES), F32).at[0, GDN_HEADS:2 * GDN_HEADS].set(v.astype(F32))

    x2 = x.reshape(m, D_MODEL)
    for l in range(depth):
        w_l = jnp.where(valid[None, :], jnp.take(w_in[l], take, axis=1), 0.0).astype(MXU_DTYPE)
        proj = _in_proj(x2, attn_norm[l].reshape(1, D_MODEL), w_l)

        o_gdn = _gdn(proj, gdn_conv_w[l], lane_pad(gdn_a_log[l]), lane_pad(gdn_dt_bias[l]),
                     jnp.tile(gdn_norm[l], GDN_HEADS).reshape(1, GDN_WIDTH), bd256,
                     _head_expander(0), _head_expander(GDN_HEADS), batch, seq)

        qg = jnp.tile(nsa_q_norm[l], NSA_HEADS).reshape(1, NSA_WIDTH)
        kg = jnp.tile(nsa_k_norm[l, 1:3], (1, NSA_KV_HEADS))
        qn, qr, ksa, kw, vs, vw, gt = _nsa_prep(proj, rope_t, rope_ex, rope_cp, qg, kg, bd512, bd128, seq)

        w1bd, w2bd = _compress_weights(nsa_cmp_w1[l], nsa_cmp_w2[l])
        kvc = _compress(proj, w1bd, nsa_cmp_pe[l].reshape(2, CMP_LEN * HEAD), nsa_cmp_w1[l].astype(MXU_DTYPE),
                        w2bd, jnp.tile(nsa_k_norm[l, 0], NSA_KV_HEADS).reshape(1, NSA_KV_WIDTH), bd128,
                        batch, seq)

        bound = (HEAD ** 0.5) * jnp.max(jnp.abs(nsa_q_norm[l])) * jnp.max(jnp.abs(nsa_k_norm[l]), axis=1)
        cshift = jnp.zeros((1, LANES), F32).at[0, 0:3].set((bound * LOG2E).astype(F32))
        attn_args = (qn, qr, gt, kvc, ksa, vs, kw, vw, msel, cshift)
        o_nsa = lax.cond(
            jnp.max(bound) <= MAX_SOFTMAX_SHIFT,
            lambda a: _nsa_attn(*a, batch, seq, True),
            lambda a: _nsa_attn(*a, batch, seq, False),
            attn_args)

        x2 = _out_ffn(x2, o_gdn, o_nsa, proj, conv_w[l], jnp.take(w_out[l], wo_rows, axis=0).astype(MXU_DTYPE),
                      ffn_norm[l].reshape(1, D_MODEL), w_gate_up[l].astype(MXU_DTYPE),
                      w_down[l].astype(MXU_DTYPE), seq)
    return x2.reshape(batch, seq, D_MODEL)
```

```python
import functools

import numpy as np
import jax
import jax.numpy as jnp
from jax import lax
from jax.experimental import pallas as pl
from jax.experimental.pallas import tpu as pltpu

F32 = jnp.float32
MXU_DTYPE = jnp.bfloat16

D_MODEL = 1024
HEAD = 64
EPS = 1e-6
NEG_INF = -1e30
GDN_HEADS = 4
GDN_WIDTH = 256
GDN_CONV = 4
GDN_CHUNK = 64
NSA_HEADS = 8
NSA_WIDTH = 512
NSA_KV_HEADS = 2
NSA_GROUP = 4
NSA_KV_WIDTH = 128
CMP_STRIDE = 16
CMP_LEN = 32
SLC_BLOCK = 64
N_SELECT = 16
WINDOW = 512
CONV_WIDTH = 256
CONV_K = 3
ROPE_THETA = 500000.0
ROT_DIM = 16
D_FF = 2816
IN_SIZES = (256, 256, 256, 256, 4, 4, 512, 128, 128, 128, 128, 128, 128, 24, 256, 256, 256)
D_IN = sum(IN_SIZES)

LANES = 128
SUBLANES = 8
VMEM_LIMIT = 56 * 1024 * 1024

G_W = 1024 + LANES
N_W = 512 + 4 * 128 + LANES
C_W = 3 * CONV_WIDTH
P_W = 2 * NSA_KV_WIDTH
G_OFF, N_OFF, C_OFF, P_OFF = 0, G_W, G_W + N_W, G_W + N_W + C_W
PROJ_W = G_W + N_W + C_W + P_W
Q_SLOT_HEADS = (0, 4, 1, 5, 2, 6, 3, 7)
N_KS, N_KW, N_VS, N_VW, N_GATE = (NSA_WIDTH + k * LANES for k in range(5))

SEL_BIG = 16384.0
MAX_SOFTMAX_SHIFT = 40.0
LOG2E = 1.4426950408889634

TM_PROJ = 512
TM_FFN = 512
T_GDN = 512
T_PREP = 512
TQ = 256
TQ_RUNMAX = 128
KC = 1024
FF_CHUNK = 256


def _dot(a, b):
    return jnp.dot(a.astype(MXU_DTYPE), b.astype(MXU_DTYPE), preferred_element_type=F32)


def _dot_nt(a, b):
    return lax.dot_general(a.astype(MXU_DTYPE), b.astype(MXU_DTYPE), (((1,), (1,)), ((), ())),
                           preferred_element_type=F32)


def _dot_tn(a, b):
    return lax.dot_general(a.astype(MXU_DTYPE), b.astype(MXU_DTYPE), (((0,), (0,)), ((), ())),
                           preferred_element_type=F32)


def _split3(x):
    a = x.astype(MXU_DTYPE)
    r = x - a.astype(F32)
    b = r.astype(MXU_DTYPE)
    c = (r - b.astype(F32)).astype(MXU_DTYPE)
    return a, b, c


def _dot_hi_l(x, m):
    a, b, c = _split3(x)
    f = lambda t: jnp.dot(t, m, preferred_element_type=F32)
    return f(a) + f(b) + f(c)


def _dot_hi2_l(x, m):
    a = x.astype(MXU_DTYPE)
    b = (x - a.astype(F32)).astype(MXU_DTYPE)
    return jnp.dot(a, m, preferred_element_type=F32) + jnp.dot(b, m, preferred_element_type=F32)


def _silu(x):
    return x * jax.nn.sigmoid(x)


def _shift_rows(x, halo, s, rows):
    y = pltpu.roll(x, s, 0)
    for r in range(s):
        y = jnp.where(rows == r, halo[SUBLANES - s + r:SUBLANES - s + r + 1, :], y)
    return y


def _in_proj_kernel(x_ref, g_ref, w_ref, o_ref):
    x = x_ref[...]
    ms = jnp.mean(x * x, axis=-1, keepdims=True)
    h = (x * lax.rsqrt(ms + EPS) * g_ref[...]).astype(MXU_DTYPE)
    for c in range(0, PROJ_W, 256):
        o_ref[:, c:c + 256] = jnp.dot(h, w_ref[:, c:c + 256], preferred_element_type=F32)


def _in_proj(x2, gain, w):
    m = x2.shape[0]
    tm = TM_PROJ
    return pl.pallas_call(
        _in_proj_kernel,
        grid=(m // tm,),
        in_specs=[pl.BlockSpec((tm, D_MODEL), lambda i: (i, 0)),
                  pl.BlockSpec((1, D_MODEL), lambda i: (0, 0)),
                  pl.BlockSpec((D_MODEL, PROJ_W), lambda i: (0, 0))],
        out_specs=pl.BlockSpec((tm, PROJ_W), lambda i: (i, 0)),
        out_shape=jax.ShapeDtypeStruct((m, PROJ_W), F32),
        compiler_params=pltpu.CompilerParams(dimension_semantics=("parallel",),
                                             vmem_limit_bytes=VMEM_LIMIT),
        name="in_proj",
    )(x2, gain, w)


def _gdn_kernel(g_ref, gh_ref, cw_ref, alog_ref, dt_ref, gn_ref, bd_ref, eb_ref, eg_ref, o_ref, s_ref, *, tt):
    t_idx = pl.program_id(1)

    @pl.when(t_idx == 0)
    def _():
        s_ref[...] = jnp.zeros_like(s_ref)

    keep = jnp.where(t_idx == 0, 0.0, 1.0)
    rows = lax.broadcasted_iota(jnp.int32, (tt, 1), 0)
    x = g_ref[:, 0:3 * GDN_WIDTH]
    hx = gh_ref[:, 0:3 * GDN_WIDTH] * keep
    w = cw_ref[...]
    y = w[GDN_CONV - 1:GDN_CONV, :] * x
    for s in range(1, GDN_CONV):
        y = y + w[GDN_CONV - 1 - s:GDN_CONV - s, :] * _shift_rows(x, hx, s, rows)
    y = _silu(y)
    q = y[:, 0:GDN_WIDTH]
    k = y[:, GDN_WIDTH:2 * GDN_WIDTH]
    v = y[:, 2 * GDN_WIDTH:3 * GDN_WIDTH]
    bd = bd_ref[...]
    q = q * lax.rsqrt(_dot_hi2_l(q * q, bd) + EPS) * (HEAD ** -0.5)
    k = k * lax.rsqrt(_dot_hi2_l(k * k, bd) + EPS)

    gg = g_ref[:, 4 * GDN_WIDTH:4 * GDN_WIDTH + LANES]
    lane = lax.broadcasted_iota(jnp.int32, (1, LANES), 1)
    xa = gg + dt_ref[...]
    softplus = jnp.maximum(xa, 0.0) + jnp.log1p(jnp.exp(-jnp.abs(xa)))
    g2 = jnp.where(lane < GDN_HEADS, jax.nn.sigmoid(gg), -jnp.exp(alog_ref[...]) * softplus)
    cs = g2.T
    lane_t = lax.broadcasted_iota(jnp.int32, (1, tt), 1) % GDN_CHUNK
    step = 1
    while step < GDN_CHUNK:
        cs = cs + jnp.where(lane_t >= step, pltpu.roll(cs, step, 1), 0.0)
        step *= 2
    gcum_t = cs
    gcum = cs.T
    beta_e = _dot_hi_l(g2, eb_ref[...])
    gcum_e = _dot_hi_l(gcum, eg_ref[...])
    eg_e = jnp.exp(gcum_e)
    kb = k * beta_e
    rv = v * beta_e
    rk = kb * eg_e
    qd = q * eg_e

    nst = GDN_HEADS * GDN_CHUNK
    ri = lax.broadcasted_iota(jnp.int32, (nst, nst), 0)
    ci = lax.broadcasted_iota(jnp.int32, (nst, nst), 1)
    same = (ri // GDN_CHUNK) == (ci // GDN_CHUNK)
    m_tril = same & ((ri % GDN_CHUNK) >= (ci % GDN_CHUNK))
    m_strict = same & ((ri % GDN_CHUNK) > (ci % GDN_CHUNK))
    tile4 = lambda t: jnp.concatenate([t] * GDN_HEADS, axis=0)
    expand = lambda t: tile4(t.astype(MXU_DTYPE)) * bd

    nchunk = tt // GDN_CHUNK
    rss = [slice(n * GDN_CHUNK, (n + 1) * GDN_CHUNK) for n in range(nchunk)]
    g_last = [gcum_e[(n + 1) * GDN_CHUNK - 1:(n + 1) * GDN_CHUNK, :] for n in range(nchunk)]
    rmat, pw, qk = [], [], []
    for n, rs in enumerate(rss):
        g_row = jnp.concatenate([gcum_t[GDN_HEADS + h:GDN_HEADS + h + 1, rs] for h in range(GDN_HEADS)], axis=1)
        decay = jnp.where(m_tril, jnp.exp(jnp.where(m_tril, tile4(gcum_e[rs]) - g_row, 0.0)), 0.0)
        k4 = tile4(k[rs])
        a = jnp.where(m_strict, _dot_nt(expand(kb[rs]), k4) * decay, 0.0)
        qk.append(jnp.where(m_tril, _dot_nt(expand(q[rs]), k4) * decay, 0.0))
        rmat.append(-a)
        pw.append(-a)
    for _ in range(GDN_CHUNK.bit_length() - 2):
        nxt_pw, nxt_r = [], []
        for r, t in zip(rmat, pw):
            t2 = _dot(t, t)
            nxt_pw.append(t2)
            nxt_r.append(r + t2 + _dot(r, t2))
        pw, rmat = nxt_pw, nxt_r
    u, wm = [], []
    for n, rs in enumerate(rss):
        rv_x = jnp.where(same, tile4(rv[rs]), 0.0)
        rk_x = jnp.where(same, tile4(rk[rs]), 0.0)
        u.append(rv_x + _dot(rmat[n], rv_x))
        wm.append(rk_x + _dot(rmat[n], rk_x))

    st = s_ref[...]
    outs = []
    for n, rs in enumerate(rss):
        kd = k[rs] * jnp.exp(g_last[n] - gcum_e[rs])
        v_new = u[n] - _dot(wm[n], st)
        o_x = _dot(expand(qd[rs]), st) + _dot(qk[n], v_new)
        st = st * jnp.exp(g_last[n]) + _dot_tn(expand(kd), v_new)
        o = o_x[0:GDN_CHUNK]
        for h in range(1, GDN_HEADS):
            o = o + o_x[h * GDN_CHUNK:(h + 1) * GDN_CHUNK]
        outs.append(o)
    s_ref[...] = st
    o_all = jnp.concatenate(outs, axis=0)
    on = o_all * lax.rsqrt(_dot_hi2_l(o_all * o_all, bd) * (1.0 / HEAD) + EPS) * gn_ref[...]
    o_ref[...] = (on * _silu(g_ref[:, 3 * GDN_WIDTH:4 * GDN_WIDTH])).astype(o_ref.dtype)


def _gdn(proj, conv_w, alog_pad, dt_pad, gnorm, bd256, eb, eg, batch, seq):
    tt = T_GDN
    nt = seq // tt
    hb = tt // SUBLANES
    return pl.pallas_call(
        functools.partial(_gdn_kernel, tt=tt),
        grid=(batch, nt),
        in_specs=[pl.BlockSpec((tt, G_W), lambda b, t: (b * nt + t, 0)),
                  pl.BlockSpec((SUBLANES, G_W), lambda b, t: (jnp.maximum((b * nt + t) * hb - 1, 0), 0)),
                  pl.BlockSpec((GDN_CONV, 3 * GDN_WIDTH), lambda b, t: (0, 0)),
                  pl.BlockSpec((1, LANES), lambda b, t: (0, 0)),
                  pl.BlockSpec((1, LANES), lambda b, t: (0, 0)),
                  pl.BlockSpec((1, GDN_WIDTH), lambda b, t: (0, 0)),
                  pl.BlockSpec((GDN_WIDTH, GDN_WIDTH), lambda b, t: (0, 0)),
                  pl.BlockSpec((LANES, GDN_WIDTH), lambda b, t: (0, 0)),
                  pl.BlockSpec((LANES, GDN_WIDTH), lambda b, t: (0, 0))],
        out_specs=pl.BlockSpec((tt, GDN_WIDTH), lambda b, t: (b * nt + t, 0)),
        out_shape=jax.ShapeDtypeStruct((batch * seq, GDN_WIDTH), MXU_DTYPE),
        scratch_shapes=[pltpu.VMEM((GDN_WIDTH, GDN_WIDTH), F32)],
        compiler_params=pltpu.CompilerParams(dimension_semantics=("parallel", "arbitrary"),
                                             vmem_limit_bytes=VMEM_LIMIT),
        name="gdn",
    )(proj, proj, conv_w, alog_pad, dt_pad, gnorm, bd256, eb, eg)


def _rope(x, c, s1, s2):
    wdt = x.shape[1]
    return x * c + pltpu.roll(x, wdt - ROT_DIM // 2, 1) * s1 + pltpu.roll(x, ROT_DIM // 2, 1) * s2


def _nsa_prep_kernel(n_ref, t_ref, ex_ref, cp_ref, qg_ref, kg_ref, bdq_ref, bdk_ref,
                     qn_ref, qr_ref, ksa_ref, kw_ref, vs_ref, vw_ref, gt_ref, *, tt, seq):
    table = t_ref[...]
    c1 = _dot_hi2_l(table, ex_ref[0]) + cp_ref[...]
    s1 = _dot_hi2_l(table, ex_ref[1])
    s2 = _dot_hi2_l(table, ex_ref[2])
    c4 = jnp.concatenate([c1] * 4, axis=1)
    s14 = jnp.concatenate([s1] * 4, axis=1)
    s24 = jnp.concatenate([s2] * 4, axis=1)
    q = n_ref[:, 0:NSA_WIDTH]
    qn = q * lax.rsqrt(_dot_hi2_l(q * q, bdq_ref[...]) * (1.0 / HEAD) + EPS) * qg_ref[...]
    qn = qn * (HEAD ** -0.5 * LOG2E)
    qn_ref[...] = qn.astype(qn_ref.dtype)
    qr_ref[...] = _rope(qn, c4, s14, s24).astype(qr_ref.dtype)
    bdk = bdk_ref[...]
    ks = n_ref[:, N_KS:N_KS + LANES]
    kw = n_ref[:, N_KW:N_KW + LANES]
    ks = ks * lax.rsqrt(_dot_hi2_l(ks * ks, bdk) * (1.0 / HEAD) + EPS) * kg_ref[0:1, :]
    kw = kw * lax.rsqrt(_dot_hi2_l(kw * kw, bdk) * (1.0 / HEAD) + EPS) * kg_ref[1:2, :]
    ksa_ref[:, 0:LANES] = _rope(ks, c1, s1, s2).astype(ksa_ref.dtype)
    rows = (lax.broadcasted_iota(jnp.int32, (tt, LANES), 0) + pl.program_id(0) * tt) % seq
    lane = lax.broadcasted_iota(jnp.int32, (tt, LANES), 1)
    ksa_ref[:, LANES:2 * LANES] = jnp.where((rows // SLC_BLOCK) % LANES == lane, 1.0, 0.0).astype(ksa_ref.dtype)
    kw_ref[...] = _rope(kw, c1, s1, s2).astype(kw_ref.dtype)
    vs_ref[...] = n_ref[:, N_VS:N_VS + LANES].astype(vs_ref.dtype)
    vw_ref[...] = n_ref[:, N_VW:N_VW + LANES].astype(vw_ref.dtype)
    gt_ref[...] = jax.nn.sigmoid(n_ref[:, N_GATE:N_GATE + LANES])


def _nsa_prep(proj, rope_t, rope_ex, rope_cp, qg, kg, bd512, bd128, seq):
    m = proj.shape[0]
    tt = T_PREP
    row = lambda w: pl.BlockSpec((tt, w), lambda i: (i, 0))
    full = lambda a, b: pl.BlockSpec((a, b), lambda i: (0, 0))
    sds = lambda w, dt: jax.ShapeDtypeStruct((m, w), dt)
    return pl.pallas_call(
        functools.partial(_nsa_prep_kernel, tt=tt, seq=seq),
        grid=(m // tt,),
        in_specs=[pl.BlockSpec((tt, N_W), lambda i: (i, 1)), row(LANES),
                  pl.BlockSpec((3, LANES, LANES), lambda i: (0, 0, 0)), full(1, LANES),
                  full(1, NSA_WIDTH), full(2, LANES), full(NSA_WIDTH, NSA_WIDTH), full(LANES, LANES)],
        out_specs=[row(NSA_WIDTH), row(NSA_WIDTH), row(2 * LANES), row(LANES), row(LANES), row(LANES),
                   row(LANES)],
        out_shape=[sds(NSA_WIDTH, MXU_DTYPE), sds(NSA_WIDTH, MXU_DTYPE), sds(2 * LANES, MXU_DTYPE),
                   sds(LANES, MXU_DTYPE), sds(LANES, MXU_DTYPE), sds(LANES, MXU_DTYPE), sds(LANES, F32)],
        compiler_params=pltpu.CompilerParams(dimension_semantics=("parallel",),
                                             vmem_limit_bytes=VMEM_LIMIT),
        name="nsa_prep",
    )(proj, rope_t, rope_ex, rope_cp, qg, kg, bd512, bd128)


def _compress_kernel(xk_ref, xv_ref, w1bd_ref, pe_ref, w1_ref, w2bd_ref, kg_ref, bd_ref, o_ref, *, nb):
    acc = None
    for t in range(CMP_STRIDE):
        xt = jnp.concatenate([xk_ref[pl.ds(t, nb, stride=CMP_STRIDE), :],
                              xv_ref[pl.ds(t, nb, stride=CMP_STRIDE), :]], axis=1)
        part = _dot(xt, w1bd_ref[t])
        acc = part if acc is None else acc + part
    pe_terms = [_dot(jnp.broadcast_to(pe_ref[c:c + 1, :], (SUBLANES, CMP_LEN * HEAD)), w1_ref[c])[0:1, :]
                for c in range(2)]
    pe_all = jnp.concatenate([pe_terms[0]] * NSA_KV_HEADS + [pe_terms[1]] * NSA_KV_HEADS, axis=1)
    half = 2 * NSA_KV_WIDTH
    pre = acc[:, 0:half] + pltpu.roll(acc[:, half:2 * half], nb - 1, 0) + pe_all
    y = _dot(jax.nn.gelu(pre), w2bd_ref[...])
    rows = lax.broadcasted_iota(jnp.int32, (nb, 1), 0)
    y = jnp.where(rows < nb - 1, y, 0.0)
    yk = y[:, 0:NSA_KV_WIDTH]
    yk = yk * lax.rsqrt(_dot_hi2_l(yk * yk, bd_ref[...]) * (1.0 / HEAD) + EPS) * kg_ref[...]
    o_ref[0, 0] = yk.astype(o_ref.dtype)
    o_ref[1, 0] = y[:, NSA_KV_WIDTH:2 * NSA_KV_WIDTH].astype(o_ref.dtype)


def _compress(proj, w1bd, pe, w1, w2bd, kg0, bd128, batch, seq):
    nb = seq // CMP_STRIDE
    full = lambda *s: pl.BlockSpec(s, lambda b: (0,) * len(s))
    return pl.pallas_call(
        functools.partial(_compress_kernel, nb=nb),
        grid=(batch,),
        in_specs=[pl.BlockSpec((seq, NSA_KV_WIDTH), lambda b: (b, P_OFF // NSA_KV_WIDTH)),
                  pl.BlockSpec((seq, NSA_KV_WIDTH), lambda b: (b, P_OFF // NSA_KV_WIDTH + 1)),
                  full(CMP_STRIDE, P_W, 2 * P_W), full(2, CMP_LEN * HEAD), full(2, CMP_LEN * HEAD, HEAD),
                  full(P_W, P_W), full(1, NSA_KV_WIDTH), full(LANES, LANES)],
        out_specs=pl.BlockSpec((2, 1, nb, NSA_KV_WIDTH), lambda b: (0, b, 0, 0)),
        out_shape=jax.ShapeDtypeStruct((2, batch, nb, NSA_KV_WIDTH), MXU_DTYPE),
        compiler_params=pltpu.CompilerParams(dimension_semantics=("parallel",),
                                             vmem_limit_bytes=VMEM_LIMIT),
        name="compress",
    )(proj, proj, w1bd, pe, w1, w2bd, kg0, bd128)


def _nsa_attn_kernel(*refs, tq, seq, nwb, kc, ncp, nslp, shifted):
    qn_ref, qr_ref, gt_ref, kcmp_ref, vcmp_ref, ksa_ref, vs_ref = refs[:7]
    kw_refs = refs[7:7 + nwb]
    vw_refs = refs[7 + nwb:7 + 2 * nwb]
    msel_ref, cs_ref, egate_ref = refs[7 + 2 * nwb:10 + 2 * nwb]
    o_ref = refs[10 + 2 * nwb]
    bias_scr, qaug_scr, m_scr, l_scr, acc_scr, ow_scr, cand_scr, oc_scr, slc_scr, e_scr = refs[11 + 2 * nwb:]
    nslot = NSA_HEADS
    i = pl.program_id(1)
    c_cmp, c_slc, c_win = cs_ref[0:1, 0:1], cs_ref[0:1, 1:2], cs_ref[0:1, 2:3]
    lane = lax.broadcasted_iota(jnp.int32, (1, LANES), 1)
    lo = lane < HEAD

    def stack(q_ref):
        sl = [q_ref[:, j * LANES:(j + 1) * LANES] for j in range(NSA_GROUP)]
        zero = jnp.zeros_like(sl[0])
        return jnp.concatenate([jnp.where(lo, s, zero) for s in sl] + [jnp.where(lo, zero, s) for s in sl], axis=0)

    tpos = i * tq + lax.broadcasted_iota(jnp.int32, (tq, 1), 0)
    tpos_st = i * tq + lax.broadcasted_iota(jnp.int32, (nslot * tq, 1), 0) % tq
    gates = gt_ref[...]

    qn_st = stack(qn_ref)
    if shifted:
        ccw = min(2 * LANES, ncp)
        n_vis = (i + 1) * (tq // CMP_STRIDE) - 1
        n_cch = (n_vis + ccw - 1) // ccw
        qaug_scr[:, 0:LANES] = qn_st
        l_scr[...] = jnp.zeros(l_scr.shape, F32)
        acc_scr[...] = jnp.zeros(acc_scr.shape, F32)

        def cmp_scores(ch, carry):
            k0 = pl.multiple_of(ch * ccw, ccw)
            s = _dot_nt(qaug_scr[:, 0:LANES], kcmp_ref[0, 0, pl.ds(k0, ccw), :])
            ckey = k0 + lax.broadcasted_iota(jnp.int32, (1, ccw), 1)
            vis = (ckey * CMP_STRIDE + (CMP_LEN - 1)) <= tpos
            cbias = jnp.where(vis, -c_cmp, -SEL_BIG)
            for slot in range(nslot):
                rs = slice(slot * tq, (slot + 1) * tq)
                e = jnp.exp2(s[rs] + cbias)
                e_scr[rs, pl.ds(k0, ccw)] = e
                part = e[:, 0:LANES]
                for t in range(1, ccw // LANES):
                    part = part + e[:, t * LANES:(t + 1) * LANES]
                l_scr[rs, :] += part
            acc_scr[...] += _dot(e_scr[:, pl.ds(k0, ccw)], vcmp_ref[0, 0, pl.ds(k0, ccw), :])
            return carry

        lax.fori_loop(0, n_cch, cmp_scores, 0)
        l_c = jnp.sum(l_scr[...], axis=-1, keepdims=True)
        rinv_c = 1.0 / jnp.where(l_c > 0.0, l_c, 1.0)
        oc_scr[...] = acc_scr[...] * rinv_c
        l_scr[...] = jnp.broadcast_to(rinv_c, l_scr.shape)
        slc_scr[...] = jnp.zeros(slc_scr.shape, F32)

        def cmp_importance(ch, carry):
            k0 = pl.multiple_of(ch * ccw, ccw)
            for g in range(NSA_KV_HEADS):
                ps = None
                for r in range(NSA_GROUP):
                    rs = slice((g * NSA_GROUP + r) * tq, (g * NSA_GROUP + r + 1) * tq)
                    rinv_l = jnp.concatenate([l_scr[rs, :]] * (ccw // LANES), axis=1)
                    p = e_scr[rs, pl.ds(k0, ccw)] * rinv_l
                    ps = p if ps is None else ps + p
                slc_scr[g] += _dot_hi_l(ps, msel_ref[pl.ds(k0, ccw), :])
            return carry

        lax.fori_loop(0, n_cch, cmp_importance, 0)
        slc_rows = [slc_scr[g] for g in range(NSA_KV_HEADS)]
    else:
        ckey = lax.broadcasted_iota(jnp.int32, (1, ncp), 1)
        cmask = (ckey * CMP_STRIDE + (CMP_LEN - 1)) <= tpos
        s_all = _dot_nt(qn_st, kcmp_ref[0, 0])
        psum = [None, None]
        e_parts, rinv_parts = [], []
        for slot in range(nslot):
            s = jnp.where(cmask, s_all[slot * tq:(slot + 1) * tq], NEG_INF)
            e = jnp.where(cmask, jnp.exp2(s - jnp.max(s, axis=-1, keepdims=True)), 0.0)
            l = jnp.sum(e, axis=-1, keepdims=True)
            rinv = 1.0 / jnp.where(l > 0.0, l, 1.0)
            e_parts.append(e.astype(MXU_DTYPE))
            rinv_parts.append(rinv)
            p = e * rinv
            g = slot // NSA_GROUP
            psum[g] = p if psum[g] is None else psum[g] + p
        oc_scr[...] = (_dot(jnp.concatenate(e_parts, axis=0), vcmp_ref[0, 0])
                       * jnp.concatenate(rinv_parts, axis=0))
        slc_rows = [_dot_hi_l(psum[g], msel_ref[...]) for g in range(NSA_KV_HEADS)]

    qr_st = stack(qr_ref)
    kwc = jnp.concatenate([r[...] for r in kw_refs], axis=0)
    vwc = jnp.concatenate([r[...] for r in vw_refs], axis=0)
    sw = _dot_nt(qr_st, kwc)
    if shifted:
        rr = lax.broadcasted_iota(jnp.int32, (tq, tq), 0)
        cc = lax.broadcasted_iota(jnp.int32, (tq, tq), 1)
        parts = []
        for jb in range(nwb):
            shift_b = c_win + jnp.where(i - (nwb - 1) + jb >= 0, 0.0, SEL_BIG)
            blk = jnp.exp2(sw[:, jb * tq:(jb + 1) * tq] - shift_b)
            if jb == 0 or jb == nwb - 1:
                vis = (cc > rr) if jb == 0 else (cc <= rr)
                blk = jnp.concatenate([jnp.where(vis, blk[s * tq:(s + 1) * tq], 0.0) for s in range(nslot)],
                                      axis=0)
            parts.append(blk)
        ew = jnp.concatenate(parts, axis=1)
    else:
        kpos_w = (i - (nwb - 1)) * tq + lax.broadcasted_iota(jnp.int32, (1, nwb * tq), 1)
        dist = tpos_st - kpos_w
        wmask = (dist >= 0) & (dist < WINDOW) & (kpos_w >= 0)
        sw = jnp.where(wmask, sw, NEG_INF)
        ew = jnp.exp2(sw - jnp.max(sw, axis=-1, keepdims=True))
    ow_scr[...] = _dot(ew, vwc) * (1.0 / jnp.sum(ew, axis=-1, keepdims=True))

    jrow = lax.broadcasted_iota(jnp.int32, (nslp, 1), 0)
    jrowf = jrow.astype(F32)
    cur_t = (i * tq + lax.broadcasted_iota(jnp.int32, (1, tq), 1)) // SLC_BLOCK
    forced = (jrow == 0) | (jrow == cur_t) | (jrow == cur_t - 1)
    causal = jrow <= cur_t
    shift_s = c_slc if shifted else 0.0
    n_free = N_SELECT - 3
    to_bias = lambda sel: ((sel - 1.0) * SEL_BIG - shift_s).T.astype(bias_scr.dtype)
    n_bad = None
    for g in range(NSA_KV_HEADS):
        slc = slc_rows[g].T
        cand = jnp.where(causal & jnp.logical_not(forced), slc, -1.0)
        cand_scr[g] = cand
        c = cand
        for _ in range(n_free):
            c = jnp.where(c == jnp.max(c, axis=0, keepdims=True), -2.0, c)
        picked = (c == -2.0) & (cand >= 0.0)
        n_picked = jnp.sum(jnp.where(picked, 1.0, 0.0), axis=0, keepdims=True)
        n_real = jnp.sum(jnp.where(cand >= 0.0, 1.0, 0.0), axis=0, keepdims=True)
        bad = jnp.where(n_picked == jnp.minimum(n_real, float(n_free)), 0.0, 1.0)
        n_bad = bad if n_bad is None else n_bad + bad
        bias_scr[g] = to_bias(jnp.where(forced | picked, 1.0, 0.0))

    @pl.when(jnp.max(n_bad) > 0.0)
    def _():
        for g in range(NSA_KV_HEADS):
            cand = cand_scr[g]
            sel = jnp.where(forced, 1.0, 0.0)
            for _ in range(n_free):
                mx = jnp.max(cand, axis=0, keepdims=True)
                first = jnp.min(jnp.where(cand == mx, jrowf, float(nslp)), axis=0, keepdims=True)
                hit = jrowf == first
                sel = jnp.where(hit, 1.0, sel)
                cand = jnp.where(hit, -2.0, cand)
            bias_scr[g] = to_bias(jnp.where(causal, sel, 0.0))

    qaug_scr[:, 0:LANES] = qr_st
    if not shifted:
        m_scr[...] = jnp.full(m_scr.shape, NEG_INF, F32)
    l_scr[...] = jnp.zeros(l_scr.shape, F32)
    acc_scr[...] = jnp.zeros(acc_scr.shape, F32)
    group_keys = LANES * SLC_BLOCK

    def scores(key0, width):
        @pl.when(key0 % group_keys == 0)
        def _():
            off = pl.multiple_of((key0 // group_keys) * LANES, LANES)
            b0 = bias_scr[0, :, pl.ds(off, LANES)]
            b1 = bias_scr[1, :, pl.ds(off, LANES)]
            qaug_scr[:, LANES:2 * LANES] = jnp.concatenate([b0] * NSA_GROUP + [b1] * NSA_GROUP, axis=0)

        return _dot_nt(qaug_scr[...], ksa_ref[pl.ds(pl.multiple_of(key0, width), width), :])

    def accumulate(s, key0, width):
        k0 = pl.multiple_of(key0, width)
        if shifted:
            p = jnp.exp2(s)
            psum_l = p[:, 0:LANES]
            for t in range(1, width // LANES):
                psum_l = psum_l + p[:, t * LANES:(t + 1) * LANES]
            l_scr[...] += psum_l
            acc_scr[...] += _dot(p, vs_ref[pl.ds(k0, width), :])
        else:
            m_old = m_scr[...]
            m_new = jnp.maximum(m_old, jnp.max(s, axis=-1, keepdims=True))
            alpha = jnp.exp2(m_old - m_new)
            p = jnp.exp2(s - m_new[:, 0:1])
            l_scr[...] = alpha * l_scr[...] + jnp.sum(p, axis=-1, keepdims=True)
            acc_scr[...] = alpha * acc_scr[...] + _dot(p, vs_ref[pl.ds(k0, width), :])
            m_scr[...] = m_new

    n_full = (i * tq) // kc
    n_sub = (i * tq - n_full * kc) // tq

    def body_full(c, carry):
        accumulate(scores(c * kc, kc), c * kc, kc)
        return carry

    lax.fori_loop(0, n_full, body_full, 0)

    def body_sub(j, carry):
        key0 = n_full * kc + j * tq
        accumulate(scores(key0, tq), key0, tq)
        return carry

    lax.fori_loop(0, n_sub, body_sub, 0)
    col = lax.broadcasted_iota(jnp.int32, (1, tq), 1)
    accumulate(jnp.where(col <= tpos_st - i * tq, scores(i * tq, tq), -SEL_BIG), i * tq, tq)
    if shifted:
        o_s = acc_scr[...] * (1.0 / jnp.sum(l_scr[...], axis=-1, keepdims=True))
    else:
        o_s = acc_scr[...] / l_scr[...]
    o_w = ow_scr[...]

    gate_e = _dot_hi2_l(gates, egate_ref[...])
    for j in range(NSA_GROUP):
        r0 = slice(j * tq, (j + 1) * tq)
        r1 = slice((NSA_GROUP + j) * tq, (NSA_GROUP + j + 1) * tq)
        out = None
        for br, o_b in enumerate((oc_scr[...], o_s, o_w)):
            lanes = slice(br * NSA_WIDTH + j * LANES, br * NSA_WIDTH + (j + 1) * LANES)
            term = gate_e[:, lanes] * jnp.where(lo, o_b[r0], o_b[r1])
            out = term if out is None else out + term
        o_ref[:, j * LANES:(j + 1) * LANES] = out.astype(o_ref.dtype)


def _gate_expander():
    mat = np.zeros((LANES, 3 * NSA_WIDTH), np.float32)
    for br in range(3):
        for j in range(NSA_GROUP):
            for g in range(NSA_KV_HEADS):
                head = g * NSA_GROUP + j
                c0 = br * NSA_WIDTH + j * LANES + g * HEAD
                mat[3 * head + br, c0:c0 + HEAD] = 1.0
    return jnp.asarray(mat, MXU_DTYPE)


def _nsa_attn(qn, qr, gt, kvc, ksa, vs, kw, vw, msel, cshift, batch, seq, shifted):
    tq, kc = (TQ if shifted else TQ_RUNMAX), KC
    nq = seq // tq
    nwb = WINDOW // tq + 1
    ncp = seq // CMP_STRIDE
    nslp = msel.shape[1]
    row = lambda w: pl.BlockSpec((tq, w), lambda b, i: (b * nq + i, 0))
    once = pl.Buffered(1)
    win = lambda jb: pl.BlockSpec((tq, LANES), lambda b, i: (b * nq + jnp.maximum(i - (nwb - 1) + jb, 0), 0))
    in_specs = ([row(NSA_WIDTH), row(NSA_WIDTH), row(LANES),
                 pl.BlockSpec((1, 1, ncp, LANES), lambda b, i: (0, b, 0, 0), pipeline_mode=once),
                 pl.BlockSpec((1, 1, ncp, LANES), lambda b, i: (1, b, 0, 0), pipeline_mode=once),
                 pl.BlockSpec((seq, 2 * LANES), lambda b, i: (b, 0), pipeline_mode=once),
                 pl.BlockSpec((seq, LANES), lambda b, i: (b, 0), pipeline_mode=once)]
                + [win(jb) for jb in range(nwb)] + [win(jb) for jb in range(nwb)]
                + [pl.BlockSpec((ncp, nslp), lambda b, i: (0, 0), pipeline_mode=once),
                   pl.BlockSpec((1, LANES), lambda b, i: (0, 0)),
                   pl.BlockSpec((LANES, 3 * NSA_WIDTH), lambda b, i: (0, 0), pipeline_mode=once)])
    return pl.pallas_call(
        functools.partial(_nsa_attn_kernel, tq=tq, seq=seq, nwb=nwb, kc=kc, ncp=ncp, nslp=nslp,
                          shifted=shifted),
        grid=(batch, nq),
        in_specs=in_specs,
        out_specs=row(NSA_WIDTH),
        out_shape=jax.ShapeDtypeStruct((batch * seq, NSA_WIDTH), MXU_DTYPE),
        scratch_shapes=[pltpu.VMEM((NSA_KV_HEADS, tq, nslp), MXU_DTYPE),
                        pltpu.VMEM((NSA_HEADS * tq, 2 * LANES), MXU_DTYPE),
                        pltpu.VMEM((NSA_HEADS * tq, LANES), F32),
                        pltpu.VMEM((NSA_HEADS * tq, LANES), F32),
                        pltpu.VMEM((NSA_HEADS * tq, LANES), F32),
                        pltpu.VMEM((NSA_HEADS * tq, LANES), F32),
                        pltpu.VMEM((NSA_KV_HEADS, nslp, tq), F32),
                        pltpu.VMEM((NSA_HEADS * tq, LANES), F32),
                        pltpu.VMEM((NSA_KV_HEADS, tq, nslp), F32),
                        pltpu.VMEM((NSA_HEADS * tq, ncp) if shifted else (SUBLANES, LANES), F32)],
        compiler_params=pltpu.CompilerParams(dimension_semantics=("parallel", "arbitrary"),
                                             vmem_limit_bytes=VMEM_LIMIT),
        name="nsa_attn" if shifted else "nsa_attn_runmax",
    )(qn, qr, gt, kvc, kvc, ksa, vs, *([kw] * nwb), *([vw] * nwb), msel, cshift, _gate_expander())


def _out_ffn_kernel(x_ref, og_ref, on_ref, c_ref, ch_ref, cw_ref, wo_ref, fg_ref, wgu_ref, wd_ref, o_ref,
                    *, tm, seq):
    i = pl.program_id(0)
    keep = jnp.where((i * tm) % seq == 0, 0.0, 1.0)
    rows = lax.broadcasted_iota(jnp.int32, (tm, 1), 0)
    u = c_ref[:, CONV_WIDTH:2 * CONV_WIDTH] * c_ref[:, 2 * CONV_WIDTH:3 * CONV_WIDTH]
    hu = ch_ref[:, CONV_WIDTH:2 * CONV_WIDTH] * ch_ref[:, 2 * CONV_WIDTH:3 * CONV_WIDTH] * keep
    w = cw_ref[...]
    conv = w[CONV_K - 1:CONV_K, :] * u
    for s in range(1, CONV_K):
        conv = conv + w[CONV_K - 1 - s:CONV_K - s, :] * _shift_rows(u, hu, s, rows)
    oc = c_ref[:, 0:CONV_WIDTH] * conv
    x1 = (x_ref[...] + _dot(og_ref[...], wo_ref[0:GDN_WIDTH, :])
          + _dot(on_ref[...], wo_ref[GDN_WIDTH:GDN_WIDTH + NSA_WIDTH, :])
          + _dot(oc, wo_ref[GDN_WIDTH + NSA_WIDTH:, :]))
    ms = jnp.mean(x1 * x1, axis=-1, keepdims=True)
    h2 = (x1 * lax.rsqrt(ms + EPS) * fg_ref[...]).astype(MXU_DTYPE)
    o_ref[...] = x1
    for c0 in range(0, D_FF, FF_CHUNK):
        gate = jnp.dot(h2, wgu_ref[:, c0:c0 + FF_CHUNK], preferred_element_type=F32)
        up = jnp.dot(h2, wgu_ref[:, D_FF + c0:D_FF + c0 + FF_CHUNK], preferred_element_type=F32)
        o_ref[...] += _dot(_silu(gate) * up, wd_ref[c0:c0 + FF_CHUNK, :])


def _out_ffn(x2, o_gdn, o_nsa, proj, conv_w, w_out, fgain, wgu, wd, seq):
    m = x2.shape[0]
    tm = TM_FFN
    hb = tm // SUBLANES
    full = lambda a, b: pl.BlockSpec((a, b), lambda i: (0, 0), pipeline_mode=pl.Buffered(1))
    return pl.pallas_call(
        functools.partial(_out_ffn_kernel, tm=tm, seq=seq),
        grid=(m // tm,),
        in_specs=[pl.BlockSpec((tm, D_MODEL), lambda i: (i, 0)),
                  pl.BlockSpec((tm, GDN_WIDTH), lambda i: (i, 0)),
                  pl.BlockSpec((tm, NSA_WIDTH), lambda i: (i, 0)),
                  pl.BlockSpec((tm, C_W), lambda i: (i, C_OFF // C_W)),
                  pl.BlockSpec((SUBLANES, C_W), lambda i: (jnp.maximum(i * hb - 1, 0), C_OFF // C_W)),
                  full(CONV_K, CONV_WIDTH), full(D_MODEL, D_MODEL), full(1, D_MODEL),
                  full(D_MODEL, 2 * D_FF), full(D_FF, D_MODEL)],
        out_specs=pl.BlockSpec((tm, D_MODEL), lambda i: (i, 0)),
        out_shape=jax.ShapeDtypeStruct((m, D_MODEL), F32),
        compiler_params=pltpu.CompilerParams(dimension_semantics=("parallel",),
                                             vmem_limit_bytes=VMEM_LIMIT),
        name="out_ffn",
    )(x2, o_gdn, o_nsa, proj, proj, conv_w, w_out, fgain, wgu, wd)


def _proj_column_map():
    offs = np.concatenate([[0], np.cumsum(IN_SIZES)])
    seg = lambda k: np.arange(offs[k], offs[k + 1])
    pad = lambda n: -np.ones(n, np.int64)
    nq = seg(6).reshape(NSA_HEADS, HEAD)[list(Q_SLOT_HEADS)].reshape(-1)
    cols = np.concatenate([
        seg(0), seg(1), seg(2), seg(3), seg(4), seg(5), pad(LANES - 2 * GDN_HEADS),
        nq, seg(9), seg(11), seg(10), seg(12), seg(13), pad(LANES - 3 * NSA_HEADS),
        seg(14), seg(15), seg(16),
        seg(7), seg(8)])
    assert cols.shape[0] == PROJ_W
    return cols


def _block_diag_ones(n):
    idx = np.arange(n) // HEAD
    return jnp.asarray(idx[:, None] == idx[None, :], MXU_DTYPE)


def _head_expander(first_lane):
    mat = np.zeros((LANES, GDN_WIDTH), np.float32)
    for h in range(GDN_HEADS):
        mat[first_lane + h, h * HEAD:(h + 1) * HEAD] = 1.0
    return jnp.asarray(mat, MXU_DTYPE)


def _compress_weights(w1, w2):
    nslab = 2 * NSA_KV_HEADS
    slab_kv = np.arange(nslab) // NSA_KV_HEADS
    eye = jnp.eye(nslab, dtype=MXU_DTYPE)
    w1r = w1.astype(MXU_DTYPE).reshape(2, 2, CMP_STRIDE, HEAD, HEAD)[slab_kv]
    w1bd = jnp.einsum("shtde,sS->tsdhSe", w1r, eye)
    w2bd = jnp.einsum("sde,sS->sdSe", w2.astype(MXU_DTYPE)[slab_kv], eye)
    return w1bd.reshape(CMP_STRIDE, P_W, 2 * P_W), w2bd.reshape(P_W, P_W)


def _rope_table(positions):
    half = ROT_DIM // 2
    inv = jnp.float32(ROPE_THETA) ** (-jnp.arange(0, ROT_DIM, 2, dtype=jnp.float32) / ROT_DIM)
    inv_l = jnp.concatenate([inv, inv, jnp.zeros((LANES - ROT_DIM,), F32)])
    ang = positions.astype(jnp.float32).reshape(-1)[:, None] * inv_l[None, :]
    lane = jnp.arange(LANES)[None, :]
    return jnp.where(lane < half, jnp.cos(ang), jnp.where(lane < ROT_DIM, jnp.sin(ang), 0.0))


def _rope_expanders():
    half = ROT_DIM // 2
    ex = np.zeros((3, LANES, LANES), np.float32)
    cpat = np.zeros((1, LANES), np.float32)
    for j in range(LANES):
        d = j % HEAD
        if d < ROT_DIM:
            ex[0, d % half, j] = 1.0
        else:
            cpat[0, j] = 1.0
        if d < half:
            ex[1, half + d, j] = -1.0
        elif d < ROT_DIM:
            ex[2, half + (d - half), j] = 1.0
    return jnp.asarray(ex, MXU_DTYPE), jnp.asarray(cpat, F32)


def _selection_matrix(ncp, nslp):
    ratio = SLC_BLOCK // CMP_STRIDE
    frac = np.minimum(CMP_LEN, SLC_BLOCK - CMP_STRIDE * np.arange(ratio)).astype(np.float64) / CMP_LEN
    mat = np.zeros((ncp, nslp), np.float32)
    c = np.arange(ncp)
    mat[c, c // ratio] = frac[c % ratio]
    nxt = c // ratio + 1
    ok = nxt < nslp
    mat[c[ok], nxt[ok]] += (1.0 - frac[c % ratio])[ok]
    return jnp.asarray(mat, MXU_DTYPE)


def kernel(x, positions, attn_norm, w_in, gdn_conv_w, gdn_a_log, gdn_dt_bias, gdn_norm, nsa_q_norm,
           nsa_k_norm, nsa_cmp_pe, nsa_cmp_w1, nsa_cmp_w2, conv_w, w_out, ffn_norm, w_gate_up, w_down):
    batch, seq, _ = x.shape
    depth = w_in.shape[0]
    m = batch * seq
    assert seq % max(TM_PROJ, TM_FFN, T_PREP, KC) == 0 and (seq // CMP_STRIDE) % LANES == 0
    nb = seq // CMP_STRIDE
    nslp = -(-(seq // SLC_BLOCK) // LANES) * LANES

    cols = _proj_column_map()
    take = jnp.asarray(np.maximum(cols, 0), jnp.int32)
    valid = jnp.asarray(cols >= 0)
    assert Q_SLOT_HEADS == tuple(g * NSA_GROUP + j for j in range(NSA_GROUP) for g in range(NSA_KV_HEADS))
    bd128, bd256, bd512 = _block_diag_ones(128), _block_diag_ones(256), _block_diag_ones(512)
    rope_t = _rope_table(positions)
    rope_ex, rope_cp = _rope_expanders()
    msel = _selection_matrix(nb, nslp)
    lane_pad = lambda v: jnp.zeros((1, LANES), F32).at[0, GDN_HEADS:2 * GDN_HEADS].set(v.astype(F32))

    x2 = x.reshape(m, D_MODEL)
    for l in range(depth):
        w_l = jnp.where(valid[None, :], jnp.take(w_in[l], take, axis=1), 0.0).astype(MXU_DTYPE)
        proj = _in_proj(x2, attn_norm[l].reshape(1, D_MODEL), w_l)

        o_gdn = _gdn(proj, gdn_conv_w[l], lane_pad(gdn_a_log[l]), lane_pad(gdn_dt_bias[l]),
                     jnp.tile(gdn_norm[l], GDN_HEADS).reshape(1, GDN_WIDTH), bd256,
                     _head_expander(0), _head_expander(GDN_HEADS), batch, seq)

        qg = jnp.tile(nsa_q_norm[l], NSA_HEADS).reshape(1, NSA_WIDTH)
        kg = jnp.tile(nsa_k_norm[l, 1:3], (1, NSA_KV_HEADS))
        qn, qr, ksa, kw, vs, vw, gt = _nsa_prep(proj, rope_t, rope_ex, rope_cp, qg, kg, bd512, bd128, seq)

        w1bd, w2bd = _compress_weights(nsa_cmp_w1[l], nsa_cmp_w2[l])
        kvc = _compress(proj, w1bd, nsa_cmp_pe[l].reshape(2, CMP_LEN * HEAD), nsa_cmp_w1[l].astype(MXU_DTYPE),
                        w2bd, jnp.tile(nsa_k_norm[l, 0], NSA_KV_HEADS).reshape(1, NSA_KV_WIDTH), bd128,
                        batch, seq)

        bound = (HEAD ** 0.5) * jnp.max(jnp.abs(nsa_q_norm[l])) * jnp.max(jnp.abs(nsa_k_norm[l]), axis=1)
        cshift = jnp.zeros((1, LANES), F32).at[0, 0:3].set((bound * LOG2E).astype(F32))
        attn_args = (qn, qr, gt, kvc, ksa, vs, kw, vw, msel, cshift)
        o_nsa = lax.cond(
            jnp.max(bound) <= MAX_SOFTMAX_SHIFT,
            lambda a: _nsa_attn(*a, batch, seq, True),
            lambda a: _nsa_attn(*a, batch, seq, False),
            attn_args)

        wo = w_out[l].astype(MXU_DTYPE)
        wo_nsa = wo[GDN_WIDTH:GDN_WIDTH + NSA_WIDTH].reshape(NSA_KV_HEADS, NSA_GROUP, HEAD, D_MODEL)
        wo = jnp.concatenate([wo[:GDN_WIDTH], wo_nsa.transpose(1, 0, 2, 3).reshape(NSA_WIDTH, D_MODEL),
                              wo[GDN_WIDTH + NSA_WIDTH:]], axis=0)
        x2 = _out_ffn(x2, o_gdn, o_nsa, proj, conv_w[l], wo,
                      ffn_norm[l].reshape(1, D_MODEL), w_gate_up[l].astype(MXU_DTYPE),
                      w_down[l].astype(MXU_DTYPE), seq)
    return x2.reshape(batch, seq, D_MODEL)
```

```python
import functools

import numpy as np
import jax
import jax.numpy as jnp
from jax import lax
from jax.experimental import pallas as pl
from jax.experimental.pallas import tpu as pltpu

F32 = jnp.float32
MXU_DTYPE = jnp.bfloat16

D_MODEL = 1024
HEAD = 64
EPS = 1e-6
NEG_INF = -1e30
GDN_HEADS = 4
GDN_WIDTH = 256
GDN_CONV = 4
GDN_CHUNK = 64
NSA_HEADS = 8
NSA_WIDTH = 512
NSA_KV_HEADS = 2
NSA_GROUP = 4
NSA_KV_WIDTH = 128
CMP_STRIDE = 16
CMP_LEN = 32
SLC_BLOCK = 64
N_SELECT = 16
WINDOW = 512
CONV_WIDTH = 256
CONV_K = 3
ROPE_THETA = 500000.0
ROT_DIM = 16
D_FF = 2816
IN_SIZES = (256, 256, 256, 256, 4, 4, 512, 128, 128, 128, 128, 128, 128, 24, 256, 256, 256)
D_IN = sum(IN_SIZES)

LANES = 128
SUBLANES = 8
VMEM_LIMIT = 56 * 1024 * 1024

C_W = 3 * CONV_WIDTH
P_W = 2 * NSA_KV_WIDTH
GM_W = 4 * GDN_WIDTH
GG_W = LANES
N_W = 512 + 4 * 128 + LANES
C_OFF, P_OFF, GM_OFF, GG_OFF = 0, C_W, C_W + P_W, C_W + P_W + GM_W
PROJ_W = C_W + P_W + GM_W + GG_W
N_OFF = PROJ_W
W_COLS = PROJ_W + N_W
Q_SLOT_HEADS = (0, 4, 1, 5, 2, 6, 3, 7)
N_KS, N_KW, N_VS, N_VW, N_GATE = (NSA_WIDTH + k * LANES for k in range(5))

SEL_BIG = 16384.0
MAX_SOFTMAX_SHIFT = 40.0
LOG2E = 1.4426950408889634

TM_PROJ = 512
TM_FFN = 512
T_GDN = 512
TQ = 256
TQ_RUNMAX = 128
KC = 1024
FF_CHUNK = 256


def _dot(a, b):
    return jnp.dot(a.astype(MXU_DTYPE), b.astype(MXU_DTYPE), preferred_element_type=F32)


def _dot_nt(a, b):
    return lax.dot_general(a.astype(MXU_DTYPE), b.astype(MXU_DTYPE), (((1,), (1,)), ((), ())),
                           preferred_element_type=F32)


def _dot_tn(a, b):
    return lax.dot_general(a.astype(MXU_DTYPE), b.astype(MXU_DTYPE), (((0,), (0,)), ((), ())),
                           preferred_element_type=F32)


def _split3(x):
    a = x.astype(MXU_DTYPE)
    r = x - a.astype(F32)
    b = r.astype(MXU_DTYPE)
    c = (r - b.astype(F32)).astype(MXU_DTYPE)
    return a, b, c


def _dot_hi_l(x, m):
    a, b, c = _split3(x)
    f = lambda t: jnp.dot(t, m, preferred_element_type=F32)
    return f(a) + f(b) + f(c)


def _dot_hi2_l(x, m):
    a = x.astype(MXU_DTYPE)
    b = (x - a.astype(F32)).astype(MXU_DTYPE)
    return jnp.dot(a, m, preferred_element_type=F32) + jnp.dot(b, m, preferred_element_type=F32)


def _silu(x):
    return x * jax.nn.sigmoid(x)


def _shift_rows(x, halo, s, rows):
    y = pltpu.roll(x, s, 0)
    for r in range(s):
        y = jnp.where(rows == r, halo[SUBLANES - s + r:SUBLANES - s + r + 1, :], y)
    return y


def _in_proj_kernel(x_ref, g_ref, w_ref, t_ref, ex_ref, cp_ref, qg_ref, kg_ref, bdq_ref, bdk_ref,
                    o_ref, qn_ref, qr_ref, ksa_ref, kw_ref, vs_ref, vw_ref, gt_ref, *, tm, seq):
    x = x_ref[...]
    ms = jnp.mean(x * x, axis=-1, keepdims=True)
    h = (x * lax.rsqrt(ms + EPS) * g_ref[...]).astype(MXU_DTYPE)
    pending = [(c, min(256, PROJ_W - c)) for c in range(0, PROJ_W, 256)]

    def emit(count):
        for _ in range(min(count, len(pending))):
            c, cw = pending.pop(0)
            o_ref[:, c:c + cw] = jnp.dot(h, w_ref[:, c:c + cw], preferred_element_type=F32)

    nblk = jnp.dot(h, w_ref[:, N_OFF:N_OFF + N_W], preferred_element_type=F32)
    _nsa_operands(nblk, t_ref, ex_ref, cp_ref, qg_ref, kg_ref, bdq_ref, bdk_ref,
                  qn_ref, qr_ref, ksa_ref, kw_ref, vs_ref, vw_ref, gt_ref, tt=tm, seq=seq, between=emit)
    emit(len(pending))


def _in_proj(x2, gain, w, rope_t, rope_ex, rope_cp, qg, kg, bd512, bd128, seq):
    m = x2.shape[0]
    tm = TM_PROJ
    row = lambda wd: pl.BlockSpec((tm, wd), lambda i: (i, 0))
    full = lambda a, b: pl.BlockSpec((a, b), lambda i: (0, 0))
    sds = lambda wd, dt: jax.ShapeDtypeStruct((m, wd), dt)
    return pl.pallas_call(
        functools.partial(_in_proj_kernel, tm=tm, seq=seq),
        grid=(m // tm,),
        in_specs=[row(D_MODEL), full(1, D_MODEL), full(D_MODEL, W_COLS), row(LANES),
                  pl.BlockSpec((3, LANES, LANES), lambda i: (0, 0, 0)), full(1, LANES),
                  full(1, NSA_WIDTH), full(2, LANES), full(NSA_WIDTH, NSA_WIDTH), full(LANES, LANES)],
        out_specs=[row(PROJ_W), row(NSA_WIDTH), row(NSA_WIDTH), row(2 * LANES), row(LANES), row(LANES),
                   row(LANES), row(LANES)],
        out_shape=[sds(PROJ_W, F32), sds(NSA_WIDTH, MXU_DTYPE), sds(NSA_WIDTH, MXU_DTYPE),
                   sds(2 * LANES, MXU_DTYPE), sds(LANES, MXU_DTYPE), sds(LANES, MXU_DTYPE),
                   sds(LANES, MXU_DTYPE), sds(LANES, F32)],
        compiler_params=pltpu.CompilerParams(dimension_semantics=("parallel",),
                                             vmem_limit_bytes=VMEM_LIMIT),
        name="in_proj",
    )(x2, gain, w, rope_t, rope_ex, rope_cp, qg, kg, bd512, bd128)


def _gdn_kernel(g_ref, gh_ref, gg_ref, cw_ref, alog_ref, dt_ref, gn_ref, bd_ref, eb_ref, eg_ref, o_ref, s_ref,
                *, tt):
    t_idx = pl.program_id(1)

    @pl.when(t_idx == 0)
    def _():
        s_ref[...] = jnp.zeros_like(s_ref)

    keep = jnp.where(t_idx == 0, 0.0, 1.0)
    rows = lax.broadcasted_iota(jnp.int32, (tt, 1), 0)
    x = g_ref[:, 0:3 * GDN_WIDTH]
    hx = gh_ref[:, 0:3 * GDN_WIDTH] * keep
    w = cw_ref[...]
    y = w[GDN_CONV - 1:GDN_CONV, :] * x
    for s in range(1, GDN_CONV):
        y = y + w[GDN_CONV - 1 - s:GDN_CONV - s, :] * _shift_rows(x, hx, s, rows)
    y = _silu(y)
    q = y[:, 0:GDN_WIDTH]
    k = y[:, GDN_WIDTH:2 * GDN_WIDTH]
    v = y[:, 2 * GDN_WIDTH:3 * GDN_WIDTH]
    bd = bd_ref[...]
    q = q * lax.rsqrt(_dot_hi2_l(q * q, bd) + EPS) * (HEAD ** -0.5)
    k = k * lax.rsqrt(_dot_hi2_l(k * k, bd) + EPS)

    gg = gg_ref[...]
    lane = lax.broadcasted_iota(jnp.int32, (1, LANES), 1)
    xa = gg + dt_ref[...]
    softplus = jnp.maximum(xa, 0.0) + jnp.log1p(jnp.exp(-jnp.abs(xa)))
    g2 = jnp.where(lane < GDN_HEADS, jax.nn.sigmoid(gg), -jnp.exp(alog_ref[...]) * softplus)
    cs = g2.T
    lane_t = lax.broadcasted_iota(jnp.int32, (1, tt), 1) % GDN_CHUNK
    step = 1
    while step < GDN_CHUNK:
        cs = cs + jnp.where(lane_t >= step, pltpu.roll(cs, step, 1), 0.0)
        step *= 2
    gcum_t = cs
    gcum = cs.T
    beta_e = _dot_hi_l(g2, eb_ref[...])
    gcum_e = _dot_hi_l(gcum, eg_ref[...])
    eg_e = jnp.exp(gcum_e)
    kb = k * beta_e
    rv = v * beta_e
    rk = kb * eg_e
    qd = q * eg_e

    nst = GDN_HEADS * GDN_CHUNK
    ri = lax.broadcasted_iota(jnp.int32, (nst, nst), 0)
    ci = lax.broadcasted_iota(jnp.int32, (nst, nst), 1)
    same = (ri // GDN_CHUNK) == (ci // GDN_CHUNK)
    m_tril = same & ((ri % GDN_CHUNK) >= (ci % GDN_CHUNK))
    m_strict = same & ((ri % GDN_CHUNK) > (ci % GDN_CHUNK))
    tile4 = lambda t: jnp.concatenate([t] * GDN_HEADS, axis=0)
    expand = lambda t: tile4(t.astype(MXU_DTYPE)) * bd

    nchunk = tt // GDN_CHUNK
    rss = [slice(n * GDN_CHUNK, (n + 1) * GDN_CHUNK) for n in range(nchunk)]
    g_last = [gcum_e[(n + 1) * GDN_CHUNK - 1:(n + 1) * GDN_CHUNK, :] for n in range(nchunk)]
    rmat, pw, qk = [], [], []
    for n, rs in enumerate(rss):
        g_row = jnp.concatenate([gcum_t[GDN_HEADS + h:GDN_HEADS + h + 1, rs] for h in range(GDN_HEADS)], axis=1)
        decay = jnp.where(m_tril, jnp.exp(jnp.where(m_tril, tile4(gcum_e[rs]) - g_row, 0.0)), 0.0)
        k4 = tile4(k[rs])
        a = jnp.where(m_strict, _dot_nt(expand(kb[rs]), k4) * decay, 0.0)
        qk.append(jnp.where(m_tril, _dot_nt(expand(q[rs]), k4) * decay, 0.0))
        rmat.append(-a)
        pw.append(-a)
    for _ in range(GDN_CHUNK.bit_length() - 2):
        nxt_pw, nxt_r = [], []
        for r, t in zip(rmat, pw):
            t2 = _dot(t, t)
            nxt_pw.append(t2)
            nxt_r.append(r + t2 + _dot(r, t2))
        pw, rmat = nxt_pw, nxt_r
    u, wm = [], []
    for n, rs in enumerate(rss):
        rv_x = jnp.where(same, tile4(rv[rs]), 0.0)
        rk_x = jnp.where(same, tile4(rk[rs]), 0.0)
        u.append(rv_x + _dot(rmat[n], rv_x))
        wm.append(rk_x + _dot(rmat[n], rk_x))

    st = s_ref[...]
    outs = []
    for n, rs in enumerate(rss):
        kd = k[rs] * jnp.exp(g_last[n] - gcum_e[rs])
        v_new = u[n] - _dot(wm[n], st)
        o_x = _dot(expand(qd[rs]), st) + _dot(qk[n], v_new)
        st = st * jnp.exp(g_last[n]) + _dot_tn(expand(kd), v_new)
        o = o_x[0:GDN_CHUNK]
        for h in range(1, GDN_HEADS):
            o = o + o_x[h * GDN_CHUNK:(h + 1) * GDN_CHUNK]
        outs.append(o)
    s_ref[...] = st
    o_all = jnp.concatenate(outs, axis=0)
    on = o_all * lax.rsqrt(_dot_hi2_l(o_all * o_all, bd) * (1.0 / HEAD) + EPS) * gn_ref[...]
    o_ref[...] = (on * _silu(g_ref[:, 3 * GDN_WIDTH:4 * GDN_WIDTH])).astype(o_ref.dtype)


def _gdn(proj, conv_w, alog_pad, dt_pad, gnorm, bd256, eb, eg, batch, seq):
    tt = T_GDN
    nt = seq // tt
    hb = tt // SUBLANES
    return pl.pallas_call(
        functools.partial(_gdn_kernel, tt=tt),
        grid=(batch, nt),
        in_specs=[pl.BlockSpec((tt, GM_W), lambda b, t: (b * nt + t, GM_OFF // GM_W)),
                  pl.BlockSpec((SUBLANES, GM_W),
                               lambda b, t: (jnp.maximum((b * nt + t) * hb - 1, 0), GM_OFF // GM_W)),
                  pl.BlockSpec((tt, GG_W), lambda b, t: (b * nt + t, GG_OFF // GG_W)),
                  pl.BlockSpec((GDN_CONV, 3 * GDN_WIDTH), lambda b, t: (0, 0)),
                  pl.BlockSpec((1, LANES), lambda b, t: (0, 0)),
                  pl.BlockSpec((1, LANES), lambda b, t: (0, 0)),
                  pl.BlockSpec((1, GDN_WIDTH), lambda b, t: (0, 0)),
                  pl.BlockSpec((GDN_WIDTH, GDN_WIDTH), lambda b, t: (0, 0)),
                  pl.BlockSpec((LANES, GDN_WIDTH), lambda b, t: (0, 0)),
                  pl.BlockSpec((LANES, GDN_WIDTH), lambda b, t: (0, 0))],
        out_specs=pl.BlockSpec((tt, GDN_WIDTH), lambda b, t: (b * nt + t, 0)),
        out_shape=jax.ShapeDtypeStruct((batch * seq, GDN_WIDTH), MXU_DTYPE),
        scratch_shapes=[pltpu.VMEM((GDN_WIDTH, GDN_WIDTH), F32)],
        compiler_params=pltpu.CompilerParams(dimension_semantics=("parallel", "arbitrary"),
                                             vmem_limit_bytes=VMEM_LIMIT),
        name="gdn",
    )(proj, proj, proj, conv_w, alog_pad, dt_pad, gnorm, bd256, eb, eg)


def _rope(x, c, s1, s2):
    wdt = x.shape[1]
    return x * c + pltpu.roll(x, wdt - ROT_DIM // 2, 1) * s1 + pltpu.roll(x, ROT_DIM // 2, 1) * s2


def _nsa_operands(n_blk, t_ref, ex_ref, cp_ref, qg_ref, kg_ref, bdq_ref, bdk_ref,
                  qn_ref, qr_ref, ksa_ref, kw_ref, vs_ref, vw_ref, gt_ref, *, tt, seq, between):
    between(3)
    table = t_ref[...]
    c1 = _dot_hi2_l(table, ex_ref[0]) + cp_ref[...]
    s1 = _dot_hi2_l(table, ex_ref[1])
    s2 = _dot_hi2_l(table, ex_ref[2])
    c4 = jnp.concatenate([c1] * 4, axis=1)
    s14 = jnp.concatenate([s1] * 4, axis=1)
    s24 = jnp.concatenate([s2] * 4, axis=1)
    q = n_blk[:, 0:NSA_WIDTH]
    qn = q * lax.rsqrt(_dot_hi2_l(q * q, bdq_ref[...]) * (1.0 / HEAD) + EPS) * qg_ref[...]
    qn = qn * (HEAD ** -0.5 * LOG2E)
    qn_ref[...] = qn.astype(qn_ref.dtype)
    qr_ref[...] = _rope(qn, c4, s14, s24).astype(qr_ref.dtype)
    between(3)
    bdk = bdk_ref[...]
    ks = n_blk[:, N_KS:N_KS + LANES]
    kw = n_blk[:, N_KW:N_KW + LANES]
    ks = ks * lax.rsqrt(_dot_hi2_l(ks * ks, bdk) * (1.0 / HEAD) + EPS) * kg_ref[0:1, :]
    kw = kw * lax.rsqrt(_dot_hi2_l(kw * kw, bdk) * (1.0 / HEAD) + EPS) * kg_ref[1:2, :]
    ksa_ref[:, 0:LANES] = _rope(ks, c1, s1, s2).astype(ksa_ref.dtype)
    rows = (lax.broadcasted_iota(jnp.int32, (tt, LANES), 0) + pl.program_id(0) * tt) % seq
    lane = lax.broadcasted_iota(jnp.int32, (tt, LANES), 1)
    ksa_ref[:, LANES:2 * LANES] = jnp.where((rows // SLC_BLOCK) % LANES == lane, 1.0, 0.0).astype(ksa_ref.dtype)
    kw_ref[...] = _rope(kw, c1, s1, s2).astype(kw_ref.dtype)
    vs_ref[...] = n_blk[:, N_VS:N_VS + LANES].astype(vs_ref.dtype)
    vw_ref[...] = n_blk[:, N_VW:N_VW + LANES].astype(vw_ref.dtype)
    gt_ref[...] = jax.nn.sigmoid(n_blk[:, N_GATE:N_GATE + LANES])


def _compress_kernel(xk_ref, xv_ref, w1bd_ref, pe_ref, w1_ref, w2bd_ref, kg_ref, bd_ref, o_ref, *, nb):
    acc = None
    for t in range(CMP_STRIDE):
        xt = jnp.concatenate([xk_ref[pl.ds(t, nb, stride=CMP_STRIDE), :],
                              xv_ref[pl.ds(t, nb, stride=CMP_STRIDE), :]], axis=1)
        part = _dot(xt, w1bd_ref[t])
        acc = part if acc is None else acc + part
    pe_terms = [_dot(jnp.broadcast_to(pe_ref[c:c + 1, :], (SUBLANES, CMP_LEN * HEAD)), w1_ref[c])[0:1, :]
                for c in range(2)]
    pe_all = jnp.concatenate([pe_terms[0]] * NSA_KV_HEADS + [pe_terms[1]] * NSA_KV_HEADS, axis=1)
    half = 2 * NSA_KV_WIDTH
    pre = acc[:, 0:half] + pltpu.roll(acc[:, half:2 * half], nb - 1, 0) + pe_all
    y = _dot(jax.nn.gelu(pre), w2bd_ref[...])
    rows = lax.broadcasted_iota(jnp.int32, (nb, 1), 0)
    y = jnp.where(rows < nb - 1, y, 0.0)
    yk = y[:, 0:NSA_KV_WIDTH]
    yk = yk * lax.rsqrt(_dot_hi2_l(yk * yk, bd_ref[...]) * (1.0 / HEAD) + EPS) * kg_ref[...]
    o_ref[0, 0] = yk.astype(o_ref.dtype)
    o_ref[1, 0] = y[:, NSA_KV_WIDTH:2 * NSA_KV_WIDTH].astype(o_ref.dtype)


def _compress(proj, w1bd, pe, w1, w2bd, kg0, bd128, batch, seq):
    nb = seq // CMP_STRIDE
    full = lambda *s: pl.BlockSpec(s, lambda b: (0,) * len(s))
    return pl.pallas_call(
        functools.partial(_compress_kernel, nb=nb),
        grid=(batch,),
        in_specs=[pl.BlockSpec((seq, NSA_KV_WIDTH), lambda b: (b, P_OFF // NSA_KV_WIDTH)),
                  pl.BlockSpec((seq, NSA_KV_WIDTH), lambda b: (b, P_OFF // NSA_KV_WIDTH + 1)),
                  full(CMP_STRIDE, P_W, 2 * P_W), full(2, CMP_LEN * HEAD), full(2, CMP_LEN * HEAD, HEAD),
                  full(P_W, P_W), full(1, NSA_KV_WIDTH), full(LANES, LANES)],
        out_specs=pl.BlockSpec((2, 1, nb, NSA_KV_WIDTH), lambda b: (0, b, 0, 0)),
        out_shape=jax.ShapeDtypeStruct((2, batch, nb, NSA_KV_WIDTH), MXU_DTYPE),
        compiler_params=pltpu.CompilerParams(dimension_semantics=("parallel",),
                                             vmem_limit_bytes=VMEM_LIMIT),
        name="compress",
    )(proj, proj, w1bd, pe, w1, w2bd, kg0, bd128)


def _nsa_attn_kernel(*refs, tq, seq, nwb, kc, ncp, nslp, shifted):
    qn_ref, qr_ref, gt_ref, kcmp_ref, vcmp_ref, ksa_ref, vs_ref = refs[:7]
    kw_refs = refs[7:7 + nwb]
    vw_refs = refs[7 + nwb:7 + 2 * nwb]
    msel_ref, cs_ref, egate_ref = refs[7 + 2 * nwb:10 + 2 * nwb]
    o_ref = refs[10 + 2 * nwb]
    bias_scr, qaug_scr, m_scr, l_scr, acc_scr, ow_scr, cand_scr, oc_scr, slc_scr, e_scr = refs[11 + 2 * nwb:]
    nslot = NSA_HEADS
    i = pl.program_id(1)
    c_cmp, c_slc, c_win = cs_ref[0:1, 0:1], cs_ref[0:1, 1:2], cs_ref[0:1, 2:3]
    lane = lax.broadcasted_iota(jnp.int32, (1, LANES), 1)
    lo = lane < HEAD

    def stack(q_ref):
        sl = [q_ref[:, j * LANES:(j + 1) * LANES] for j in range(NSA_GROUP)]
        zero = jnp.zeros_like(sl[0])
        return jnp.concatenate([jnp.where(lo, s, zero) for s in sl] + [jnp.where(lo, zero, s) for s in sl], axis=0)

    tpos = i * tq + lax.broadcasted_iota(jnp.int32, (tq, 1), 0)
    tpos_st = i * tq + lax.broadcasted_iota(jnp.int32, (nslot * tq, 1), 0) % tq
    gates = gt_ref[...]

    qn_st = stack(qn_ref)
    if shifted:
        ccw = min(2 * LANES, ncp)
        n_vis = (i + 1) * (tq // CMP_STRIDE) - 1
        n_cch = (n_vis + ccw - 1) // ccw
        qaug_scr[:, 0:LANES] = qn_st
        l_scr[...] = jnp.zeros(l_scr.shape, F32)
        acc_scr[...] = jnp.zeros(acc_scr.shape, F32)

        def cmp_scores(ch, carry):
            k0 = pl.multiple_of(ch * ccw, ccw)
            s = _dot_nt(qaug_scr[:, 0:LANES], kcmp_ref[0, 0, pl.ds(k0, ccw), :])
            ckey = k0 + lax.broadcasted_iota(jnp.int32, (1, ccw), 1)
            vis = (ckey * CMP_STRIDE + (CMP_LEN - 1)) <= tpos
            cbias = jnp.where(vis, -c_cmp, -SEL_BIG)
            for slot in range(nslot):
                rs = slice(slot * tq, (slot + 1) * tq)
                e = jnp.exp2(s[rs] + cbias)
                e_scr[rs, pl.ds(k0, ccw)] = e
                part = e[:, 0:LANES]
                for t in range(1, ccw // LANES):
                    part = part + e[:, t * LANES:(t + 1) * LANES]
                l_scr[rs, :] += part
            acc_scr[...] += _dot(e_scr[:, pl.ds(k0, ccw)], vcmp_ref[0, 0, pl.ds(k0, ccw), :])
            return carry

        lax.fori_loop(0, n_cch, cmp_scores, 0)
        l_c = jnp.sum(l_scr[...], axis=-1, keepdims=True)
        rinv_c = 1.0 / jnp.where(l_c > 0.0, l_c, 1.0)
        oc_scr[...] = acc_scr[...] * rinv_c
        l_scr[...] = jnp.broadcast_to(rinv_c, l_scr.shape)
        slc_scr[...] = jnp.zeros(slc_scr.shape, F32)

        def cmp_importance(ch, carry):
            k0 = pl.multiple_of(ch * ccw, ccw)
            for g in range(NSA_KV_HEADS):
                ps = None
                for r in range(NSA_GROUP):
                    rs = slice((g * NSA_GROUP + r) * tq, (g * NSA_GROUP + r + 1) * tq)
                    rinv_l = jnp.concatenate([l_scr[rs, :]] * (ccw // LANES), axis=1)
                    p = e_scr[rs, pl.ds(k0, ccw)] * rinv_l
                    ps = p if ps is None else ps + p
                slc_scr[g] += _dot_hi_l(ps, msel_ref[pl.ds(k0, ccw), :])
            return carry

        lax.fori_loop(0, n_cch, cmp_importance, 0)
        slc_rows = [slc_scr[g] for g in range(NSA_KV_HEADS)]
    else:
        ckey = lax.broadcasted_iota(jnp.int32, (1, ncp), 1)
        cmask = (ckey * CMP_STRIDE + (CMP_LEN - 1)) <= tpos
        s_all = _dot_nt(qn_st, kcmp_ref[0, 0])
        psum = [None, None]
        e_parts, rinv_parts = [], []
        for slot in range(nslot):
            s = jnp.where(cmask, s_all[slot * tq:(slot + 1) * tq], NEG_INF)
            e = jnp.where(cmask, jnp.exp2(s - jnp.max(s, axis=-1, keepdims=True)), 0.0)
            l = jnp.sum(e, axis=-1, keepdims=True)
            rinv = 1.0 / jnp.where(l > 0.0, l, 1.0)
            e_parts.append(e.astype(MXU_DTYPE))
            rinv_parts.append(rinv)
            p = e * rinv
            g = slot // NSA_GROUP
            psum[g] = p if psum[g] is None else psum[g] + p
        oc_scr[...] = (_dot(jnp.concatenate(e_parts, axis=0), vcmp_ref[0, 0])
                       * jnp.concatenate(rinv_parts, axis=0))
        slc_rows = [_dot_hi_l(psum[g], msel_ref[...]) for g in range(NSA_KV_HEADS)]

    qr_st = stack(qr_ref)
    kwc = jnp.concatenate([r[...] for r in kw_refs], axis=0)
    vwc = jnp.concatenate([r[...] for r in vw_refs], axis=0)
    sw = _dot_nt(qr_st, kwc)
    if shifted:
        rr = lax.broadcasted_iota(jnp.int32, (tq, tq), 0)
        cc = lax.broadcasted_iota(jnp.int32, (tq, tq), 1)
        parts = []
        for jb in range(nwb):
            shift_b = c_win + jnp.where(i - (nwb - 1) + jb >= 0, 0.0, SEL_BIG)
            blk = jnp.exp2(sw[:, jb * tq:(jb + 1) * tq] - shift_b)
            if jb == 0 or jb == nwb - 1:
                vis = (cc > rr) if jb == 0 else (cc <= rr)
                blk = jnp.concatenate([jnp.where(vis, blk[s * tq:(s + 1) * tq], 0.0) for s in range(nslot)],
                                      axis=0)
            parts.append(blk)
        ew = jnp.concatenate(parts, axis=1)
    else:
        kpos_w = (i - (nwb - 1)) * tq + lax.broadcasted_iota(jnp.int32, (1, nwb * tq), 1)
        dist = tpos_st - kpos_w
        wmask = (dist >= 0) & (dist < WINDOW) & (kpos_w >= 0)
        sw = jnp.where(wmask, sw, NEG_INF)
        ew = jnp.exp2(sw - jnp.max(sw, axis=-1, keepdims=True))
    ow_scr[...] = _dot(ew, vwc) * (1.0 / jnp.sum(ew, axis=-1, keepdims=True))

    jrow = lax.broadcasted_iota(jnp.int32, (nslp, 1), 0)
    jrowf = jrow.astype(F32)
    cur_t = (i * tq + lax.broadcasted_iota(jnp.int32, (1, tq), 1)) // SLC_BLOCK
    forced = (jrow == 0) | (jrow == cur_t) | (jrow == cur_t - 1)
    causal = jrow <= cur_t
    shift_s = c_slc if shifted else 0.0
    n_free = N_SELECT - 3
    to_bias = lambda sel: ((sel - 1.0) * SEL_BIG - shift_s).T.astype(bias_scr.dtype)
    n_bad = None
    for g in range(NSA_KV_HEADS):
        slc = slc_rows[g].T
        cand = jnp.where(causal & jnp.logical_not(forced), slc, -1.0)
        cand_scr[g] = cand
        c = cand
        for _ in range(n_free):
            c = jnp.where(c == jnp.max(c, axis=0, keepdims=True), -2.0, c)
        picked = (c == -2.0) & (cand >= 0.0)
        n_picked = jnp.sum(jnp.where(picked, 1.0, 0.0), axis=0, keepdims=True)
        n_real = jnp.sum(jnp.where(cand >= 0.0, 1.0, 0.0), axis=0, keepdims=True)
        bad = jnp.where(n_picked == jnp.minimum(n_real, float(n_free)), 0.0, 1.0)
        n_bad = bad if n_bad is None else n_bad + bad
        bias_scr[g] = to_bias(jnp.where(forced | picked, 1.0, 0.0))

    @pl.when(jnp.max(n_bad) > 0.0)
    def _():
        for g in range(NSA_KV_HEADS):
            cand = cand_scr[g]
            sel = jnp.where(forced, 1.0, 0.0)
            for _ in range(n_free):
                mx = jnp.max(cand, axis=0, keepdims=True)
                first = jnp.min(jnp.where(cand == mx, jrowf, float(nslp)), axis=0, keepdims=True)
                hit = jrowf == first
                sel = jnp.where(hit, 1.0, sel)
                cand = jnp.where(hit, -2.0, cand)
            bias_scr[g] = to_bias(jnp.where(causal, sel, 0.0))

    qaug_scr[:, 0:LANES] = qr_st
    if not shifted:
        m_scr[...] = jnp.full(m_scr.shape, NEG_INF, F32)
    l_scr[...] = jnp.zeros(l_scr.shape, F32)
    acc_scr[...] = jnp.zeros(acc_scr.shape, F32)
    group_keys = LANES * SLC_BLOCK

    def scores(key0, width):
        @pl.when(key0 % group_keys == 0)
        def _():
            off = pl.multiple_of((key0 // group_keys) * LANES, LANES)
            b0 = bias_scr[0, :, pl.ds(off, LANES)]
            b1 = bias_scr[1, :, pl.ds(off, LANES)]
            qaug_scr[:, LANES:2 * LANES] = jnp.concatenate([b0] * NSA_GROUP + [b1] * NSA_GROUP, axis=0)

        return _dot_nt(qaug_scr[...], ksa_ref[pl.ds(pl.multiple_of(key0, width), width), :])

    def accumulate(s, key0, width):
        k0 = pl.multiple_of(key0, width)
        if shifted:
            p = jnp.exp2(s)
            psum_l = p[:, 0:LANES]
            for t in range(1, width // LANES):
                psum_l = psum_l + p[:, t * LANES:(t + 1) * LANES]
            l_scr[...] += psum_l
            acc_scr[...] += _dot(p, vs_ref[pl.ds(k0, width), :])
        else:
            m_old = m_scr[...]
            m_new = jnp.maximum(m_old, jnp.max(s, axis=-1, keepdims=True))
            alpha = jnp.exp2(m_old - m_new)
            p = jnp.exp2(s - m_new[:, 0:1])
            l_scr[...] = alpha * l_scr[...] + jnp.sum(p, axis=-1, keepdims=True)
            acc_scr[...] = alpha * acc_scr[...] + _dot(p, vs_ref[pl.ds(k0, width), :])
            m_scr[...] = m_new

    n_full = (i * tq) // kc
    n_sub = (i * tq - n_full * kc) // tq

    def body_full(c, carry):
        accumulate(scores(c * kc, kc), c * kc, kc)
        return carry

    lax.fori_loop(0, n_full, body_full, 0)

    def body_sub(j, carry):
        key0 = n_full * kc + j * tq
        accumulate(scores(key0, tq), key0, tq)
        return carry

    lax.fori_loop(0, n_sub, body_sub, 0)
    col = lax.broadcasted_iota(jnp.int32, (1, tq), 1)
    accumulate(jnp.where(col <= tpos_st - i * tq, scores(i * tq, tq), -SEL_BIG), i * tq, tq)
    if shifted:
        o_s = acc_scr[...] * (1.0 / jnp.sum(l_scr[...], axis=-1, keepdims=True))
    else:
        o_s = acc_scr[...] / l_scr[...]
    o_w = ow_scr[...]

    gate_e = _dot_hi2_l(gates, egate_ref[...])
    for j in range(NSA_GROUP):
        r0 = slice(j * tq, (j + 1) * tq)
        r1 = slice((NSA_GROUP + j) * tq, (NSA_GROUP + j + 1) * tq)
        out = None
        for br, o_b in enumerate((oc_scr[...], o_s, o_w)):
            lanes = slice(br * NSA_WIDTH + j * LANES, br * NSA_WIDTH + (j + 1) * LANES)
            term = gate_e[:, lanes] * jnp.where(lo, o_b[r0], o_b[r1])
            out = term if out is None else out + term
        o_ref[:, j * LANES:(j + 1) * LANES] = out.astype(o_ref.dtype)


def _gate_expander():
    mat = np.zeros((LANES, 3 * NSA_WIDTH), np.float32)
    for br in range(3):
        for j in range(NSA_GROUP):
            for g in range(NSA_KV_HEADS):
                head = g * NSA_GROUP + j
                c0 = br * NSA_WIDTH + j * LANES + g * HEAD
                mat[3 * head + br, c0:c0 + HEAD] = 1.0
    return jnp.asarray(mat, MXU_DTYPE)


def _nsa_attn(qn, qr, gt, kvc, ksa, vs, kw, vw, msel, cshift, batch, seq, shifted):
    tq, kc = (TQ if shifted else TQ_RUNMAX), KC
    nq = seq // tq
    nwb = WINDOW // tq + 1
    ncp = seq // CMP_STRIDE
    nslp = msel.shape[1]
    row = lambda w: pl.BlockSpec((tq, w), lambda b, i: (b * nq + i, 0))
    once = pl.Buffered(1)
    win = lambda jb: pl.BlockSpec((tq, LANES), lambda b, i: (b * nq + jnp.maximum(i - (nwb - 1) + jb, 0), 0))
    in_specs = ([row(NSA_WIDTH), row(NSA_WIDTH), row(LANES),
                 pl.BlockSpec((1, 1, ncp, LANES), lambda b, i: (0, b, 0, 0), pipeline_mode=once),
                 pl.BlockSpec((1, 1, ncp, LANES), lambda b, i: (1, b, 0, 0), pipeline_mode=once),
                 pl.BlockSpec((seq, 2 * LANES), lambda b, i: (b, 0), pipeline_mode=once),
                 pl.BlockSpec((seq, LANES), lambda b, i: (b, 0), pipeline_mode=once)]
                + [win(jb) for jb in range(nwb)] + [win(jb) for jb in range(nwb)]
                + [pl.BlockSpec((ncp, nslp), lambda b, i: (0, 0), pipeline_mode=once),
                   pl.BlockSpec((1, LANES), lambda b, i: (0, 0)),
                   pl.BlockSpec((LANES, 3 * NSA_WIDTH), lambda b, i: (0, 0), pipeline_mode=once)])
    return pl.pallas_call(
        functools.partial(_nsa_attn_kernel, tq=tq, seq=seq, nwb=nwb, kc=kc, ncp=ncp, nslp=nslp,
                          shifted=shifted),
        grid=(batch, nq),
        in_specs=in_specs,
        out_specs=row(NSA_WIDTH),
        out_shape=jax.ShapeDtypeStruct((batch * seq, NSA_WIDTH), MXU_DTYPE),
        scratch_shapes=[pltpu.VMEM((NSA_KV_HEADS, tq, nslp), MXU_DTYPE),
                        pltpu.VMEM((NSA_HEADS * tq, 2 * LANES), MXU_DTYPE),
                        pltpu.VMEM((NSA_HEADS * tq, LANES), F32),
                        pltpu.VMEM((NSA_HEADS * tq, LANES), F32),
                        pltpu.VMEM((NSA_HEADS * tq, LANES), F32),
                        pltpu.VMEM((NSA_HEADS * tq, LANES), F32),
                        pltpu.VMEM((NSA_KV_HEADS, nslp, tq), F32),
                        pltpu.VMEM((NSA_HEADS * tq, LANES), F32),
                        pltpu.VMEM((NSA_KV_HEADS, tq, nslp), F32),
                        pltpu.VMEM((NSA_HEADS * tq, ncp) if shifted else (SUBLANES, LANES), F32)],
        compiler_params=pltpu.CompilerParams(dimension_semantics=("parallel", "arbitrary"),
                                             vmem_limit_bytes=VMEM_LIMIT),
        name="nsa_attn" if shifted else "nsa_attn_runmax",
    )(qn, qr, gt, kvc, kvc, ksa, vs, *([kw] * nwb), *([vw] * nwb), msel, cshift, _gate_expander())


def _out_ffn_kernel(x_ref, og_ref, on_ref, c_ref, ch_ref, cw_ref, wo_ref, fg_ref, wgu_ref, wd_ref, o_ref,
                    *, tm, seq):
    i = pl.program_id(0)
    keep = jnp.where((i * tm) % seq == 0, 0.0, 1.0)
    rows = lax.broadcasted_iota(jnp.int32, (tm, 1), 0)
    u = c_ref[:, CONV_WIDTH:2 * CONV_WIDTH] * c_ref[:, 2 * CONV_WIDTH:3 * CONV_WIDTH]
    hu = ch_ref[:, CONV_WIDTH:2 * CONV_WIDTH] * ch_ref[:, 2 * CONV_WIDTH:3 * CONV_WIDTH] * keep
    w = cw_ref[...]
    conv = w[CONV_K - 1:CONV_K, :] * u
    for s in range(1, CONV_K):
        conv = conv + w[CONV_K - 1 - s:CONV_K - s, :] * _shift_rows(u, hu, s, rows)
    oc = c_ref[:, 0:CONV_WIDTH] * conv
    x1 = (x_ref[...] + _dot(og_ref[...], wo_ref[0:GDN_WIDTH, :])
          + _dot(on_ref[...], wo_ref[GDN_WIDTH:GDN_WIDTH + NSA_WIDTH, :])
          + _dot(oc, wo_ref[GDN_WIDTH + NSA_WIDTH:, :]))
    ms = jnp.mean(x1 * x1, axis=-1, keepdims=True)
    h2 = (x1 * lax.rsqrt(ms + EPS) * fg_ref[...]).astype(MXU_DTYPE)
    o_ref[...] = x1
    for c0 in range(0, D_FF, FF_CHUNK):
        gate = jnp.dot(h2, wgu_ref[:, c0:c0 + FF_CHUNK], preferred_element_type=F32)
        up = jnp.dot(h2, wgu_ref[:, D_FF + c0:D_FF + c0 + FF_CHUNK], preferred_element_type=F32)
        o_ref[...] += _dot(_silu(gate) * up, wd_ref[c0:c0 + FF_CHUNK, :])


def _out_ffn(x2, o_gdn, o_nsa, proj, conv_w, w_out, fgain, wgu, wd, seq):
    m = x2.shape[0]
    tm = TM_FFN
    hb = tm // SUBLANES
    full = lambda a, b: pl.BlockSpec((a, b), lambda i: (0, 0), pipeline_mode=pl.Buffered(1))
    return pl.pallas_call(
        functools.partial(_out_ffn_kernel, tm=tm, seq=seq),
        grid=(m // tm,),
        in_specs=[pl.BlockSpec((tm, D_MODEL), lambda i: (i, 0)),
                  pl.BlockSpec((tm, GDN_WIDTH), lambda i: (i, 0)),
                  pl.BlockSpec((tm, NSA_WIDTH), lambda i: (i, 0)),
                  pl.BlockSpec((tm, C_W), lambda i: (i, C_OFF // C_W)),
                  pl.BlockSpec((SUBLANES, C_W), lambda i: (jnp.maximum(i * hb - 1, 0), C_OFF // C_W)),
                  full(CONV_K, CONV_WIDTH), full(D_MODEL, D_MODEL), full(1, D_MODEL),
                  full(D_MODEL, 2 * D_FF), full(D_FF, D_MODEL)],
        out_specs=pl.BlockSpec((tm, D_MODEL), lambda i: (i, 0)),
        out_shape=jax.ShapeDtypeStruct((m, D_MODEL), F32),
        compiler_params=pltpu.CompilerParams(dimension_semantics=("parallel",),
                                             vmem_limit_bytes=VMEM_LIMIT),
        name="out_ffn",
    )(x2, o_gdn, o_nsa, proj, proj, conv_w, w_out, fgain, wgu, wd)


def _proj_column_map():
    offs = np.concatenate([[0], np.cumsum(IN_SIZES)])
    seg = lambda k: np.arange(offs[k], offs[k + 1])
    pad = lambda n: -np.ones(n, np.int64)
    nq = seg(6).reshape(NSA_HEADS, HEAD)[list(Q_SLOT_HEADS)].reshape(-1)
    cols = np.concatenate([
        seg(14), seg(15), seg(16),
        seg(7), seg(8),
        seg(0), seg(1), seg(2), seg(3),
        seg(4), seg(5), pad(GG_W - 2 * GDN_HEADS),
        nq, seg(9), seg(11), seg(10), seg(12), seg(13), pad(LANES - 3 * NSA_HEADS)])
    assert cols.shape[0] == W_COLS
    return cols


def _block_diag_ones(n):
    idx = np.arange(n) // HEAD
    return jnp.asarray(idx[:, None] == idx[None, :], MXU_DTYPE)


def _head_expander(first_lane):
    mat = np.zeros((LANES, GDN_WIDTH), np.float32)
    for h in range(GDN_HEADS):
        mat[first_lane + h, h * HEAD:(h + 1) * HEAD] = 1.0
    return jnp.asarray(mat, MXU_DTYPE)


def _compress_weights(w1, w2):
    nslab = 2 * NSA_KV_HEADS
    slab_kv = np.arange(nslab) // NSA_KV_HEADS
    eye = jnp.eye(nslab, dtype=MXU_DTYPE)
    w1r = w1.astype(MXU_DTYPE).reshape(2, 2, CMP_STRIDE, HEAD, HEAD)[slab_kv]
    w1bd = jnp.einsum("shtde,sS->tsdhSe", w1r, eye)
    w2bd = jnp.einsum("sde,sS->sdSe", w2.astype(MXU_DTYPE)[slab_kv], eye)
    return w1bd.reshape(CMP_STRIDE, P_W, 2 * P_W), w2bd.reshape(P_W, P_W)


def _rope_table(positions):
    half = ROT_DIM // 2
    inv = jnp.float32(ROPE_THETA) ** (-jnp.arange(0, ROT_DIM, 2, dtype=jnp.float32) / ROT_DIM)
    inv_l = jnp.concatenate([inv, inv, jnp.zeros((LANES - ROT_DIM,), F32)])
    ang = positions.astype(jnp.float32).reshape(-1)[:, None] * inv_l[None, :]
    lane = jnp.arange(LANES)[None, :]
    return jnp.where(lane < half, jnp.cos(ang), jnp.where(lane < ROT_DIM, jnp.sin(ang), 0.0))


def _rope_expanders():
    half = ROT_DIM // 2
    ex = np.zeros((3, LANES, LANES), np.float32)
    cpat = np.zeros((1, LANES), np.float32)
    for j in range(LANES):
        d = j % HEAD
        if d < ROT_DIM:
            ex[0, d % half, j] = 1.0
        else:
            cpat[0, j] = 1.0
        if d < half:
            ex[1, half + d, j] = -1.0
        elif d < ROT_DIM:
            ex[2, half + (d - half), j] = 1.0
    return jnp.asarray(ex, MXU_DTYPE), jnp.asarray(cpat, F32)


def _selection_matrix(ncp, nslp):
    ratio = SLC_BLOCK // CMP_STRIDE
    frac = np.minimum(CMP_LEN, SLC_BLOCK - CMP_STRIDE * np.arange(ratio)).astype(np.float64) / CMP_LEN
    mat = np.zeros((ncp, nslp), np.float32)
    c = np.arange(ncp)
    mat[c, c // ratio] = frac[c % ratio]
    nxt = c // ratio + 1
    ok = nxt < nslp
    mat[c[ok], nxt[ok]] += (1.0 - frac[c % ratio])[ok]
    return jnp.asarray(mat, MXU_DTYPE)


def kernel(x, positions, attn_norm, w_in, gdn_conv_w, gdn_a_log, gdn_dt_bias, gdn_norm, nsa_q_norm,
           nsa_k_norm, nsa_cmp_pe, nsa_cmp_w1, nsa_cmp_w2, conv_w, w_out, ffn_norm, w_gate_up, w_down):
    batch, seq, _ = x.shape
    depth = w_in.shape[0]
    m = batch * seq
    assert seq % max(TM_PROJ, TM_FFN, T_GDN, KC) == 0 and (seq // CMP_STRIDE) % LANES == 0
    nb = seq // CMP_STRIDE
    nslp = -(-(seq // SLC_BLOCK) // LANES) * LANES

    cols = _proj_column_map()
    take = jnp.asarray(np.maximum(cols, 0), jnp.int32)
    valid = jnp.asarray(cols >= 0)
    assert Q_SLOT_HEADS == tuple(g * NSA_GROUP + j for j in range(NSA_GROUP) for g in range(NSA_KV_HEADS))
    bd128, bd256, bd512 = _block_diag_ones(128), _block_diag_ones(256), _block_diag_ones(512)
    rope_t = _rope_table(positions)
    rope_ex, rope_cp = _rope_expanders()
    msel = _selection_matrix(nb, nslp)
    lane_pad = lambda v: jnp.zeros((1, LANES), F32).at[0, GDN_HEADS:2 * GDN_HEADS].set(v.astype(F32))

    x2 = x.reshape(m, D_MODEL)
    for l in range(depth):
        w_l = jnp.where(valid[None, :], jnp.take(w_in[l], take, axis=1), 0.0).astype(MXU_DTYPE)
        qg = jnp.tile(nsa_q_norm[l], NSA_HEADS).reshape(1, NSA_WIDTH)
        kg = jnp.tile(nsa_k_norm[l, 1:3], (1, NSA_KV_HEADS))
        proj, qn, qr, ksa, kw, vs, vw, gt = _in_proj(x2, attn_norm[l].reshape(1, D_MODEL), w_l, rope_t, rope_ex,
                                                     rope_cp, qg, kg, bd512, bd128, seq)

        o_gdn = _gdn(proj, gdn_conv_w[l], lane_pad(gdn_a_log[l]), lane_pad(gdn_dt_bias[l]),
                     jnp.tile(gdn_norm[l], GDN_HEADS).reshape(1, GDN_WIDTH), bd256,
                     _head_expander(0), _head_expander(GDN_HEADS), batch, seq)

        w1bd, w2bd = _compress_weights(nsa_cmp_w1[l], nsa_cmp_w2[l])
        kvc = _compress(proj, w1bd, nsa_cmp_pe[l].reshape(2, CMP_LEN * HEAD), nsa_cmp_w1[l].astype(MXU_DTYPE),
                        w2bd, jnp.tile(nsa_k_norm[l, 0], NSA_KV_HEADS).reshape(1, NSA_KV_WIDTH), bd128,
                        batch, seq)

        bound = (HEAD ** 0.5) * jnp.max(jnp.abs(nsa_q_norm[l])) * jnp.max(jnp.abs(nsa_k_norm[l]), axis=1)
        cshift = jnp.zeros((1, LANES), F32).at[0, 0:3].set((bound * LOG2E).astype(F32))
        attn_args = (qn, qr, gt, kvc, ksa, vs, kw, vw, msel, cshift)
        o_nsa = lax.cond(
            jnp.max(bound) <= MAX_SOFTMAX_SHIFT,
            lambda a: _nsa_attn(*a, batch, seq, True),
            lambda a: _nsa_attn(*a, batch, seq, False),
            attn_args)

        wo = w_out[l].astype(MXU_DTYPE)
        wo_nsa = wo[GDN_WIDTH:GDN_WIDTH + NSA_WIDTH].reshape(NSA_KV_HEADS, NSA_GROUP, HEAD, D_MODEL)
        wo = jnp.concatenate([wo[:GDN_WIDTH], wo_nsa.transpose(1, 0, 2, 3).reshape(NSA_WIDTH, D_MODEL),
                              wo[GDN_WIDTH + NSA_WIDTH:]], axis=0)
        x2 = _out_ffn(x2, o_gdn, o_nsa, proj, conv_w[l], wo,
                      ffn_norm[l].reshape(1, D_MODEL), w_gate_up[l].astype(MXU_DTYPE),
                      w_down[l].astype(MXU_DTYPE), seq)
    return x2.reshape(batch, seq, D_MODEL)
```

```python
import functools

import numpy as np
import jax
import jax.numpy as jnp
from jax import lax
from jax.experimental import pallas as pl
from jax.experimental.pallas import tpu as pltpu

F32 = jnp.float32
MXU_DTYPE = jnp.bfloat16

D_MODEL = 1024
HEAD = 64
EPS = 1e-6
NEG_INF = -1e30
GDN_HEADS = 4
GDN_WIDTH = 256
GDN_CONV = 4
GDN_CHUNK = 64
NSA_HEADS = 8
NSA_WIDTH = 512
NSA_KV_HEADS = 2
NSA_GROUP = 4
NSA_KV_WIDTH = 128
CMP_STRIDE = 16
CMP_LEN = 32
SLC_BLOCK = 64
N_SELECT = 16
WINDOW = 512
CONV_WIDTH = 256
CONV_K = 3
ROPE_THETA = 500000.0
ROT_DIM = 16
D_FF = 2816
IN_SIZES = (256, 256, 256, 256, 4, 4, 512, 128, 128, 128, 128, 128, 128, 24, 256, 256, 256)
D_IN = sum(IN_SIZES)

LANES = 128
SUBLANES = 8
VMEM_LIMIT = 56 * 1024 * 1024

C_W = 3 * CONV_WIDTH
P_W = 2 * NSA_KV_WIDTH
GM_W = 4 * GDN_WIDTH
GG_W = LANES
N_W = 512 + 4 * 128 + LANES
C_OFF, P_OFF, GM_OFF, GG_OFF = 0, C_W, C_W + P_W, C_W + P_W + GM_W
PROJ_W = C_W + P_W + GM_W + GG_W
N_OFF = PROJ_W
W_COLS = PROJ_W + N_W
Q_SLOT_HEADS = (0, 4, 1, 5, 2, 6, 3, 7)
N_KS, N_KW, N_VS, N_VW, N_GATE = (NSA_WIDTH + k * LANES for k in range(5))

SEL_BIG = 16384.0
MAX_SOFTMAX_SHIFT = 40.0
LOG2E = 1.4426950408889634

TM_PROJ = 512
TM_FFN = 512
T_GDN = 512
TQ = 256
TQ_RUNMAX = 128
KC = 1024
CHUNKS_PER_TRIP = 4
FF_CHUNK = 256


def _dot(a, b):
    return jnp.dot(a.astype(MXU_DTYPE), b.astype(MXU_DTYPE), preferred_element_type=F32)


def _dot_nt(a, b):
    return lax.dot_general(a.astype(MXU_DTYPE), b.astype(MXU_DTYPE), (((1,), (1,)), ((), ())),
                           preferred_element_type=F32)


def _dot_tn(a, b):
    return lax.dot_general(a.astype(MXU_DTYPE), b.astype(MXU_DTYPE), (((0,), (0,)), ((), ())),
                           preferred_element_type=F32)


def _split3(x):
    a = x.astype(MXU_DTYPE)
    r = x - a.astype(F32)
    b = r.astype(MXU_DTYPE)
    c = (r - b.astype(F32)).astype(MXU_DTYPE)
    return a, b, c


def _dot_hi_l(x, m):
    a, b, c = _split3(x)
    f = lambda t: jnp.dot(t, m, preferred_element_type=F32)
    return f(a) + f(b) + f(c)


def _dot_hi2_l(x, m):
    a = x.astype(MXU_DTYPE)
    b = (x - a.astype(F32)).astype(MXU_DTYPE)
    return jnp.dot(a, m, preferred_element_type=F32) + jnp.dot(b, m, preferred_element_type=F32)


def _silu(x):
    return x * jax.nn.sigmoid(x)


def _shift_rows(x, halo, s, rows):
    y = pltpu.roll(x, s, 0)
    for r in range(s):
        y = jnp.where(rows == r, halo[SUBLANES - s + r:SUBLANES - s + r + 1, :], y)
    return y


def _in_proj_kernel(x_ref, g_ref, w_ref, t_ref, ex_ref, cp_ref, qg_ref, kg_ref, bdq_ref, bdk_ref,
                    o_ref, qn_ref, qr_ref, ksa_ref, kw_ref, vs_ref, vw_ref, gt_ref, *, tm, seq):
    x = x_ref[...]
    ms = jnp.mean(x * x, axis=-1, keepdims=True)
    h = (x * lax.rsqrt(ms + EPS) * g_ref[...]).astype(MXU_DTYPE)
    pending = [(c, min(256, PROJ_W - c)) for c in range(0, PROJ_W, 256)]

    def emit(count):
        for _ in range(min(count, len(pending))):
            c, cw = pending.pop(0)
            o_ref[:, c:c + cw] = jnp.dot(h, w_ref[:, c:c + cw], preferred_element_type=F32)

    nblk = jnp.dot(h, w_ref[:, N_OFF:N_OFF + N_W], preferred_element_type=F32)
    _nsa_operands(nblk, t_ref, ex_ref, cp_ref, qg_ref, kg_ref, bdq_ref, bdk_ref,
                  qn_ref, qr_ref, ksa_ref, kw_ref, vs_ref, vw_ref, gt_ref, tt=tm, seq=seq, between=emit)
    emit(len(pending))


def _in_proj(x2, gain, w, rope_t, rope_ex, rope_cp, qg, kg, bd512, bd128, seq):
    m = x2.shape[0]
    tm = TM_PROJ
    row = lambda wd: pl.BlockSpec((tm, wd), lambda i: (i, 0))
    full = lambda a, b: pl.BlockSpec((a, b), lambda i: (0, 0))
    sds = lambda wd, dt: jax.ShapeDtypeStruct((m, wd), dt)
    return pl.pallas_call(
        functools.partial(_in_proj_kernel, tm=tm, seq=seq),
        grid=(m // tm,),
        in_specs=[row(D_MODEL), full(1, D_MODEL), full(D_MODEL, W_COLS), row(LANES),
                  pl.BlockSpec((3, LANES, LANES), lambda i: (0, 0, 0)), full(1, LANES),
                  full(1, NSA_WIDTH), full(2, LANES), full(NSA_WIDTH, NSA_WIDTH), full(LANES, LANES)],
        out_specs=[row(PROJ_W), row(NSA_WIDTH), row(NSA_WIDTH), row(2 * LANES), row(LANES), row(LANES),
                   row(LANES), row(LANES)],
        out_shape=[sds(PROJ_W, F32), sds(NSA_WIDTH, MXU_DTYPE), sds(NSA_WIDTH, MXU_DTYPE),
                   sds(2 * LANES, MXU_DTYPE), sds(LANES, MXU_DTYPE), sds(LANES, MXU_DTYPE),
                   sds(LANES, MXU_DTYPE), sds(LANES, F32)],
        compiler_params=pltpu.CompilerParams(dimension_semantics=("parallel",),
                                             vmem_limit_bytes=VMEM_LIMIT),
        name="in_proj",
    )(x2, gain, w, rope_t, rope_ex, rope_cp, qg, kg, bd512, bd128)


def _gdn_kernel(g_ref, gh_ref, gg_ref, cw_ref, alog_ref, dt_ref, gn_ref, bd_ref, eb_ref, eg_ref, o_ref, s_ref,
                *, tt):
    t_idx = pl.program_id(1)

    @pl.when(t_idx == 0)
    def _():
        s_ref[...] = jnp.zeros_like(s_ref)

    keep = jnp.where(t_idx == 0, 0.0, 1.0)
    rows = lax.broadcasted_iota(jnp.int32, (tt, 1), 0)
    x = g_ref[:, 0:3 * GDN_WIDTH]
    hx = gh_ref[:, 0:3 * GDN_WIDTH] * keep
    w = cw_ref[...]
    y = w[GDN_CONV - 1:GDN_CONV, :] * x
    for s in range(1, GDN_CONV):
        y = y + w[GDN_CONV - 1 - s:GDN_CONV - s, :] * _shift_rows(x, hx, s, rows)
    y = _silu(y)
    q = y[:, 0:GDN_WIDTH]
    k = y[:, GDN_WIDTH:2 * GDN_WIDTH]
    v = y[:, 2 * GDN_WIDTH:3 * GDN_WIDTH]
    bd = bd_ref[...]
    q = q * lax.rsqrt(_dot_hi2_l(q * q, bd) + EPS) * (HEAD ** -0.5)
    k = k * lax.rsqrt(_dot_hi2_l(k * k, bd) + EPS)

    gg = gg_ref[...]
    lane = lax.broadcasted_iota(jnp.int32, (1, LANES), 1)
    xa = gg + dt_ref[...]
    softplus = jnp.maximum(xa, 0.0) + jnp.log1p(jnp.exp(-jnp.abs(xa)))
    g2 = jnp.where(lane < GDN_HEADS, jax.nn.sigmoid(gg), -jnp.exp(alog_ref[...]) * softplus)
    cs = g2.T
    lane_t = lax.broadcasted_iota(jnp.int32, (1, tt), 1) % GDN_CHUNK
    step = 1
    while step < GDN_CHUNK:
        cs = cs + jnp.where(lane_t >= step, pltpu.roll(cs, step, 1), 0.0)
        step *= 2
    gcum_t = cs
    gcum = cs.T
    beta_e = _dot_hi_l(g2, eb_ref[...])
    gcum_e = _dot_hi_l(gcum, eg_ref[...])
    eg_e = jnp.exp(gcum_e)
    kb = k * beta_e
    rv = v * beta_e
    rk = kb * eg_e
    qd = q * eg_e

    nst = GDN_HEADS * GDN_CHUNK
    ri = lax.broadcasted_iota(jnp.int32, (nst, nst), 0)
    ci = lax.broadcasted_iota(jnp.int32, (nst, nst), 1)
    same = (ri // GDN_CHUNK) == (ci // GDN_CHUNK)
    m_tril = same & ((ri % GDN_CHUNK) >= (ci % GDN_CHUNK))
    m_strict = same & ((ri % GDN_CHUNK) > (ci % GDN_CHUNK))
    tile4 = lambda t: jnp.concatenate([t] * GDN_HEADS, axis=0)
    expand = lambda t: tile4(t.astype(MXU_DTYPE)) * bd

    nchunk = tt // GDN_CHUNK
    rss = [slice(n * GDN_CHUNK, (n + 1) * GDN_CHUNK) for n in range(nchunk)]
    g_last = [gcum_e[(n + 1) * GDN_CHUNK - 1:(n + 1) * GDN_CHUNK, :] for n in range(nchunk)]
    rmat, pw, qk = [], [], []
    for n, rs in enumerate(rss):
        g_row = jnp.concatenate([gcum_t[GDN_HEADS + h:GDN_HEADS + h + 1, rs] for h in range(GDN_HEADS)], axis=1)
        decay = jnp.where(m_tril, jnp.exp(jnp.where(m_tril, tile4(gcum_e[rs]) - g_row, 0.0)), 0.0)
        k4 = tile4(k[rs])
        a = jnp.where(m_strict, _dot_nt(expand(kb[rs]), k4) * decay, 0.0)
        qk.append(jnp.where(m_tril, _dot_nt(expand(q[rs]), k4) * decay, 0.0))
        rmat.append(-a)
        pw.append(-a)
    for _ in range(GDN_CHUNK.bit_length() - 2):
        nxt_pw, nxt_r = [], []
        for r, t in zip(rmat, pw):
            t2 = _dot(t, t)
            nxt_pw.append(t2)
            nxt_r.append(r + t2 + _dot(r, t2))
        pw, rmat = nxt_pw, nxt_r
    u, wm = [], []
    for n, rs in enumerate(rss):
        rv_x = jnp.where(same, tile4(rv[rs]), 0.0)
        rk_x = jnp.where(same, tile4(rk[rs]), 0.0)
        u.append(rv_x + _dot(rmat[n], rv_x))
        wm.append(rk_x + _dot(rmat[n], rk_x))

    st = s_ref[...]
    outs = []
    for n, rs in enumerate(rss):
        kd = k[rs] * jnp.exp(g_last[n] - gcum_e[rs])
        v_new = u[n] - _dot(wm[n], st)
        o_x = _dot(expand(qd[rs]), st) + _dot(qk[n], v_new)
        st = st * jnp.exp(g_last[n]) + _dot_tn(expand(kd), v_new)
        o = o_x[0:GDN_CHUNK]
        for h in range(1, GDN_HEADS):
            o = o + o_x[h * GDN_CHUNK:(h + 1) * GDN_CHUNK]
        outs.append(o)
    s_ref[...] = st
    o_all = jnp.concatenate(outs, axis=0)
    on = o_all * lax.rsqrt(_dot_hi2_l(o_all * o_all, bd) * (1.0 / HEAD) + EPS) * gn_ref[...]
    o_ref[...] = (on * _silu(g_ref[:, 3 * GDN_WIDTH:4 * GDN_WIDTH])).astype(o_ref.dtype)


def _gdn(proj, conv_w, alog_pad, dt_pad, gnorm, bd256, eb, eg, batch, seq):
    tt = T_GDN
    nt = seq // tt
    hb = tt // SUBLANES
    return pl.pallas_call(
        functools.partial(_gdn_kernel, tt=tt),
        grid=(batch, nt),
        in_specs=[pl.BlockSpec((tt, GM_W), lambda b, t: (b * nt + t, GM_OFF // GM_W)),
                  pl.BlockSpec((SUBLANES, GM_W),
                               lambda b, t: (jnp.maximum((b * nt + t) * hb - 1, 0), GM_OFF // GM_W)),
                  pl.BlockSpec((tt, GG_W), lambda b, t: (b * nt + t, GG_OFF // GG_W)),
                  pl.BlockSpec((GDN_CONV, 3 * GDN_WIDTH), lambda b, t: (0, 0)),
                  pl.BlockSpec((1, LANES), lambda b, t: (0, 0)),
                  pl.BlockSpec((1, LANES), lambda b, t: (0, 0)),
                  pl.BlockSpec((1, GDN_WIDTH), lambda b, t: (0, 0)),
                  pl.BlockSpec((GDN_WIDTH, GDN_WIDTH), lambda b, t: (0, 0)),
                  pl.BlockSpec((LANES, GDN_WIDTH), lambda b, t: (0, 0)),
                  pl.BlockSpec((LANES, GDN_WIDTH), lambda b, t: (0, 0))],
        out_specs=pl.BlockSpec((tt, GDN_WIDTH), lambda b, t: (b * nt + t, 0)),
        out_shape=jax.ShapeDtypeStruct((batch * seq, GDN_WIDTH), MXU_DTYPE),
        scratch_shapes=[pltpu.VMEM((GDN_WIDTH, GDN_WIDTH), F32)],
        compiler_params=pltpu.CompilerParams(dimension_semantics=("parallel", "arbitrary"),
                                             vmem_limit_bytes=VMEM_LIMIT),
        name="gdn",
    )(proj, proj, proj, conv_w, alog_pad, dt_pad, gnorm, bd256, eb, eg)


def _rope(x, c, s1, s2):
    wdt = x.shape[1]
    return x * c + pltpu.roll(x, wdt - ROT_DIM // 2, 1) * s1 + pltpu.roll(x, ROT_DIM // 2, 1) * s2


def _nsa_operands(n_blk, t_ref, ex_ref, cp_ref, qg_ref, kg_ref, bdq_ref, bdk_ref,
                  qn_ref, qr_ref, ksa_ref, kw_ref, vs_ref, vw_ref, gt_ref, *, tt, seq, between):
    between(3)
    table = t_ref[...]
    c1 = _dot_hi2_l(table, ex_ref[0]) + cp_ref[...]
    s1 = _dot_hi2_l(table, ex_ref[1])
    s2 = _dot_hi2_l(table, ex_ref[2])
    c4 = jnp.concatenate([c1] * 4, axis=1)
    s14 = jnp.concatenate([s1] * 4, axis=1)
    s24 = jnp.concatenate([s2] * 4, axis=1)
    q = n_blk[:, 0:NSA_WIDTH]
    qn = q * lax.rsqrt(_dot_hi2_l(q * q, bdq_ref[...]) * (1.0 / HEAD) + EPS) * qg_ref[...]
    qn = qn * (HEAD ** -0.5 * LOG2E)
    qn_ref[...] = qn.astype(qn_ref.dtype)
    qr_ref[...] = _rope(qn, c4, s14, s24).astype(qr_ref.dtype)
    between(3)
    bdk = bdk_ref[...]
    ks = n_blk[:, N_KS:N_KS + LANES]
    kw = n_blk[:, N_KW:N_KW + LANES]
    ks = ks * lax.rsqrt(_dot_hi2_l(ks * ks, bdk) * (1.0 / HEAD) + EPS) * kg_ref[0:1, :]
    kw = kw * lax.rsqrt(_dot_hi2_l(kw * kw, bdk) * (1.0 / HEAD) + EPS) * kg_ref[1:2, :]
    ksa_ref[:, 0:LANES] = _rope(ks, c1, s1, s2).astype(ksa_ref.dtype)
    rows = (lax.broadcasted_iota(jnp.int32, (tt, LANES), 0) + pl.program_id(0) * tt) % seq
    lane = lax.broadcasted_iota(jnp.int32, (tt, LANES), 1)
    ksa_ref[:, LANES:2 * LANES] = jnp.where((rows // SLC_BLOCK) % LANES == lane, 1.0, 0.0).astype(ksa_ref.dtype)
    kw_ref[...] = _rope(kw, c1, s1, s2).astype(kw_ref.dtype)
    vs_ref[...] = n_blk[:, N_VS:N_VS + LANES].astype(vs_ref.dtype)
    vw_ref[...] = n_blk[:, N_VW:N_VW + LANES].astype(vw_ref.dtype)
    gt_ref[...] = jax.nn.sigmoid(n_blk[:, N_GATE:N_GATE + LANES])


def _compress_kernel(xk_ref, xv_ref, w1bd_ref, pe_ref, w1_ref, w2bd_ref, kg_ref, bd_ref, o_ref, *, nb):
    acc = None
    for t in range(CMP_STRIDE):
        xt = jnp.concatenate([xk_ref[pl.ds(t, nb, stride=CMP_STRIDE), :],
                              xv_ref[pl.ds(t, nb, stride=CMP_STRIDE), :]], axis=1)
        part = _dot(xt, w1bd_ref[t])
        acc = part if acc is None else acc + part
    pe_terms = [_dot(jnp.broadcast_to(pe_ref[c:c + 1, :], (SUBLANES, CMP_LEN * HEAD)), w1_ref[c])[0:1, :]
                for c in range(2)]
    pe_all = jnp.concatenate([pe_terms[0]] * NSA_KV_HEADS + [pe_terms[1]] * NSA_KV_HEADS, axis=1)
    half = 2 * NSA_KV_WIDTH
    pre = acc[:, 0:half] + pltpu.roll(acc[:, half:2 * half], nb - 1, 0) + pe_all
    y = _dot(jax.nn.gelu(pre), w2bd_ref[...])
    rows = lax.broadcasted_iota(jnp.int32, (nb, 1), 0)
    y = jnp.where(rows < nb - 1, y, 0.0)
    yk = y[:, 0:NSA_KV_WIDTH]
    yk = yk * lax.rsqrt(_dot_hi2_l(yk * yk, bd_ref[...]) * (1.0 / HEAD) + EPS) * kg_ref[...]
    o_ref[0, 0] = yk.astype(o_ref.dtype)
    o_ref[1, 0] = y[:, NSA_KV_WIDTH:2 * NSA_KV_WIDTH].astype(o_ref.dtype)


def _compress(proj, w1bd, pe, w1, w2bd, kg0, bd128, batch, seq):
    nb = seq // CMP_STRIDE
    full = lambda *s: pl.BlockSpec(s, lambda b: (0,) * len(s))
    return pl.pallas_call(
        functools.partial(_compress_kernel, nb=nb),
        grid=(batch,),
        in_specs=[pl.BlockSpec((seq, NSA_KV_WIDTH), lambda b: (b, P_OFF // NSA_KV_WIDTH)),
                  pl.BlockSpec((seq, NSA_KV_WIDTH), lambda b: (b, P_OFF // NSA_KV_WIDTH + 1)),
                  full(CMP_STRIDE, P_W, 2 * P_W), full(2, CMP_LEN * HEAD), full(2, CMP_LEN * HEAD, HEAD),
                  full(P_W, P_W), full(1, NSA_KV_WIDTH), full(LANES, LANES)],
        out_specs=pl.BlockSpec((2, 1, nb, NSA_KV_WIDTH), lambda b: (0, b, 0, 0)),
        out_shape=jax.ShapeDtypeStruct((2, batch, nb, NSA_KV_WIDTH), MXU_DTYPE),
        compiler_params=pltpu.CompilerParams(dimension_semantics=("parallel",),
                                             vmem_limit_bytes=VMEM_LIMIT),
        name="compress",
    )(proj, proj, w1bd, pe, w1, w2bd, kg0, bd128)


def _nsa_attn_kernel(*refs, tq, seq, nwb, kc, ncp, nslp, shifted):
    qn_ref, qr_ref, gt_ref, kcmp_ref, vcmp_ref, ksa_ref, vs_ref = refs[:7]
    kw_refs = refs[7:7 + nwb]
    vw_refs = refs[7 + nwb:7 + 2 * nwb]
    msel_ref, cs_ref, egate_ref = refs[7 + 2 * nwb:10 + 2 * nwb]
    o_ref = refs[10 + 2 * nwb]
    bias_scr, qaug_scr, m_scr, l_scr, acc_scr, ow_scr, cand_scr, oc_scr, slc_scr, e_scr = refs[11 + 2 * nwb:]
    nslot = NSA_HEADS
    i = pl.program_id(1)
    c_cmp, c_slc, c_win = cs_ref[0:1, 0:1], cs_ref[0:1, 1:2], cs_ref[0:1, 2:3]
    lane = lax.broadcasted_iota(jnp.int32, (1, LANES), 1)
    lo = lane < HEAD

    def stack(q_ref):
        sl = [q_ref[:, j * LANES:(j + 1) * LANES] for j in range(NSA_GROUP)]
        zero = jnp.zeros_like(sl[0])
        return jnp.concatenate([jnp.where(lo, s, zero) for s in sl] + [jnp.where(lo, zero, s) for s in sl], axis=0)

    tpos = i * tq + lax.broadcasted_iota(jnp.int32, (tq, 1), 0)
    tpos_st = i * tq + lax.broadcasted_iota(jnp.int32, (nslot * tq, 1), 0) % tq
    gates = gt_ref[...]

    qn_st = stack(qn_ref)
    if shifted:
        ccw = min(2 * LANES, ncp)
        n_vis = (i + 1) * (tq // CMP_STRIDE) - 1
        n_cch = (n_vis + ccw - 1) // ccw
        qaug_scr[:, 0:LANES] = qn_st
        l_scr[...] = jnp.zeros(l_scr.shape, F32)
        acc_scr[...] = jnp.zeros(acc_scr.shape, F32)

        def cmp_scores(ch, carry):
            k0 = pl.multiple_of(ch * ccw, ccw)
            s = _dot_nt(qaug_scr[:, 0:LANES], kcmp_ref[0, 0, pl.ds(k0, ccw), :])
            ckey = k0 + lax.broadcasted_iota(jnp.int32, (1, ccw), 1)
            vis = (ckey * CMP_STRIDE + (CMP_LEN - 1)) <= tpos
            cbias = jnp.where(vis, -c_cmp, -SEL_BIG)
            for slot in range(nslot):
                rs = slice(slot * tq, (slot + 1) * tq)
                e = jnp.exp2(s[rs] + cbias)
                e_scr[rs, pl.ds(k0, ccw)] = e
                part = e[:, 0:LANES]
                for t in range(1, ccw // LANES):
                    part = part + e[:, t * LANES:(t + 1) * LANES]
                l_scr[rs, :] += part
            acc_scr[...] += _dot(e_scr[:, pl.ds(k0, ccw)], vcmp_ref[0, 0, pl.ds(k0, ccw), :])
            return carry

        lax.fori_loop(0, n_cch, cmp_scores, 0)
        l_c = jnp.sum(l_scr[...], axis=-1, keepdims=True)
        rinv_c = 1.0 / jnp.where(l_c > 0.0, l_c, 1.0)
        oc_scr[...] = acc_scr[...] * rinv_c
        l_scr[...] = jnp.broadcast_to(rinv_c, l_scr.shape)
        slc_scr[...] = jnp.zeros(slc_scr.shape, F32)

        def cmp_importance(ch, carry):
            k0 = pl.multiple_of(ch * ccw, ccw)
            for g in range(NSA_KV_HEADS):
                ps = None
                for r in range(NSA_GROUP):
                    rs = slice((g * NSA_GROUP + r) * tq, (g * NSA_GROUP + r + 1) * tq)
                    rinv_l = jnp.concatenate([l_scr[rs, :]] * (ccw // LANES), axis=1)
                    p = e_scr[rs, pl.ds(k0, ccw)] * rinv_l
                    ps = p if ps is None else ps + p
                slc_scr[g] += _dot_hi_l(ps, msel_ref[pl.ds(k0, ccw), :])
            return carry

        lax.fori_loop(0, n_cch, cmp_importance, 0)
        slc_rows = [slc_scr[g] for g in range(NSA_KV_HEADS)]
    else:
        ckey = lax.broadcasted_iota(jnp.int32, (1, ncp), 1)
        cmask = (ckey * CMP_STRIDE + (CMP_LEN - 1)) <= tpos
        s_all = _dot_nt(qn_st, kcmp_ref[0, 0])
        psum = [None, None]
        e_parts, rinv_parts = [], []
        for slot in range(nslot):
            s = jnp.where(cmask, s_all[slot * tq:(slot + 1) * tq], NEG_INF)
            e = jnp.where(cmask, jnp.exp2(s - jnp.max(s, axis=-1, keepdims=True)), 0.0)
            l = jnp.sum(e, axis=-1, keepdims=True)
            rinv = 1.0 / jnp.where(l > 0.0, l, 1.0)
            e_parts.append(e.astype(MXU_DTYPE))
            rinv_parts.append(rinv)
            p = e * rinv
            g = slot // NSA_GROUP
            psum[g] = p if psum[g] is None else psum[g] + p
        oc_scr[...] = (_dot(jnp.concatenate(e_parts, axis=0), vcmp_ref[0, 0])
                       * jnp.concatenate(rinv_parts, axis=0))
        slc_rows = [_dot_hi_l(psum[g], msel_ref[...]) for g in range(NSA_KV_HEADS)]

    qr_st = stack(qr_ref)
    kwc = jnp.concatenate([r[...] for r in kw_refs], axis=0)
    vwc = jnp.concatenate([r[...] for r in vw_refs], axis=0)
    sw = _dot_nt(qr_st, kwc)
    if shifted:
        rr = lax.broadcasted_iota(jnp.int32, (tq, tq), 0)
        cc = lax.broadcasted_iota(jnp.int32, (tq, tq), 1)
        parts = []
        for jb in range(nwb):
            shift_b = c_win + jnp.where(i - (nwb - 1) + jb >= 0, 0.0, SEL_BIG)
            blk = jnp.exp2(sw[:, jb * tq:(jb + 1) * tq] - shift_b)
            if jb == 0 or jb == nwb - 1:
                vis = (cc > rr) if jb == 0 else (cc <= rr)
                blk = jnp.concatenate([jnp.where(vis, blk[s * tq:(s + 1) * tq], 0.0) for s in range(nslot)],
                                      axis=0)
            parts.append(blk)
        ew = jnp.concatenate(parts, axis=1)
    else:
        kpos_w = (i - (nwb - 1)) * tq + lax.broadcasted_iota(jnp.int32, (1, nwb * tq), 1)
        dist = tpos_st - kpos_w
        wmask = (dist >= 0) & (dist < WINDOW) & (kpos_w >= 0)
        sw = jnp.where(wmask, sw, NEG_INF)
        ew = jnp.exp2(sw - jnp.max(sw, axis=-1, keepdims=True))
    ow_scr[...] = _dot(ew, vwc) * (1.0 / jnp.sum(ew, axis=-1, keepdims=True))

    jrow = lax.broadcasted_iota(jnp.int32, (nslp, 1), 0)
    jrowf = jrow.astype(F32)
    cur_t = (i * tq + lax.broadcasted_iota(jnp.int32, (1, tq), 1)) // SLC_BLOCK
    forced = (jrow == 0) | (jrow == cur_t) | (jrow == cur_t - 1)
    causal = jrow <= cur_t
    shift_s = c_slc if shifted else 0.0
    n_free = N_SELECT - 3
    to_bias = lambda sel: ((sel - 1.0) * SEL_BIG - shift_s).T.astype(bias_scr.dtype)
    n_bad = None
    for g in range(NSA_KV_HEADS):
        slc = slc_rows[g].T
        cand = jnp.where(causal & jnp.logical_not(forced), slc, -1.0)
        cand_scr[g] = cand
        c = cand
        for _ in range(n_free):
            c = jnp.where(c == jnp.max(c, axis=0, keepdims=True), -2.0, c)
        picked = (c == -2.0) & (cand >= 0.0)
        n_picked = jnp.sum(jnp.where(picked, 1.0, 0.0), axis=0, keepdims=True)
        n_real = jnp.sum(jnp.where(cand >= 0.0, 1.0, 0.0), axis=0, keepdims=True)
        bad = jnp.where(n_picked == jnp.minimum(n_real, float(n_free)), 0.0, 1.0)
        n_bad = bad if n_bad is None else n_bad + bad
        bias_scr[g] = to_bias(jnp.where(forced | picked, 1.0, 0.0))

    @pl.when(jnp.max(n_bad) > 0.0)
    def _():
        for g in range(NSA_KV_HEADS):
            cand = cand_scr[g]
            sel = jnp.where(forced, 1.0, 0.0)
            for _ in range(n_free):
                mx = jnp.max(cand, axis=0, keepdims=True)
                first = jnp.min(jnp.where(cand == mx, jrowf, float(nslp)), axis=0, keepdims=True)
                hit = jrowf == first
                sel = jnp.where(hit, 1.0, sel)
                cand = jnp.where(hit, -2.0, cand)
            bias_scr[g] = to_bias(jnp.where(causal, sel, 0.0))

    qaug_scr[:, 0:LANES] = qr_st
    if not shifted:
        m_scr[...] = jnp.full(m_scr.shape, NEG_INF, F32)
    l_scr[...] = jnp.zeros(l_scr.shape, F32)
    acc_scr[...] = jnp.zeros(acc_scr.shape, F32)
    group_keys = LANES * SLC_BLOCK

    def load_bias(key0):
        @pl.when(key0 % group_keys == 0)
        def _():
            off = pl.multiple_of((key0 // group_keys) * LANES, LANES)
            b0 = bias_scr[0, :, pl.ds(off, LANES)]
            b1 = bias_scr[1, :, pl.ds(off, LANES)]
            qaug_scr[:, LANES:2 * LANES] = jnp.concatenate([b0] * NSA_GROUP + [b1] * NSA_GROUP, axis=0)

    def qk(key0, width):
        return _dot_nt(qaug_scr[...], ksa_ref[pl.ds(pl.multiple_of(key0, width), width), :])

    def scores(key0, width):
        load_bias(key0)
        return qk(key0, width)

    def accumulate(s, key0, width):
        k0 = pl.multiple_of(key0, width)
        if shifted:
            p = jnp.exp2(s)
            psum_l = p[:, 0:LANES]
            for t in range(1, width // LANES):
                psum_l = psum_l + p[:, t * LANES:(t + 1) * LANES]
            l_scr[...] += psum_l
            acc_scr[...] += _dot(p, vs_ref[pl.ds(k0, width), :])
        else:
            m_old = m_scr[...]
            m_new = jnp.maximum(m_old, jnp.max(s, axis=-1, keepdims=True))
            alpha = jnp.exp2(m_old - m_new)
            p = jnp.exp2(s - m_new[:, 0:1])
            l_scr[...] = alpha * l_scr[...] + jnp.sum(p, axis=-1, keepdims=True)
            acc_scr[...] = alpha * acc_scr[...] + _dot(p, vs_ref[pl.ds(k0, width), :])
            m_scr[...] = m_new

    per_trip = CHUNKS_PER_TRIP if (shifted and group_keys % (CHUNKS_PER_TRIP * kc) == 0) else 1
    pair = per_trip * kc
    n_pair = (i * tq) // pair if per_trip > 1 else 0
    n_full = (i * tq - n_pair * pair) // kc
    before_sub = n_pair * pair + n_full * kc
    n_sub = (i * tq - before_sub) // tq

    def body_pair(c, carry):
        key0 = c * pair
        load_bias(key0)
        for h in range(per_trip):
            accumulate(qk(key0 + h * kc, kc), key0 + h * kc, kc)
        return carry

    def body_full(c, carry):
        key0 = n_pair * pair + c * kc
        accumulate(scores(key0, kc), key0, kc)
        return carry

    lax.fori_loop(0, n_pair, body_pair, 0)
    lax.fori_loop(0, n_full, body_full, 0)

    def body_sub(j, carry):
        key0 = before_sub + j * tq
        accumulate(scores(key0, tq), key0, tq)
        return carry

    lax.fori_loop(0, n_sub, body_sub, 0)
    col = lax.broadcasted_iota(jnp.int32, (1, tq), 1)
    accumulate(jnp.where(col <= tpos_st - i * tq, scores(i * tq, tq), -SEL_BIG), i * tq, tq)
    if shifted:
        o_s = acc_scr[...] * (1.0 / jnp.sum(l_scr[...], axis=-1, keepdims=True))
    else:
        o_s = acc_scr[...] / l_scr[...]
    o_w = ow_scr[...]

    gate_e = _dot_hi2_l(gates, egate_ref[...])
    for j in range(NSA_GROUP):
        r0 = slice(j * tq, (j + 1) * tq)
        r1 = slice((NSA_GROUP + j) * tq, (NSA_GROUP + j + 1) * tq)
        out = None
        for br, o_b in enumerate((oc_scr[...], o_s, o_w)):
            lanes = slice(br * NSA_WIDTH + j * LANES, br * NSA_WIDTH + (j + 1) * LANES)
            term = gate_e[:, lanes] * jnp.where(lo, o_b[r0], o_b[r1])
            out = term if out is None else out + term
        o_ref[:, j * LANES:(j + 1) * LANES] = out.astype(o_ref.dtype)


def _gate_expander():
    mat = np.zeros((LANES, 3 * NSA_WIDTH), np.float32)
    for br in range(3):
        for j in range(NSA_GROUP):
            for g in range(NSA_KV_HEADS):
                head = g * NSA_GROUP + j
                c0 = br * NSA_WIDTH + j * LANES + g * HEAD
                mat[3 * head + br, c0:c0 + HEAD] = 1.0
    return jnp.asarray(mat, MXU_DTYPE)


def _nsa_attn(qn, qr, gt, kvc, ksa, vs, kw, vw, msel, cshift, batch, seq, shifted):
    tq, kc = (TQ if shifted else TQ_RUNMAX), KC
    nq = seq // tq
    nwb = WINDOW // tq + 1
    ncp = seq // CMP_STRIDE
    nslp = msel.shape[1]
    row = lambda w: pl.BlockSpec((tq, w), lambda b, i: (b * nq + i, 0))
    once = pl.Buffered(1)
    win = lambda jb: pl.BlockSpec((tq, LANES), lambda b, i: (b * nq + jnp.maximum(i - (nwb - 1) + jb, 0), 0))
    in_specs = ([row(NSA_WIDTH), row(NSA_WIDTH), row(LANES),
                 pl.BlockSpec((1, 1, ncp, LANES), lambda b, i: (0, b, 0, 0), pipeline_mode=once),
                 pl.BlockSpec((1, 1, ncp, LANES), lambda b, i: (1, b, 0, 0), pipeline_mode=once),
                 pl.BlockSpec((seq, 2 * LANES), lambda b, i: (b, 0), pipeline_mode=once),
                 pl.BlockSpec((seq, LANES), lambda b, i: (b, 0), pipeline_mode=once)]
                + [win(jb) for jb in range(nwb)] + [win(jb) for jb in range(nwb)]
                + [pl.BlockSpec((ncp, nslp), lambda b, i: (0, 0), pipeline_mode=once),
                   pl.BlockSpec((1, LANES), lambda b, i: (0, 0)),
                   pl.BlockSpec((LANES, 3 * NSA_WIDTH), lambda b, i: (0, 0), pipeline_mode=once)])
    return pl.pallas_call(
        functools.partial(_nsa_attn_kernel, tq=tq, seq=seq, nwb=nwb, kc=kc, ncp=ncp, nslp=nslp,
                          shifted=shifted),
        grid=(batch, nq),
        in_specs=in_specs,
        out_specs=row(NSA_WIDTH),
        out_shape=jax.ShapeDtypeStruct((batch * seq, NSA_WIDTH), MXU_DTYPE),
        scratch_shapes=[pltpu.VMEM((NSA_KV_HEADS, tq, nslp), MXU_DTYPE),
                        pltpu.VMEM((NSA_HEADS * tq, 2 * LANES), MXU_DTYPE),
                        pltpu.VMEM((NSA_HEADS * tq, LANES), F32),
                        pltpu.VMEM((NSA_HEADS * tq, LANES), F32),
                        pltpu.VMEM((NSA_HEADS * tq, LANES), F32),
                        pltpu.VMEM((NSA_HEADS * tq, LANES), F32),
                        pltpu.VMEM((NSA_KV_HEADS, nslp, tq), F32),
                        pltpu.VMEM((NSA_HEADS * tq, LANES), F32),
                        pltpu.VMEM((NSA_KV_HEADS, tq, nslp), F32),
                        pltpu.VMEM((NSA_HEADS * tq, ncp) if shifted else (SUBLANES, LANES), F32)],
        compiler_params=pltpu.CompilerParams(dimension_semantics=("parallel", "arbitrary"),
                                             vmem_limit_bytes=VMEM_LIMIT),
        name="nsa_attn" if shifted else "nsa_attn_runmax",
    )(qn, qr, gt, kvc, kvc, ksa, vs, *([kw] * nwb), *([vw] * nwb), msel, cshift, _gate_expander())


def _out_ffn_kernel(x_ref, og_ref, on_ref, c_ref, ch_ref, cw_ref, wo_ref, fg_ref, wgu_ref, wd_ref, o_ref,
                    *, tm, seq):
    i = pl.program_id(0)
    keep = jnp.where((i * tm) % seq == 0, 0.0, 1.0)
    rows = lax.broadcasted_iota(jnp.int32, (tm, 1), 0)
    u = c_ref[:, CONV_WIDTH:2 * CONV_WIDTH] * c_ref[:, 2 * CONV_WIDTH:3 * CONV_WIDTH]
    hu = ch_ref[:, CONV_WIDTH:2 * CONV_WIDTH] * ch_ref[:, 2 * CONV_WIDTH:3 * CONV_WIDTH] * keep
    w = cw_ref[...]
    conv = w[CONV_K - 1:CONV_K, :] * u
    for s in range(1, CONV_K):
        conv = conv + w[CONV_K - 1 - s:CONV_K - s, :] * _shift_rows(u, hu, s, rows)
    oc = c_ref[:, 0:CONV_WIDTH] * conv
    x1 = (x_ref[...] + _dot(og_ref[...], wo_ref[0:GDN_WIDTH, :])
          + _dot(on_ref[...], wo_ref[GDN_WIDTH:GDN_WIDTH + NSA_WIDTH, :])
          + _dot(oc, wo_ref[GDN_WIDTH + NSA_WIDTH:, :]))
    ms = jnp.mean(x1 * x1, axis=-1, keepdims=True)
    h2 = (x1 * lax.rsqrt(ms + EPS) * fg_ref[...]).astype(MXU_DTYPE)
    o_ref[...] = x1
    for c0 in range(0, D_FF, FF_CHUNK):
        gate = jnp.dot(h2, wgu_ref[:, c0:c0 + FF_CHUNK], preferred_element_type=F32)
        up = jnp.dot(h2, wgu_ref[:, D_FF + c0:D_FF + c0 + FF_CHUNK], preferred_element_type=F32)
        o_ref[...] += _dot(_silu(gate) * up, wd_ref[c0:c0 + FF_CHUNK, :])


def _out_ffn(x2, o_gdn, o_nsa, proj, conv_w, w_out, fgain, wgu, wd, seq):
    m = x2.shape[0]
    tm = TM_FFN
    hb = tm // SUBLANES
    full = lambda a, b: pl.BlockSpec((a, b), lambda i: (0, 0), pipeline_mode=pl.Buffered(1))
    return pl.pallas_call(
        functools.partial(_out_ffn_kernel, tm=tm, seq=seq),
        grid=(m // tm,),
        in_specs=[pl.BlockSpec((tm, D_MODEL), lambda i: (i, 0)),
                  pl.BlockSpec((tm, GDN_WIDTH), lambda i: (i, 0)),
                  pl.BlockSpec((tm, NSA_WIDTH), lambda i: (i, 0)),
                  pl.BlockSpec((tm, C_W), lambda i: (i, C_OFF // C_W)),
                  pl.BlockSpec((SUBLANES, C_W), lambda i: (jnp.maximum(i * hb - 1, 0), C_OFF // C_W)),
                  full(CONV_K, CONV_WIDTH), full(D_MODEL, D_MODEL), full(1, D_MODEL),
                  full(D_MODEL, 2 * D_FF), full(D_FF, D_MODEL)],
        out_specs=pl.BlockSpec((tm, D_MODEL), lambda i: (i, 0)),
        out_shape=jax.ShapeDtypeStruct((m, D_MODEL), F32),
        compiler_params=pltpu.CompilerParams(dimension_semantics=("parallel",),
                                             vmem_limit_bytes=VMEM_LIMIT),
        name="out_ffn",
    )(x2, o_gdn, o_nsa, proj, proj, conv_w, w_out, fgain, wgu, wd)


def _proj_column_map():
    offs = np.concatenate([[0], np.cumsum(IN_SIZES)])
    seg = lambda k: np.arange(offs[k], offs[k + 1])
    pad = lambda n: -np.ones(n, np.int64)
    nq = seg(6).reshape(NSA_HEADS, HEAD)[list(Q_SLOT_HEADS)].reshape(-1)
    cols = np.concatenate([
        seg(14), seg(15), seg(16),
        seg(7), seg(8),
        seg(0), seg(1), seg(2), seg(3),
        seg(4), seg(5), pad(GG_W - 2 * GDN_HEADS),
        nq, seg(9), seg(11), seg(10), seg(12), seg(13), pad(LANES - 3 * NSA_HEADS)])
    assert cols.shape[0] == W_COLS
    return cols


def _block_diag_ones(n):
    idx = np.arange(n) // HEAD
    return jnp.asarray(idx[:, None] == idx[None, :], MXU_DTYPE)


def _head_expander(first_lane):
    mat = np.zeros((LANES, GDN_WIDTH), np.float32)
    for h in range(GDN_HEADS):
        mat[first_lane + h, h * HEAD:(h + 1) * HEAD] = 1.0
    return jnp.asarray(mat, MXU_DTYPE)


def _compress_weights(w1, w2):
    nslab = 2 * NSA_KV_HEADS
    slab_kv = np.arange(nslab) // NSA_KV_HEADS
    eye = jnp.eye(nslab, dtype=MXU_DTYPE)
    w1r = w1.astype(MXU_DTYPE).reshape(2, 2, CMP_STRIDE, HEAD, HEAD)[slab_kv]
    w1bd = jnp.einsum("shtde,sS->tsdhSe", w1r, eye)
    w2bd = jnp.einsum("sde,sS->sdSe", w2.astype(MXU_DTYPE)[slab_kv], eye)
    return w1bd.reshape(CMP_STRIDE, P_W, 2 * P_W), w2bd.reshape(P_W, P_W)


def _rope_table(positions):
    half = ROT_DIM // 2
    inv = jnp.float32(ROPE_THETA) ** (-jnp.arange(0, ROT_DIM, 2, dtype=jnp.float32) / ROT_DIM)
    inv_l = jnp.concatenate([inv, inv, jnp.zeros((LANES - ROT_DIM,), F32)])
    ang = positions.astype(jnp.float32).reshape(-1)[:, None] * inv_l[None, :]
    lane = jnp.arange(LANES)[None, :]
    return jnp.where(lane < half, jnp.cos(ang), jnp.where(lane < ROT_DIM, jnp.sin(ang), 0.0))


def _rope_expanders():
    half = ROT_DIM // 2
    ex = np.zeros((3, LANES, LANES), np.float32)
    cpat = np.zeros((1, LANES), np.float32)
    for j in range(LANES):
        d = j % HEAD
        if d < ROT_DIM:
            ex[0, d % half, j] = 1.0
        else:
            cpat[0, j] = 1.0
        if d < half:
            ex[1, half + d, j] = -1.0
        elif d < ROT_DIM:
            ex[2, half + (d - half), j] = 1.0
    return jnp.asarray(ex, MXU_DTYPE), jnp.asarray(cpat, F32)


def _selection_matrix(ncp, nslp):
    ratio = SLC_BLOCK // CMP_STRIDE
    frac = np.minimum(CMP_LEN, SLC_BLOCK - CMP_STRIDE * np.arange(ratio)).astype(np.float64) / CMP_LEN
    mat = np.zeros((ncp, nslp), np.float32)
    c = np.arange(ncp)
    mat[c, c // ratio] = frac[c % ratio]
    nxt = c // ratio + 1
    ok = nxt < nslp
    mat[c[ok], nxt[ok]] += (1.0 - frac[c % ratio])[ok]
    return jnp.asarray(mat, MXU_DTYPE)


def kernel(x, positions, attn_norm, w_in, gdn_conv_w, gdn_a_log, gdn_dt_bias, gdn_norm, nsa_q_norm,
           nsa_k_norm, nsa_cmp_pe, nsa_cmp_w1, nsa_cmp_w2, conv_w, w_out, ffn_norm, w_gate_up, w_down):
    batch, seq, _ = x.shape
    depth = w_in.shape[0]
    m = batch * seq
    assert seq % max(TM_PROJ, TM_FFN, T_GDN, KC) == 0 and (seq // CMP_STRIDE) % LANES == 0
    nb = seq // CMP_STRIDE
    nslp = -(-(seq // SLC_BLOCK) // LANES) * LANES

    cols = _proj_column_map()
    take = jnp.asarray(np.maximum(cols, 0), jnp.int32)
    valid = jnp.asarray(cols >= 0)
    assert Q_SLOT_HEADS == tuple(g * NSA_GROUP + j for j in range(NSA_GROUP) for g in range(NSA_KV_HEADS))
    bd128, bd256, bd512 = _block_diag_ones(128), _block_diag_ones(256), _block_diag_ones(512)
    rope_t = _rope_table(positions)
    rope_ex, rope_cp = _rope_expanders()
    msel = _selection_matrix(nb, nslp)
    lane_pad = lambda v: jnp.zeros((1, LANES), F32).at[0, GDN_HEADS:2 * GDN_HEADS].set(v.astype(F32))

    x2 = x.reshape(m, D_MODEL)
    for l in range(depth):
        w_l = jnp.where(valid[None, :], jnp.take(w_in[l], take, axis=1), 0.0).astype(MXU_DTYPE)
        qg = jnp.tile(nsa_q_norm[l], NSA_HEADS).reshape(1, NSA_WIDTH)
        kg = jnp.tile(nsa_k_norm[l, 1:3], (1, NSA_KV_HEADS))
        proj, qn, qr, ksa, kw, vs, vw, gt = _in_proj(x2, attn_norm[l].reshape(1, D_MODEL), w_l, rope_t, rope_ex,
                                                     rope_cp, qg, kg, bd512, bd128, seq)

        o_gdn = _gdn(proj, gdn_conv_w[l], lane_pad(gdn_a_log[l]), lane_pad(gdn_dt_bias[l]),
                     jnp.tile(gdn_norm[l], GDN_HEADS).reshape(1, GDN_WIDTH), bd256,
                     _head_expander(0), _head_expander(GDN_HEADS), batch, seq)

        w1bd, w2bd = _compress_weights(nsa_cmp_w1[l], nsa_cmp_w2[l])
        kvc = _compress(proj, w1bd, nsa_cmp_pe[l].reshape(2, CMP_LEN * HEAD), nsa_cmp_w1[l].astype(MXU_DTYPE),
                        w2bd, jnp.tile(nsa_k_norm[l, 0], NSA_KV_HEADS).reshape(1, NSA_KV_WIDTH), bd128,
                        batch, seq)

        bound = (HEAD ** 0.5) * jnp.max(jnp.abs(nsa_q_norm[l])) * jnp.max(jnp.abs(nsa_k_norm[l]), axis=1)
        cshift = jnp.zeros((1, LANES), F32).at[0, 0:3].set((bound * LOG2E).astype(F32))
        attn_args = (qn, qr, gt, kvc, ksa, vs, kw, vw, msel, cshift)
        o_nsa = lax.cond(
            jnp.max(bound) <= MAX_SOFTMAX_SHIFT,
            lambda a: _nsa_attn(*a, batch, seq, True),
            lambda a: _nsa_attn(*a, batch, seq, False),
            attn_args)

        wo = w_out[l].astype(MXU_DTYPE)
        wo_nsa = wo[GDN_WIDTH:GDN_WIDTH + NSA_WIDTH].reshape(NSA_KV_HEADS, NSA_GROUP, HEAD, D_MODEL)
        wo = jnp.concatenate([wo[:GDN_WIDTH], wo_nsa.transpose(1, 0, 2, 3).reshape(NSA_WIDTH, D_MODEL),
                              wo[GDN_WIDTH + NSA_WIDTH:]], axis=0)
        x2 = _out_ffn(x2, o_gdn, o_nsa, proj, conv_w[l], wo,
                      ffn_norm[l].reshape(1, D_MODEL), w_gate_up[l].astype(MXU_DTYPE),
                      w_down[l].astype(MXU_DTYPE), seq)
    return x2.reshape(batch, seq, D_MODEL)
```

```python
import functools

import numpy as np
import jax
import jax.numpy as jnp
from jax import lax
from jax.experimental import pallas as pl
from jax.experimental.pallas import tpu as pltpu

F32 = jnp.float32
MXU_DTYPE = jnp.bfloat16

D_MODEL = 1024
HEAD = 64
EPS = 1e-6
NEG_INF = -1e30
GDN_HEADS = 4
GDN_WIDTH = 256
GDN_CONV = 4
GDN_CHUNK = 64
NSA_HEADS = 8
NSA_WIDTH = 512
NSA_KV_HEADS = 2
NSA_GROUP = 4
NSA_KV_WIDTH = 128
CMP_STRIDE = 16
CMP_LEN = 32
SLC_BLOCK = 64
N_SELECT = 16
WINDOW = 512
CONV_WIDTH = 256
CONV_K = 3
ROPE_THETA = 500000.0
ROT_DIM = 16
D_FF = 2816
IN_SIZES = (256, 256, 256, 256, 4, 4, 512, 128, 128, 128, 128, 128, 128, 24, 256, 256, 256)
D_IN = sum(IN_SIZES)

LANES = 128
SUBLANES = 8
VMEM_LIMIT = 56 * 1024 * 1024

C_W = 3 * CONV_WIDTH
P_W = 2 * NSA_KV_WIDTH
GM_W = 4 * GDN_WIDTH
GG_W = LANES
N_W = 512 + 4 * 128 + LANES
C_OFF, P_OFF, GM_OFF, GG_OFF = 0, C_W, C_W + P_W, C_W + P_W + GM_W
PROJ_W = C_W + P_W + GM_W + GG_W
N_OFF = PROJ_W
W_COLS = PROJ_W + N_W
Q_SLOT_HEADS = (0, 4, 1, 5, 2, 6, 3, 7)
N_KS, N_KW, N_VS, N_VW, N_GATE = (NSA_WIDTH + k * LANES for k in range(5))

SEL_BIG = 16384.0
MAX_SOFTMAX_SHIFT = 40.0
LOG2E = 1.4426950408889634

TM_PROJ = 512
TM_FFN = 512
T_GDN = 512
TQ = 256
TQ_RUNMAX = 128
KC = 1024
CHUNKS_PER_TRIP = 4
FF_CHUNK = 256


def _dot(a, b):
    return jnp.dot(a.astype(MXU_DTYPE), b.astype(MXU_DTYPE), preferred_element_type=F32)


def _dot_nt(a, b):
    return lax.dot_general(a.astype(MXU_DTYPE), b.astype(MXU_DTYPE), (((1,), (1,)), ((), ())),
                           preferred_element_type=F32)


def _dot_tn(a, b):
    return lax.dot_general(a.astype(MXU_DTYPE), b.astype(MXU_DTYPE), (((0,), (0,)), ((), ())),
                           preferred_element_type=F32)


def _split3(x):
    a = x.astype(MXU_DTYPE)
    r = x - a.astype(F32)
    b = r.astype(MXU_DTYPE)
    c = (r - b.astype(F32)).astype(MXU_DTYPE)
    return a, b, c


def _dot_hi_l(x, m):
    a, b, c = _split3(x)
    f = lambda t: jnp.dot(t, m, preferred_element_type=F32)
    return f(a) + f(b) + f(c)


def _dot_hi2_l(x, m):
    a = x.astype(MXU_DTYPE)
    b = (x - a.astype(F32)).astype(MXU_DTYPE)
    return jnp.dot(a, m, preferred_element_type=F32) + jnp.dot(b, m, preferred_element_type=F32)


def _silu(x):
    return x * jax.nn.sigmoid(x)


def _shift_rows(x, halo, s, rows):
    y = pltpu.roll(x, s, 0)
    for r in range(s):
        y = jnp.where(rows == r, halo[SUBLANES - s + r:SUBLANES - s + r + 1, :], y)
    return y


def _in_proj_kernel(x_ref, g_ref, w_ref, t_ref, ex_ref, cp_ref, qg_ref, kg_ref, bdq_ref, bdk_ref,
                    o_ref, qn_ref, qr_ref, ksa_ref, kw_ref, vs_ref, vw_ref, gt_ref, *, tm, seq):
    x = x_ref[...]
    ms = jnp.mean(x * x, axis=-1, keepdims=True)
    h = (x * lax.rsqrt(ms + EPS) * g_ref[...]).astype(MXU_DTYPE)
    pending = [(c, min(256, PROJ_W - c)) for c in range(0, PROJ_W, 256)]

    def emit(count):
        for _ in range(min(count, len(pending))):
            c, cw = pending.pop(0)
            o_ref[:, c:c + cw] = jnp.dot(h, w_ref[:, c:c + cw], preferred_element_type=F32)

    nblk = jnp.dot(h, w_ref[:, N_OFF:N_OFF + N_W], preferred_element_type=F32)
    _nsa_operands(nblk, t_ref, ex_ref, cp_ref, qg_ref, kg_ref, bdq_ref, bdk_ref,
                  qn_ref, qr_ref, ksa_ref, kw_ref, vs_ref, vw_ref, gt_ref, tt=tm, seq=seq, between=emit)
    emit(len(pending))


def _in_proj(x2, gain, w, rope_t, rope_ex, rope_cp, qg, kg, bd512, bd128, seq):
    m = x2.shape[0]
    tm = TM_PROJ
    row = lambda wd: pl.BlockSpec((tm, wd), lambda i: (i, 0))
    full = lambda a, b: pl.BlockSpec((a, b), lambda i: (0, 0))
    sds = lambda wd, dt: jax.ShapeDtypeStruct((m, wd), dt)
    return pl.pallas_call(
        functools.partial(_in_proj_kernel, tm=tm, seq=seq),
        grid=(m // tm,),
        in_specs=[row(D_MODEL), full(1, D_MODEL), full(D_MODEL, W_COLS), row(LANES),
                  pl.BlockSpec((3, LANES, LANES), lambda i: (0, 0, 0)), full(1, LANES),
                  full(1, NSA_WIDTH), full(2, LANES), full(NSA_WIDTH, NSA_WIDTH), full(LANES, LANES)],
        out_specs=[row(PROJ_W), row(NSA_WIDTH), row(NSA_WIDTH), row(2 * LANES), row(LANES), row(LANES),
                   row(LANES), row(LANES)],
        out_shape=[sds(PROJ_W, F32), sds(NSA_WIDTH, MXU_DTYPE), sds(NSA_WIDTH, MXU_DTYPE),
                   sds(2 * LANES, MXU_DTYPE), sds(LANES, MXU_DTYPE), sds(LANES, MXU_DTYPE),
                   sds(LANES, MXU_DTYPE), sds(LANES, F32)],
        compiler_params=pltpu.CompilerParams(dimension_semantics=("parallel",),
                                             vmem_limit_bytes=VMEM_LIMIT),
        name="in_proj",
    )(x2, gain, w, rope_t, rope_ex, rope_cp, qg, kg, bd512, bd128)


def _gdn_kernel(g_ref, gh_ref, gg_ref, cw_ref, alog_ref, dt_ref, gn_ref, bd_ref, eb_ref, eg_ref, o_ref, s_ref,
                *, tt):
    t_idx = pl.program_id(1)

    @pl.when(t_idx == 0)
    def _():
        s_ref[...] = jnp.zeros_like(s_ref)

    keep = jnp.where(t_idx == 0, 0.0, 1.0)
    rows = lax.broadcasted_iota(jnp.int32, (tt, 1), 0)
    x = g_ref[:, 0:3 * GDN_WIDTH]
    hx = gh_ref[:, 0:3 * GDN_WIDTH] * keep
    w = cw_ref[...]
    y = w[GDN_CONV - 1:GDN_CONV, :] * x
    for s in range(1, GDN_CONV):
        y = y + w[GDN_CONV - 1 - s:GDN_CONV - s, :] * _shift_rows(x, hx, s, rows)
    y = _silu(y)
    q = y[:, 0:GDN_WIDTH]
    k = y[:, GDN_WIDTH:2 * GDN_WIDTH]
    v = y[:, 2 * GDN_WIDTH:3 * GDN_WIDTH]
    bd = bd_ref[...]
    q = q * lax.rsqrt(_dot_hi2_l(q * q, bd) + EPS) * (HEAD ** -0.5)
    k = k * lax.rsqrt(_dot_hi2_l(k * k, bd) + EPS)

    gg = gg_ref[...]
    lane = lax.broadcasted_iota(jnp.int32, (1, LANES), 1)
    xa = gg + dt_ref[...]
    softplus = jnp.maximum(xa, 0.0) + jnp.log1p(jnp.exp(-jnp.abs(xa)))
    g2 = jnp.where(lane < GDN_HEADS, jax.nn.sigmoid(gg), -jnp.exp(alog_ref[...]) * softplus)
    cs = g2.T
    lane_t = lax.broadcasted_iota(jnp.int32, (1, tt), 1) % GDN_CHUNK
    step = 1
    while step < GDN_CHUNK:
        cs = cs + jnp.where(lane_t >= step, pltpu.roll(cs, step, 1), 0.0)
        step *= 2
    gcum_t = cs
    gcum = cs.T
    beta_e = _dot_hi_l(g2, eb_ref[...])
    gcum_e = _dot_hi_l(gcum, eg_ref[...])
    eg_e = jnp.exp(gcum_e)
    kb = k * beta_e
    rv = v * beta_e
    rk = kb * eg_e
    qd = q * eg_e

    nst = GDN_HEADS * GDN_CHUNK
    ri = lax.broadcasted_iota(jnp.int32, (nst, nst), 0)
    ci = lax.broadcasted_iota(jnp.int32, (nst, nst), 1)
    same = (ri // GDN_CHUNK) == (ci // GDN_CHUNK)
    m_tril = same & ((ri % GDN_CHUNK) >= (ci % GDN_CHUNK))
    m_strict = same & ((ri % GDN_CHUNK) > (ci % GDN_CHUNK))
    tile4 = lambda t: jnp.concatenate([t] * GDN_HEADS, axis=0)
    expand = lambda t: tile4(t.astype(MXU_DTYPE)) * bd

    nchunk = tt // GDN_CHUNK
    rss = [slice(n * GDN_CHUNK, (n + 1) * GDN_CHUNK) for n in range(nchunk)]
    g_last = [gcum_e[(n + 1) * GDN_CHUNK - 1:(n + 1) * GDN_CHUNK, :] for n in range(nchunk)]
    rmat, pw, qk = [], [], []
    for n, rs in enumerate(rss):
        g_row = jnp.concatenate([gcum_t[GDN_HEADS + h:GDN_HEADS + h + 1, rs] for h in range(GDN_HEADS)], axis=1)
        decay = jnp.where(m_tril, jnp.exp(jnp.where(m_tril, tile4(gcum_e[rs]) - g_row, 0.0)), 0.0)
        k4 = tile4(k[rs])
        a = jnp.where(m_strict, _dot_nt(expand(kb[rs]), k4) * decay, 0.0)
        qk.append(jnp.where(m_tril, _dot_nt(expand(q[rs]), k4) * decay, 0.0))
        rmat.append(-a)
        pw.append(-a)
    for _ in range(GDN_CHUNK.bit_length() - 2):
        nxt_pw, nxt_r = [], []
        for r, t in zip(rmat, pw):
            t2 = _dot(t, t)
            nxt_pw.append(t2)
            nxt_r.append(r + t2 + _dot(r, t2))
        pw, rmat = nxt_pw, nxt_r
    u, wm = [], []
    for n, rs in enumerate(rss):
        rv_x = jnp.where(same, tile4(rv[rs]), 0.0)
        rk_x = jnp.where(same, tile4(rk[rs]), 0.0)
        u.append(rv_x + _dot(rmat[n], rv_x))
        wm.append(rk_x + _dot(rmat[n], rk_x))

    st = s_ref[...]
    outs = []
    for n, rs in enumerate(rss):
        kd = k[rs] * jnp.exp(g_last[n] - gcum_e[rs])
        v_new = u[n] - _dot(wm[n], st)
        o_x = _dot(expand(qd[rs]), st) + _dot(qk[n], v_new)
        st = st * jnp.exp(g_last[n]) + _dot_tn(expand(kd), v_new)
        o = o_x[0:GDN_CHUNK]
        for h in range(1, GDN_HEADS):
            o = o + o_x[h * GDN_CHUNK:(h + 1) * GDN_CHUNK]
        outs.append(o)
    s_ref[...] = st
    o_all = jnp.concatenate(outs, axis=0)
    on = o_all * lax.rsqrt(_dot_hi2_l(o_all * o_all, bd) * (1.0 / HEAD) + EPS) * gn_ref[...]
    o_ref[...] = (on * _silu(g_ref[:, 3 * GDN_WIDTH:4 * GDN_WIDTH])).astype(o_ref.dtype)


def _gdn(proj, conv_w, alog_pad, dt_pad, gnorm, bd256, eb, eg, batch, seq):
    tt = T_GDN
    nt = seq // tt
    hb = tt // SUBLANES
    return pl.pallas_call(
        functools.partial(_gdn_kernel, tt=tt),
        grid=(batch, nt),
        in_specs=[pl.BlockSpec((tt, GM_W), lambda b, t: (b * nt + t, GM_OFF // GM_W)),
                  pl.BlockSpec((SUBLANES, GM_W),
                               lambda b, t: (jnp.maximum((b * nt + t) * hb - 1, 0), GM_OFF // GM_W)),
                  pl.BlockSpec((tt, GG_W), lambda b, t: (b * nt + t, GG_OFF // GG_W)),
                  pl.BlockSpec((GDN_CONV, 3 * GDN_WIDTH), lambda b, t: (0, 0)),
                  pl.BlockSpec((1, LANES), lambda b, t: (0, 0)),
                  pl.BlockSpec((1, LANES), lambda b, t: (0, 0)),
                  pl.BlockSpec((1, GDN_WIDTH), lambda b, t: (0, 0)),
                  pl.BlockSpec((GDN_WIDTH, GDN_WIDTH), lambda b, t: (0, 0)),
                  pl.BlockSpec((LANES, GDN_WIDTH), lambda b, t: (0, 0)),
                  pl.BlockSpec((LANES, GDN_WIDTH), lambda b, t: (0, 0))],
        out_specs=pl.BlockSpec((tt, GDN_WIDTH), lambda b, t: (b * nt + t, 0)),
        out_shape=jax.ShapeDtypeStruct((batch * seq, GDN_WIDTH), MXU_DTYPE),
        scratch_shapes=[pltpu.VMEM((GDN_WIDTH, GDN_WIDTH), F32)],
        compiler_params=pltpu.CompilerParams(dimension_semantics=("parallel", "arbitrary"),
                                             vmem_limit_bytes=VMEM_LIMIT),
        name="gdn",
    )(proj, proj, proj, conv_w, alog_pad, dt_pad, gnorm, bd256, eb, eg)


def _rope(x, c, s1, s2):
    wdt = x.shape[1]
    return x * c + pltpu.roll(x, wdt - ROT_DIM // 2, 1) * s1 + pltpu.roll(x, ROT_DIM // 2, 1) * s2


def _nsa_operands(n_blk, t_ref, ex_ref, cp_ref, qg_ref, kg_ref, bdq_ref, bdk_ref,
                  qn_ref, qr_ref, ksa_ref, kw_ref, vs_ref, vw_ref, gt_ref, *, tt, seq, between):
    between(3)
    table = t_ref[...]
    c1 = _dot_hi2_l(table, ex_ref[0]) + cp_ref[...]
    s1 = _dot_hi2_l(table, ex_ref[1])
    s2 = _dot_hi2_l(table, ex_ref[2])
    c4 = jnp.concatenate([c1] * 4, axis=1)
    s14 = jnp.concatenate([s1] * 4, axis=1)
    s24 = jnp.concatenate([s2] * 4, axis=1)
    q = n_blk[:, 0:NSA_WIDTH]
    qn = q * lax.rsqrt(_dot_hi2_l(q * q, bdq_ref[...]) * (1.0 / HEAD) + EPS) * qg_ref[...]
    qn = qn * (HEAD ** -0.5 * LOG2E)
    qn_ref[...] = qn.astype(qn_ref.dtype)
    qr_ref[...] = _rope(qn, c4, s14, s24).astype(qr_ref.dtype)
    between(3)
    bdk = bdk_ref[...]
    ks = n_blk[:, N_KS:N_KS + LANES]
    kw = n_blk[:, N_KW:N_KW + LANES]
    ks = ks * lax.rsqrt(_dot_hi2_l(ks * ks, bdk) * (1.0 / HEAD) + EPS) * kg_ref[0:1, :]
    kw = kw * lax.rsqrt(_dot_hi2_l(kw * kw, bdk) * (1.0 / HEAD) + EPS) * kg_ref[1:2, :]
    ksa_ref[:, 0:LANES] = _rope(ks, c1, s1, s2).astype(ksa_ref.dtype)
    rows = (lax.broadcasted_iota(jnp.int32, (tt, LANES), 0) + pl.program_id(0) * tt) % seq
    lane = lax.broadcasted_iota(jnp.int32, (tt, LANES), 1)
    ksa_ref[:, LANES:2 * LANES] = jnp.where((rows // SLC_BLOCK) % LANES == lane, 1.0, 0.0).astype(ksa_ref.dtype)
    kw_ref[...] = _rope(kw, c1, s1, s2).astype(kw_ref.dtype)
    vs_ref[...] = n_blk[:, N_VS:N_VS + LANES].astype(vs_ref.dtype)
    vw_ref[...] = n_blk[:, N_VW:N_VW + LANES].astype(vw_ref.dtype)
    gt_ref[...] = jax.nn.sigmoid(n_blk[:, N_GATE:N_GATE + LANES])


def _compress_kernel(xk_ref, xv_ref, w1bd_ref, pe_ref, w1_ref, w2bd_ref, kg_ref, bd_ref, o_ref, *, nb):
    acc = None
    for t in range(CMP_STRIDE):
        xt = jnp.concatenate([xk_ref[pl.ds(t, nb, stride=CMP_STRIDE), :],
                              xv_ref[pl.ds(t, nb, stride=CMP_STRIDE), :]], axis=1)
        part = _dot(xt, w1bd_ref[t])
        acc = part if acc is None else acc + part
    pe_terms = [_dot(jnp.broadcast_to(pe_ref[c:c + 1, :], (SUBLANES, CMP_LEN * HEAD)), w1_ref[c])[0:1, :]
                for c in range(2)]
    pe_all = jnp.concatenate([pe_terms[0]] * NSA_KV_HEADS + [pe_terms[1]] * NSA_KV_HEADS, axis=1)
    half = 2 * NSA_KV_WIDTH
    pre = acc[:, 0:half] + pltpu.roll(acc[:, half:2 * half], nb - 1, 0) + pe_all
    y = _dot(jax.nn.gelu(pre), w2bd_ref[...])
    rows = lax.broadcasted_iota(jnp.int32, (nb, 1), 0)
    y = jnp.where(rows < nb - 1, y, 0.0)
    yk = y[:, 0:NSA_KV_WIDTH]
    yk = yk * lax.rsqrt(_dot_hi2_l(yk * yk, bd_ref[...]) * (1.0 / HEAD) + EPS) * kg_ref[...]
    o_ref[0, 0] = yk.astype(o_ref.dtype)
    o_ref[1, 0] = y[:, NSA_KV_WIDTH:2 * NSA_KV_WIDTH].astype(o_ref.dtype)


def _compress(proj, w1bd, pe, w1, w2bd, kg0, bd128, batch, seq):
    nb = seq // CMP_STRIDE
    full = lambda *s: pl.BlockSpec(s, lambda b: (0,) * len(s))
    return pl.pallas_call(
        functools.partial(_compress_kernel, nb=nb),
        grid=(batch,),
        in_specs=[pl.BlockSpec((seq, NSA_KV_WIDTH), lambda b: (b, P_OFF // NSA_KV_WIDTH)),
                  pl.BlockSpec((seq, NSA_KV_WIDTH), lambda b: (b, P_OFF // NSA_KV_WIDTH + 1)),
                  full(CMP_STRIDE, P_W, 2 * P_W), full(2, CMP_LEN * HEAD), full(2, CMP_LEN * HEAD, HEAD),
                  full(P_W, P_W), full(1, NSA_KV_WIDTH), full(LANES, LANES)],
        out_specs=pl.BlockSpec((2, 1, nb, NSA_KV_WIDTH), lambda b: (0, b, 0, 0)),
        out_shape=jax.ShapeDtypeStruct((2, batch, nb, NSA_KV_WIDTH), MXU_DTYPE),
        compiler_params=pltpu.CompilerParams(dimension_semantics=("parallel",),
                                             vmem_limit_bytes=VMEM_LIMIT),
        name="compress",
    )(proj, proj, w1bd, pe, w1, w2bd, kg0, bd128)


def _nsa_attn_kernel(*refs, tq, seq, nwb, kc, ncp, nslp, shifted):
    qn_ref, qr_ref, gt_ref, kcmp_ref, vcmp_ref, ksa_ref, vs_ref = refs[:7]
    kw_refs = refs[7:7 + nwb]
    vw_refs = refs[7 + nwb:7 + 2 * nwb]
    msel_ref, cs_ref, egate_ref = refs[7 + 2 * nwb:10 + 2 * nwb]
    o_ref = refs[10 + 2 * nwb]
    bias_scr, qaug_scr, m_scr, l_scr, acc_scr, ow_scr, cand_scr, oc_scr, slc_scr, e_scr = refs[11 + 2 * nwb:]
    nslot = NSA_HEADS
    i = pl.program_id(1)
    c_cmp, c_slc, c_win = cs_ref[0:1, 0:1], cs_ref[0:1, 1:2], cs_ref[0:1, 2:3]
    lane = lax.broadcasted_iota(jnp.int32, (1, LANES), 1)
    lo = lane < HEAD

    def stack(q_ref):
        sl = [q_ref[:, j * LANES:(j + 1) * LANES] for j in range(NSA_GROUP)]
        zero = jnp.zeros_like(sl[0])
        return jnp.concatenate([jnp.where(lo, s, zero) for s in sl] + [jnp.where(lo, zero, s) for s in sl], axis=0)

    tpos = i * tq + lax.broadcasted_iota(jnp.int32, (tq, 1), 0)
    tpos_st = i * tq + lax.broadcasted_iota(jnp.int32, (nslot * tq, 1), 0) % tq
    gates = gt_ref[...]

    qn_st = stack(qn_ref)
    if shifted:
        ccw = min(2 * LANES, ncp)
        n_vis = (i + 1) * (tq // CMP_STRIDE) - 1
        n_cch = (n_vis + ccw - 1) // ccw
        qaug_scr[:, 0:LANES] = qn_st
        l_scr[...] = jnp.zeros(l_scr.shape, F32)
        acc_scr[...] = jnp.zeros(acc_scr.shape, F32)

        def cmp_scores(ch, carry):
            k0 = pl.multiple_of(ch * ccw, ccw)
            s = _dot_nt(qaug_scr[:, 0:LANES], kcmp_ref[0, 0, pl.ds(k0, ccw), :])
            ckey = k0 + lax.broadcasted_iota(jnp.int32, (1, ccw), 1)
            vis = (ckey * CMP_STRIDE + (CMP_LEN - 1)) <= tpos
            cbias = jnp.where(vis, -c_cmp, -SEL_BIG)
            for slot in range(nslot):
                rs = slice(slot * tq, (slot + 1) * tq)
                e = jnp.exp2(s[rs] + cbias)
                e_scr[rs, pl.ds(k0, ccw)] = e
                part = e[:, 0:LANES]
                for t in range(1, ccw // LANES):
                    part = part + e[:, t * LANES:(t + 1) * LANES]
                l_scr[rs, :] += part
            acc_scr[...] += _dot(e_scr[:, pl.ds(k0, ccw)], vcmp_ref[0, 0, pl.ds(k0, ccw), :])
            return carry

        lax.fori_loop(0, n_cch, cmp_scores, 0)
        l_c = jnp.sum(l_scr[...], axis=-1, keepdims=True)
        rinv_c = 1.0 / jnp.where(l_c > 0.0, l_c, 1.0)
        oc_scr[...] = acc_scr[...] * rinv_c
        l_scr[...] = jnp.broadcast_to(rinv_c, l_scr.shape)
        slc_scr[...] = jnp.zeros(slc_scr.shape, F32)

        def cmp_importance(ch, carry):
            k0 = pl.multiple_of(ch * ccw, ccw)
            for g in range(NSA_KV_HEADS):
                ps = None
                for r in range(NSA_GROUP):
                    rs = slice((g * NSA_GROUP + r) * tq, (g * NSA_GROUP + r + 1) * tq)
                    rinv_l = jnp.concatenate([l_scr[rs, :]] * (ccw // LANES), axis=1)
                    p = e_scr[rs, pl.ds(k0, ccw)] * rinv_l
                    ps = p if ps is None else ps + p
                slc_scr[g] += _dot_hi_l(ps, msel_ref[pl.ds(k0, ccw), :])
            return carry

        lax.fori_loop(0, n_cch, cmp_importance, 0)
        slc_rows = [slc_scr[g] for g in range(NSA_KV_HEADS)]
    else:
        ckey = lax.broadcasted_iota(jnp.int32, (1, ncp), 1)
        cmask = (ckey * CMP_STRIDE + (CMP_LEN - 1)) <= tpos
        s_all = _dot_nt(qn_st, kcmp_ref[0, 0])
        psum = [None, None]
        e_parts, rinv_parts = [], []
        for slot in range(nslot):
            s = jnp.where(cmask, s_all[slot * tq:(slot + 1) * tq], NEG_INF)
            e = jnp.where(cmask, jnp.exp2(s - jnp.max(s, axis=-1, keepdims=True)), 0.0)
            l = jnp.sum(e, axis=-1, keepdims=True)
            rinv = 1.0 / jnp.where(l > 0.0, l, 1.0)
            e_parts.append(e.astype(MXU_DTYPE))
            rinv_parts.append(rinv)
            p = e * rinv
            g = slot // NSA_GROUP
            psum[g] = p if psum[g] is None else psum[g] + p
        oc_scr[...] = (_dot(jnp.concatenate(e_parts, axis=0), vcmp_ref[0, 0])
                       * jnp.concatenate(rinv_parts, axis=0))
        slc_rows = [_dot_hi_l(psum[g], msel_ref[...]) for g in range(NSA_KV_HEADS)]

    qr_st = stack(qr_ref)
    kwc = jnp.concatenate([r[...] for r in kw_refs], axis=0)
    vwc = jnp.concatenate([r[...] for r in vw_refs], axis=0)
    sw = _dot_nt(qr_st, kwc)
    if shifted:
        rr = lax.broadcasted_iota(jnp.int32, (tq, tq), 0)
        cc = lax.broadcasted_iota(jnp.int32, (tq, tq), 1)
        parts = []
        for jb in range(nwb):
            shift_b = c_win + jnp.where(i - (nwb - 1) + jb >= 0, 0.0, SEL_BIG)
            blk = jnp.exp2(sw[:, jb * tq:(jb + 1) * tq] - shift_b)
            if jb == 0 or jb == nwb - 1:
                vis = (cc > rr) if jb == 0 else (cc <= rr)
                blk = jnp.concatenate([jnp.where(vis, blk[s * tq:(s + 1) * tq], 0.0) for s in range(nslot)],
                                      axis=0)
            parts.append(blk)
        ew = jnp.concatenate(parts, axis=1)
    else:
        kpos_w = (i - (nwb - 1)) * tq + lax.broadcasted_iota(jnp.int32, (1, nwb * tq), 1)
        dist = tpos_st - kpos_w
        wmask = (dist >= 0) & (dist < WINDOW) & (kpos_w >= 0)
        sw = jnp.where(wmask, sw, NEG_INF)
        ew = jnp.exp2(sw - jnp.max(sw, axis=-1, keepdims=True))
    ow_scr[...] = _dot(ew, vwc) * (1.0 / jnp.sum(ew, axis=-1, keepdims=True))

    jrow = lax.broadcasted_iota(jnp.int32, (nslp, 1), 0)
    jrowf = jrow.astype(F32)
    cur_t = (i * tq + lax.broadcasted_iota(jnp.int32, (1, tq), 1)) // SLC_BLOCK
    forced = (jrow == 0) | (jrow == cur_t) | (jrow == cur_t - 1)
    causal = jrow <= cur_t
    shift_s = c_slc if shifted else 0.0
    n_free = N_SELECT - 3
    to_bias = lambda sel: ((sel - 1.0) * SEL_BIG - shift_s).T.astype(bias_scr.dtype)
    n_bad = None
    for g in range(NSA_KV_HEADS):
        slc = slc_rows[g].T
        cand = jnp.where(causal & jnp.logical_not(forced), slc, -1.0)
        cand_scr[g] = cand
        c = cand
        for _ in range(n_free):
            c = jnp.where(c == jnp.max(c, axis=0, keepdims=True), -2.0, c)
        picked = (c == -2.0) & (cand >= 0.0)
        n_picked = jnp.sum(jnp.where(picked, 1.0, 0.0), axis=0, keepdims=True)
        n_real = jnp.sum(jnp.where(cand >= 0.0, 1.0, 0.0), axis=0, keepdims=True)
        bad = jnp.where(n_picked == jnp.minimum(n_real, float(n_free)), 0.0, 1.0)
        n_bad = bad if n_bad is None else n_bad + bad
        bias_scr[g] = to_bias(jnp.where(forced | picked, 1.0, 0.0))

    @pl.when(jnp.max(n_bad) > 0.0)
    def _():
        for g in range(NSA_KV_HEADS):
            cand = cand_scr[g]
            sel = jnp.where(forced, 1.0, 0.0)
            for _ in range(n_free):
                mx = jnp.max(cand, axis=0, keepdims=True)
                first = jnp.min(jnp.where(cand == mx, jrowf, float(nslp)), axis=0, keepdims=True)
                hit = jrowf == first
                sel = jnp.where(hit, 1.0, sel)
                cand = jnp.where(hit, -2.0, cand)
            bias_scr[g] = to_bias(jnp.where(causal, sel, 0.0))

    qaug_scr[:, 0:LANES] = qr_st
    if not shifted:
        m_scr[...] = jnp.full(m_scr.shape, NEG_INF, F32)
    l_scr[...] = jnp.zeros(l_scr.shape, F32)
    acc_scr[...] = jnp.zeros(acc_scr.shape, F32)
    group_keys = LANES * SLC_BLOCK

    def load_bias(key0):
        @pl.when(key0 % group_keys == 0)
        def _():
            off = pl.multiple_of((key0 // group_keys) * LANES, LANES)
            b0 = bias_scr[0, :, pl.ds(off, LANES)]
            b1 = bias_scr[1, :, pl.ds(off, LANES)]
            qaug_scr[:, LANES:2 * LANES] = jnp.concatenate([b0] * NSA_GROUP + [b1] * NSA_GROUP, axis=0)

    def qk(key0, width):
        return _dot_nt(qaug_scr[...], ksa_ref[pl.ds(pl.multiple_of(key0, width), width), :])

    def scores(key0, width):
        load_bias(key0)
        return qk(key0, width)

    def accumulate(s, key0, width):
        k0 = pl.multiple_of(key0, width)
        if shifted:
            p = jnp.exp2(s)
            psum_l = p[:, 0:LANES]
            for t in range(1, width // LANES):
                psum_l = psum_l + p[:, t * LANES:(t + 1) * LANES]
            l_scr[...] += psum_l
            acc_scr[...] += _dot(p, vs_ref[pl.ds(k0, width), :])
        else:
            m_old = m_scr[...]
            m_new = jnp.maximum(m_old, jnp.max(s, axis=-1, keepdims=True))
            alpha = jnp.exp2(m_old - m_new)
            p = jnp.exp2(s - m_new[:, 0:1])
            l_scr[...] = alpha * l_scr[...] + jnp.sum(p, axis=-1, keepdims=True)
            acc_scr[...] = alpha * acc_scr[...] + _dot(p, vs_ref[pl.ds(k0, width), :])
            m_scr[...] = m_new

    per_trip = CHUNKS_PER_TRIP if (shifted and group_keys % (CHUNKS_PER_TRIP * kc) == 0) else 1
    pair = per_trip * kc
    n_pair = (i * tq) // pair if per_trip > 1 else 0
    n_full = (i * tq - n_pair * pair) // kc
    before_sub = n_pair * pair + n_full * kc
    n_sub = (i * tq - before_sub) // tq

    def body_pair(c, carry):
        key0 = c * pair
        load_bias(key0)
        for h in range(per_trip):
            accumulate(qk(key0 + h * kc, kc), key0 + h * kc, kc)
        return carry

    def body_full(c, carry):
        key0 = n_pair * pair + c * kc
        accumulate(scores(key0, kc), key0, kc)
        return carry

    lax.fori_loop(0, n_pair, body_pair, 0)
    lax.fori_loop(0, n_full, body_full, 0)

    def own_chunk():
        col = lax.broadcasted_iota(jnp.int32, (1, tq), 1)
        accumulate(jnp.where(col <= tpos_st - i * tq, qk(i * tq, tq), -SEL_BIG), i * tq, tq)

    load_bias(before_sub)
    if shifted:
        def tail(count):
            for j in range(count):
                accumulate(qk(before_sub + j * tq, tq), before_sub + j * tq, tq)
            own_chunk()

        for count in range(kc // tq):
            pl.when(n_sub == count)(functools.partial(tail, count))
    else:
        def body_sub(j, carry):
            key0 = before_sub + j * tq
            accumulate(qk(key0, tq), key0, tq)
            return carry

        lax.fori_loop(0, n_sub, body_sub, 0)
        own_chunk()
    if shifted:
        o_s = acc_scr[...] * (1.0 / jnp.sum(l_scr[...], axis=-1, keepdims=True))
    else:
        o_s = acc_scr[...] / l_scr[...]
    o_w = ow_scr[...]

    gate_e = _dot_hi2_l(gates, egate_ref[...])
    for j in range(NSA_GROUP):
        r0 = slice(j * tq, (j + 1) * tq)
        r1 = slice((NSA_GROUP + j) * tq, (NSA_GROUP + j + 1) * tq)
        out = None
        for br, o_b in enumerate((oc_scr[...], o_s, o_w)):
            lanes = slice(br * NSA_WIDTH + j * LANES, br * NSA_WIDTH + (j + 1) * LANES)
            term = gate_e[:, lanes] * jnp.where(lo, o_b[r0], o_b[r1])
            out = term if out is None else out + term
        o_ref[:, j * LANES:(j + 1) * LANES] = out.astype(o_ref.dtype)


def _gate_expander():
    mat = np.zeros((LANES, 3 * NSA_WIDTH), np.float32)
    for br in range(3):
        for j in range(NSA_GROUP):
            for g in range(NSA_KV_HEADS):
                head = g * NSA_GROUP + j
                c0 = br * NSA_WIDTH + j * LANES + g * HEAD
                mat[3 * head + br, c0:c0 + HEAD] = 1.0
    return jnp.asarray(mat, MXU_DTYPE)


def _nsa_attn(qn, qr, gt, kvc, ksa, vs, kw, vw, msel, cshift, batch, seq, shifted):
    tq, kc = (TQ if shifted else TQ_RUNMAX), KC
    nq = seq // tq
    nwb = WINDOW // tq + 1
    ncp = seq // CMP_STRIDE
    nslp = msel.shape[1]
    row = lambda w: pl.BlockSpec((tq, w), lambda b, i: (b * nq + i, 0))
    once = pl.Buffered(1)
    win = lambda jb: pl.BlockSpec((tq, LANES), lambda b, i: (b * nq + jnp.maximum(i - (nwb - 1) + jb, 0), 0))
    in_specs = ([row(NSA_WIDTH), row(NSA_WIDTH), row(LANES),
                 pl.BlockSpec((1, 1, ncp, LANES), lambda b, i: (0, b, 0, 0), pipeline_mode=once),
                 pl.BlockSpec((1, 1, ncp, LANES), lambda b, i: (1, b, 0, 0), pipeline_mode=once),
                 pl.BlockSpec((seq, 2 * LANES), lambda b, i: (b, 0), pipeline_mode=once),
                 pl.BlockSpec((seq, LANES), lambda b, i: (b, 0), pipeline_mode=once)]
                + [win(jb) for jb in range(nwb)] + [win(jb) for jb in range(nwb)]
                + [pl.BlockSpec((ncp, nslp), lambda b, i: (0, 0), pipeline_mode=once),
                   pl.BlockSpec((1, LANES), lambda b, i: (0, 0)),
                   pl.BlockSpec((LANES, 3 * NSA_WIDTH), lambda b, i: (0, 0), pipeline_mode=once)])
    return pl.pallas_call(
        functools.partial(_nsa_attn_kernel, tq=tq, seq=seq, nwb=nwb, kc=kc, ncp=ncp, nslp=nslp,
                          shifted=shifted),
        grid=(batch, nq),
        in_specs=in_specs,
        out_specs=row(NSA_WIDTH),
        out_shape=jax.ShapeDtypeStruct((batch * seq, NSA_WIDTH), MXU_DTYPE),
        scratch_shapes=[pltpu.VMEM((NSA_KV_HEADS, tq, nslp), MXU_DTYPE),
                        pltpu.VMEM((NSA_HEADS * tq, 2 * LANES), MXU_DTYPE),
                        pltpu.VMEM((NSA_HEADS * tq, LANES), F32),
                        pltpu.VMEM((NSA_HEADS * tq, LANES), F32),
                        pltpu.VMEM((NSA_HEADS * tq, LANES), F32),
                        pltpu.VMEM((NSA_HEADS * tq, LANES), F32),
                        pltpu.VMEM((NSA_KV_HEADS, nslp, tq), F32),
                        pltpu.VMEM((NSA_HEADS * tq, LANES), F32),
                        pltpu.VMEM((NSA_KV_HEADS, tq, nslp), F32),
                        pltpu.VMEM((NSA_HEADS * tq, ncp) if shifted else (SUBLANES, LANES), F32)],
        compiler_params=pltpu.CompilerParams(dimension_semantics=("parallel", "arbitrary"),
                                             vmem_limit_bytes=VMEM_LIMIT),
        name="nsa_attn" if shifted else "nsa_attn_runmax",
    )(qn, qr, gt, kvc, kvc, ksa, vs, *([kw] * nwb), *([vw] * nwb), msel, cshift, _gate_expander())


def _out_ffn_kernel(x_ref, og_ref, on_ref, c_ref, ch_ref, cw_ref, wo_ref, fg_ref, wgu_ref, wd_ref, o_ref,
                    *, tm, seq):
    i = pl.program_id(0)
    keep = jnp.where((i * tm) % seq == 0, 0.0, 1.0)
    rows = lax.broadcasted_iota(jnp.int32, (tm, 1), 0)
    u = c_ref[:, CONV_WIDTH:2 * CONV_WIDTH] * c_ref[:, 2 * CONV_WIDTH:3 * CONV_WIDTH]
    hu = ch_ref[:, CONV_WIDTH:2 * CONV_WIDTH] * ch_ref[:, 2 * CONV_WIDTH:3 * CONV_WIDTH] * keep
    w = cw_ref[...]
    conv = w[CONV_K - 1:CONV_K, :] * u
    for s in range(1, CONV_K):
        conv = conv + w[CONV_K - 1 - s:CONV_K - s, :] * _shift_rows(u, hu, s, rows)
    oc = c_ref[:, 0:CONV_WIDTH] * conv
    x1 = (x_ref[...] + _dot(og_ref[...], wo_ref[0:GDN_WIDTH, :])
          + _dot(on_ref[...], wo_ref[GDN_WIDTH:GDN_WIDTH + NSA_WIDTH, :])
          + _dot(oc, wo_ref[GDN_WIDTH + NSA_WIDTH:, :]))
    ms = jnp.mean(x1 * x1, axis=-1, keepdims=True)
    h2 = (x1 * lax.rsqrt(ms + EPS) * fg_ref[...]).astype(MXU_DTYPE)
    o_ref[...] = x1
    for c0 in range(0, D_FF, FF_CHUNK):
        gate = jnp.dot(h2, wgu_ref[:, c0:c0 + FF_CHUNK], preferred_element_type=F32)
        up = jnp.dot(h2, wgu_ref[:, D_FF + c0:D_FF + c0 + FF_CHUNK], preferred_element_type=F32)
        o_ref[...] += _dot(_silu(gate) * up, wd_ref[c0:c0 + FF_CHUNK, :])


def _out_ffn(x2, o_gdn, o_nsa, proj, conv_w, w_out, fgain, wgu, wd, seq):
    m = x2.shape[0]
    tm = TM_FFN
    hb = tm // SUBLANES
    full = lambda a, b: pl.BlockSpec((a, b), lambda i: (0, 0), pipeline_mode=pl.Buffered(1))
    return pl.pallas_call(
        functools.partial(_out_ffn_kernel, tm=tm, seq=seq),
        grid=(m // tm,),
        in_specs=[pl.BlockSpec((tm, D_MODEL), lambda i: (i, 0)),
                  pl.BlockSpec((tm, GDN_WIDTH), lambda i: (i, 0)),
                  pl.BlockSpec((tm, NSA_WIDTH), lambda i: (i, 0)),
                  pl.BlockSpec((tm, C_W), lambda i: (i, C_OFF // C_W)),
                  pl.BlockSpec((SUBLANES, C_W), lambda i: (jnp.maximum(i * hb - 1, 0), C_OFF // C_W)),
                  full(CONV_K, CONV_WIDTH), full(D_MODEL, D_MODEL), full(1, D_MODEL),
                  full(D_MODEL, 2 * D_FF), full(D_FF, D_MODEL)],
        out_specs=pl.BlockSpec((tm, D_MODEL), lambda i: (i, 0)),
        out_shape=jax.ShapeDtypeStruct((m, D_MODEL), F32),
        compiler_params=pltpu.CompilerParams(dimension_semantics=("parallel",),
                                             vmem_limit_bytes=VMEM_LIMIT),
        name="out_ffn",
    )(x2, o_gdn, o_nsa, proj, proj, conv_w, w_out, fgain, wgu, wd)


def _proj_column_map():
    offs = np.concatenate([[0], np.cumsum(IN_SIZES)])
    seg = lambda k: np.arange(offs[k], offs[k + 1])
    pad = lambda n: -np.ones(n, np.int64)
    nq = seg(6).reshape(NSA_HEADS, HEAD)[list(Q_SLOT_HEADS)].reshape(-1)
    cols = np.concatenate([
        seg(14), seg(15), seg(16),
        seg(7), seg(8),
        seg(0), seg(1), seg(2), seg(3),
        seg(4), seg(5), pad(GG_W - 2 * GDN_HEADS),
        nq, seg(9), seg(11), seg(10), seg(12), seg(13), pad(LANES - 3 * NSA_HEADS)])
    assert cols.shape[0] == W_COLS
    return cols


def _block_diag_ones(n):
    idx = np.arange(n) // HEAD
    return jnp.asarray(idx[:, None] == idx[None, :], MXU_DTYPE)


def _head_expander(first_lane):
    mat = np.zeros((LANES, GDN_WIDTH), np.float32)
    for h in range(GDN_HEADS):
        mat[first_lane + h, h * HEAD:(h + 1) * HEAD] = 1.0
    return jnp.asarray(mat, MXU_DTYPE)


def _compress_weights(w1, w2):
    nslab = 2 * NSA_KV_HEADS
    slab_kv = np.arange(nslab) // NSA_KV_HEADS
    eye = jnp.eye(nslab, dtype=MXU_DTYPE)
    w1r = w1.astype(MXU_DTYPE).reshape(2, 2, CMP_STRIDE, HEAD, HEAD)[slab_kv]
    w1bd = jnp.einsum("shtde,sS->tsdhSe", w1r, eye)
    w2bd = jnp.einsum("sde,sS->sdSe", w2.astype(MXU_DTYPE)[slab_kv], eye)
    return w1bd.reshape(CMP_STRIDE, P_W, 2 * P_W), w2bd.reshape(P_W, P_W)


def _rope_table(positions):
    half = ROT_DIM // 2
    inv = jnp.float32(ROPE_THETA) ** (-jnp.arange(0, ROT_DIM, 2, dtype=jnp.float32) / ROT_DIM)
    inv_l = jnp.concatenate([inv, inv, jnp.zeros((LANES - ROT_DIM,), F32)])
    ang = positions.astype(jnp.float32).reshape(-1)[:, None] * inv_l[None, :]
    lane = jnp.arange(LANES)[None, :]
    return jnp.where(lane < half, jnp.cos(ang), jnp.where(lane < ROT_DIM, jnp.sin(ang), 0.0))


def _rope_expanders():
    half = ROT_DIM // 2
    ex = np.zeros((3, LANES, LANES), np.float32)
    cpat = np.zeros((1, LANES), np.float32)
    for j in range(LANES):
        d = j % HEAD
        if d < ROT_DIM:
            ex[0, d % half, j] = 1.0
        else:
            cpat[0, j] = 1.0
        if d < half:
            ex[1, half + d, j] = -1.0
        elif d < ROT_DIM:
            ex[2, half + (d - half), j] = 1.0
    return jnp.asarray(ex, MXU_DTYPE), jnp.asarray(cpat, F32)


def _selection_matrix(ncp, nslp):
    ratio = SLC_BLOCK // CMP_STRIDE
    frac = np.minimum(CMP_LEN, SLC_BLOCK - CMP_STRIDE * np.arange(ratio)).astype(np.float64) / CMP_LEN
    mat = np.zeros((ncp, nslp), np.float32)
    c = np.arange(ncp)
    mat[c, c // ratio] = frac[c % ratio]
    nxt = c // ratio + 1
    ok = nxt < nslp
    mat[c[ok], nxt[ok]] += (1.0 - frac[c % ratio])[ok]
    return jnp.asarray(mat, MXU_DTYPE)


def kernel(x, positions, attn_norm, w_in, gdn_conv_w, gdn_a_log, gdn_dt_bias, gdn_norm, nsa_q_norm,
           nsa_k_norm, nsa_cmp_pe, nsa_cmp_w1, nsa_cmp_w2, conv_w, w_out, ffn_norm, w_gate_up, w_down):
    batch, seq, _ = x.shape
    depth = w_in.shape[0]
    m = batch * seq
    assert seq % max(TM_PROJ, TM_FFN, T_GDN, KC) == 0 and (seq // CMP_STRIDE) % LANES == 0
    nb = seq // CMP_STRIDE
    nslp = -(-(seq // SLC_BLOCK) // LANES) * LANES

    cols = _proj_column_map()
    take = jnp.asarray(np.maximum(cols, 0), jnp.int32)
    valid = jnp.asarray(cols >= 0)
    assert Q_SLOT_HEADS == tuple(g * NSA_GROUP + j for j in range(NSA_GROUP) for g in range(NSA_KV_HEADS))
    bd128, bd256, bd512 = _block_diag_ones(128), _block_diag_ones(256), _block_diag_ones(512)
    rope_t = _rope_table(positions)
    rope_ex, rope_cp = _rope_expanders()
    msel = _selection_matrix(nb, nslp)
    lane_pad = lambda v: jnp.zeros((1, LANES), F32).at[0, GDN_HEADS:2 * GDN_HEADS].set(v.astype(F32))

    x2 = x.reshape(m, D_MODEL)
    for l in range(depth):
        w_l = jnp.where(valid[None, :], jnp.take(w_in[l], take, axis=1), 0.0).astype(MXU_DTYPE)
        qg = jnp.tile(nsa_q_norm[l], NSA_HEADS).reshape(1, NSA_WIDTH)
        kg = jnp.tile(nsa_k_norm[l, 1:3], (1, NSA_KV_HEADS))
        proj, qn, qr, ksa, kw, vs, vw, gt = _in_proj(x2, attn_norm[l].reshape(1, D_MODEL), w_l, rope_t, rope_ex,
                                                     rope_cp, qg, kg, bd512, bd128, seq)

        o_gdn = _gdn(proj, gdn_conv_w[l], lane_pad(gdn_a_log[l]), lane_pad(gdn_dt_bias[l]),
                     jnp.tile(gdn_norm[l], GDN_HEADS).reshape(1, GDN_WIDTH), bd256,
                     _head_expander(0), _head_expander(GDN_HEADS), batch, seq)

        w1bd, w2bd = _compress_weights(nsa_cmp_w1[l], nsa_cmp_w2[l])
        kvc = _compress(proj, w1bd, nsa_cmp_pe[l].reshape(2, CMP_LEN * HEAD), nsa_cmp_w1[l].astype(MXU_DTYPE),
                        w2bd, jnp.tile(nsa_k_norm[l, 0], NSA_KV_HEADS).reshape(1, NSA_KV_WIDTH), bd128,
                        batch, seq)

        bound = (HEAD ** 0.5) * jnp.max(jnp.abs(nsa_q_norm[l])) * jnp.max(jnp.abs(nsa_k_norm[l]), axis=1)
        cshift = jnp.zeros((1, LANES), F32).at[0, 0:3].set((bound * LOG2E).astype(F32))
        attn_args = (qn, qr, gt, kvc, ksa, vs, kw, vw, msel, cshift)
        o_nsa = lax.cond(
            jnp.max(bound) <= MAX_SOFTMAX_SHIFT,
            lambda a: _nsa_attn(*a, batch, seq, True),
            lambda a: _nsa_attn(*a, batch, seq, False),
            attn_args)

        wo = w_out[l].astype(MXU_DTYPE)
        wo_nsa = wo[GDN_WIDTH:GDN_WIDTH + NSA_WIDTH].reshape(NSA_KV_HEADS, NSA_GROUP, HEAD, D_MODEL)
        wo = jnp.concatenate([wo[:GDN_WIDTH], wo_nsa.transpose(1, 0, 2, 3).reshape(NSA_WIDTH, D_MODEL),
                              wo[GDN_WIDTH + NSA_WIDTH:]], axis=0)
        x2 = _out_ffn(x2, o_gdn, o_nsa, proj, conv_w[l], wo,
                      ffn_norm[l].reshape(1, D_MODEL), w_gate_up[l].astype(MXU_DTYPE),
                      w_down[l].astype(MXU_DTYPE), seq)
    return x2.reshape(batch, seq, D_MODEL)
```

```python
import functools

import numpy as np
import jax
import jax.numpy as jnp
from jax import lax
from jax.experimental import pallas as pl
from jax.experimental.pallas import tpu as pltpu

F32 = jnp.float32
MXU_DTYPE = jnp.bfloat16

D_MODEL = 1024
HEAD = 64
EPS = 1e-6
NEG_INF = -1e30
GDN_HEADS = 4
GDN_WIDTH = 256
GDN_CONV = 4
GDN_CHUNK = 64
NSA_HEADS = 8
NSA_WIDTH = 512
NSA_KV_HEADS = 2
NSA_GROUP = 4
NSA_KV_WIDTH = 128
CMP_STRIDE = 16
CMP_LEN = 32
SLC_BLOCK = 64
N_SELECT = 16
WINDOW = 512
CONV_WIDTH = 256
CONV_K = 3
ROPE_THETA = 500000.0
ROT_DIM = 16
D_FF = 2816
IN_SIZES = (256, 256, 256, 256, 4, 4, 512, 128, 128, 128, 128, 128, 128, 24, 256, 256, 256)
D_IN = sum(IN_SIZES)

LANES = 128
SUBLANES = 8
VMEM_LIMIT = 56 * 1024 * 1024

C_W = 3 * CONV_WIDTH
P_W = 2 * NSA_KV_WIDTH
GM_W = 4 * GDN_WIDTH
GG_W = LANES
N_W = 512 + 4 * 128 + LANES
C_OFF, P_OFF, GM_OFF, GG_OFF = 0, C_W, C_W + P_W, C_W + P_W + GM_W
PROJ_W = C_W + P_W + GM_W + GG_W
N_OFF = PROJ_W
W_COLS = PROJ_W + N_W
Q_SLOT_HEADS = (0, 4, 1, 5, 2, 6, 3, 7)
N_KS, N_KW, N_VS, N_VW, N_GATE = (NSA_WIDTH + k * LANES for k in range(5))

SEL_BIG = 16384.0
MAX_SOFTMAX_SHIFT = 40.0
LOG2E = 1.4426950408889634

TM_PROJ = 512
TM_FFN = 512
T_GDN = 512
TQ = 256
TQ_RUNMAX = 128
KC = 1024
CHUNKS_PER_TRIP = 4
FF_CHUNK = 256


def _dot(a, b):
    return jnp.dot(a.astype(MXU_DTYPE), b.astype(MXU_DTYPE), preferred_element_type=F32)


def _dot_nt(a, b):
    return lax.dot_general(a.astype(MXU_DTYPE), b.astype(MXU_DTYPE), (((1,), (1,)), ((), ())),
                           preferred_element_type=F32)


def _dot_tn(a, b):
    return lax.dot_general(a.astype(MXU_DTYPE), b.astype(MXU_DTYPE), (((0,), (0,)), ((), ())),
                           preferred_element_type=F32)


def _split3(x):
    a = x.astype(MXU_DTYPE)
    r = x - a.astype(F32)
    b = r.astype(MXU_DTYPE)
    c = (r - b.astype(F32)).astype(MXU_DTYPE)
    return a, b, c


def _dot_hi_l(x, m):
    a, b, c = _split3(x)
    f = lambda t: jnp.dot(t, m, preferred_element_type=F32)
    return f(a) + f(b) + f(c)


def _dot_hi2_l(x, m):
    a = x.astype(MXU_DTYPE)
    b = (x - a.astype(F32)).astype(MXU_DTYPE)
    return jnp.dot(a, m, preferred_element_type=F32) + jnp.dot(b, m, preferred_element_type=F32)


def _silu(x):
    return x * jax.nn.sigmoid(x)


def _shift_rows(x, halo, s, rows):
    y = pltpu.roll(x, s, 0)
    for r in range(s):
        y = jnp.where(rows == r, halo[SUBLANES - s + r:SUBLANES - s + r + 1, :], y)
    return y


def _in_proj_kernel(x_ref, g_ref, w_ref, t_ref, ex_ref, cp_ref, qg_ref, kg_ref, bdq_ref, bdk_ref,
                    o_ref, qn_ref, qr_ref, ksa_ref, kw_ref, vs_ref, vw_ref, gt_ref, *, tm, seq):
    x = x_ref[...]
    ms = jnp.mean(x * x, axis=-1, keepdims=True)
    h = (x * lax.rsqrt(ms + EPS) * g_ref[...]).astype(MXU_DTYPE)
    pending = [(c, min(256, PROJ_W - c)) for c in range(0, PROJ_W, 256)]

    def emit(count):
        for _ in range(min(count, len(pending))):
            c, cw = pending.pop(0)
            o_ref[:, c:c + cw] = jnp.dot(h, w_ref[:, c:c + cw], preferred_element_type=F32)

    nblk = jnp.dot(h, w_ref[:, N_OFF:N_OFF + N_W], preferred_element_type=F32)
    _nsa_operands(nblk, t_ref, ex_ref, cp_ref, qg_ref, kg_ref, bdq_ref, bdk_ref,
                  qn_ref, qr_ref, ksa_ref, kw_ref, vs_ref, vw_ref, gt_ref, tt=tm, seq=seq, between=emit)
    emit(len(pending))


def _in_proj(x2, gain, w, rope_t, rope_ex, rope_cp, qg, kg, bd512, bd128, seq):
    m = x2.shape[0]
    tm = TM_PROJ
    row = lambda wd: pl.BlockSpec((tm, wd), lambda i: (i, 0))
    full = lambda a, b: pl.BlockSpec((a, b), lambda i: (0, 0))
    sds = lambda wd, dt: jax.ShapeDtypeStruct((m, wd), dt)
    return pl.pallas_call(
        functools.partial(_in_proj_kernel, tm=tm, seq=seq),
        grid=(m // tm,),
        in_specs=[row(D_MODEL), full(1, D_MODEL), full(D_MODEL, W_COLS), row(LANES),
                  pl.BlockSpec((3, LANES, LANES), lambda i: (0, 0, 0)), full(1, LANES),
                  full(1, NSA_WIDTH), full(2, LANES), full(NSA_WIDTH, NSA_WIDTH), full(LANES, LANES)],
        out_specs=[row(PROJ_W), row(NSA_WIDTH), row(NSA_WIDTH), row(2 * LANES), row(LANES), row(LANES),
                   row(LANES), row(LANES)],
        out_shape=[sds(PROJ_W, F32), sds(NSA_WIDTH, MXU_DTYPE), sds(NSA_WIDTH, MXU_DTYPE),
                   sds(2 * LANES, MXU_DTYPE), sds(LANES, MXU_DTYPE), sds(LANES, MXU_DTYPE),
                   sds(LANES, MXU_DTYPE), sds(LANES, F32)],
        compiler_params=pltpu.CompilerParams(dimension_semantics=("parallel",),
                                             vmem_limit_bytes=VMEM_LIMIT),
        name="in_proj",
    )(x2, gain, w, rope_t, rope_ex, rope_cp, qg, kg, bd512, bd128)


def _gdn_kernel(g_ref, gh_ref, gg_ref, cw_ref, alog_ref, dt_ref, gn_ref, bd_ref, eb_ref, eg_ref, o_ref, s_ref,
                *, tt):
    t_idx = pl.program_id(1)

    @pl.when(t_idx == 0)
    def _():
        s_ref[...] = jnp.zeros_like(s_ref)

    keep = jnp.where(t_idx == 0, 0.0, 1.0)
    rows = lax.broadcasted_iota(jnp.int32, (tt, 1), 0)
    x = g_ref[:, 0:3 * GDN_WIDTH]
    hx = gh_ref[:, 0:3 * GDN_WIDTH] * keep
    w = cw_ref[...]
    y = w[GDN_CONV - 1:GDN_CONV, :] * x
    for s in range(1, GDN_CONV):
        y = y + w[GDN_CONV - 1 - s:GDN_CONV - s, :] * _shift_rows(x, hx, s, rows)
    y = _silu(y)
    q = y[:, 0:GDN_WIDTH]
    k = y[:, GDN_WIDTH:2 * GDN_WIDTH]
    v = y[:, 2 * GDN_WIDTH:3 * GDN_WIDTH]
    bd = bd_ref[...]
    q = q * lax.rsqrt(_dot_hi2_l(q * q, bd) + EPS) * (HEAD ** -0.5)
    k = k * lax.rsqrt(_dot_hi2_l(k * k, bd) + EPS)

    gg = gg_ref[...]
    lane = lax.broadcasted_iota(jnp.int32, (1, LANES), 1)
    xa = gg + dt_ref[...]
    softplus = jnp.maximum(xa, 0.0) + jnp.log1p(jnp.exp(-jnp.abs(xa)))
    g2 = jnp.where(lane < GDN_HEADS, jax.nn.sigmoid(gg), -jnp.exp(alog_ref[...]) * softplus)
    cs = g2.T
    lane_t = lax.broadcasted_iota(jnp.int32, (1, tt), 1) % GDN_CHUNK
    step = 1
    while step < GDN_CHUNK:
        cs = cs + jnp.where(lane_t >= step, pltpu.roll(cs, step, 1), 0.0)
        step *= 2
    gcum_t = cs
    gcum = cs.T
    beta_e = _dot_hi_l(g2, eb_ref[...])
    gcum_e = _dot_hi_l(gcum, eg_ref[...])
    eg_e = jnp.exp(gcum_e)
    kb = k * beta_e
    rv = v * beta_e
    rk = kb * eg_e
    qd = q * eg_e

    nst = GDN_HEADS * GDN_CHUNK
    ri = lax.broadcasted_iota(jnp.int32, (nst, nst), 0)
    ci = lax.broadcasted_iota(jnp.int32, (nst, nst), 1)
    same = (ri // GDN_CHUNK) == (ci // GDN_CHUNK)
    m_tril = same & ((ri % GDN_CHUNK) >= (ci % GDN_CHUNK))
    m_strict = same & ((ri % GDN_CHUNK) > (ci % GDN_CHUNK))
    tile4 = lambda t: jnp.concatenate([t] * GDN_HEADS, axis=0)
    expand = lambda t: tile4(t.astype(MXU_DTYPE)) * bd

    nchunk = tt // GDN_CHUNK
    rss = [slice(n * GDN_CHUNK, (n + 1) * GDN_CHUNK) for n in range(nchunk)]
    g_last = [gcum_e[(n + 1) * GDN_CHUNK - 1:(n + 1) * GDN_CHUNK, :] for n in range(nchunk)]
    rmat, pw, qk = [], [], []
    for n, rs in enumerate(rss):
        g_row = jnp.concatenate([gcum_t[GDN_HEADS + h:GDN_HEADS + h + 1, rs] for h in range(GDN_HEADS)], axis=1)
        decay = jnp.where(m_tril, jnp.exp(jnp.where(m_tril, tile4(gcum_e[rs]) - g_row, 0.0)), 0.0)
        k4 = tile4(k[rs])
        a = jnp.where(m_strict, _dot_nt(expand(kb[rs]), k4) * decay, 0.0)
        qk.append(jnp.where(m_tril, _dot_nt(expand(q[rs]), k4) * decay, 0.0))
        rmat.append(-a)
        pw.append(-a)
    for _ in range(GDN_CHUNK.bit_length() - 2):
        nxt_pw, nxt_r = [], []
        for r, t in zip(rmat, pw):
            t2 = _dot(t, t)
            nxt_pw.append(t2)
            nxt_r.append(r + t2 + _dot(r, t2))
        pw, rmat = nxt_pw, nxt_r
    u, wm = [], []
    for n, rs in enumerate(rss):
        rv_x = jnp.where(same, tile4(rv[rs]), 0.0)
        rk_x = jnp.where(same, tile4(rk[rs]), 0.0)
        u.append(rv_x + _dot(rmat[n], rv_x))
        wm.append(rk_x + _dot(rmat[n], rk_x))

    st = s_ref[...]
    outs = []
    for n, rs in enumerate(rss):
        kd = k[rs] * jnp.exp(g_last[n] - gcum_e[rs])
        v_new = u[n] - _dot(wm[n], st)
        o_x = _dot(expand(qd[rs]), st) + _dot(qk[n], v_new)
        st = st * jnp.exp(g_last[n]) + _dot_tn(expand(kd), v_new)
        o = o_x[0:GDN_CHUNK]
        for h in range(1, GDN_HEADS):
            o = o + o_x[h * GDN_CHUNK:(h + 1) * GDN_CHUNK]
        outs.append(o)
    s_ref[...] = st
    o_all = jnp.concatenate(outs, axis=0)
    on = o_all * lax.rsqrt(_dot_hi2_l(o_all * o_all, bd) * (1.0 / HEAD) + EPS) * gn_ref[...]
    o_ref[...] = (on * _silu(g_ref[:, 3 * GDN_WIDTH:4 * GDN_WIDTH])).astype(o_ref.dtype)


def _gdn(proj, conv_w, alog_pad, dt_pad, gnorm, bd256, eb, eg, batch, seq):
    tt = T_GDN
    nt = seq // tt
    hb = tt // SUBLANES
    return pl.pallas_call(
        functools.partial(_gdn_kernel, tt=tt),
        grid=(batch, nt),
        in_specs=[pl.BlockSpec((tt, GM_W), lambda b, t: (b * nt + t, GM_OFF // GM_W)),
                  pl.BlockSpec((SUBLANES, GM_W),
                               lambda b, t: (jnp.maximum((b * nt + t) * hb - 1, 0), GM_OFF // GM_W)),
                  pl.BlockSpec((tt, GG_W), lambda b, t: (b * nt + t, GG_OFF // GG_W)),
                  pl.BlockSpec((GDN_CONV, 3 * GDN_WIDTH), lambda b, t: (0, 0)),
                  pl.BlockSpec((1, LANES), lambda b, t: (0, 0)),
                  pl.BlockSpec((1, LANES), lambda b, t: (0, 0)),
                  pl.BlockSpec((1, GDN_WIDTH), lambda b, t: (0, 0)),
                  pl.BlockSpec((GDN_WIDTH, GDN_WIDTH), lambda b, t: (0, 0)),
                  pl.BlockSpec((LANES, GDN_WIDTH), lambda b, t: (0, 0)),
                  pl.BlockSpec((LANES, GDN_WIDTH), lambda b, t: (0, 0))],
        out_specs=pl.BlockSpec((tt, GDN_WIDTH), lambda b, t: (b * nt + t, 0)),
        out_shape=jax.ShapeDtypeStruct((batch * seq, GDN_WIDTH), MXU_DTYPE),
        scratch_shapes=[pltpu.VMEM((GDN_WIDTH, GDN_WIDTH), F32)],
        compiler_params=pltpu.CompilerParams(dimension_semantics=("parallel", "arbitrary"),
                                             vmem_limit_bytes=VMEM_LIMIT),
        name="gdn",
    )(proj, proj, proj, conv_w, alog_pad, dt_pad, gnorm, bd256, eb, eg)


def _rope(x, c, s1, s2):
    wdt = x.shape[1]
    return x * c + pltpu.roll(x, wdt - ROT_DIM // 2, 1) * s1 + pltpu.roll(x, ROT_DIM // 2, 1) * s2


def _nsa_operands(n_blk, t_ref, ex_ref, cp_ref, qg_ref, kg_ref, bdq_ref, bdk_ref,
                  qn_ref, qr_ref, ksa_ref, kw_ref, vs_ref, vw_ref, gt_ref, *, tt, seq, between):
    between(3)
    table = t_ref[...]
    c1 = _dot_hi2_l(table, ex_ref[0]) + cp_ref[...]
    s1 = _dot_hi2_l(table, ex_ref[1])
    s2 = _dot_hi2_l(table, ex_ref[2])
    c4 = jnp.concatenate([c1] * 4, axis=1)
    s14 = jnp.concatenate([s1] * 4, axis=1)
    s24 = jnp.concatenate([s2] * 4, axis=1)
    q = n_blk[:, 0:NSA_WIDTH]
    qn = q * lax.rsqrt(_dot_hi2_l(q * q, bdq_ref[...]) * (1.0 / HEAD) + EPS) * qg_ref[...]
    qn = qn * (HEAD ** -0.5 * LOG2E)
    qn_ref[...] = qn.astype(qn_ref.dtype)
    qr_ref[...] = _rope(qn, c4, s14, s24).astype(qr_ref.dtype)
    between(3)
    bdk = bdk_ref[...]
    ks = n_blk[:, N_KS:N_KS + LANES]
    kw = n_blk[:, N_KW:N_KW + LANES]
    ks = ks * lax.rsqrt(_dot_hi2_l(ks * ks, bdk) * (1.0 / HEAD) + EPS) * kg_ref[0:1, :]
    kw = kw * lax.rsqrt(_dot_hi2_l(kw * kw, bdk) * (1.0 / HEAD) + EPS) * kg_ref[1:2, :]
    ksa_ref[:, 0:LANES] = _rope(ks, c1, s1, s2).astype(ksa_ref.dtype)
    rows = (lax.broadcasted_iota(jnp.int32, (tt, LANES), 0) + pl.program_id(0) * tt) % seq
    lane = lax.broadcasted_iota(jnp.int32, (tt, LANES), 1)
    ksa_ref[:, LANES:2 * LANES] = jnp.where((rows // SLC_BLOCK) % LANES == lane, 1.0, 0.0).astype(ksa_ref.dtype)
    kw_ref[...] = _rope(kw, c1, s1, s2).astype(kw_ref.dtype)
    vs_ref[...] = n_blk[:, N_VS:N_VS + LANES].astype(vs_ref.dtype)
    vw_ref[...] = n_blk[:, N_VW:N_VW + LANES].astype(vw_ref.dtype)
    gt_ref[...] = jax.nn.sigmoid(n_blk[:, N_GATE:N_GATE + LANES])


def _compress_kernel(xk_ref, xv_ref, w1bd_ref, pe_ref, w1_ref, w2bd_ref, kg_ref, bd_ref, o_ref, *, nb):
    acc = None
    for t in range(CMP_STRIDE):
        xt = jnp.concatenate([xk_ref[pl.ds(t, nb, stride=CMP_STRIDE), :],
                              xv_ref[pl.ds(t, nb, stride=CMP_STRIDE), :]], axis=1)
        part = _dot(xt, w1bd_ref[t])
        acc = part if acc is None else acc + part
    pe_terms = [_dot(jnp.broadcast_to(pe_ref[c:c + 1, :], (SUBLANES, CMP_LEN * HEAD)), w1_ref[c])[0:1, :]
                for c in range(2)]
    pe_all = jnp.concatenate([pe_terms[0]] * NSA_KV_HEADS + [pe_terms[1]] * NSA_KV_HEADS, axis=1)
    half = 2 * NSA_KV_WIDTH
    pre = acc[:, 0:half] + pltpu.roll(acc[:, half:2 * half], nb - 1, 0) + pe_all
    y = _dot(jax.nn.gelu(pre), w2bd_ref[...])
    rows = lax.broadcasted_iota(jnp.int32, (nb, 1), 0)
    y = jnp.where(rows < nb - 1, y, 0.0)
    yk = y[:, 0:NSA_KV_WIDTH]
    yk = yk * lax.rsqrt(_dot_hi2_l(yk * yk, bd_ref[...]) * (1.0 / HEAD) + EPS) * kg_ref[...]
    o_ref[0, 0] = yk.astype(o_ref.dtype)
    o_ref[1, 0] = y[:, NSA_KV_WIDTH:2 * NSA_KV_WIDTH].astype(o_ref.dtype)


def _compress(proj, w1bd, pe, w1, w2bd, kg0, bd128, batch, seq):
    nb = seq // CMP_STRIDE
    full = lambda *s: pl.BlockSpec(s, lambda b: (0,) * len(s))
    return pl.pallas_call(
        functools.partial(_compress_kernel, nb=nb),
        grid=(batch,),
        in_specs=[pl.BlockSpec((seq, NSA_KV_WIDTH), lambda b: (b, P_OFF // NSA_KV_WIDTH)),
                  pl.BlockSpec((seq, NSA_KV_WIDTH), lambda b: (b, P_OFF // NSA_KV_WIDTH + 1)),
                  full(CMP_STRIDE, P_W, 2 * P_W), full(2, CMP_LEN * HEAD), full(2, CMP_LEN * HEAD, HEAD),
                  full(P_W, P_W), full(1, NSA_KV_WIDTH), full(LANES, LANES)],
        out_specs=pl.BlockSpec((2, 1, nb, NSA_KV_WIDTH), lambda b: (0, b, 0, 0)),
        out_shape=jax.ShapeDtypeStruct((2, batch, nb, NSA_KV_WIDTH), MXU_DTYPE),
        compiler_params=pltpu.CompilerParams(dimension_semantics=("parallel",),
                                             vmem_limit_bytes=VMEM_LIMIT),
        name="compress",
    )(proj, proj, w1bd, pe, w1, w2bd, kg0, bd128)


def _nsa_attn_kernel(*refs, tq, seq, nwb, kc, ncp, nslp, shifted):
    qn_ref, qr_ref, gt_ref, kcmp_ref, vcmp_ref, ksa_ref, vs_ref = refs[:7]
    kw_refs = refs[7:7 + nwb]
    vw_refs = refs[7 + nwb:7 + 2 * nwb]
    msel_ref, cs_ref, egate_ref = refs[7 + 2 * nwb:10 + 2 * nwb]
    o_ref = refs[10 + 2 * nwb]
    bias_scr, qaug_scr, m_scr, l_scr, acc_scr, ow_scr, cand_scr, oc_scr, slc_scr, e_scr = refs[11 + 2 * nwb:]
    nslot = NSA_HEADS
    i = pl.program_id(1)
    c_cmp, c_slc, c_win = cs_ref[0:1, 0:1], cs_ref[0:1, 1:2], cs_ref[0:1, 2:3]
    lane = lax.broadcasted_iota(jnp.int32, (1, LANES), 1)
    lo = lane < HEAD

    def stack(q_ref):
        sl = [q_ref[:, j * LANES:(j + 1) * LANES] for j in range(NSA_GROUP)]
        zero = jnp.zeros_like(sl[0])
        return jnp.concatenate([jnp.where(lo, s, zero) for s in sl] + [jnp.where(lo, zero, s) for s in sl], axis=0)

    tpos = i * tq + lax.broadcasted_iota(jnp.int32, (tq, 1), 0)
    tpos_st = i * tq + lax.broadcasted_iota(jnp.int32, (nslot * tq, 1), 0) % tq
    gates = gt_ref[...]

    qn_st = stack(qn_ref)
    if shifted:
        ccw = min(2 * LANES, ncp)
        n_vis = (i + 1) * (tq // CMP_STRIDE) - 1
        n_cch = (n_vis + ccw - 1) // ccw
        qaug_scr[:, 0:LANES] = qn_st
        l_scr[...] = jnp.zeros(l_scr.shape, F32)
        acc_scr[...] = jnp.zeros(acc_scr.shape, F32)

        def cmp_scores(ch):
            k0 = ch * ccw
            s = _dot_nt(qaug_scr[:, 0:LANES], kcmp_ref[0, 0, pl.ds(k0, ccw), :])
            ckey = k0 + lax.broadcasted_iota(jnp.int32, (1, ccw), 1)
            vis = (ckey * CMP_STRIDE + (CMP_LEN - 1)) <= tpos
            cbias = jnp.where(vis, -c_cmp, -SEL_BIG)
            for slot in range(nslot):
                rs = slice(slot * tq, (slot + 1) * tq)
                e = jnp.exp2(s[rs] + cbias)
                e_scr[rs, pl.ds(k0, ccw)] = e
                part = e[:, 0:LANES]
                for t in range(1, ccw // LANES):
                    part = part + e[:, t * LANES:(t + 1) * LANES]
                l_scr[rs, :] += part
            acc_scr[...] += _dot(e_scr[:, pl.ds(k0, ccw)], vcmp_ref[0, 0, pl.ds(k0, ccw), :])

        def for_visible_chunks(fn):
            def variant(count):
                for ch in range(count):
                    fn(ch)

            for count in range(1, ncp // ccw + 1):
                pl.when(n_cch == count)(functools.partial(variant, count))

        for_visible_chunks(cmp_scores)
        l_c = jnp.sum(l_scr[...], axis=-1, keepdims=True)
        rinv_c = 1.0 / jnp.where(l_c > 0.0, l_c, 1.0)
        oc_scr[...] = acc_scr[...] * rinv_c
        l_scr[...] = jnp.broadcast_to(rinv_c, l_scr.shape)
        slc_scr[...] = jnp.zeros(slc_scr.shape, F32)

        def cmp_importance(ch):
            k0 = ch * ccw
            for g in range(NSA_KV_HEADS):
                ps = None
                for r in range(NSA_GROUP):
                    rs = slice((g * NSA_GROUP + r) * tq, (g * NSA_GROUP + r + 1) * tq)
                    rinv_l = jnp.concatenate([l_scr[rs, :]] * (ccw // LANES), axis=1)
                    p = e_scr[rs, pl.ds(k0, ccw)] * rinv_l
                    ps = p if ps is None else ps + p
                slc_scr[g] += _dot_hi_l(ps, msel_ref[pl.ds(k0, ccw), :])

        for_visible_chunks(cmp_importance)
        slc_rows = [slc_scr[g] for g in range(NSA_KV_HEADS)]
    else:
        ckey = lax.broadcasted_iota(jnp.int32, (1, ncp), 1)
        cmask = (ckey * CMP_STRIDE + (CMP_LEN - 1)) <= tpos
        s_all = _dot_nt(qn_st, kcmp_ref[0, 0])
        psum = [None, None]
        e_parts, rinv_parts = [], []
        for slot in range(nslot):
            s = jnp.where(cmask, s_all[slot * tq:(slot + 1) * tq], NEG_INF)
            e = jnp.where(cmask, jnp.exp2(s - jnp.max(s, axis=-1, keepdims=True)), 0.0)
            l = jnp.sum(e, axis=-1, keepdims=True)
            rinv = 1.0 / jnp.where(l > 0.0, l, 1.0)
            e_parts.append(e.astype(MXU_DTYPE))
            rinv_parts.append(rinv)
            p = e * rinv
            g = slot // NSA_GROUP
            psum[g] = p if psum[g] is None else psum[g] + p
        oc_scr[...] = (_dot(jnp.concatenate(e_parts, axis=0), vcmp_ref[0, 0])
                       * jnp.concatenate(rinv_parts, axis=0))
        slc_rows = [_dot_hi_l(psum[g], msel_ref[...]) for g in range(NSA_KV_HEADS)]

    qr_st = stack(qr_ref)
    kwc = jnp.concatenate([r[...] for r in kw_refs], axis=0)
    vwc = jnp.concatenate([r[...] for r in vw_refs], axis=0)
    sw = _dot_nt(qr_st, kwc)
    if shifted:
        rr = lax.broadcasted_iota(jnp.int32, (tq, tq), 0)
        cc = lax.broadcasted_iota(jnp.int32, (tq, tq), 1)
        parts = []
        for jb in range(nwb):
            shift_b = c_win + jnp.where(i - (nwb - 1) + jb >= 0, 0.0, SEL_BIG)
            blk = jnp.exp2(sw[:, jb * tq:(jb + 1) * tq] - shift_b)
            if jb == 0 or jb == nwb - 1:
                vis = (cc > rr) if jb == 0 else (cc <= rr)
                blk = jnp.concatenate([jnp.where(vis, blk[s * tq:(s + 1) * tq], 0.0) for s in range(nslot)],
                                      axis=0)
            parts.append(blk)
        ew = jnp.concatenate(parts, axis=1)
    else:
        kpos_w = (i - (nwb - 1)) * tq + lax.broadcasted_iota(jnp.int32, (1, nwb * tq), 1)
        dist = tpos_st - kpos_w
        wmask = (dist >= 0) & (dist < WINDOW) & (kpos_w >= 0)
        sw = jnp.where(wmask, sw, NEG_INF)
        ew = jnp.exp2(sw - jnp.max(sw, axis=-1, keepdims=True))
    ow_scr[...] = _dot(ew, vwc) * (1.0 / jnp.sum(ew, axis=-1, keepdims=True))

    jrow = lax.broadcasted_iota(jnp.int32, (nslp, 1), 0)
    jrowf = jrow.astype(F32)
    cur_t = (i * tq + lax.broadcasted_iota(jnp.int32, (1, tq), 1)) // SLC_BLOCK
    forced = (jrow == 0) | (jrow == cur_t) | (jrow == cur_t - 1)
    causal = jrow <= cur_t
    shift_s = c_slc if shifted else 0.0
    n_free = N_SELECT - 3
    to_bias = lambda sel: ((sel - 1.0) * SEL_BIG - shift_s).T.astype(bias_scr.dtype)
    n_bad = None
    for g in range(NSA_KV_HEADS):
        slc = slc_rows[g].T
        cand = jnp.where(causal & jnp.logical_not(forced), slc, -1.0)
        cand_scr[g] = cand
        c = cand
        for _ in range(n_free):
            c = jnp.where(c == jnp.max(c, axis=0, keepdims=True), -2.0, c)
        picked = (c == -2.0) & (cand >= 0.0)
        n_picked = jnp.sum(jnp.where(picked, 1.0, 0.0), axis=0, keepdims=True)
        n_real = jnp.sum(jnp.where(cand >= 0.0, 1.0, 0.0), axis=0, keepdims=True)
        bad = jnp.where(n_picked == jnp.minimum(n_real, float(n_free)), 0.0, 1.0)
        n_bad = bad if n_bad is None else n_bad + bad
        bias_scr[g] = to_bias(jnp.where(forced | picked, 1.0, 0.0))

    @pl.when(jnp.max(n_bad) > 0.0)
    def _():
        for g in range(NSA_KV_HEADS):
            cand = cand_scr[g]
            sel = jnp.where(forced, 1.0, 0.0)
            for _ in range(n_free):
                mx = jnp.max(cand, axis=0, keepdims=True)
                first = jnp.min(jnp.where(cand == mx, jrowf, float(nslp)), axis=0, keepdims=True)
                hit = jrowf == first
                sel = jnp.where(hit, 1.0, sel)
                cand = jnp.where(hit, -2.0, cand)
            bias_scr[g] = to_bias(jnp.where(causal, sel, 0.0))

    qaug_scr[:, 0:LANES] = qr_st
    if not shifted:
        m_scr[...] = jnp.full(m_scr.shape, NEG_INF, F32)
    l_scr[...] = jnp.zeros(l_scr.shape, F32)
    acc_scr[...] = jnp.zeros(acc_scr.shape, F32)
    group_keys = LANES * SLC_BLOCK

    def load_bias(key0):
        @pl.when(key0 % group_keys == 0)
        def _():
            off = pl.multiple_of((key0 // group_keys) * LANES, LANES)
            b0 = bias_scr[0, :, pl.ds(off, LANES)]
            b1 = bias_scr[1, :, pl.ds(off, LANES)]
            qaug_scr[:, LANES:2 * LANES] = jnp.concatenate([b0] * NSA_GROUP + [b1] * NSA_GROUP, axis=0)

    def qk(key0, width):
        return _dot_nt(qaug_scr[...], ksa_ref[pl.ds(pl.multiple_of(key0, width), width), :])

    def scores(key0, width):
        load_bias(key0)
        return qk(key0, width)

    def accumulate(s, key0, width):
        k0 = pl.multiple_of(key0, width)
        if shifted:
            p = jnp.exp2(s)
            psum_l = p[:, 0:LANES]
            for t in range(1, width // LANES):
                psum_l = psum_l + p[:, t * LANES:(t + 1) * LANES]
            l_scr[...] += psum_l
            acc_scr[...] += _dot(p, vs_ref[pl.ds(k0, width), :])
        else:
            m_old = m_scr[...]
            m_new = jnp.maximum(m_old, jnp.max(s, axis=-1, keepdims=True))
            alpha = jnp.exp2(m_old - m_new)
            p = jnp.exp2(s - m_new[:, 0:1])
            l_scr[...] = alpha * l_scr[...] + jnp.sum(p, axis=-1, keepdims=True)
            acc_scr[...] = alpha * acc_scr[...] + _dot(p, vs_ref[pl.ds(k0, width), :])
            m_scr[...] = m_new

    per_trip = CHUNKS_PER_TRIP if (shifted and group_keys % (CHUNKS_PER_TRIP * kc) == 0) else 1
    pair = per_trip * kc
    n_pair = (i * tq) // pair if per_trip > 1 else 0
    n_full = (i * tq - n_pair * pair) // kc
    before_sub = n_pair * pair + n_full * kc
    n_sub = (i * tq - before_sub) // tq

    def body_pair(c, carry):
        key0 = c * pair
        load_bias(key0)
        for h in range(per_trip):
            accumulate(qk(key0 + h * kc, kc), key0 + h * kc, kc)
        return carry

    def body_full(c, carry):
        key0 = n_pair * pair + c * kc
        accumulate(scores(key0, kc), key0, kc)
        return carry

    lax.fori_loop(0, n_pair, body_pair, 0)
    lax.fori_loop(0, n_full, body_full, 0)

    def own_chunk():
        col = lax.broadcasted_iota(jnp.int32, (1, tq), 1)
        accumulate(jnp.where(col <= tpos_st - i * tq, qk(i * tq, tq), -SEL_BIG), i * tq, tq)

    load_bias(before_sub)
    if shifted:
        def tail(count):
            for j in range(count):
                accumulate(qk(before_sub + j * tq, tq), before_sub + j * tq, tq)
            own_chunk()

        for count in range(kc // tq):
            pl.when(n_sub == count)(functools.partial(tail, count))
    else:
        def body_sub(j, carry):
            key0 = before_sub + j * tq
            accumulate(qk(key0, tq), key0, tq)
            return carry

        lax.fori_loop(0, n_sub, body_sub, 0)
        own_chunk()
    if shifted:
        o_s = acc_scr[...] * (1.0 / jnp.sum(l_scr[...], axis=-1, keepdims=True))
    else:
        o_s = acc_scr[...] / l_scr[...]
    o_w = ow_scr[...]

    gate_e = _dot_hi2_l(gates, egate_ref[...])
    for j in range(NSA_GROUP):
        r0 = slice(j * tq, (j + 1) * tq)
        r1 = slice((NSA_GROUP + j) * tq, (NSA_GROUP + j + 1) * tq)
        out = None
        for br, o_b in enumerate((oc_scr[...], o_s, o_w)):
            lanes = slice(br * NSA_WIDTH + j * LANES, br * NSA_WIDTH + (j + 1) * LANES)
            term = gate_e[:, lanes] * jnp.where(lo, o_b[r0], o_b[r1])
            out = term if out is None else out + term
        o_ref[:, j * LANES:(j + 1) * LANES] = out.astype(o_ref.dtype)


def _gate_expander():
    mat = np.zeros((LANES, 3 * NSA_WIDTH), np.float32)
    for br in range(3):
        for j in range(NSA_GROUP):
            for g in range(NSA_KV_HEADS):
                head = g * NSA_GROUP + j
                c0 = br * NSA_WIDTH + j * LANES + g * HEAD
                mat[3 * head + br, c0:c0 + HEAD] = 1.0
    return jnp.asarray(mat, MXU_DTYPE)


def _nsa_attn(qn, qr, gt, kvc, ksa, vs, kw, vw, msel, cshift, batch, seq, shifted):
    tq, kc = (TQ if shifted else TQ_RUNMAX), KC
    nq = seq // tq
    nwb = WINDOW // tq + 1
    ncp = seq // CMP_STRIDE
    nslp = msel.shape[1]
    row = lambda w: pl.BlockSpec((tq, w), lambda b, i: (b * nq + i, 0))
    once = pl.Buffered(1)
    win = lambda jb: pl.BlockSpec((tq, LANES), lambda b, i: (b * nq + jnp.maximum(i - (nwb - 1) + jb, 0), 0))
    in_specs = ([row(NSA_WIDTH), row(NSA_WIDTH), row(LANES),
                 pl.BlockSpec((1, 1, ncp, LANES), lambda b, i: (0, b, 0, 0), pipeline_mode=once),
                 pl.BlockSpec((1, 1, ncp, LANES), lambda b, i: (1, b, 0, 0), pipeline_mode=once),
                 pl.BlockSpec((seq, 2 * LANES), lambda b, i: (b, 0), pipeline_mode=once),
                 pl.BlockSpec((seq, LANES), lambda b, i: (b, 0), pipeline_mode=once)]
                + [win(jb) for jb in range(nwb)] + [win(jb) for jb in range(nwb)]
                + [pl.BlockSpec((ncp, nslp), lambda b, i: (0, 0), pipeline_mode=once),
                   pl.BlockSpec((1, LANES), lambda b, i: (0, 0)),
                   pl.BlockSpec((LANES, 3 * NSA_WIDTH), lambda b, i: (0, 0), pipeline_mode=once)])
    return pl.pallas_call(
        functools.partial(_nsa_attn_kernel, tq=tq, seq=seq, nwb=nwb, kc=kc, ncp=ncp, nslp=nslp,
                          shifted=shifted),
        grid=(batch, nq),
        in_specs=in_specs,
        out_specs=row(NSA_WIDTH),
        out_shape=jax.ShapeDtypeStruct((batch * seq, NSA_WIDTH), MXU_DTYPE),
        scratch_shapes=[pltpu.VMEM((NSA_KV_HEADS, tq, nslp), MXU_DTYPE),
                        pltpu.VMEM((NSA_HEADS * tq, 2 * LANES), MXU_DTYPE),
                        pltpu.VMEM((NSA_HEADS * tq, LANES), F32),
                        pltpu.VMEM((NSA_HEADS * tq, LANES), F32),
                        pltpu.VMEM((NSA_HEADS * tq, LANES), F32),
                        pltpu.VMEM((NSA_HEADS * tq, LANES), F32),
                        pltpu.VMEM((NSA_KV_HEADS, nslp, tq), F32),
                        pltpu.VMEM((NSA_HEADS * tq, LANES), F32),
                        pltpu.VMEM((NSA_KV_HEADS, tq, nslp), F32),
                        pltpu.VMEM((NSA_HEADS * tq, ncp) if shifted else (SUBLANES, LANES), F32)],
        compiler_params=pltpu.CompilerParams(dimension_semantics=("parallel", "arbitrary"),
                                             vmem_limit_bytes=VMEM_LIMIT),
        name="nsa_attn" if shifted else "nsa_attn_runmax",
    )(qn, qr, gt, kvc, kvc, ksa, vs, *([kw] * nwb), *([vw] * nwb), msel, cshift, _gate_expander())


def _out_ffn_kernel(x_ref, og_ref, on_ref, c_ref, ch_ref, cw_ref, wo_ref, fg_ref, wgu_ref, wd_ref, o_ref,
                    *, tm, seq):
    i = pl.program_id(0)
    keep = jnp.where((i * tm) % seq == 0, 0.0, 1.0)
    rows = lax.broadcasted_iota(jnp.int32, (tm, 1), 0)
    u = c_ref[:, CONV_WIDTH:2 * CONV_WIDTH] * c_ref[:, 2 * CONV_WIDTH:3 * CONV_WIDTH]
    hu = ch_ref[:, CONV_WIDTH:2 * CONV_WIDTH] * ch_ref[:, 2 * CONV_WIDTH:3 * CONV_WIDTH] * keep
    w = cw_ref[...]
    conv = w[CONV_K - 1:CONV_K, :] * u
    for s in range(1, CONV_K):
        conv = conv + w[CONV_K - 1 - s:CONV_K - s, :] * _shift_rows(u, hu, s, rows)
    oc = c_ref[:, 0:CONV_WIDTH] * conv
    x1 = (x_ref[...] + _dot(og_ref[...], wo_ref[0:GDN_WIDTH, :])
          + _dot(on_ref[...], wo_ref[GDN_WIDTH:GDN_WIDTH + NSA_WIDTH, :])
          + _dot(oc, wo_ref[GDN_WIDTH + NSA_WIDTH:, :]))
    ms = jnp.mean(x1 * x1, axis=-1, keepdims=True)
    h2 = (x1 * lax.rsqrt(ms + EPS) * fg_ref[...]).astype(MXU_DTYPE)
    o_ref[...] = x1
    for c0 in range(0, D_FF, FF_CHUNK):
        gate = jnp.dot(h2, wgu_ref[:, c0:c0 + FF_CHUNK], preferred_element_type=F32)
        up = jnp.dot(h2, wgu_ref[:, D_FF + c0:D_FF + c0 + FF_CHUNK], preferred_element_type=F32)
        o_ref[...] += _dot(_silu(gate) * up, wd_ref[c0:c0 + FF_CHUNK, :])


def _out_ffn(x2, o_gdn, o_nsa, proj, conv_w, w_out, fgain, wgu, wd, seq):
    m = x2.shape[0]
    tm = TM_FFN
    hb = tm // SUBLANES
    full = lambda a, b: pl.BlockSpec((a, b), lambda i: (0, 0), pipeline_mode=pl.Buffered(1))
    return pl.pallas_call(
        functools.partial(_out_ffn_kernel, tm=tm, seq=seq),
        grid=(m // tm,),
        in_specs=[pl.BlockSpec((tm, D_MODEL), lambda i: (i, 0)),
                  pl.BlockSpec((tm, GDN_WIDTH), lambda i: (i, 0)),
                  pl.BlockSpec((tm, NSA_WIDTH), lambda i: (i, 0)),
                  pl.BlockSpec((tm, C_W), lambda i: (i, C_OFF // C_W)),
                  pl.BlockSpec((SUBLANES, C_W), lambda i: (jnp.maximum(i * hb - 1, 0), C_OFF // C_W)),
                  full(CONV_K, CONV_WIDTH), full(D_MODEL, D_MODEL), full(1, D_MODEL),
                  full(D_MODEL, 2 * D_FF), full(D_FF, D_MODEL)],
        out_specs=pl.BlockSpec((tm, D_MODEL), lambda i: (i, 0)),
        out_shape=jax.ShapeDtypeStruct((m, D_MODEL), F32),
        compiler_params=pltpu.CompilerParams(dimension_semantics=("parallel",),
                                             vmem_limit_bytes=VMEM_LIMIT),
        name="out_ffn",
    )(x2, o_gdn, o_nsa, proj, proj, conv_w, w_out, fgain, wgu, wd)


def _proj_column_map():
    offs = np.concatenate([[0], np.cumsum(IN_SIZES)])
    seg = lambda k: np.arange(offs[k], offs[k + 1])
    pad = lambda n: -np.ones(n, np.int64)
    nq = seg(6).reshape(NSA_HEADS, HEAD)[list(Q_SLOT_HEADS)].reshape(-1)
    cols = np.concatenate([
        seg(14), seg(15), seg(16),
        seg(7), seg(8),
        seg(0), seg(1), seg(2), seg(3),
        seg(4), seg(5), pad(GG_W - 2 * GDN_HEADS),
        nq, seg(9), seg(11), seg(10), seg(12), seg(13), pad(LANES - 3 * NSA_HEADS)])
    assert cols.shape[0] == W_COLS
    return cols


def _block_diag_ones(n):
    idx = np.arange(n) // HEAD
    return jnp.asarray(idx[:, None] == idx[None, :], MXU_DTYPE)


def _head_expander(first_lane):
    mat = np.zeros((LANES, GDN_WIDTH), np.float32)
    for h in range(GDN_HEADS):
        mat[first_lane + h, h * HEAD:(h + 1) * HEAD] = 1.0
    return jnp.asarray(mat, MXU_DTYPE)


def _compress_weights(w1, w2):
    nslab = 2 * NSA_KV_HEADS
    slab_kv = np.arange(nslab) // NSA_KV_HEADS
    eye = jnp.eye(nslab, dtype=MXU_DTYPE)
    w1r = w1.astype(MXU_DTYPE).reshape(2, 2, CMP_STRIDE, HEAD, HEAD)[slab_kv]
    w1bd = jnp.einsum("shtde,sS->tsdhSe", w1r, eye)
    w2bd = jnp.einsum("sde,sS->sdSe", w2.astype(MXU_DTYPE)[slab_kv], eye)
    return w1bd.reshape(CMP_STRIDE, P_W, 2 * P_W), w2bd.reshape(P_W, P_W)


def _rope_table(positions):
    half = ROT_DIM // 2
    inv = jnp.float32(ROPE_THETA) ** (-jnp.arange(0, ROT_DIM, 2, dtype=jnp.float32) / ROT_DIM)
    inv_l = jnp.concatenate([inv, inv, jnp.zeros((LANES - ROT_DIM,), F32)])
    ang = positions.astype(jnp.float32).reshape(-1)[:, None] * inv_l[None, :]
    lane = jnp.arange(LANES)[None, :]
    return jnp.where(lane < half, jnp.cos(ang), jnp.where(lane < ROT_DIM, jnp.sin(ang), 0.0))


def _rope_expanders():
    half = ROT_DIM // 2
    ex = np.zeros((3, LANES, LANES), np.float32)
    cpat = np.zeros((1, LANES), np.float32)
    for j in range(LANES):
        d = j % HEAD
        if d < ROT_DIM:
            ex[0, d % half, j] = 1.0
        else:
            cpat[0, j] = 1.0
        if d < half:
            ex[1, half + d, j] = -1.0
        elif d < ROT_DIM:
            ex[2, half + (d - half), j] = 1.0
    return jnp.asarray(ex, MXU_DTYPE), jnp.asarray(cpat, F32)


def _selection_matrix(ncp, nslp):
    ratio = SLC_BLOCK // CMP_STRIDE
    frac = np.minimum(CMP_LEN, SLC_BLOCK - CMP_STRIDE * np.arange(ratio)).astype(np.float64) / CMP_LEN
    mat = np.zeros((ncp, nslp), np.float32)
    c = np.arange(ncp)
    mat[c, c // ratio] = frac[c % ratio]
    nxt = c // ratio + 1
    ok = nxt < nslp
    mat[c[ok], nxt[ok]] += (1.0 - frac[c % ratio])[ok]
    return jnp.asarray(mat, MXU_DTYPE)


def kernel(x, positions, attn_norm, w_in, gdn_conv_w, gdn_a_log, gdn_dt_bias, gdn_norm, nsa_q_norm,
           nsa_k_norm, nsa_cmp_pe, nsa_cmp_w1, nsa_cmp_w2, conv_w, w_out, ffn_norm, w_gate_up, w_down):
    batch, seq, _ = x.shape
    depth = w_in.shape[0]
    m = batch * seq
    assert seq % max(TM_PROJ, TM_FFN, T_GDN, KC) == 0 and (seq // CMP_STRIDE) % LANES == 0
    nb = seq // CMP_STRIDE
    nslp = -(-(seq // SLC_BLOCK) // LANES) * LANES

    cols = _proj_column_map()
    take = jnp.asarray(np.maximum(cols, 0), jnp.int32)
    valid = jnp.asarray(cols >= 0)
    assert Q_SLOT_HEADS == tuple(g * NSA_GROUP + j for j in range(NSA_GROUP) for g in range(NSA_KV_HEADS))
    bd128, bd256, bd512 = _block_diag_ones(128), _block_diag_ones(256), _block_diag_ones(512)
    rope_t = _rope_table(positions)
    rope_ex, rope_cp = _rope_expanders()
    msel = _selection_matrix(nb, nslp)
    lane_pad = lambda v: jnp.zeros((1, LANES), F32).at[0, GDN_HEADS:2 * GDN_HEADS].set(v.astype(F32))

    x2 = x.reshape(m, D_MODEL)
    for l in range(depth):
        w_l = jnp.where(valid[None, :], jnp.take(w_in[l], take, axis=1), 0.0).astype(MXU_DTYPE)
        qg = jnp.tile(nsa_q_norm[l], NSA_HEADS).reshape(1, NSA_WIDTH)
        kg = jnp.tile(nsa_k_norm[l, 1:3], (1, NSA_KV_HEADS))
        proj, qn, qr, ksa, kw, vs, vw, gt = _in_proj(x2, attn_norm[l].reshape(1, D_MODEL), w_l, rope_t, rope_ex,
                                                     rope_cp, qg, kg, bd512, bd128, seq)

        o_gdn = _gdn(proj, gdn_conv_w[l], lane_pad(gdn_a_log[l]), lane_pad(gdn_dt_bias[l]),
                     jnp.tile(gdn_norm[l], GDN_HEADS).reshape(1, GDN_WIDTH), bd256,
                     _head_expander(0), _head_expander(GDN_HEADS), batch, seq)

        w1bd, w2bd = _compress_weights(nsa_cmp_w1[l], nsa_cmp_w2[l])
        kvc = _compress(proj, w1bd, nsa_cmp_pe[l].reshape(2, CMP_LEN * HEAD), nsa_cmp_w1[l].astype(MXU_DTYPE),
                        w2bd, jnp.tile(nsa_k_norm[l, 0], NSA_KV_HEADS).reshape(1, NSA_KV_WIDTH), bd128,
                        batch, seq)

        bound = (HEAD ** 0.5) * jnp.max(jnp.abs(nsa_q_norm[l])) * jnp.max(jnp.abs(nsa_k_norm[l]), axis=1)
        cshift = jnp.zeros((1, LANES), F32).at[0, 0:3].set((bound * LOG2E).astype(F32))
        attn_args = (qn, qr, gt, kvc, ksa, vs, kw, vw, msel, cshift)
        o_nsa = lax.cond(
            jnp.max(bound) <= MAX_SOFTMAX_SHIFT,
            lambda a: _nsa_attn(*a, batch, seq, True),
            lambda a: _nsa_attn(*a, batch, seq, False),
            attn_args)

        wo = w_out[l].astype(MXU_DTYPE)
        wo_nsa = wo[GDN_WIDTH:GDN_WIDTH + NSA_WIDTH].reshape(NSA_KV_HEADS, NSA_GROUP, HEAD, D_MODEL)
        wo = jnp.concatenate([wo[:GDN_WIDTH], wo_nsa.transpose(1, 0, 2, 3).reshape(NSA_WIDTH, D_MODEL),
                              wo[GDN_WIDTH + NSA_WIDTH:]], axis=0)
        x2 = _out_ffn(x2, o_gdn, o_nsa, proj, conv_w[l], wo,
                      ffn_norm[l].reshape(1, D_MODEL), w_gate_up[l].astype(MXU_DTYPE),
                      w_down[l].astype(MXU_DTYPE), seq)
    return x2.reshape(batch, seq, D_MODEL)
```

```python
import functools

import numpy as np
import jax
import jax.numpy as jnp
from jax import lax
from jax.experimental import pallas as pl
from jax.experimental.pallas import tpu as pltpu

F32 = jnp.float32
MXU_DTYPE = jnp.bfloat16

D_MODEL = 1024
HEAD = 64
EPS = 1e-6
NEG_INF = -1e30
GDN_HEADS = 4
GDN_WIDTH = 256
GDN_CONV = 4
GDN_CHUNK = 64
NSA_HEADS = 8
NSA_WIDTH = 512
NSA_KV_HEADS = 2
NSA_GROUP = 4
NSA_KV_WIDTH = 128
CMP_STRIDE = 16
CMP_LEN = 32
SLC_BLOCK = 64
N_SELECT = 16
WINDOW = 512
CONV_WIDTH = 256
CONV_K = 3
ROPE_THETA = 500000.0
ROT_DIM = 16
D_FF = 2816
IN_SIZES = (256, 256, 256, 256, 4, 4, 512, 128, 128, 128, 128, 128, 128, 24, 256, 256, 256)

LANES = 128
SUBLANES = 8
VMEM_LIMIT = 56 * 1024 * 1024

C_W = 3 * CONV_WIDTH
P_W = 2 * NSA_KV_WIDTH
GM_W = 4 * GDN_WIDTH
GG_W = LANES
N_W = 512 + 4 * 128 + LANES
C_OFF, P_OFF, GM_OFF, GG_OFF = 0, C_W, C_W + P_W, C_W + P_W + GM_W
PROJ_W = C_W + P_W + GM_W + GG_W
N_OFF = PROJ_W
W_COLS = PROJ_W + N_W
Q_SLOT_HEADS = (0, 4, 1, 5, 2, 6, 3, 7)
N_KS, N_KW, N_VS, N_VW, N_GATE = (NSA_WIDTH + k * LANES for k in range(5))

SEL_BIG = 16384.0
MAX_SOFTMAX_SHIFT = 40.0
LOG2E = 1.4426950408889634

TM_PROJ = 512
TM_FFN = 512
T_GDN = 512
TQ = 256
TQ_RUNMAX = 128
KC = 1024
CHUNKS_PER_TRIP = 4
FF_CHUNK = 256


def _dot(a, b):
    return jnp.dot(a.astype(MXU_DTYPE), b.astype(MXU_DTYPE), preferred_element_type=F32)


def _dot_nt(a, b):
    return lax.dot_general(a.astype(MXU_DTYPE), b.astype(MXU_DTYPE), (((1,), (1,)), ((), ())),
                           preferred_element_type=F32)


def _dot_tn(a, b):
    return lax.dot_general(a.astype(MXU_DTYPE), b.astype(MXU_DTYPE), (((0,), (0,)), ((), ())),
                           preferred_element_type=F32)


def _split3(x):
    a = x.astype(MXU_DTYPE)
    r = x - a.astype(F32)
    b = r.astype(MXU_DTYPE)
    c = (r - b.astype(F32)).astype(MXU_DTYPE)
    return a, b, c


def _dot_hi_l(x, m):
    a, b, c = _split3(x)
    f = lambda t: jnp.dot(t, m, preferred_element_type=F32)
    return f(a) + f(b) + f(c)


def _dot_hi2_l(x, m):
    a = x.astype(MXU_DTYPE)
    b = (x - a.astype(F32)).astype(MXU_DTYPE)
    return jnp.dot(a, m, preferred_element_type=F32) + jnp.dot(b, m, preferred_element_type=F32)


def _silu(x):
    return x * jax.nn.sigmoid(x)


def _shift_rows(x, halo, s, rows):
    y = pltpu.roll(x, s, 0)
    for r in range(s):
        y = jnp.where(rows == r, halo[SUBLANES - s + r:SUBLANES - s + r + 1, :], y)
    return y


def _in_proj_kernel(x_ref, g_ref, w_ref, t_ref, ex_ref, cp_ref, qg_ref, kg_ref, bdq_ref, bdk_ref,
                    o_ref, qn_ref, qr_ref, ksa_ref, kw_ref, vs_ref, vw_ref, gt_ref, *, tm, seq):
    x = x_ref[...]
    ms = jnp.mean(x * x, axis=-1, keepdims=True)
    h = (x * lax.rsqrt(ms + EPS) * g_ref[...]).astype(MXU_DTYPE)
    pending = [(c, min(256, PROJ_W - c)) for c in range(0, PROJ_W, 256)]

    def emit(count):
        for _ in range(min(count, len(pending))):
            c, cw = pending.pop(0)
            o_ref[:, c:c + cw] = jnp.dot(h, w_ref[:, c:c + cw], preferred_element_type=F32)

    nblk = jnp.dot(h, w_ref[:, N_OFF:N_OFF + N_W], preferred_element_type=F32)
    _nsa_operands(nblk, t_ref, ex_ref, cp_ref, qg_ref, kg_ref, bdq_ref, bdk_ref,
                  qn_ref, qr_ref, ksa_ref, kw_ref, vs_ref, vw_ref, gt_ref, tt=tm, seq=seq, between=emit)
    emit(len(pending))


def _in_proj(x2, gain, w, rope_t, rope_ex, rope_cp, qg, kg, bd512, bd128, seq):
    m = x2.shape[0]
    tm = TM_PROJ
    row = lambda wd: pl.BlockSpec((tm, wd), lambda i: (i, 0))
    full = lambda a, b: pl.BlockSpec((a, b), lambda i: (0, 0))
    sds = lambda wd, dt: jax.ShapeDtypeStruct((m, wd), dt)
    return pl.pallas_call(
        functools.partial(_in_proj_kernel, tm=tm, seq=seq),
        grid=(m // tm,),
        in_specs=[row(D_MODEL), full(1, D_MODEL), full(D_MODEL, W_COLS), row(LANES),
                  pl.BlockSpec((3, LANES, LANES), lambda i: (0, 0, 0)), full(1, LANES),
                  full(1, NSA_WIDTH), full(2, LANES), full(NSA_WIDTH, NSA_WIDTH), full(LANES, LANES)],
        out_specs=[row(PROJ_W), row(NSA_WIDTH), row(NSA_WIDTH), row(2 * LANES), row(LANES), row(LANES),
                   row(LANES), row(LANES)],
        out_shape=[sds(PROJ_W, F32), sds(NSA_WIDTH, MXU_DTYPE), sds(NSA_WIDTH, MXU_DTYPE),
                   sds(2 * LANES, MXU_DTYPE), sds(LANES, MXU_DTYPE), sds(LANES, MXU_DTYPE),
                   sds(LANES, MXU_DTYPE), sds(LANES, F32)],
        compiler_params=pltpu.CompilerParams(dimension_semantics=("parallel",),
                                             vmem_limit_bytes=VMEM_LIMIT),
        name="in_proj",
    )(x2, gain, w, rope_t, rope_ex, rope_cp, qg, kg, bd512, bd128)


def _gdn_kernel(g_ref, gh_ref, gg_ref, cw_ref, alog_ref, dt_ref, gn_ref, bd_ref, eb_ref, eg_ref, o_ref, s_ref,
                *, tt):
    t_idx = pl.program_id(1)

    @pl.when(t_idx == 0)
    def _():
        s_ref[...] = jnp.zeros_like(s_ref)

    keep = jnp.where(t_idx == 0, 0.0, 1.0)
    rows = lax.broadcasted_iota(jnp.int32, (tt, 1), 0)
    x = g_ref[:, 0:3 * GDN_WIDTH]
    hx = gh_ref[:, 0:3 * GDN_WIDTH] * keep
    w = cw_ref[...]
    y = w[GDN_CONV - 1:GDN_CONV, :] * x
    for s in range(1, GDN_CONV):
        y = y + w[GDN_CONV - 1 - s:GDN_CONV - s, :] * _shift_rows(x, hx, s, rows)
    y = _silu(y)
    q = y[:, 0:GDN_WIDTH]
    k = y[:, GDN_WIDTH:2 * GDN_WIDTH]
    v = y[:, 2 * GDN_WIDTH:3 * GDN_WIDTH]
    bd = bd_ref[...]
    q = q * lax.rsqrt(_dot_hi2_l(q * q, bd) + EPS) * (HEAD ** -0.5)
    k = k * lax.rsqrt(_dot_hi2_l(k * k, bd) + EPS)

    gg = gg_ref[...]
    lane = lax.broadcasted_iota(jnp.int32, (1, LANES), 1)
    xa = gg + dt_ref[...]
    softplus = jnp.maximum(xa, 0.0) + jnp.log1p(jnp.exp(-jnp.abs(xa)))
    g2 = jnp.where(lane < GDN_HEADS, jax.nn.sigmoid(gg), -jnp.exp(alog_ref[...]) * softplus)
    cs = g2.T
    lane_t = lax.broadcasted_iota(jnp.int32, (1, tt), 1) % GDN_CHUNK
    step = 1
    while step < GDN_CHUNK:
        cs = cs + jnp.where(lane_t >= step, pltpu.roll(cs, step, 1), 0.0)
        step *= 2
    gcum_t = cs
    gcum = cs.T
    beta_e = _dot_hi_l(g2, eb_ref[...])
    gcum_e = _dot_hi_l(gcum, eg_ref[...])
    eg_e = jnp.exp(gcum_e)
    kb = k * beta_e
    rv = v * beta_e
    rk = kb * eg_e
    qd = q * eg_e

    nst = GDN_HEADS * GDN_CHUNK
    ri = lax.broadcasted_iota(jnp.int32, (nst, nst), 0)
    ci = lax.broadcasted_iota(jnp.int32, (nst, nst), 1)
    same = (ri // GDN_CHUNK) == (ci // GDN_CHUNK)
    m_tril = same & ((ri % GDN_CHUNK) >= (ci % GDN_CHUNK))
    m_strict = same & ((ri % GDN_CHUNK) > (ci % GDN_CHUNK))
    tile4 = lambda t: jnp.concatenate([t] * GDN_HEADS, axis=0)
    expand = lambda t: tile4(t.astype(MXU_DTYPE)) * bd

    nchunk = tt // GDN_CHUNK
    rss = [slice(n * GDN_CHUNK, (n + 1) * GDN_CHUNK) for n in range(nchunk)]
    g_last = [gcum_e[(n + 1) * GDN_CHUNK - 1:(n + 1) * GDN_CHUNK, :] for n in range(nchunk)]
    rmat, pw, qk = [], [], []
    for n, rs in enumerate(rss):
        g_row = jnp.concatenate([gcum_t[GDN_HEADS + h:GDN_HEADS + h + 1, rs] for h in range(GDN_HEADS)], axis=1)
        decay = jnp.where(m_tril, jnp.exp(jnp.where(m_tril, tile4(gcum_e[rs]) - g_row, 0.0)), 0.0)
        k4 = tile4(k[rs])
        a = jnp.where(m_strict, _dot_nt(expand(kb[rs]), k4) * decay, 0.0)
        qk.append(jnp.where(m_tril, _dot_nt(expand(q[rs]), k4) * decay, 0.0))
        rmat.append(-a)
        pw.append(-a)
    for _ in range(GDN_CHUNK.bit_length() - 2):
        nxt_pw, nxt_r = [], []
        for r, t in zip(rmat, pw):
            t2 = _dot(t, t)
            nxt_pw.append(t2)
            nxt_r.append(r + t2 + _dot(r, t2))
        pw, rmat = nxt_pw, nxt_r
    u, wm = [], []
    for n, rs in enumerate(rss):
        rv_x = jnp.where(same, tile4(rv[rs]), 0.0)
        rk_x = jnp.where(same, tile4(rk[rs]), 0.0)
        u.append(rv_x + _dot(rmat[n], rv_x))
        wm.append(rk_x + _dot(rmat[n], rk_x))

    st = s_ref[...]
    outs = []
    for n, rs in enumerate(rss):
        kd = k[rs] * jnp.exp(g_last[n] - gcum_e[rs])
        v_new = u[n] - _dot(wm[n], st)
        o_x = _dot(expand(qd[rs]), st) + _dot(qk[n], v_new)
        st = st * jnp.exp(g_last[n]) + _dot_tn(expand(kd), v_new)
        o = o_x[0:GDN_CHUNK]
        for h in range(1, GDN_HEADS):
            o = o + o_x[h * GDN_CHUNK:(h + 1) * GDN_CHUNK]
        outs.append(o)
    s_ref[...] = st
    o_all = jnp.concatenate(outs, axis=0)
    on = o_all * lax.rsqrt(_dot_hi2_l(o_all * o_all, bd) * (1.0 / HEAD) + EPS) * gn_ref[...]
    o_ref[...] = (on * _silu(g_ref[:, 3 * GDN_WIDTH:4 * GDN_WIDTH])).astype(o_ref.dtype)


def _gdn(proj, conv_w, alog_pad, dt_pad, gnorm, bd256, eb, eg, batch, seq):
    tt = T_GDN
    nt = seq // tt
    hb = tt // SUBLANES
    return pl.pallas_call(
        functools.partial(_gdn_kernel, tt=tt),
        grid=(batch, nt),
        in_specs=[pl.BlockSpec((tt, GM_W), lambda b, t: (b * nt + t, GM_OFF // GM_W)),
                  pl.BlockSpec((SUBLANES, GM_W),
                               lambda b, t: (jnp.maximum((b * nt + t) * hb - 1, 0), GM_OFF // GM_W)),
                  pl.BlockSpec((tt, GG_W), lambda b, t: (b * nt + t, GG_OFF // GG_W)),
                  pl.BlockSpec((GDN_CONV, 3 * GDN_WIDTH), lambda b, t: (0, 0)),
                  pl.BlockSpec((1, LANES), lambda b, t: (0, 0)),
                  pl.BlockSpec((1, LANES), lambda b, t: (0, 0)),
                  pl.BlockSpec((1, GDN_WIDTH), lambda b, t: (0, 0)),
                  pl.BlockSpec((GDN_WIDTH, GDN_WIDTH), lambda b, t: (0, 0)),
                  pl.BlockSpec((LANES, GDN_WIDTH), lambda b, t: (0, 0)),
                  pl.BlockSpec((LANES, GDN_WIDTH), lambda b, t: (0, 0))],
        out_specs=pl.BlockSpec((tt, GDN_WIDTH), lambda b, t: (b * nt + t, 0)),
        out_shape=jax.ShapeDtypeStruct((batch * seq, GDN_WIDTH), MXU_DTYPE),
        scratch_shapes=[pltpu.VMEM((GDN_WIDTH, GDN_WIDTH), F32)],
        compiler_params=pltpu.CompilerParams(dimension_semantics=("parallel", "arbitrary"),
                                             vmem_limit_bytes=VMEM_LIMIT),
        name="gdn",
    )(proj, proj, proj, conv_w, alog_pad, dt_pad, gnorm, bd256, eb, eg)


def _rope(x, c, s1, s2):
    wdt = x.shape[1]
    return x * c + pltpu.roll(x, wdt - ROT_DIM // 2, 1) * s1 + pltpu.roll(x, ROT_DIM // 2, 1) * s2


def _nsa_operands(n_blk, t_ref, ex_ref, cp_ref, qg_ref, kg_ref, bdq_ref, bdk_ref,
                  qn_ref, qr_ref, ksa_ref, kw_ref, vs_ref, vw_ref, gt_ref, *, tt, seq, between):
    between(3)
    table = t_ref[...]
    c1 = _dot_hi2_l(table, ex_ref[0]) + cp_ref[...]
    s1 = _dot_hi2_l(table, ex_ref[1])
    s2 = _dot_hi2_l(table, ex_ref[2])
    c4 = jnp.concatenate([c1] * 4, axis=1)
    s14 = jnp.concatenate([s1] * 4, axis=1)
    s24 = jnp.concatenate([s2] * 4, axis=1)
    q = n_blk[:, 0:NSA_WIDTH]
    qn = q * lax.rsqrt(_dot_hi2_l(q * q, bdq_ref[...]) * (1.0 / HEAD) + EPS) * qg_ref[...]
    qn = qn * (HEAD ** -0.5 * LOG2E)
    qn_ref[...] = qn.astype(qn_ref.dtype)
    qr_ref[...] = _rope(qn, c4, s14, s24).astype(qr_ref.dtype)
    between(3)
    bdk = bdk_ref[...]
    ks = n_blk[:, N_KS:N_KS + LANES]
    kw = n_blk[:, N_KW:N_KW + LANES]
    ks = ks * lax.rsqrt(_dot_hi2_l(ks * ks, bdk) * (1.0 / HEAD) + EPS) * kg_ref[0:1, :]
    kw = kw * lax.rsqrt(_dot_hi2_l(kw * kw, bdk) * (1.0 / HEAD) + EPS) * kg_ref[1:2, :]
    ksa_ref[:, 0:LANES] = _rope(ks, c1, s1, s2).astype(ksa_ref.dtype)
    rows = (lax.broadcasted_iota(jnp.int32, (tt, LANES), 0) + pl.program_id(0) * tt) % seq
    lane = lax.broadcasted_iota(jnp.int32, (tt, LANES), 1)
    ksa_ref[:, LANES:2 * LANES] = jnp.where((rows // SLC_BLOCK) % LANES == lane, 1.0, 0.0).astype(ksa_ref.dtype)
    kw_ref[...] = _rope(kw, c1, s1, s2).astype(kw_ref.dtype)
    vs_ref[...] = n_blk[:, N_VS:N_VS + LANES].astype(vs_ref.dtype)
    vw_ref[...] = n_blk[:, N_VW:N_VW + LANES].astype(vw_ref.dtype)
    gt_ref[...] = jax.nn.sigmoid(n_blk[:, N_GATE:N_GATE + LANES])


def _compress_kernel(xk_ref, xv_ref, w1bd_ref, pe_ref, w1_ref, w2bd_ref, kg_ref, bd_ref, o_ref, *, nb):
    acc = None
    for t in range(CMP_STRIDE):
        xt = jnp.concatenate([xk_ref[pl.ds(t, nb, stride=CMP_STRIDE), :],
                              xv_ref[pl.ds(t, nb, stride=CMP_STRIDE), :]], axis=1)
        part = _dot(xt, w1bd_ref[t])
        acc = part if acc is None else acc + part
    pe_terms = [_dot(jnp.broadcast_to(pe_ref[c:c + 1, :], (SUBLANES, CMP_LEN * HEAD)), w1_ref[c])[0:1, :]
                for c in range(2)]
    pe_all = jnp.concatenate([pe_terms[0]] * NSA_KV_HEADS + [pe_terms[1]] * NSA_KV_HEADS, axis=1)
    half = 2 * NSA_KV_WIDTH
    pre = acc[:, 0:half] + pltpu.roll(acc[:, half:2 * half], nb - 1, 0) + pe_all
    y = _dot(jax.nn.gelu(pre), w2bd_ref[...])
    rows = lax.broadcasted_iota(jnp.int32, (nb, 1), 0)
    y = jnp.where(rows < nb - 1, y, 0.0)
    yk = y[:, 0:NSA_KV_WIDTH]
    yk = yk * lax.rsqrt(_dot_hi2_l(yk * yk, bd_ref[...]) * (1.0 / HEAD) + EPS) * kg_ref[...]
    o_ref[0, 0] = yk.astype(o_ref.dtype)
    o_ref[1, 0] = y[:, NSA_KV_WIDTH:2 * NSA_KV_WIDTH].astype(o_ref.dtype)


def _compress(proj, w1bd, pe, w1, w2bd, kg0, bd128, batch, seq):
    nb = seq // CMP_STRIDE
    full = lambda *s: pl.BlockSpec(s, lambda b: (0,) * len(s))
    return pl.pallas_call(
        functools.partial(_compress_kernel, nb=nb),
        grid=(batch,),
        in_specs=[pl.BlockSpec((seq, NSA_KV_WIDTH), lambda b: (b, P_OFF // NSA_KV_WIDTH)),
                  pl.BlockSpec((seq, NSA_KV_WIDTH), lambda b: (b, P_OFF // NSA_KV_WIDTH + 1)),
                  full(CMP_STRIDE, P_W, 2 * P_W), full(2, CMP_LEN * HEAD), full(2, CMP_LEN * HEAD, HEAD),
                  full(P_W, P_W), full(1, NSA_KV_WIDTH), full(LANES, LANES)],
        out_specs=pl.BlockSpec((2, 1, nb, NSA_KV_WIDTH), lambda b: (0, b, 0, 0)),
        out_shape=jax.ShapeDtypeStruct((2, batch, nb, NSA_KV_WIDTH), MXU_DTYPE),
        compiler_params=pltpu.CompilerParams(dimension_semantics=("parallel",),
                                             vmem_limit_bytes=VMEM_LIMIT),
        name="compress",
    )(proj, proj, w1bd, pe, w1, w2bd, kg0, bd128)


def _nsa_attn_kernel(*refs, tq, seq, nwb, kc, ncp, nslp, shifted):
    qn_ref, qr_ref, gt_ref, kcmp_ref, vcmp_ref, ksa_ref, vs_ref = refs[:7]
    kw_refs = refs[7:7 + nwb]
    vw_refs = refs[7 + nwb:7 + 2 * nwb]
    msel_ref, cs_ref, egate_ref = refs[7 + 2 * nwb:10 + 2 * nwb]
    o_ref = refs[10 + 2 * nwb]
    (bias_scr, qaug_scr, m_scr, l_scr, acc_scr, ow_scr, cand_scr, oc_scr, slc_scr, e_scr,
     nbad_scr) = refs[11 + 2 * nwb:]
    nslot = NSA_HEADS
    i = pl.program_id(1)
    c_cmp, c_slc, c_win = cs_ref[0:1, 0:1], cs_ref[0:1, 1:2], cs_ref[0:1, 2:3]
    lane = lax.broadcasted_iota(jnp.int32, (1, LANES), 1)
    lo = lane < HEAD

    def stack(q_ref):
        sl = [q_ref[:, j * LANES:(j + 1) * LANES] for j in range(NSA_GROUP)]
        zero = jnp.zeros_like(sl[0])
        return jnp.concatenate([jnp.where(lo, s, zero) for s in sl] + [jnp.where(lo, zero, s) for s in sl], axis=0)

    tpos = i * tq + lax.broadcasted_iota(jnp.int32, (tq, 1), 0)
    tpos_st = i * tq + lax.broadcasted_iota(jnp.int32, (nslot * tq, 1), 0) % tq
    gates = gt_ref[...]

    qn_st = stack(qn_ref)
    if shifted:
        ccw = min(2 * LANES, ncp)
        n_vis = (i + 1) * (tq // CMP_STRIDE) - 1
        n_cch = (n_vis + ccw - 1) // ccw
        qaug_scr[:, 0:LANES] = qn_st
        l_scr[...] = jnp.zeros(l_scr.shape, F32)
        acc_scr[...] = jnp.zeros(acc_scr.shape, F32)

        def cmp_scores(ch):
            k0 = ch * ccw
            s = _dot_nt(qaug_scr[:, 0:LANES], kcmp_ref[0, 0, pl.ds(k0, ccw), :])
            ckey = k0 + lax.broadcasted_iota(jnp.int32, (1, ccw), 1)
            vis = (ckey * CMP_STRIDE + (CMP_LEN - 1)) <= tpos
            cbias = jnp.where(vis, -c_cmp, -SEL_BIG)
            for slot in range(nslot):
                rs = slice(slot * tq, (slot + 1) * tq)
                e = jnp.exp2(s[rs] + cbias)
                e_scr[rs, pl.ds(k0, ccw)] = e
                part = e[:, 0:LANES]
                for t in range(1, ccw // LANES):
                    part = part + e[:, t * LANES:(t + 1) * LANES]
                l_scr[rs, :] += part
            acc_scr[...] += _dot(e_scr[:, pl.ds(k0, ccw)], vcmp_ref[0, 0, pl.ds(k0, ccw), :])

        def for_visible_chunks(fn):
            def variant(count):
                for ch in range(count):
                    fn(ch)

            for count in range(1, ncp // ccw + 1):
                pl.when(n_cch == count)(functools.partial(variant, count))

        for_visible_chunks(cmp_scores)
        l_c = jnp.sum(l_scr[...], axis=-1, keepdims=True)
        rinv_c = 1.0 / jnp.where(l_c > 0.0, l_c, 1.0)
        oc_scr[...] = acc_scr[...] * rinv_c
        l_scr[...] = jnp.broadcast_to(rinv_c, l_scr.shape)
        slc_scr[...] = jnp.zeros(slc_scr.shape, F32)

        def cmp_importance(ch):
            k0 = ch * ccw
            for g in range(NSA_KV_HEADS):
                ps = None
                for r in range(NSA_GROUP):
                    rs = slice((g * NSA_GROUP + r) * tq, (g * NSA_GROUP + r + 1) * tq)
                    rinv_l = jnp.concatenate([l_scr[rs, :]] * (ccw // LANES), axis=1)
                    p = e_scr[rs, pl.ds(k0, ccw)] * rinv_l
                    ps = p if ps is None else ps + p
                slc_scr[g] += _dot_hi_l(ps, msel_ref[pl.ds(k0, ccw), :])

        for_visible_chunks(cmp_importance)
        slc_rows = [slc_scr[g] for g in range(NSA_KV_HEADS)]
    else:
        ckey = lax.broadcasted_iota(jnp.int32, (1, ncp), 1)
        cmask = (ckey * CMP_STRIDE + (CMP_LEN - 1)) <= tpos
        s_all = _dot_nt(qn_st, kcmp_ref[0, 0])
        psum = [None, None]
        e_parts, rinv_parts = [], []
        for slot in range(nslot):
            s = jnp.where(cmask, s_all[slot * tq:(slot + 1) * tq], NEG_INF)
            e = jnp.where(cmask, jnp.exp2(s - jnp.max(s, axis=-1, keepdims=True)), 0.0)
            l = jnp.sum(e, axis=-1, keepdims=True)
            rinv = 1.0 / jnp.where(l > 0.0, l, 1.0)
            e_parts.append(e.astype(MXU_DTYPE))
            rinv_parts.append(rinv)
            p = e * rinv
            g = slot // NSA_GROUP
            psum[g] = p if psum[g] is None else psum[g] + p
        oc_scr[...] = (_dot(jnp.concatenate(e_parts, axis=0), vcmp_ref[0, 0])
                       * jnp.concatenate(rinv_parts, axis=0))
        slc_rows = [_dot_hi_l(psum[g], msel_ref[...]) for g in range(NSA_KV_HEADS)]

    qr_st = stack(qr_ref)
    kwc = jnp.concatenate([r[...] for r in kw_refs], axis=0)
    vwc = jnp.concatenate([r[...] for r in vw_refs], axis=0)
    sw = _dot_nt(qr_st, kwc)
    if shifted:
        rr = lax.broadcasted_iota(jnp.int32, (tq, tq), 0)
        cc = lax.broadcasted_iota(jnp.int32, (tq, tq), 1)
        parts = []
        for jb in range(nwb):
            shift_b = c_win + jnp.where(i - (nwb - 1) + jb >= 0, 0.0, SEL_BIG)
            blk = jnp.exp2(sw[:, jb * tq:(jb + 1) * tq] - shift_b)
            if jb == 0 or jb == nwb - 1:
                vis = (cc > rr) if jb == 0 else (cc <= rr)
                blk = jnp.concatenate([jnp.where(vis, blk[s * tq:(s + 1) * tq], 0.0) for s in range(nslot)],
                                      axis=0)
            parts.append(blk)
        ew = jnp.concatenate(parts, axis=1)
    else:
        kpos_w = (i - (nwb - 1)) * tq + lax.broadcasted_iota(jnp.int32, (1, nwb * tq), 1)
        dist = tpos_st - kpos_w
        wmask = (dist >= 0) & (dist < WINDOW) & (kpos_w >= 0)
        sw = jnp.where(wmask, sw, NEG_INF)
        ew = jnp.exp2(sw - jnp.max(sw, axis=-1, keepdims=True))
    ow_scr[...] = _dot(ew, vwc) * (1.0 / jnp.sum(ew, axis=-1, keepdims=True))

    jrow = lax.broadcasted_iota(jnp.int32, (nslp, 1), 0)
    jrowf = jrow.astype(F32)
    cur_t = (i * tq + lax.broadcasted_iota(jnp.int32, (1, tq), 1)) // SLC_BLOCK
    forced = (jrow == 0) | (jrow == cur_t) | (jrow == cur_t - 1)
    causal = jrow <= cur_t
    shift_s = c_slc if shifted else 0.0
    n_free = N_SELECT - 3
    to_bias = lambda sel: ((sel - 1.0) * SEL_BIG - shift_s).T.astype(bias_scr.dtype)
    for g in range(NSA_KV_HEADS):
        cand_scr[g] = jnp.where(causal & jnp.logical_not(forced), slc_rows[g].T, -1.0)

    def pick_distinct(rows):
        n_bad = None
        for g in range(NSA_KV_HEADS):
            cand = cand_scr[g, 0:rows, :]
            c = cand
            for _ in range(n_free):
                c = jnp.where(c == jnp.max(c, axis=0, keepdims=True), -2.0, c)
            picked = jnp.where((c == -2.0) & (cand >= 0.0), 1.0, 0.0)
            n_picked = jnp.sum(picked, axis=0, keepdims=True)
            n_real = jnp.sum(jnp.where(cand >= 0.0, 1.0, 0.0), axis=0, keepdims=True)
            bad = jnp.where(n_picked == jnp.minimum(n_real, float(n_free)), 0.0, 1.0)
            n_bad = bad if n_bad is None else n_bad + bad
            if rows < nslp:
                picked = jnp.concatenate([picked, jnp.zeros((nslp - rows, tq), F32)], axis=0)
            bias_scr[g] = to_bias(jnp.where(forced, 1.0, picked))
        nbad_scr[...] = jnp.broadcast_to(n_bad, nbad_scr.shape)

    early = (i + 1) * tq <= (nslp // 2) * SLC_BLOCK
    pl.when(early)(functools.partial(pick_distinct, nslp // 2))
    pl.when(jnp.logical_not(early))(functools.partial(pick_distinct, nslp))

    @pl.when(jnp.max(nbad_scr[...]) > 0.0)
    def _():
        for g in range(NSA_KV_HEADS):
            cand = cand_scr[g]
            sel = jnp.where(forced, 1.0, 0.0)
            for _ in range(n_free):
                mx = jnp.max(cand, axis=0, keepdims=True)
                first = jnp.min(jnp.where(cand == mx, jrowf, float(nslp)), axis=0, keepdims=True)
                hit = jrowf == first
                sel = jnp.where(hit, 1.0, sel)
                cand = jnp.where(hit, -2.0, cand)
            bias_scr[g] = to_bias(jnp.where(causal, sel, 0.0))

    qaug_scr[:, 0:LANES] = qr_st
    if not shifted:
        m_scr[...] = jnp.full(m_scr.shape, NEG_INF, F32)
    l_scr[...] = jnp.zeros(l_scr.shape, F32)
    acc_scr[...] = jnp.zeros(acc_scr.shape, F32)
    group_keys = LANES * SLC_BLOCK

    def load_bias(key0):
        @pl.when(key0 % group_keys == 0)
        def _():
            off = pl.multiple_of((key0 // group_keys) * LANES, LANES)
            b0 = bias_scr[0, :, pl.ds(off, LANES)]
            b1 = bias_scr[1, :, pl.ds(off, LANES)]
            qaug_scr[:, LANES:2 * LANES] = jnp.concatenate([b0] * NSA_GROUP + [b1] * NSA_GROUP, axis=0)

    def qk(key0, width):
        return _dot_nt(qaug_scr[...], ksa_ref[pl.ds(pl.multiple_of(key0, width), width), :])

    def scores(key0, width):
        load_bias(key0)
        return qk(key0, width)

    def accumulate(s, key0, width):
        k0 = pl.multiple_of(key0, width)
        if shifted:
            p = jnp.exp2(s)
            psum_l = p[:, 0:LANES]
            for t in range(1, width // LANES):
                psum_l = psum_l + p[:, t * LANES:(t + 1) * LANES]
            l_scr[...] += psum_l
            acc_scr[...] += _dot(p, vs_ref[pl.ds(k0, width), :])
        else:
            m_old = m_scr[...]
            m_new = jnp.maximum(m_old, jnp.max(s, axis=-1, keepdims=True))
            alpha = jnp.exp2(m_old - m_new)
            p = jnp.exp2(s - m_new[:, 0:1])
            l_scr[...] = alpha * l_scr[...] + jnp.sum(p, axis=-1, keepdims=True)
            acc_scr[...] = alpha * acc_scr[...] + _dot(p, vs_ref[pl.ds(k0, width), :])
            m_scr[...] = m_new

    per_trip = CHUNKS_PER_TRIP if (shifted and group_keys % (CHUNKS_PER_TRIP * kc) == 0) else 1
    pair = per_trip * kc
    n_pair = (i * tq) // pair if per_trip > 1 else 0
    n_full = (i * tq - n_pair * pair) // kc
    before_sub = n_pair * pair + n_full * kc
    n_sub = (i * tq - before_sub) // tq

    def body_pair(c, carry):
        key0 = c * pair
        load_bias(key0)
        for h in range(per_trip):
            accumulate(qk(key0 + h * kc, kc), key0 + h * kc, kc)
        return carry

    def body_full(c, carry):
        key0 = n_pair * pair + c * kc
        accumulate(scores(key0, kc), key0, kc)
        return carry

    lax.fori_loop(0, n_pair, body_pair, 0)
    lax.fori_loop(0, n_full, body_full, 0)

    def own_chunk():
        col = lax.broadcasted_iota(jnp.int32, (1, tq), 1)
        accumulate(jnp.where(col <= tpos_st - i * tq, qk(i * tq, tq), -SEL_BIG), i * tq, tq)

    load_bias(before_sub)
    if shifted:
        def tail(count):
            for j in range(count):
                accumulate(qk(before_sub + j * tq, tq), before_sub + j * tq, tq)
            own_chunk()

        for count in range(kc // tq):
            pl.when(n_sub == count)(functools.partial(tail, count))
    else:
        def body_sub(j, carry):
            key0 = before_sub + j * tq
            accumulate(qk(key0, tq), key0, tq)
            return carry

        lax.fori_loop(0, n_sub, body_sub, 0)
        own_chunk()
    if shifted:
        o_s = acc_scr[...] * (1.0 / jnp.sum(l_scr[...], axis=-1, keepdims=True))
    else:
        o_s = acc_scr[...] / l_scr[...]
    o_w = ow_scr[...]

    gate_e = _dot_hi2_l(gates, egate_ref[...])
    for j in range(NSA_GROUP):
        r0 = slice(j * tq, (j + 1) * tq)
        r1 = slice((NSA_GROUP + j) * tq, (NSA_GROUP + j + 1) * tq)
        out = None
        for br, o_b in enumerate((oc_scr[...], o_s, o_w)):
            lanes = slice(br * NSA_WIDTH + j * LANES, br * NSA_WIDTH + (j + 1) * LANES)
            term = gate_e[:, lanes] * jnp.where(lo, o_b[r0], o_b[r1])
            out = term if out is None else out + term
        o_ref[:, j * LANES:(j + 1) * LANES] = out.astype(o_ref.dtype)


def _gate_expander():
    mat = np.zeros((LANES, 3 * NSA_WIDTH), np.float32)
    for br in range(3):
        for j in range(NSA_GROUP):
            for g in range(NSA_KV_HEADS):
                head = g * NSA_GROUP + j
                c0 = br * NSA_WIDTH + j * LANES + g * HEAD
                mat[3 * head + br, c0:c0 + HEAD] = 1.0
    return jnp.asarray(mat, MXU_DTYPE)


def _nsa_attn(qn, qr, gt, kvc, ksa, vs, kw, vw, msel, cshift, batch, seq, shifted):
    tq, kc = (TQ if shifted else TQ_RUNMAX), KC
    nq = seq // tq
    nwb = WINDOW // tq + 1
    ncp = seq // CMP_STRIDE
    nslp = msel.shape[1]
    row = lambda w: pl.BlockSpec((tq, w), lambda b, i: (b * nq + i, 0))
    once = pl.Buffered(1)
    win = lambda jb: pl.BlockSpec((tq, LANES), lambda b, i: (b * nq + jnp.maximum(i - (nwb - 1) + jb, 0), 0))
    in_specs = ([row(NSA_WIDTH), row(NSA_WIDTH), row(LANES),
                 pl.BlockSpec((1, 1, ncp, LANES), lambda b, i: (0, b, 0, 0), pipeline_mode=once),
                 pl.BlockSpec((1, 1, ncp, LANES), lambda b, i: (1, b, 0, 0), pipeline_mode=once),
                 pl.BlockSpec((seq, 2 * LANES), lambda b, i: (b, 0), pipeline_mode=once),
                 pl.BlockSpec((seq, LANES), lambda b, i: (b, 0), pipeline_mode=once)]
                + [win(jb) for jb in range(nwb)] + [win(jb) for jb in range(nwb)]
                + [pl.BlockSpec((ncp, nslp), lambda b, i: (0, 0), pipeline_mode=once),
                   pl.BlockSpec((1, LANES), lambda b, i: (0, 0)),
                   pl.BlockSpec((LANES, 3 * NSA_WIDTH), lambda b, i: (0, 0), pipeline_mode=once)])
    return pl.pallas_call(
        functools.partial(_nsa_attn_kernel, tq=tq, seq=seq, nwb=nwb, kc=kc, ncp=ncp, nslp=nslp,
                          shifted=shifted),
        grid=(batch, nq),
        in_specs=in_specs,
        out_specs=row(NSA_WIDTH),
        out_shape=jax.ShapeDtypeStruct((batch * seq, NSA_WIDTH), MXU_DTYPE),
        scratch_shapes=[pltpu.VMEM((NSA_KV_HEADS, tq, nslp), MXU_DTYPE),
                        pltpu.VMEM((NSA_HEADS * tq, 2 * LANES), MXU_DTYPE),
                        pltpu.VMEM((NSA_HEADS * tq, LANES), F32),
                        pltpu.VMEM((NSA_HEADS * tq, LANES), F32),
                        pltpu.VMEM((NSA_HEADS * tq, LANES), F32),
                        pltpu.VMEM((NSA_HEADS * tq, LANES), F32),
                        pltpu.VMEM((NSA_KV_HEADS, nslp, tq), F32),
                        pltpu.VMEM((NSA_HEADS * tq, LANES), F32),
                        pltpu.VMEM((NSA_KV_HEADS, tq, nslp), F32),
                        pltpu.VMEM((NSA_HEADS * tq, ncp) if shifted else (SUBLANES, LANES), F32),
                        pltpu.VMEM((SUBLANES, tq), F32)],
        compiler_params=pltpu.CompilerParams(dimension_semantics=("parallel", "arbitrary"),
                                             vmem_limit_bytes=VMEM_LIMIT),
        name="nsa_attn" if shifted else "nsa_attn_runmax",
    )(qn, qr, gt, kvc, kvc, ksa, vs, *([kw] * nwb), *([vw] * nwb), msel, cshift, _gate_expander())


def _out_ffn_kernel(x_ref, og_ref, on_ref, c_ref, ch_ref, cw_ref, wo_ref, fg_ref, wgu_ref, wd_ref, o_ref,
                    *, tm, seq):
    i = pl.program_id(0)
    keep = jnp.where((i * tm) % seq == 0, 0.0, 1.0)
    rows = lax.broadcasted_iota(jnp.int32, (tm, 1), 0)
    u = c_ref[:, CONV_WIDTH:2 * CONV_WIDTH] * c_ref[:, 2 * CONV_WIDTH:3 * CONV_WIDTH]
    hu = ch_ref[:, CONV_WIDTH:2 * CONV_WIDTH] * ch_ref[:, 2 * CONV_WIDTH:3 * CONV_WIDTH] * keep
    w = cw_ref[...]
    conv = w[CONV_K - 1:CONV_K, :] * u
    for s in range(1, CONV_K):
        conv = conv + w[CONV_K - 1 - s:CONV_K - s, :] * _shift_rows(u, hu, s, rows)
    oc = c_ref[:, 0:CONV_WIDTH] * conv
    x1 = (x_ref[...] + _dot(og_ref[...], wo_ref[0:GDN_WIDTH, :])
          + _dot(on_ref[...], wo_ref[GDN_WIDTH:GDN_WIDTH + NSA_WIDTH, :])
          + _dot(oc, wo_ref[GDN_WIDTH + NSA_WIDTH:, :]))
    ms = jnp.mean(x1 * x1, axis=-1, keepdims=True)
    h2 = (x1 * lax.rsqrt(ms + EPS) * fg_ref[...]).astype(MXU_DTYPE)
    o_ref[...] = x1
    for c0 in range(0, D_FF, FF_CHUNK):
        gate = jnp.dot(h2, wgu_ref[:, c0:c0 + FF_CHUNK], preferred_element_type=F32)
        up = jnp.dot(h2, wgu_ref[:, D_FF + c0:D_FF + c0 + FF_CHUNK], preferred_element_type=F32)
        o_ref[...] += _dot(_silu(gate) * up, wd_ref[c0:c0 + FF_CHUNK, :])


def _out_ffn(x2, o_gdn, o_nsa, proj, conv_w, w_out, fgain, wgu, wd, seq):
    m = x2.shape[0]
    tm = TM_FFN
    hb = tm // SUBLANES
    full = lambda a, b: pl.BlockSpec((a, b), lambda i: (0, 0), pipeline_mode=pl.Buffered(1))
    return pl.pallas_call(
        functools.partial(_out_ffn_kernel, tm=tm, seq=seq),
        grid=(m // tm,),
        in_specs=[pl.BlockSpec((tm, D_MODEL), lambda i: (i, 0)),
                  pl.BlockSpec((tm, GDN_WIDTH), lambda i: (i, 0)),
                  pl.BlockSpec((tm, NSA_WIDTH), lambda i: (i, 0)),
                  pl.BlockSpec((tm, C_W), lambda i: (i, C_OFF // C_W)),
                  pl.BlockSpec((SUBLANES, C_W), lambda i: (jnp.maximum(i * hb - 1, 0), C_OFF // C_W)),
                  full(CONV_K, CONV_WIDTH), full(D_MODEL, D_MODEL), full(1, D_MODEL),
                  full(D_MODEL, 2 * D_FF), full(D_FF, D_MODEL)],
        out_specs=pl.BlockSpec((tm, D_MODEL), lambda i: (i, 0)),
        out_shape=jax.ShapeDtypeStruct((m, D_MODEL), F32),
        compiler_params=pltpu.CompilerParams(dimension_semantics=("parallel",),
                                             vmem_limit_bytes=VMEM_LIMIT),
        name="out_ffn",
    )(x2, o_gdn, o_nsa, proj, proj, conv_w, w_out, fgain, wgu, wd)


def _proj_column_map():
    offs = np.concatenate([[0], np.cumsum(IN_SIZES)])
    seg = lambda k: np.arange(offs[k], offs[k + 1])
    pad = lambda n: -np.ones(n, np.int64)
    nq = seg(6).reshape(NSA_HEADS, HEAD)[list(Q_SLOT_HEADS)].reshape(-1)
    cols = np.concatenate([
        seg(14), seg(15), seg(16),
        seg(7), seg(8),
        seg(0), seg(1), seg(2), seg(3),
        seg(4), seg(5), pad(GG_W - 2 * GDN_HEADS),
        nq, seg(9), seg(11), seg(10), seg(12), seg(13), pad(LANES - 3 * NSA_HEADS)])
    assert cols.shape[0] == W_COLS
    return cols


def _block_diag_ones(n):
    idx = np.arange(n) // HEAD
    return jnp.asarray(idx[:, None] == idx[None, :], MXU_DTYPE)


def _head_expander(first_lane):
    mat = np.zeros((LANES, GDN_WIDTH), np.float32)
    for h in range(GDN_HEADS):
        mat[first_lane + h, h * HEAD:(h + 1) * HEAD] = 1.0
    return jnp.asarray(mat, MXU_DTYPE)


def _compress_weights(w1, w2):
    nslab = 2 * NSA_KV_HEADS
    slab_kv = np.arange(nslab) // NSA_KV_HEADS
    eye = jnp.eye(nslab, dtype=MXU_DTYPE)
    w1r = w1.astype(MXU_DTYPE).reshape(2, 2, CMP_STRIDE, HEAD, HEAD)[slab_kv]
    w1bd = jnp.einsum("shtde,sS->tsdhSe", w1r, eye)
    w2bd = jnp.einsum("sde,sS->sdSe", w2.astype(MXU_DTYPE)[slab_kv], eye)
    return w1bd.reshape(CMP_STRIDE, P_W, 2 * P_W), w2bd.reshape(P_W, P_W)


def _rope_table(positions):
    half = ROT_DIM // 2
    inv = jnp.float32(ROPE_THETA) ** (-jnp.arange(0, ROT_DIM, 2, dtype=jnp.float32) / ROT_DIM)
    inv_l = jnp.concatenate([inv, inv, jnp.zeros((LANES - ROT_DIM,), F32)])
    ang = positions.astype(jnp.float32).reshape(-1)[:, None] * inv_l[None, :]
    lane = jnp.arange(LANES)[None, :]
    return jnp.where(lane < half, jnp.cos(ang), jnp.where(lane < ROT_DIM, jnp.sin(ang), 0.0))


def _rope_expanders():
    half = ROT_DIM // 2
    ex = np.zeros((3, LANES, LANES), np.float32)
    cpat = np.zeros((1, LANES), np.float32)
    for j in range(LANES):
        d = j % HEAD
        if d < ROT_DIM:
            ex[0, d % half, j] = 1.0
        else:
            cpat[0, j] = 1.0
        if d < half:
            ex[1, half + d, j] = -1.0
        elif d < ROT_DIM:
            ex[2, half + (d - half), j] = 1.0
    return jnp.asarray(ex, MXU_DTYPE), jnp.asarray(cpat, F32)


def _selection_matrix(ncp, nslp):
    ratio = SLC_BLOCK // CMP_STRIDE
    frac = np.minimum(CMP_LEN, SLC_BLOCK - CMP_STRIDE * np.arange(ratio)).astype(np.float64) / CMP_LEN
    mat = np.zeros((ncp, nslp), np.float32)
    c = np.arange(ncp)
    mat[c, c // ratio] = frac[c % ratio]
    nxt = c // ratio + 1
    ok = nxt < nslp
    mat[c[ok], nxt[ok]] += (1.0 - frac[c % ratio])[ok]
    return jnp.asarray(mat, MXU_DTYPE)


def kernel(x, positions, attn_norm, w_in, gdn_conv_w, gdn_a_log, gdn_dt_bias, gdn_norm, nsa_q_norm,
           nsa_k_norm, nsa_cmp_pe, nsa_cmp_w1, nsa_cmp_w2, conv_w, w_out, ffn_norm, w_gate_up, w_down):
    batch, seq, _ = x.shape
    depth = w_in.shape[0]
    m = batch * seq
    assert seq % max(TM_PROJ, TM_FFN, T_GDN, KC) == 0 and (seq // CMP_STRIDE) % LANES == 0
    nb = seq // CMP_STRIDE
    nslp = -(-(seq // SLC_BLOCK) // LANES) * LANES

    cols = _proj_column_map()
    take = jnp.asarray(np.maximum(cols, 0), jnp.int32)
    valid = jnp.asarray(cols >= 0)
    assert Q_SLOT_HEADS == tuple(g * NSA_GROUP + j for j in range(NSA_GROUP) for g in range(NSA_KV_HEADS))
    bd128, bd256, bd512 = _block_diag_ones(128), _block_diag_ones(256), _block_diag_ones(512)
    rope_t = _rope_table(positions)
    rope_ex, rope_cp = _rope_expanders()
    msel = _selection_matrix(nb, nslp)
    lane_pad = lambda v: jnp.zeros((1, LANES), F32).at[0, GDN_HEADS:2 * GDN_HEADS].set(v.astype(F32))

    x2 = x.reshape(m, D_MODEL)
    for l in range(depth):
        w_l = jnp.where(valid[None, :], jnp.take(w_in[l], take, axis=1), 0.0).astype(MXU_DTYPE)
        qg = jnp.tile(nsa_q_norm[l], NSA_HEADS).reshape(1, NSA_WIDTH)
        kg = jnp.tile(nsa_k_norm[l, 1:3], (1, NSA_KV_HEADS))
        proj, qn, qr, ksa, kw, vs, vw, gt = _in_proj(x2, attn_norm[l].reshape(1, D_MODEL), w_l, rope_t, rope_ex,
                                                     rope_cp, qg, kg, bd512, bd128, seq)

        o_gdn = _gdn(proj, gdn_conv_w[l], lane_pad(gdn_a_log[l]), lane_pad(gdn_dt_bias[l]),
                     jnp.tile(gdn_norm[l], GDN_HEADS).reshape(1, GDN_WIDTH), bd256,
                     _head_expander(0), _head_expander(GDN_HEADS), batch, seq)

        w1bd, w2bd = _compress_weights(nsa_cmp_w1[l], nsa_cmp_w2[l])
        kvc = _compress(proj, w1bd, nsa_cmp_pe[l].reshape(2, CMP_LEN * HEAD), nsa_cmp_w1[l].astype(MXU_DTYPE),
                        w2bd, jnp.tile(nsa_k_norm[l, 0], NSA_KV_HEADS).reshape(1, NSA_KV_WIDTH), bd128,
                        batch, seq)

        bound = (HEAD ** 0.5) * jnp.max(jnp.abs(nsa_q_norm[l])) * jnp.max(jnp.abs(nsa_k_norm[l]), axis=1)
        cshift = jnp.zeros((1, LANES), F32).at[0, 0:3].set((bound * LOG2E).astype(F32))
        attn_args = (qn, qr, gt, kvc, ksa, vs, kw, vw, msel, cshift)
        o_nsa = lax.cond(
            jnp.max(bound) <= MAX_SOFTMAX_SHIFT,
            lambda a: _nsa_attn(*a, batch, seq, True),
            lambda a: _nsa_attn(*a, batch, seq, False),
            attn_args)

        wo = w_out[l].astype(MXU_DTYPE)
        wo_nsa = wo[GDN_WIDTH:GDN_WIDTH + NSA_WIDTH].reshape(NSA_KV_HEADS, NSA_GROUP, HEAD, D_MODEL)
        wo = jnp.concatenate([wo[:GDN_WIDTH], wo_nsa.transpose(1, 0, 2, 3).reshape(NSA_WIDTH, D_MODEL),
                              wo[GDN_WIDTH + NSA_WIDTH:]], axis=0)
        x2 = _out_ffn(x2, o_gdn, o_nsa, proj, conv_w[l], wo,
                      ffn_norm[l].reshape(1, D_MODEL), w_gate_up[l].astype(MXU_DTYPE),
                      w_down[l].astype(MXU_DTYPE), seq)
    return x2.reshape(batch, seq, D_MODEL)
```

```python
import functools

import numpy as np
import jax
import jax.numpy as jnp
from jax import lax
from jax.experimental import pallas as pl
from jax.experimental.pallas import tpu as pltpu

F32 = jnp.float32
MXU_DTYPE = jnp.bfloat16

D_MODEL = 1024
HEAD = 64
EPS = 1e-6
NEG_INF = -1e30
GDN_HEADS = 4
GDN_WIDTH = 256
GDN_CONV = 4
GDN_CHUNK = 64
NSA_HEADS = 8
NSA_WIDTH = 512
NSA_KV_HEADS = 2
NSA_GROUP = 4
NSA_KV_WIDTH = 128
CMP_STRIDE = 16
CMP_LEN = 32
SLC_BLOCK = 64
N_SELECT = 16
WINDOW = 512
CONV_WIDTH = 256
CONV_K = 3
ROPE_THETA = 500000.0
ROT_DIM = 16
D_FF = 2816
IN_SIZES = (256, 256, 256, 256, 4, 4, 512, 128, 128, 128, 128, 128, 128, 24, 256, 256, 256)
D_IN = sum(IN_SIZES)

LANES = 128
SUBLANES = 8
VMEM_LIMIT = 56 * 1024 * 1024

C_W = 3 * CONV_WIDTH
P_W = 2 * NSA_KV_WIDTH
GM_W = 4 * GDN_WIDTH
GG_W = LANES
N_W = 512 + 4 * 128 + LANES
C_OFF, P_OFF, GM_OFF, GG_OFF = 0, C_W, C_W + P_W, C_W + P_W + GM_W
PROJ_W = C_W + P_W + GM_W + GG_W
N_OFF = PROJ_W
W_COLS = PROJ_W + N_W
Q_SLOT_HEADS = (0, 4, 1, 5, 2, 6, 3, 7)
N_KS, N_KW, N_VS, N_VW, N_GATE = (NSA_WIDTH + k * LANES for k in range(5))

SEL_BIG = 16384.0
MAX_SOFTMAX_SHIFT = 40.0
LOG2E = 1.4426950408889634

TM_PROJ = 512
TM_FFN = 512
T_GDN = 512
TQ = 256
TQ_RUNMAX = 128
KC = 1024
CHUNKS_PER_TRIP = 4
FF_CHUNK = 256


def _dot(a, b):
    return jnp.dot(a.astype(MXU_DTYPE), b.astype(MXU_DTYPE), preferred_element_type=F32)


def _dot_nt(a, b):
    return lax.dot_general(a.astype(MXU_DTYPE), b.astype(MXU_DTYPE), (((1,), (1,)), ((), ())),
                           preferred_element_type=F32)


def _dot_tn(a, b):
    return lax.dot_general(a.astype(MXU_DTYPE), b.astype(MXU_DTYPE), (((0,), (0,)), ((), ())),
                           preferred_element_type=F32)


def _split3(x):
    a = x.astype(MXU_DTYPE)
    r = x - a.astype(F32)
    b = r.astype(MXU_DTYPE)
    c = (r - b.astype(F32)).astype(MXU_DTYPE)
    return a, b, c


def _dot_hi_l(x, m):
    a, b, c = _split3(x)
    f = lambda t: jnp.dot(t, m, preferred_element_type=F32)
    return f(a) + f(b) + f(c)


def _dot_hi2_l(x, m):
    a = x.astype(MXU_DTYPE)
    b = (x - a.astype(F32)).astype(MXU_DTYPE)
    return jnp.dot(a, m, preferred_element_type=F32) + jnp.dot(b, m, preferred_element_type=F32)


def _silu(x):
    return x * jax.nn.sigmoid(x)


def _shift_rows(x, halo, s, rows):
    y = pltpu.roll(x, s, 0)
    for r in range(s):
        y = jnp.where(rows == r, halo[SUBLANES - s + r:SUBLANES - s + r + 1, :], y)
    return y


def _in_proj_kernel(x_ref, g_ref, w_ref, t_ref, ex_ref, cp_ref, qg_ref, kg_ref, bdq_ref, bdk_ref,
                    o_ref, qn_ref, qr_ref, ksa_ref, kw_ref, vs_ref, vw_ref, gt_ref, *, tm, seq):
    x = x_ref[...]
    ms = jnp.mean(x * x, axis=-1, keepdims=True)
    h = (x * lax.rsqrt(ms + EPS) * g_ref[...]).astype(MXU_DTYPE)
    pending = [(c, min(256, PROJ_W - c)) for c in range(0, PROJ_W, 256)]

    def emit(count):
        for _ in range(min(count, len(pending))):
            c, cw = pending.pop(0)
            o_ref[:, c:c + cw] = jnp.dot(h, w_ref[:, c:c + cw], preferred_element_type=F32)

    nblk = jnp.dot(h, w_ref[:, N_OFF:N_OFF + N_W], preferred_element_type=F32)
    _nsa_operands(nblk, t_ref, ex_ref, cp_ref, qg_ref, kg_ref, bdq_ref, bdk_ref,
                  qn_ref, qr_ref, ksa_ref, kw_ref, vs_ref, vw_ref, gt_ref, tt=tm, seq=seq, between=emit)
    emit(len(pending))


def _in_proj(x2, gain, w, rope_t, rope_ex, rope_cp, qg, kg, bd512, bd128, seq):
    m = x2.shape[0]
    tm = TM_PROJ
    row = lambda wd: pl.BlockSpec((tm, wd), lambda i: (i, 0))
    full = lambda a, b: pl.BlockSpec((a, b), lambda i: (0, 0))
    sds = lambda wd, dt: jax.ShapeDtypeStruct((m, wd), dt)
    return pl.pallas_call(
        functools.partial(_in_proj_kernel, tm=tm, seq=seq),
        grid=(m // tm,),
        in_specs=[row(D_MODEL), full(1, D_MODEL), full(D_MODEL, W_COLS), row(LANES),
                  pl.BlockSpec((3, LANES, LANES), lambda i: (0, 0, 0)), full(1, LANES),
                  full(1, NSA_WIDTH), full(2, LANES), full(NSA_WIDTH, NSA_WIDTH), full(LANES, LANES)],
        out_specs=[row(PROJ_W), row(NSA_WIDTH), row(NSA_WIDTH), row(2 * LANES), row(LANES), row(LANES),
                   row(LANES), row(LANES)],
        out_shape=[sds(PROJ_W, F32), sds(NSA_WIDTH, MXU_DTYPE), sds(NSA_WIDTH, MXU_DTYPE),
                   sds(2 * LANES, MXU_DTYPE), sds(LANES, MXU_DTYPE), sds(LANES, MXU_DTYPE),
                   sds(LANES, MXU_DTYPE), sds(LANES, F32)],
        compiler_params=pltpu.CompilerParams(dimension_semantics=("parallel",),
                                             vmem_limit_bytes=VMEM_LIMIT),
        name="in_proj",
    )(x2, gain, w, rope_t, rope_ex, rope_cp, qg, kg, bd512, bd128)


def _gdn_kernel(g_ref, gh_ref, gg_ref, cw_ref, alog_ref, dt_ref, gn_ref, bd_ref, eb_ref, eg_ref, o_ref, s_ref,
                *, tt):
    t_idx = pl.program_id(1)

    @pl.when(t_idx == 0)
    def _():
        s_ref[...] = jnp.zeros_like(s_ref)

    keep = jnp.where(t_idx == 0, 0.0, 1.0)
    rows = lax.broadcasted_iota(jnp.int32, (tt, 1), 0)
    x = g_ref[:, 0:3 * GDN_WIDTH]
    hx = gh_ref[:, 0:3 * GDN_WIDTH] * keep
    w = cw_ref[...]
    y = w[GDN_CONV - 1:GDN_CONV, :] * x
    for s in range(1, GDN_CONV):
        y = y + w[GDN_CONV - 1 - s:GDN_CONV - s, :] * _shift_rows(x, hx, s, rows)
    y = _silu(y)
    q = y[:, 0:GDN_WIDTH]
    k = y[:, GDN_WIDTH:2 * GDN_WIDTH]
    v = y[:, 2 * GDN_WIDTH:3 * GDN_WIDTH]
    bd = bd_ref[...]
    q = q * lax.rsqrt(_dot_hi2_l(q * q, bd) + EPS) * (HEAD ** -0.5)
    k = k * lax.rsqrt(_dot_hi2_l(k * k, bd) + EPS)

    gg = gg_ref[...]
    lane = lax.broadcasted_iota(jnp.int32, (1, LANES), 1)
    xa = gg + dt_ref[...]
    softplus = jnp.maximum(xa, 0.0) + jnp.log1p(jnp.exp(-jnp.abs(xa)))
    g2 = jnp.where(lane < GDN_HEADS, jax.nn.sigmoid(gg), -jnp.exp(alog_ref[...]) * softplus)
    cs = g2.T
    lane_t = lax.broadcasted_iota(jnp.int32, (1, tt), 1) % GDN_CHUNK
    step = 1
    while step < GDN_CHUNK:
        cs = cs + jnp.where(lane_t >= step, pltpu.roll(cs, step, 1), 0.0)
        step *= 2
    gcum_t = cs
    gcum = cs.T
    beta_e = _dot_hi_l(g2, eb_ref[...])
    gcum_e = _dot_hi_l(gcum, eg_ref[...])
    eg_e = jnp.exp(gcum_e)
    kb = k * beta_e
    rv = v * beta_e
    rk = kb * eg_e
    qd = q * eg_e

    nst = GDN_HEADS * GDN_CHUNK
    ri = lax.broadcasted_iota(jnp.int32, (nst, nst), 0)
    ci = lax.broadcasted_iota(jnp.int32, (nst, nst), 1)
    same = (ri // GDN_CHUNK) == (ci // GDN_CHUNK)
    m_tril = same & ((ri % GDN_CHUNK) >= (ci % GDN_CHUNK))
    m_strict = same & ((ri % GDN_CHUNK) > (ci % GDN_CHUNK))
    tile4 = lambda t: jnp.concatenate([t] * GDN_HEADS, axis=0)
    expand = lambda t: tile4(t.astype(MXU_DTYPE)) * bd

    nchunk = tt // GDN_CHUNK
    rss = [slice(n * GDN_CHUNK, (n + 1) * GDN_CHUNK) for n in range(nchunk)]
    g_last = [gcum_e[(n + 1) * GDN_CHUNK - 1:(n + 1) * GDN_CHUNK, :] for n in range(nchunk)]
    rmat, pw, qk = [], [], []
    for n, rs in enumerate(rss):
        g_row = jnp.concatenate([gcum_t[GDN_HEADS + h:GDN_HEADS + h + 1, rs] for h in range(GDN_HEADS)], axis=1)
        decay = jnp.where(m_tril, jnp.exp(jnp.where(m_tril, tile4(gcum_e[rs]) - g_row, 0.0)), 0.0)
        k4 = tile4(k[rs])
        a = jnp.where(m_strict, _dot_nt(expand(kb[rs]), k4) * decay, 0.0)
        qk.append(jnp.where(m_tril, _dot_nt(expand(q[rs]), k4) * decay, 0.0))
        rmat.append(-a)
        pw.append(-a)
    for _ in range(GDN_CHUNK.bit_length() - 2):
        nxt_pw, nxt_r = [], []
        for r, t in zip(rmat, pw):
            t2 = _dot(t, t)
            nxt_pw.append(t2)
            nxt_r.append(r + t2 + _dot(r, t2))
        pw, rmat = nxt_pw, nxt_r
    u, wm = [], []
    for n, rs in enumerate(rss):
        rv_x = jnp.where(same, tile4(rv[rs]), 0.0)
        rk_x = jnp.where(same, tile4(rk[rs]), 0.0)
        u.append(rv_x + _dot(rmat[n], rv_x))
        wm.append(rk_x + _dot(rmat[n], rk_x))

    st = s_ref[...]
    outs = []
    for n, rs in enumerate(rss):
        kd = k[rs] * jnp.exp(g_last[n] - gcum_e[rs])
        v_new = u[n] - _dot(wm[n], st)
        o_x = _dot(expand(qd[rs]), st) + _dot(qk[n], v_new)
        st = st * jnp.exp(g_last[n]) + _dot_tn(expand(kd), v_new)
        o = o_x[0:GDN_CHUNK]
        for h in range(1, GDN_HEADS):
            o = o + o_x[h * GDN_CHUNK:(h + 1) * GDN_CHUNK]
        outs.append(o)
    s_ref[...] = st
    o_all = jnp.concatenate(outs, axis=0)
    on = o_all * lax.rsqrt(_dot_hi2_l(o_all * o_all, bd) * (1.0 / HEAD) + EPS) * gn_ref[...]
    o_ref[...] = (on * _silu(g_ref[:, 3 * GDN_WIDTH:4 * GDN_WIDTH])).astype(o_ref.dtype)


def _gdn(proj, conv_w, alog_pad, dt_pad, gnorm, bd256, eb, eg, batch, seq):
    tt = T_GDN
    nt = seq // tt
    hb = tt // SUBLANES
    return pl.pallas_call(
        functools.partial(_gdn_kernel, tt=tt),
        grid=(batch, nt),
        in_specs=[pl.BlockSpec((tt, GM_W), lambda b, t: (b * nt + t, GM_OFF // GM_W)),
                  pl.BlockSpec((SUBLANES, GM_W),
                               lambda b, t: (jnp.maximum((b * nt + t) * hb - 1, 0), GM_OFF // GM_W)),
                  pl.BlockSpec((tt, GG_W), lambda b, t: (b * nt + t, GG_OFF // GG_W)),
                  pl.BlockSpec((GDN_CONV, 3 * GDN_WIDTH), lambda b, t: (0, 0)),
                  pl.BlockSpec((1, LANES), lambda b, t: (0, 0)),
                  pl.BlockSpec((1, LANES), lambda b, t: (0, 0)),
                  pl.BlockSpec((1, GDN_WIDTH), lambda b, t: (0, 0)),
                  pl.BlockSpec((GDN_WIDTH, GDN_WIDTH), lambda b, t: (0, 0)),
                  pl.BlockSpec((LANES, GDN_WIDTH), lambda b, t: (0, 0)),
                  pl.BlockSpec((LANES, GDN_WIDTH), lambda b, t: (0, 0))],
        out_specs=pl.BlockSpec((tt, GDN_WIDTH), lambda b, t: (b * nt + t, 0)),
        out_shape=jax.ShapeDtypeStruct((batch * seq, GDN_WIDTH), MXU_DTYPE),
        scratch_shapes=[pltpu.VMEM((GDN_WIDTH, GDN_WIDTH), F32)],
        compiler_params=pltpu.CompilerParams(dimension_semantics=("parallel", "arbitrary"),
                                             vmem_limit_bytes=VMEM_LIMIT),
        name="gdn",
    )(proj, proj, proj, conv_w, alog_pad, dt_pad, gnorm, bd256, eb, eg)


def _rope(x, c, s1, s2):
    wdt = x.shape[1]
    return x * c + pltpu.roll(x, wdt - ROT_DIM // 2, 1) * s1 + pltpu.roll(x, ROT_DIM // 2, 1) * s2


def _nsa_operands(n_blk, t_ref, ex_ref, cp_ref, qg_ref, kg_ref, bdq_ref, bdk_ref,
                  qn_ref, qr_ref, ksa_ref, kw_ref, vs_ref, vw_ref, gt_ref, *, tt, seq, between):
    between(3)
    table = t_ref[...]
    c1 = _dot_hi2_l(table, ex_ref[0]) + cp_ref[...]
    s1 = _dot_hi2_l(table, ex_ref[1])
    s2 = _dot_hi2_l(table, ex_ref[2])
    c4 = jnp.concatenate([c1] * 4, axis=1)
    s14 = jnp.concatenate([s1] * 4, axis=1)
    s24 = jnp.concatenate([s2] * 4, axis=1)
    q = n_blk[:, 0:NSA_WIDTH]
    qsq = q * q
    half_w = NSA_WIDTH // 2
    bdh = bdq_ref[0:half_w, 0:half_w]
    q_ss = jnp.concatenate([_dot(qsq[:, 0:half_w], bdh), _dot(qsq[:, half_w:NSA_WIDTH], bdh)], axis=1)
    qn = q * lax.rsqrt(q_ss * (1.0 / HEAD) + EPS) * qg_ref[...]
    qn = qn * (HEAD ** -0.5 * LOG2E)
    qn_ref[...] = qn.astype(qn_ref.dtype)
    qr_ref[...] = _rope(qn, c4, s14, s24).astype(qr_ref.dtype)
    between(3)
    bdk = bdk_ref[...]
    ks = n_blk[:, N_KS:N_KS + LANES]
    kw = n_blk[:, N_KW:N_KW + LANES]
    ks = ks * lax.rsqrt(_dot(ks * ks, bdk) * (1.0 / HEAD) + EPS) * kg_ref[0:1, :]
    kw = kw * lax.rsqrt(_dot(kw * kw, bdk) * (1.0 / HEAD) + EPS) * kg_ref[1:2, :]
    ksa_ref[:, 0:LANES] = _rope(ks, c1, s1, s2).astype(ksa_ref.dtype)
    rows = (lax.broadcasted_iota(jnp.int32, (tt, LANES), 0) + pl.program_id(0) * tt) % seq
    lane = lax.broadcasted_iota(jnp.int32, (tt, LANES), 1)
    ksa_ref[:, LANES:2 * LANES] = jnp.where((rows // SLC_BLOCK) % LANES == lane, 1.0, 0.0).astype(ksa_ref.dtype)
    kw_ref[...] = _rope(kw, c1, s1, s2).astype(kw_ref.dtype)
    vs_ref[...] = n_blk[:, N_VS:N_VS + LANES].astype(vs_ref.dtype)
    vw_ref[...] = n_blk[:, N_VW:N_VW + LANES].astype(vw_ref.dtype)
    gt_ref[...] = jax.nn.sigmoid(n_blk[:, N_GATE:N_GATE + LANES])


def _compress_kernel(xk_ref, xv_ref, w1bd_ref, pe_ref, w1_ref, w2bd_ref, kg_ref, bd_ref, o_ref, *, nb):
    acc = None
    for t in range(CMP_STRIDE):
        xt = jnp.concatenate([xk_ref[pl.ds(t, nb, stride=CMP_STRIDE), :],
                              xv_ref[pl.ds(t, nb, stride=CMP_STRIDE), :]], axis=1)
        part = _dot(xt, w1bd_ref[t])
        acc = part if acc is None else acc + part
    pe_terms = [_dot(jnp.broadcast_to(pe_ref[c:c + 1, :], (SUBLANES, CMP_LEN * HEAD)), w1_ref[c])[0:1, :]
                for c in range(2)]
    pe_all = jnp.concatenate([pe_terms[0]] * NSA_KV_HEADS + [pe_terms[1]] * NSA_KV_HEADS, axis=1)
    half = 2 * NSA_KV_WIDTH
    pre = acc[:, 0:half] + pltpu.roll(acc[:, half:2 * half], nb - 1, 0) + pe_all
    y = _dot(jax.nn.gelu(pre), w2bd_ref[...])
    rows = lax.broadcasted_iota(jnp.int32, (nb, 1), 0)
    y = jnp.where(rows < nb - 1, y, 0.0)
    yk = y[:, 0:NSA_KV_WIDTH]
    yk = yk * lax.rsqrt(_dot_hi2_l(yk * yk, bd_ref[...]) * (1.0 / HEAD) + EPS) * kg_ref[...]
    o_ref[0, 0] = yk.astype(o_ref.dtype)
    o_ref[1, 0] = y[:, NSA_KV_WIDTH:2 * NSA_KV_WIDTH].astype(o_ref.dtype)


def _compress(proj, w1bd, pe, w1, w2bd, kg0, bd128, batch, seq):
    nb = seq // CMP_STRIDE
    full = lambda *s: pl.BlockSpec(s, lambda b: (0,) * len(s))
    return pl.pallas_call(
        functools.partial(_compress_kernel, nb=nb),
        grid=(batch,),
        in_specs=[pl.BlockSpec((seq, NSA_KV_WIDTH), lambda b: (b, P_OFF // NSA_KV_WIDTH)),
                  pl.BlockSpec((seq, NSA_KV_WIDTH), lambda b: (b, P_OFF // NSA_KV_WIDTH + 1)),
                  full(CMP_STRIDE, P_W, 2 * P_W), full(2, CMP_LEN * HEAD), full(2, CMP_LEN * HEAD, HEAD),
                  full(P_W, P_W), full(1, NSA_KV_WIDTH), full(LANES, LANES)],
        out_specs=pl.BlockSpec((2, 1, nb, NSA_KV_WIDTH), lambda b: (0, b, 0, 0)),
        out_shape=jax.ShapeDtypeStruct((2, batch, nb, NSA_KV_WIDTH), MXU_DTYPE),
        compiler_params=pltpu.CompilerParams(dimension_semantics=("parallel",),
                                             vmem_limit_bytes=VMEM_LIMIT),
        name="compress",
    )(proj, proj, w1bd, pe, w1, w2bd, kg0, bd128)


def _nsa_attn_kernel(*refs, tq, seq, nwb, kc, ncp, nslp, shifted):
    qn_ref, qr_ref, gt_ref, kcmp_ref, vcmp_ref, ksa_ref, vs_ref = refs[:7]
    kw_refs = refs[7:7 + nwb]
    vw_refs = refs[7 + nwb:7 + 2 * nwb]
    msel_ref, cs_ref, egate_ref = refs[7 + 2 * nwb:10 + 2 * nwb]
    o_ref = refs[10 + 2 * nwb]
    bias_scr, qaug_scr, m_scr, l_scr, acc_scr, ow_scr, cand_scr, oc_scr, slc_scr, e_scr = refs[11 + 2 * nwb:]
    nslot = NSA_HEADS
    i = pl.program_id(1)
    c_cmp, c_slc, c_win = cs_ref[0:1, 0:1], cs_ref[0:1, 1:2], cs_ref[0:1, 2:3]
    lane = lax.broadcasted_iota(jnp.int32, (1, LANES), 1)
    lo = lane < HEAD

    def stack(q_ref):
        sl = [q_ref[:, j * LANES:(j + 1) * LANES] for j in range(NSA_GROUP)]
        zero = jnp.zeros_like(sl[0])
        return jnp.concatenate([jnp.where(lo, s, zero) for s in sl] + [jnp.where(lo, zero, s) for s in sl], axis=0)

    tpos = i * tq + lax.broadcasted_iota(jnp.int32, (tq, 1), 0)
    tpos_st = i * tq + lax.broadcasted_iota(jnp.int32, (nslot * tq, 1), 0) % tq
    gates = gt_ref[...]

    qn_st = stack(qn_ref)
    if shifted:
        ccw = min(2 * LANES, ncp)
        n_vis = (i + 1) * (tq // CMP_STRIDE) - 1
        n_cch = (n_vis + ccw - 1) // ccw
        qaug_scr[:, 0:LANES] = qn_st
        l_scr[...] = jnp.zeros(l_scr.shape, F32)
        acc_scr[...] = jnp.zeros(acc_scr.shape, F32)

        def cmp_scores(ch):
            k0 = ch * ccw
            s = _dot_nt(qaug_scr[:, 0:LANES], kcmp_ref[0, 0, pl.ds(k0, ccw), :])
            ckey = k0 + lax.broadcasted_iota(jnp.int32, (1, ccw), 1)
            vis = (ckey * CMP_STRIDE + (CMP_LEN - 1)) <= tpos
            cbias = jnp.where(vis, -c_cmp, -SEL_BIG)
            for slot in range(nslot):
                rs = slice(slot * tq, (slot + 1) * tq)
                e = jnp.exp2(s[rs] + cbias)
                e_scr[rs, pl.ds(k0, ccw)] = e
                part = e[:, 0:LANES]
                for t in range(1, ccw // LANES):
                    part = part + e[:, t * LANES:(t + 1) * LANES]
                l_scr[rs, :] += part
            acc_scr[...] += _dot(e_scr[:, pl.ds(k0, ccw)], vcmp_ref[0, 0, pl.ds(k0, ccw), :])

        def for_visible_chunks(fn):
            def variant(count):
                for ch in range(count):
                    fn(ch)

            for count in range(1, ncp // ccw + 1):
                pl.when(n_cch == count)(functools.partial(variant, count))

        for_visible_chunks(cmp_scores)
        l_c = jnp.sum(l_scr[...], axis=-1, keepdims=True)
        rinv_c = 1.0 / jnp.where(l_c > 0.0, l_c, 1.0)
        oc_scr[...] = acc_scr[...] * rinv_c
        l_scr[...] = jnp.broadcast_to(rinv_c, l_scr.shape)
        slc_scr[...] = jnp.zeros(slc_scr.shape, F32)

        def cmp_importance(ch):
            k0 = ch * ccw
            for g in range(NSA_KV_HEADS):
                ps = None
                for r in range(NSA_GROUP):
                    rs = slice((g * NSA_GROUP + r) * tq, (g * NSA_GROUP + r + 1) * tq)
                    rinv_l = jnp.concatenate([l_scr[rs, :]] * (ccw // LANES), axis=1)
                    p = e_scr[rs, pl.ds(k0, ccw)] * rinv_l
                    ps = p if ps is None else ps + p
                slc_scr[g] += _dot_hi_l(ps, msel_ref[pl.ds(k0, ccw), :])

        for_visible_chunks(cmp_importance)
        slc_rows = [slc_scr[g] for g in range(NSA_KV_HEADS)]
    else:
        ckey = lax.broadcasted_iota(jnp.int32, (1, ncp), 1)
        cmask = (ckey * CMP_STRIDE + (CMP_LEN - 1)) <= tpos
        s_all = _dot_nt(qn_st, kcmp_ref[0, 0])
        psum = [None, None]
        e_parts, rinv_parts = [], []
        for slot in range(nslot):
            s = jnp.where(cmask, s_all[slot * tq:(slot + 1) * tq], NEG_INF)
            e = jnp.where(cmask, jnp.exp2(s - jnp.max(s, axis=-1, keepdims=True)), 0.0)
            l = jnp.sum(e, axis=-1, keepdims=True)
            rinv = 1.0 / jnp.where(l > 0.0, l, 1.0)
            e_parts.append(e.astype(MXU_DTYPE))
            rinv_parts.append(rinv)
            p = e * rinv
            g = slot // NSA_GROUP
            psum[g] = p if psum[g] is None else psum[g] + p
        oc_scr[...] = (_dot(jnp.concatenate(e_parts, axis=0), vcmp_ref[0, 0])
                       * jnp.concatenate(rinv_parts, axis=0))
        slc_rows = [_dot_hi_l(psum[g], msel_ref[...]) for g in range(NSA_KV_HEADS)]

    qr_st = stack(qr_ref)
    kwc = jnp.concatenate([r[...] for r in kw_refs], axis=0)
    vwc = jnp.concatenate([r[...] for r in vw_refs], axis=0)
    sw = _dot_nt(qr_st, kwc)
    if shifted:
        rr = lax.broadcasted_iota(jnp.int32, (tq, tq), 0)
        cc = lax.broadcasted_iota(jnp.int32, (tq, tq), 1)
        parts = []
        for jb in range(nwb):
            shift_b = c_win + jnp.where(i - (nwb - 1) + jb >= 0, 0.0, SEL_BIG)
            blk = jnp.exp2(sw[:, jb * tq:(jb + 1) * tq] - shift_b)
            if jb == 0 or jb == nwb - 1:
                vis = (cc > rr) if jb == 0 else (cc <= rr)
                blk = jnp.concatenate([jnp.where(vis, blk[s * tq:(s + 1) * tq], 0.0) for s in range(nslot)],
                                      axis=0)
            parts.append(blk)
        ew = jnp.concatenate(parts, axis=1)
    else:
        kpos_w = (i - (nwb - 1)) * tq + lax.broadcasted_iota(jnp.int32, (1, nwb * tq), 1)
        dist = tpos_st - kpos_w
        wmask = (dist >= 0) & (dist < WINDOW) & (kpos_w >= 0)
        sw = jnp.where(wmask, sw, NEG_INF)
        ew = jnp.exp2(sw - jnp.max(sw, axis=-1, keepdims=True))
    ow_scr[...] = _dot(ew, vwc) * (1.0 / jnp.sum(ew, axis=-1, keepdims=True))

    jrow = lax.broadcasted_iota(jnp.int32, (nslp, 1), 0)
    jrowf = jrow.astype(F32)
    cur_t = (i * tq + lax.broadcasted_iota(jnp.int32, (1, tq), 1)) // SLC_BLOCK
    forced = (jrow == 0) | (jrow == cur_t) | (jrow == cur_t - 1)
    causal = jrow <= cur_t
    shift_s = c_slc if shifted else 0.0
    n_free = N_SELECT - 3
    to_bias = lambda sel: ((sel - 1.0) * SEL_BIG - shift_s).T.astype(bias_scr.dtype)
    n_bad = None
    for g in range(NSA_KV_HEADS):
        slc = slc_rows[g].T
        cand = jnp.where(causal & jnp.logical_not(forced), slc, -1.0)
        cand_scr[g] = cand
        c = cand
        for _ in range(n_free):
            c = jnp.where(c == jnp.max(c, axis=0, keepdims=True), -2.0, c)
        picked = (c == -2.0) & (cand >= 0.0)
        n_picked = jnp.sum(jnp.where(picked, 1.0, 0.0), axis=0, keepdims=True)
        n_real = jnp.sum(jnp.where(cand >= 0.0, 1.0, 0.0), axis=0, keepdims=True)
        bad = jnp.where(n_picked == jnp.minimum(n_real, float(n_free)), 0.0, 1.0)
        n_bad = bad if n_bad is None else n_bad + bad
        bias_scr[g] = to_bias(jnp.where(forced | picked, 1.0, 0.0))

    @pl.when(jnp.max(n_bad) > 0.0)
    def _():
        for g in range(NSA_KV_HEADS):
            cand = cand_scr[g]
            sel = jnp.where(forced, 1.0, 0.0)
            for _ in range(n_free):
                mx = jnp.max(cand, axis=0, keepdims=True)
                first = jnp.min(jnp.where(cand == mx, jrowf, float(nslp)), axis=0, keepdims=True)
                hit = jrowf == first
                sel = jnp.where(hit, 1.0, sel)
                cand = jnp.where(hit, -2.0, cand)
            bias_scr[g] = to_bias(jnp.where(causal, sel, 0.0))

    qaug_scr[:, 0:LANES] = qr_st
    if not shifted:
        m_scr[...] = jnp.full(m_scr.shape, NEG_INF, F32)
    l_scr[...] = jnp.zeros(l_scr.shape, F32)
    acc_scr[...] = jnp.zeros(acc_scr.shape, F32)
    group_keys = LANES * SLC_BLOCK

    def load_bias(key0):
        @pl.when(key0 % group_keys == 0)
        def _():
            off = pl.multiple_of((key0 // group_keys) * LANES, LANES)
            b0 = bias_scr[0, :, pl.ds(off, LANES)]
            b1 = bias_scr[1, :, pl.ds(off, LANES)]
            qaug_scr[:, LANES:2 * LANES] = jnp.concatenate([b0] * NSA_GROUP + [b1] * NSA_GROUP, axis=0)

    def qk(key0, width):
        return _dot_nt(qaug_scr[...], ksa_ref[pl.ds(pl.multiple_of(key0, width), width), :])

    def scores(key0, width):
        load_bias(key0)
        return qk(key0, width)

    def accumulate(s, key0, width):
        k0 = pl.multiple_of(key0, width)
        if shifted:
            p = jnp.exp2(s)
            psum_l = p[:, 0:LANES]
            for t in range(1, width // LANES):
                psum_l = psum_l + p[:, t * LANES:(t + 1) * LANES]
            l_scr[...] += psum_l
            acc_scr[...] += _dot(p, vs_ref[pl.ds(k0, width), :])
        else:
            m_old = m_scr[...]
            m_new = jnp.maximum(m_old, jnp.max(s, axis=-1, keepdims=True))
            alpha = jnp.exp2(m_old - m_new)
            p = jnp.exp2(s - m_new[:, 0:1])
            l_scr[...] = alpha * l_scr[...] + jnp.sum(p, axis=-1, keepdims=True)
            acc_scr[...] = alpha * acc_scr[...] + _dot(p, vs_ref[pl.ds(k0, width), :])
            m_scr[...] = m_new

    per_trip = CHUNKS_PER_TRIP if (shifted and group_keys % (CHUNKS_PER_TRIP * kc) == 0) else 1
    pair = per_trip * kc
    n_pair = (i * tq) // pair if per_trip > 1 else 0
    n_full = (i * tq - n_pair * pair) // kc
    before_sub = n_pair * pair + n_full * kc
    n_sub = (i * tq - before_sub) // tq

    def body_pair(c, carry):
        key0 = c * pair
        load_bias(key0)
        for h in range(per_trip):
            accumulate(qk(key0 + h * kc, kc), key0 + h * kc, kc)
        return carry

    def body_full(c, carry):
        key0 = n_pair * pair + c * kc
        accumulate(scores(key0, kc), key0, kc)
        return carry

    lax.fori_loop(0, n_pair, body_pair, 0)
    lax.fori_loop(0, n_full, body_full, 0)

    def own_chunk():
        col = lax.broadcasted_iota(jnp.int32, (1, tq), 1)
        accumulate(jnp.where(col <= tpos_st - i * tq, qk(i * tq, tq), -SEL_BIG), i * tq, tq)

    load_bias(before_sub)
    if shifted:
        def tail(count):
            for j in range(count):
                accumulate(qk(before_sub + j * tq, tq), before_sub + j * tq, tq)
            own_chunk()

        for count in range(kc // tq):
            pl.when(n_sub == count)(functools.partial(tail, count))
    else:
        def body_sub(j, carry):
            key0 = before_sub + j * tq
            accumulate(qk(key0, tq), key0, tq)
            return carry

        lax.fori_loop(0, n_sub, body_sub, 0)
        own_chunk()
    if shifted:
        o_s = acc_scr[...] * (1.0 / jnp.sum(l_scr[...], axis=-1, keepdims=True))
    else:
        o_s = acc_scr[...] / l_scr[...]
    o_w = ow_scr[...]

    gate_e = _dot_hi2_l(gates, egate_ref[...])
    for j in range(NSA_GROUP):
        r0 = slice(j * tq, (j + 1) * tq)
        r1 = slice((NSA_GROUP + j) * tq, (NSA_GROUP + j + 1) * tq)
        out = None
        for br, o_b in enumerate((oc_scr[...], o_s, o_w)):
            lanes = slice(br * NSA_WIDTH + j * LANES, br * NSA_WIDTH + (j + 1) * LANES)
            term = gate_e[:, lanes] * jnp.where(lo, o_b[r0], o_b[r1])
            out = term if out is None else out + term
        o_ref[:, j * LANES:(j + 1) * LANES] = out.astype(o_ref.dtype)


def _gate_expander():
    mat = np.zeros((LANES, 3 * NSA_WIDTH), np.float32)
    for br in range(3):
        for j in range(NSA_GROUP):
            for g in range(NSA_KV_HEADS):
                head = g * NSA_GROUP + j
                c0 = br * NSA_WIDTH + j * LANES + g * HEAD
                mat[3 * head + br, c0:c0 + HEAD] = 1.0
    return jnp.asarray(mat, MXU_DTYPE)


def _nsa_attn(qn, qr, gt, kvc, ksa, vs, kw, vw, msel, cshift, batch, seq, shifted):
    tq, kc = (TQ if shifted else TQ_RUNMAX), KC
    nq = seq // tq
    nwb = WINDOW // tq + 1
    ncp = seq // CMP_STRIDE
    nslp = msel.shape[1]
    row = lambda w: pl.BlockSpec((tq, w), lambda b, i: (b * nq + i, 0))
    once = pl.Buffered(1)
    win = lambda jb: pl.BlockSpec((tq, LANES), lambda b, i: (b * nq + jnp.maximum(i - (nwb - 1) + jb, 0), 0))
    in_specs = ([row(NSA_WIDTH), row(NSA_WIDTH), row(LANES),
                 pl.BlockSpec((1, 1, ncp, LANES), lambda b, i: (0, b, 0, 0), pipeline_mode=once),
                 pl.BlockSpec((1, 1, ncp, LANES), lambda b, i: (1, b, 0, 0), pipeline_mode=once),
                 pl.BlockSpec((seq, 2 * LANES), lambda b, i: (b, 0), pipeline_mode=once),
                 pl.BlockSpec((seq, LANES), lambda b, i: (b, 0), pipeline_mode=once)]
                + [win(jb) for jb in range(nwb)] + [win(jb) for jb in range(nwb)]
                + [pl.BlockSpec((ncp, nslp), lambda b, i: (0, 0), pipeline_mode=once),
                   pl.BlockSpec((1, LANES), lambda b, i: (0, 0)),
                   pl.BlockSpec((LANES, 3 * NSA_WIDTH), lambda b, i: (0, 0), pipeline_mode=once)])
    return pl.pallas_call(
        functools.partial(_nsa_attn_kernel, tq=tq, seq=seq, nwb=nwb, kc=kc, ncp=ncp, nslp=nslp,
                          shifted=shifted),
        grid=(batch, nq),
        in_specs=in_specs,
        out_specs=row(NSA_WIDTH),
        out_shape=jax.ShapeDtypeStruct((batch * seq, NSA_WIDTH), MXU_DTYPE),
        scratch_shapes=[pltpu.VMEM((NSA_KV_HEADS, tq, nslp), MXU_DTYPE),
                        pltpu.VMEM((NSA_HEADS * tq, 2 * LANES), MXU_DTYPE),
                        pltpu.VMEM((NSA_HEADS * tq, LANES), F32),
                        pltpu.VMEM((NSA_HEADS * tq, LANES), F32),
                        pltpu.VMEM((NSA_HEADS * tq, LANES), F32),
                        pltpu.VMEM((NSA_HEADS * tq, LANES), F32),
                        pltpu.VMEM((NSA_KV_HEADS, nslp, tq), F32),
                        pltpu.VMEM((NSA_HEADS * tq, LANES), F32),
                        pltpu.VMEM((NSA_KV_HEADS, tq, nslp), F32),
                        pltpu.VMEM((NSA_HEADS * tq, ncp) if shifted else (SUBLANES, LANES), F32)],
        compiler_params=pltpu.CompilerParams(dimension_semantics=("parallel", "arbitrary"),
                                             vmem_limit_bytes=VMEM_LIMIT),
        name="nsa_attn" if shifted else "nsa_attn_runmax",
    )(qn, qr, gt, kvc, kvc, ksa, vs, *([kw] * nwb), *([vw] * nwb), msel, cshift, _gate_expander())


def _out_ffn_kernel(x_ref, og_ref, on_ref, c_ref, ch_ref, cw_ref, wo_ref, fg_ref, wgu_ref, wd_ref, o_ref,
                    *, tm, seq):
    i = pl.program_id(0)
    keep = jnp.where((i * tm) % seq == 0, 0.0, 1.0)
    rows = lax.broadcasted_iota(jnp.int32, (tm, 1), 0)
    u = c_ref[:, CONV_WIDTH:2 * CONV_WIDTH] * c_ref[:, 2 * CONV_WIDTH:3 * CONV_WIDTH]
    hu = ch_ref[:, CONV_WIDTH:2 * CONV_WIDTH] * ch_ref[:, 2 * CONV_WIDTH:3 * CONV_WIDTH] * keep
    w = cw_ref[...]
    conv = w[CONV_K - 1:CONV_K, :] * u
    for s in range(1, CONV_K):
        conv = conv + w[CONV_K - 1 - s:CONV_K - s, :] * _shift_rows(u, hu, s, rows)
    oc = c_ref[:, 0:CONV_WIDTH] * conv
    x1 = (x_ref[...] + _dot(og_ref[...], wo_ref[0:GDN_WIDTH, :])
          + _dot(on_ref[...], wo_ref[GDN_WIDTH:GDN_WIDTH + NSA_WIDTH, :])
          + _dot(oc, wo_ref[GDN_WIDTH + NSA_WIDTH:, :]))
    ms = jnp.mean(x1 * x1, axis=-1, keepdims=True)
    h2 = (x1 * lax.rsqrt(ms + EPS) * fg_ref[...]).astype(MXU_DTYPE)
    o_ref[...] = x1
    for c0 in range(0, D_FF, FF_CHUNK):
        gate = jnp.dot(h2, wgu_ref[:, c0:c0 + FF_CHUNK], preferred_element_type=F32)
        up = jnp.dot(h2, wgu_ref[:, D_FF + c0:D_FF + c0 + FF_CHUNK], preferred_element_type=F32)
        o_ref[...] += _dot(_silu(gate) * up, wd_ref[c0:c0 + FF_CHUNK, :])


def _out_ffn(x2, o_gdn, o_nsa, proj, conv_w, w_out, fgain, wgu, wd, seq):
    m = x2.shape[0]
    tm = TM_FFN
    hb = tm // SUBLANES
    full = lambda a, b: pl.BlockSpec((a, b), lambda i: (0, 0), pipeline_mode=pl.Buffered(1))
    return pl.pallas_call(
        functools.partial(_out_ffn_kernel, tm=tm, seq=seq),
        grid=(m // tm,),
        in_specs=[pl.BlockSpec((tm, D_MODEL), lambda i: (i, 0)),
                  pl.BlockSpec((tm, GDN_WIDTH), lambda i: (i, 0)),
                  pl.BlockSpec((tm, NSA_WIDTH), lambda i: (i, 0)),
                  pl.BlockSpec((tm, C_W), lambda i: (i, C_OFF // C_W)),
                  pl.BlockSpec((SUBLANES, C_W), lambda i: (jnp.maximum(i * hb - 1, 0), C_OFF // C_W)),
                  full(CONV_K, CONV_WIDTH), full(D_MODEL, D_MODEL), full(1, D_MODEL),
                  full(D_MODEL, 2 * D_FF), full(D_FF, D_MODEL)],
        out_specs=pl.BlockSpec((tm, D_MODEL), lambda i: (i, 0)),
        out_shape=jax.ShapeDtypeStruct((m, D_MODEL), F32),
        compiler_params=pltpu.CompilerParams(dimension_semantics=("parallel",),
                                             vmem_limit_bytes=VMEM_LIMIT),
        name="out_ffn",
    )(x2, o_gdn, o_nsa, proj, proj, conv_w, w_out, fgain, wgu, wd)


def _proj_column_map():
    offs = np.concatenate([[0], np.cumsum(IN_SIZES)])
    seg = lambda k: np.arange(offs[k], offs[k + 1])
    pad = lambda n: -np.ones(n, np.int64)
    nq = seg(6).reshape(NSA_HEADS, HEAD)[list(Q_SLOT_HEADS)].reshape(-1)
    cols = np.concatenate([
        seg(14), seg(15), seg(16),
        seg(7), seg(8),
        seg(0), seg(1), seg(2), seg(3),
        seg(4), seg(5), pad(GG_W - 2 * GDN_HEADS),
        nq, seg(9), seg(11), seg(10), seg(12), seg(13), pad(LANES - 3 * NSA_HEADS)])
    assert cols.shape[0] == W_COLS
    return cols


def _block_diag_ones(n):
    idx = np.arange(n) // HEAD
    return jnp.asarray(idx[:, None] == idx[None, :], MXU_DTYPE)


def _head_expander(first_lane):
    mat = np.zeros((LANES, GDN_WIDTH), np.float32)
    for h in range(GDN_HEADS):
        mat[first_lane + h, h * HEAD:(h + 1) * HEAD] = 1.0
    return jnp.asarray(mat, MXU_DTYPE)


def _compress_weights(w1, w2):
    nslab = 2 * NSA_KV_HEADS
    slab_kv = np.arange(nslab) // NSA_KV_HEADS
    eye = jnp.eye(nslab, dtype=MXU_DTYPE)
    w1r = w1.astype(MXU_DTYPE).reshape(2, 2, CMP_STRIDE, HEAD, HEAD)[slab_kv]
    w1bd = jnp.einsum("shtde,sS->tsdhSe", w1r, eye)
    w2bd = jnp.einsum("sde,sS->sdSe", w2.astype(MXU_DTYPE)[slab_kv], eye)
    return w1bd.reshape(CMP_STRIDE, P_W, 2 * P_W), w2bd.reshape(P_W, P_W)


def _rope_table(positions):
    half = ROT_DIM // 2
    inv = jnp.float32(ROPE_THETA) ** (-jnp.arange(0, ROT_DIM, 2, dtype=jnp.float32) / ROT_DIM)
    inv_l = jnp.concatenate([inv, inv, jnp.zeros((LANES - ROT_DIM,), F32)])
    ang = positions.astype(jnp.float32).reshape(-1)[:, None] * inv_l[None, :]
    lane = jnp.arange(LANES)[None, :]
    return jnp.where(lane < half, jnp.cos(ang), jnp.where(lane < ROT_DIM, jnp.sin(ang), 0.0))


def _rope_expanders():
    half = ROT_DIM // 2
    ex = np.zeros((3, LANES, LANES), np.float32)
    cpat = np.zeros((1, LANES), np.float32)
    for j in range(LANES):
        d = j % HEAD
        if d < ROT_DIM:
            ex[0, d % half, j] = 1.0
        else:
            cpat[0, j] = 1.0
        if d < half:
            ex[1, half + d, j] = -1.0
        elif d < ROT_DIM:
            ex[2, half + (d - half), j] = 1.0
    return jnp.asarray(ex, MXU_DTYPE), jnp.asarray(cpat, F32)


def _selection_matrix(ncp, nslp):
    ratio = SLC_BLOCK // CMP_STRIDE
    frac = np.minimum(CMP_LEN, SLC_BLOCK - CMP_STRIDE * np.arange(ratio)).astype(np.float64) / CMP_LEN
    mat = np.zeros((ncp, nslp), np.float32)
    c = np.arange(ncp)
    mat[c, c // ratio] = frac[c % ratio]
    nxt = c // ratio + 1
    ok = nxt < nslp
    mat[c[ok], nxt[ok]] += (1.0 - frac[c % ratio])[ok]
    return jnp.asarray(mat, MXU_DTYPE)


def kernel(x, positions, attn_norm, w_in, gdn_conv_w, gdn_a_log, gdn_dt_bias, gdn_norm, nsa_q_norm,
           nsa_k_norm, nsa_cmp_pe, nsa_cmp_w1, nsa_cmp_w2, conv_w, w_out, ffn_norm, w_gate_up, w_down):
    batch, seq, _ = x.shape
    depth = w_in.shape[0]
    m = batch * seq
    assert seq % max(TM_PROJ, TM_FFN, T_GDN, KC) == 0 and (seq // CMP_STRIDE) % LANES == 0
    nb = seq // CMP_STRIDE
    nslp = -(-(seq // SLC_BLOCK) // LANES) * LANES

    cols = _proj_column_map()
    take = jnp.asarray(np.maximum(cols, 0), jnp.int32)
    valid = jnp.asarray(cols >= 0)
    assert Q_SLOT_HEADS == tuple(g * NSA_GROUP + j for j in range(NSA_GROUP) for g in range(NSA_KV_HEADS))
    bd128, bd256, bd512 = _block_diag_ones(128), _block_diag_ones(256), _block_diag_ones(512)
    rope_t = _rope_table(positions)
    rope_ex, rope_cp = _rope_expanders()
    msel = _selection_matrix(nb, nslp)
    lane_pad = lambda v: jnp.zeros((1, LANES), F32).at[0, GDN_HEADS:2 * GDN_HEADS].set(v.astype(F32))

    x2 = x.reshape(m, D_MODEL)
    for l in range(depth):
        w_l = jnp.where(valid[None, :], jnp.take(w_in[l], take, axis=1), 0.0).astype(MXU_DTYPE)
        qg = jnp.tile(nsa_q_norm[l], NSA_HEADS).reshape(1, NSA_WIDTH)
        kg = jnp.tile(nsa_k_norm[l, 1:3], (1, NSA_KV_HEADS))
        proj, qn, qr, ksa, kw, vs, vw, gt = _in_proj(x2, attn_norm[l].reshape(1, D_MODEL), w_l, rope_t, rope_ex,
                                                     rope_cp, qg, kg, bd512, bd128, seq)

        o_gdn = _gdn(proj, gdn_conv_w[l], lane_pad(gdn_a_log[l]), lane_pad(gdn_dt_bias[l]),
                     jnp.tile(gdn_norm[l], GDN_HEADS).reshape(1, GDN_WIDTH), bd256,
                     _head_expander(0), _head_expander(GDN_HEADS), batch, seq)

        w1bd, w2bd = _compress_weights(nsa_cmp_w1[l], nsa_cmp_w2[l])
        kvc = _compress(proj, w1bd, nsa_cmp_pe[l].reshape(2, CMP_LEN * HEAD), nsa_cmp_w1[l].astype(MXU_DTYPE),
                        w2bd, jnp.tile(nsa_k_norm[l, 0], NSA_KV_HEADS).reshape(1, NSA_KV_WIDTH), bd128,
                        batch, seq)

        bound = (HEAD ** 0.5) * jnp.max(jnp.abs(nsa_q_norm[l])) * jnp.max(jnp.abs(nsa_k_norm[l]), axis=1)
        cshift = jnp.zeros((1, LANES), F32).at[0, 0:3].set((bound * LOG2E).astype(F32))
        attn_args = (qn, qr, gt, kvc, ksa, vs, kw, vw, msel, cshift)
        o_nsa = lax.cond(
            jnp.max(bound) <= MAX_SOFTMAX_SHIFT,
            lambda a: _nsa_attn(*a, batch, seq, True),
            lambda a: _nsa_attn(*a, batch, seq, False),
            attn_args)

        wo = w_out[l].astype(MXU_DTYPE)
        wo_nsa = wo[GDN_WIDTH:GDN_WIDTH + NSA_WIDTH].reshape(NSA_KV_HEADS, NSA_GROUP, HEAD, D_MODEL)
        wo = jnp.concatenate([wo[:GDN_WIDTH], wo_nsa.transpose(1, 0, 2, 3).reshape(NSA_WIDTH, D_MODEL),
                              wo[GDN_WIDTH + NSA_WIDTH:]], axis=0)
        x2 = _out_ffn(x2, o_gdn, o_nsa, proj, conv_w[l], wo,
                      ffn_norm[l].reshape(1, D_MODEL), w_gate_up[l].astype(MXU_DTYPE),
                      w_down[l].astype(MXU_DTYPE), seq)
    return x2.reshape(batch, seq, D_MODEL)
```

```python
import functools

import numpy as np
import jax
import jax.numpy as jnp
from jax import lax
from jax.experimental import pallas as pl
from jax.experimental.pallas import tpu as pltpu

F32 = jnp.float32
MXU_DTYPE = jnp.bfloat16

D_MODEL = 1024
HEAD = 64
EPS = 1e-6
NEG_INF = -1e30
GDN_HEADS = 4
GDN_WIDTH = 256
GDN_CONV = 4
GDN_CHUNK = 64
NSA_HEADS = 8
NSA_WIDTH = 512
NSA_KV_HEADS = 2
NSA_GROUP = 4
NSA_KV_WIDTH = 128
CMP_STRIDE = 16
CMP_LEN = 32
SLC_BLOCK = 64
N_SELECT = 16
WINDOW = 512
CONV_WIDTH = 256
CONV_K = 3
ROPE_THETA = 500000.0
ROT_DIM = 16
D_FF = 2816
IN_SIZES = (256, 256, 256, 256, 4, 4, 512, 128, 128, 128, 128, 128, 128, 24, 256, 256, 256)
D_IN = sum(IN_SIZES)

LANES = 128
SUBLANES = 8
VMEM_LIMIT = 56 * 1024 * 1024

C_W = 3 * CONV_WIDTH
P_W = 2 * NSA_KV_WIDTH
GM_W = 4 * GDN_WIDTH
GG_W = LANES
N_W = 512 + 4 * 128 + LANES
C_OFF, P_OFF, GM_OFF, GG_OFF = 0, C_W, C_W + P_W, C_W + P_W + GM_W
PROJ_W = C_W + P_W + GM_W + GG_W
N_OFF = PROJ_W
W_COLS = PROJ_W + N_W
Q_SLOT_HEADS = (0, 4, 1, 5, 2, 6, 3, 7)
N_KS, N_KW, N_VS, N_VW, N_GATE = (NSA_WIDTH + k * LANES for k in range(5))

SEL_BIG = 16384.0
MAX_SOFTMAX_SHIFT = 40.0
LOG2E = 1.4426950408889634

TM_PROJ = 512
TM_FFN = 512
T_GDN = 512
TQ = 256
TQ_RUNMAX = 128
KC = 1024
CHUNKS_PER_TRIP = 4
FF_CHUNK = 256


def _dot(a, b):
    return jnp.dot(a.astype(MXU_DTYPE), b.astype(MXU_DTYPE), preferred_element_type=F32)


def _dot_nt(a, b):
    return lax.dot_general(a.astype(MXU_DTYPE), b.astype(MXU_DTYPE), (((1,), (1,)), ((), ())),
                           preferred_element_type=F32)


def _dot_tn(a, b):
    return lax.dot_general(a.astype(MXU_DTYPE), b.astype(MXU_DTYPE), (((0,), (0,)), ((), ())),
                           preferred_element_type=F32)


def _split3(x):
    a = x.astype(MXU_DTYPE)
    r = x - a.astype(F32)
    b = r.astype(MXU_DTYPE)
    c = (r - b.astype(F32)).astype(MXU_DTYPE)
    return a, b, c


def _dot_hi_l(x, m):
    a, b, c = _split3(x)
    f = lambda t: jnp.dot(t, m, preferred_element_type=F32)
    return f(a) + f(b) + f(c)


def _dot_hi2_l(x, m):
    a = x.astype(MXU_DTYPE)
    b = (x - a.astype(F32)).astype(MXU_DTYPE)
    return jnp.dot(a, m, preferred_element_type=F32) + jnp.dot(b, m, preferred_element_type=F32)


def _silu(x):
    return x * jax.nn.sigmoid(x)


def _shift_rows(x, halo, s, rows):
    y = pltpu.roll(x, s, 0)
    for r in range(s):
        y = jnp.where(rows == r, halo[SUBLANES - s + r:SUBLANES - s + r + 1, :], y)
    return y


def _in_proj_kernel(x_ref, g_ref, w_ref, t_ref, ex_ref, cp_ref, qg_ref, kg_ref, bdq_ref, bdk_ref,
                    o_ref, qn_ref, qr_ref, ksa_ref, kw_ref, vs_ref, vw_ref, gt_ref, *, tm, seq):
    x = x_ref[...]
    ms = jnp.mean(x * x, axis=-1, keepdims=True)
    h = (x * lax.rsqrt(ms + EPS) * g_ref[...]).astype(MXU_DTYPE)
    pending = [(c, min(256, PROJ_W - c)) for c in range(0, PROJ_W, 256)]

    def emit(count):
        for _ in range(min(count, len(pending))):
            c, cw = pending.pop(0)
            o_ref[:, c:c + cw] = jnp.dot(h, w_ref[:, c:c + cw], preferred_element_type=F32)

    nblk = jnp.dot(h, w_ref[:, N_OFF:N_OFF + N_W], preferred_element_type=F32)
    _nsa_operands(nblk, t_ref, ex_ref, cp_ref, qg_ref, kg_ref, bdq_ref, bdk_ref,
                  qn_ref, qr_ref, ksa_ref, kw_ref, vs_ref, vw_ref, gt_ref, tt=tm, seq=seq, between=emit)
    emit(len(pending))


def _in_proj(x2, gain, w, rope_t, rope_ex, rope_cp, qg, kg, bd512, bd128, seq):
    m = x2.shape[0]
    tm = TM_PROJ
    row = lambda wd: pl.BlockSpec((tm, wd), lambda i: (i, 0))
    full = lambda a, b: pl.BlockSpec((a, b), lambda i: (0, 0))
    sds = lambda wd, dt: jax.ShapeDtypeStruct((m, wd), dt)
    return pl.pallas_call(
        functools.partial(_in_proj_kernel, tm=tm, seq=seq),
        grid=(m // tm,),
        in_specs=[row(D_MODEL), full(1, D_MODEL), full(D_MODEL, W_COLS), row(LANES),
                  pl.BlockSpec((3, LANES, LANES), lambda i: (0, 0, 0)), full(1, LANES),
                  full(1, NSA_WIDTH), full(2, LANES), full(NSA_WIDTH, NSA_WIDTH), full(LANES, LANES)],
        out_specs=[row(PROJ_W), row(NSA_WIDTH), row(NSA_WIDTH), row(2 * LANES), row(LANES), row(LANES),
                   row(LANES), row(LANES)],
        out_shape=[sds(PROJ_W, F32), sds(NSA_WIDTH, MXU_DTYPE), sds(NSA_WIDTH, MXU_DTYPE),
                   sds(2 * LANES, MXU_DTYPE), sds(LANES, MXU_DTYPE), sds(LANES, MXU_DTYPE),
                   sds(LANES, MXU_DTYPE), sds(LANES, F32)],
        compiler_params=pltpu.CompilerParams(dimension_semantics=("parallel",),
                                             vmem_limit_bytes=VMEM_LIMIT),
        name="in_proj",
    )(x2, gain, w, rope_t, rope_ex, rope_cp, qg, kg, bd512, bd128)


def _gdn_kernel(g_ref, gh_ref, gg_ref, cw_ref, alog_ref, dt_ref, gn_ref, bd_ref, eb_ref, eg_ref, o_ref, s_ref,
                *, tt):
    t_idx = pl.program_id(1)

    @pl.when(t_idx == 0)
    def _():
        s_ref[...] = jnp.zeros_like(s_ref)

    keep = jnp.where(t_idx == 0, 0.0, 1.0)
    rows = lax.broadcasted_iota(jnp.int32, (tt, 1), 0)
    x = g_ref[:, 0:3 * GDN_WIDTH]
    hx = gh_ref[:, 0:3 * GDN_WIDTH] * keep
    w = cw_ref[...]
    y = w[GDN_CONV - 1:GDN_CONV, :] * x
    for s in range(1, GDN_CONV):
        y = y + w[GDN_CONV - 1 - s:GDN_CONV - s, :] * _shift_rows(x, hx, s, rows)
    y = _silu(y)
    q = y[:, 0:GDN_WIDTH]
    k = y[:, GDN_WIDTH:2 * GDN_WIDTH]
    v = y[:, 2 * GDN_WIDTH:3 * GDN_WIDTH]
    bd = bd_ref[...]
    q = q * lax.rsqrt(_dot(q * q, bd) + EPS) * (HEAD ** -0.5)
    k = k * lax.rsqrt(_dot(k * k, bd) + EPS)

    gg = gg_ref[...]
    lane = lax.broadcasted_iota(jnp.int32, (1, LANES), 1)
    xa = gg + dt_ref[...]
    softplus = jnp.maximum(xa, 0.0) + jnp.log1p(jnp.exp(-jnp.abs(xa)))
    g2 = jnp.where(lane < GDN_HEADS, jax.nn.sigmoid(gg), -jnp.exp(alog_ref[...]) * softplus)
    cs = g2.T
    lane_t = lax.broadcasted_iota(jnp.int32, (1, tt), 1) % GDN_CHUNK
    step = 1
    while step < GDN_CHUNK:
        cs = cs + jnp.where(lane_t >= step, pltpu.roll(cs, step, 1), 0.0)
        step *= 2
    gcum_t = cs
    gcum = cs.T
    beta_e = _dot_hi_l(g2, eb_ref[...])
    gcum_e = _dot_hi_l(gcum, eg_ref[...])
    eg_e = jnp.exp(gcum_e)
    kb = k * beta_e
    rv = v * beta_e
    rk = kb * eg_e
    qd = q * eg_e

    nst = GDN_HEADS * GDN_CHUNK
    ri = lax.broadcasted_iota(jnp.int32, (nst, nst), 0)
    ci = lax.broadcasted_iota(jnp.int32, (nst, nst), 1)
    same = (ri // GDN_CHUNK) == (ci // GDN_CHUNK)
    m_tril = same & ((ri % GDN_CHUNK) >= (ci % GDN_CHUNK))
    m_strict = same & ((ri % GDN_CHUNK) > (ci % GDN_CHUNK))
    tile4 = lambda t: jnp.concatenate([t] * GDN_HEADS, axis=0)
    expand = lambda t: tile4(t.astype(MXU_DTYPE)) * bd

    nchunk = tt // GDN_CHUNK
    rss = [slice(n * GDN_CHUNK, (n + 1) * GDN_CHUNK) for n in range(nchunk)]
    g_last = [gcum_e[(n + 1) * GDN_CHUNK - 1:(n + 1) * GDN_CHUNK, :] for n in range(nchunk)]
    rmat, pw, qk = [], [], []
    for n, rs in enumerate(rss):
        g_row = jnp.concatenate([gcum_t[GDN_HEADS + h:GDN_HEADS + h + 1, rs] for h in range(GDN_HEADS)], axis=1)
        decay = jnp.where(m_tril, jnp.exp(jnp.where(m_tril, tile4(gcum_e[rs]) - g_row, 0.0)), 0.0)
        k4 = tile4(k[rs])
        a = jnp.where(m_strict, _dot_nt(expand(kb[rs]), k4) * decay, 0.0)
        qk.append(jnp.where(m_tril, _dot_nt(expand(q[rs]), k4) * decay, 0.0))
        rmat.append(-a)
        pw.append(-a)
    for _ in range(GDN_CHUNK.bit_length() - 2):
        nxt_pw, nxt_r = [], []
        for r, t in zip(rmat, pw):
            t2 = _dot(t, t)
            nxt_pw.append(t2)
            nxt_r.append(r + t2 + _dot(r, t2))
        pw, rmat = nxt_pw, nxt_r
    u, wm = [], []
    for n, rs in enumerate(rss):
        rv_x = jnp.where(same, tile4(rv[rs]), 0.0)
        rk_x = jnp.where(same, tile4(rk[rs]), 0.0)
        u.append(rv_x + _dot(rmat[n], rv_x))
        wm.append(rk_x + _dot(rmat[n], rk_x))

    st = s_ref[...]
    outs = []
    for n, rs in enumerate(rss):
        kd = k[rs] * jnp.exp(g_last[n] - gcum_e[rs])
        v_new = u[n] - _dot(wm[n], st)
        o_x = _dot(expand(qd[rs]), st) + _dot(qk[n], v_new)
        st = st * jnp.exp(g_last[n]) + _dot_tn(expand(kd), v_new)
        o = o_x[0:GDN_CHUNK]
        for h in range(1, GDN_HEADS):
            o = o + o_x[h * GDN_CHUNK:(h + 1) * GDN_CHUNK]
        outs.append(o)
    s_ref[...] = st
    o_all = jnp.concatenate(outs, axis=0)
    on = o_all * lax.rsqrt(_dot(o_all * o_all, bd) * (1.0 / HEAD) + EPS) * gn_ref[...]
    o_ref[...] = (on * _silu(g_ref[:, 3 * GDN_WIDTH:4 * GDN_WIDTH])).astype(o_ref.dtype)


def _gdn(proj, conv_w, alog_pad, dt_pad, gnorm, bd256, eb, eg, batch, seq):
    tt = T_GDN
    nt = seq // tt
    hb = tt // SUBLANES
    return pl.pallas_call(
        functools.partial(_gdn_kernel, tt=tt),
        grid=(batch, nt),
        in_specs=[pl.BlockSpec((tt, GM_W), lambda b, t: (b * nt + t, GM_OFF // GM_W)),
                  pl.BlockSpec((SUBLANES, GM_W),
                               lambda b, t: (jnp.maximum((b * nt + t) * hb - 1, 0), GM_OFF // GM_W)),
                  pl.BlockSpec((tt, GG_W), lambda b, t: (b * nt + t, GG_OFF // GG_W)),
                  pl.BlockSpec((GDN_CONV, 3 * GDN_WIDTH), lambda b, t: (0, 0)),
                  pl.BlockSpec((1, LANES), lambda b, t: (0, 0)),
                  pl.BlockSpec((1, LANES), lambda b, t: (0, 0)),
                  pl.BlockSpec((1, GDN_WIDTH), lambda b, t: (0, 0)),
                  pl.BlockSpec((GDN_WIDTH, GDN_WIDTH), lambda b, t: (0, 0)),
                  pl.BlockSpec((LANES, GDN_WIDTH), lambda b, t: (0, 0)),
                  pl.BlockSpec((LANES, GDN_WIDTH), lambda b, t: (0, 0))],
        out_specs=pl.BlockSpec((tt, GDN_WIDTH), lambda b, t: (b * nt + t, 0)),
        out_shape=jax.ShapeDtypeStruct((batch * seq, GDN_WIDTH), MXU_DTYPE),
        scratch_shapes=[pltpu.VMEM((GDN_WIDTH, GDN_WIDTH), F32)],
        compiler_params=pltpu.CompilerParams(dimension_semantics=("parallel", "arbitrary"),
                                             vmem_limit_bytes=VMEM_LIMIT),
        name="gdn",
    )(proj, proj, proj, conv_w, alog_pad, dt_pad, gnorm, bd256, eb, eg)


def _rope(x, c, s1, s2):
    wdt = x.shape[1]
    return x * c + pltpu.roll(x, wdt - ROT_DIM // 2, 1) * s1 + pltpu.roll(x, ROT_DIM // 2, 1) * s2


def _nsa_operands(n_blk, t_ref, ex_ref, cp_ref, qg_ref, kg_ref, bdq_ref, bdk_ref,
                  qn_ref, qr_ref, ksa_ref, kw_ref, vs_ref, vw_ref, gt_ref, *, tt, seq, between):
    between(3)
    table = t_ref[...]
    c1 = _dot_hi2_l(table, ex_ref[0]) + cp_ref[...]
    s1 = _dot_hi2_l(table, ex_ref[1])
    s2 = _dot_hi2_l(table, ex_ref[2])
    c4 = jnp.concatenate([c1] * 4, axis=1)
    s14 = jnp.concatenate([s1] * 4, axis=1)
    s24 = jnp.concatenate([s2] * 4, axis=1)
    q = n_blk[:, 0:NSA_WIDTH]
    qsq = q * q
    half_w = NSA_WIDTH // 2
    bdh = bdq_ref[0:half_w, 0:half_w]
    q_ss = jnp.concatenate([_dot(qsq[:, 0:half_w], bdh), _dot(qsq[:, half_w:NSA_WIDTH], bdh)], axis=1)
    qn = q * lax.rsqrt(q_ss * (1.0 / HEAD) + EPS) * qg_ref[...]
    qn = qn * (HEAD ** -0.5 * LOG2E)
    qn_ref[...] = qn.astype(qn_ref.dtype)
    qr_ref[...] = _rope(qn, c4, s14, s24).astype(qr_ref.dtype)
    between(3)
    bdk = bdk_ref[...]
    ks = n_blk[:, N_KS:N_KS + LANES]
    kw = n_blk[:, N_KW:N_KW + LANES]
    ks = ks * lax.rsqrt(_dot(ks * ks, bdk) * (1.0 / HEAD) + EPS) * kg_ref[0:1, :]
    kw = kw * lax.rsqrt(_dot(kw * kw, bdk) * (1.0 / HEAD) + EPS) * kg_ref[1:2, :]
    ksa_ref[:, 0:LANES] = _rope(ks, c1, s1, s2).astype(ksa_ref.dtype)
    rows = (lax.broadcasted_iota(jnp.int32, (tt, LANES), 0) + pl.program_id(0) * tt) % seq
    lane = lax.broadcasted_iota(jnp.int32, (tt, LANES), 1)
    ksa_ref[:, LANES:2 * LANES] = jnp.where((rows // SLC_BLOCK) % LANES == lane, 1.0, 0.0).astype(ksa_ref.dtype)
    kw_ref[...] = _rope(kw, c1, s1, s2).astype(kw_ref.dtype)
    vs_ref[...] = n_blk[:, N_VS:N_VS + LANES].astype(vs_ref.dtype)
    vw_ref[...] = n_blk[:, N_VW:N_VW + LANES].astype(vw_ref.dtype)
    gt_ref[...] = jax.nn.sigmoid(n_blk[:, N_GATE:N_GATE + LANES])


def _compress_kernel(xk_ref, xv_ref, w1bd_ref, pe_ref, w1_ref, w2bd_ref, kg_ref, bd_ref, o_ref, *, nb):
    acc = None
    for t in range(CMP_STRIDE):
        xt = jnp.concatenate([xk_ref[pl.ds(t, nb, stride=CMP_STRIDE), :],
                              xv_ref[pl.ds(t, nb, stride=CMP_STRIDE), :]], axis=1)
        part = _dot(xt, w1bd_ref[t])
        acc = part if acc is None else acc + part
    pe_terms = [_dot(jnp.broadcast_to(pe_ref[c:c + 1, :], (SUBLANES, CMP_LEN * HEAD)), w1_ref[c])[0:1, :]
                for c in range(2)]
    pe_all = jnp.concatenate([pe_terms[0]] * NSA_KV_HEADS + [pe_terms[1]] * NSA_KV_HEADS, axis=1)
    half = 2 * NSA_KV_WIDTH
    pre = acc[:, 0:half] + pltpu.roll(acc[:, half:2 * half], nb - 1, 0) + pe_all
    y = _dot(jax.nn.gelu(pre), w2bd_ref[...])
    rows = lax.broadcasted_iota(jnp.int32, (nb, 1), 0)
    y = jnp.where(rows < nb - 1, y, 0.0)
    yk = y[:, 0:NSA_KV_WIDTH]
    yk = yk * lax.rsqrt(_dot(yk * yk, bd_ref[...]) * (1.0 / HEAD) + EPS) * kg_ref[...]
    o_ref[0, 0] = yk.astype(o_ref.dtype)
    o_ref[1, 0] = y[:, NSA_KV_WIDTH:2 * NSA_KV_WIDTH].astype(o_ref.dtype)


def _compress(proj, w1bd, pe, w1, w2bd, kg0, bd128, batch, seq):
    nb = seq // CMP_STRIDE
    full = lambda *s: pl.BlockSpec(s, lambda b: (0,) * len(s))
    return pl.pallas_call(
        functools.partial(_compress_kernel, nb=nb),
        grid=(batch,),
        in_specs=[pl.BlockSpec((seq, NSA_KV_WIDTH), lambda b: (b, P_OFF // NSA_KV_WIDTH)),
                  pl.BlockSpec((seq, NSA_KV_WIDTH), lambda b: (b, P_OFF // NSA_KV_WIDTH + 1)),
                  full(CMP_STRIDE, P_W, 2 * P_W), full(2, CMP_LEN * HEAD), full(2, CMP_LEN * HEAD, HEAD),
                  full(P_W, P_W), full(1, NSA_KV_WIDTH), full(LANES, LANES)],
        out_specs=pl.BlockSpec((2, 1, nb, NSA_KV_WIDTH), lambda b: (0, b, 0, 0)),
        out_shape=jax.ShapeDtypeStruct((2, batch, nb, NSA_KV_WIDTH), MXU_DTYPE),
        compiler_params=pltpu.CompilerParams(dimension_semantics=("parallel",),
                                             vmem_limit_bytes=VMEM_LIMIT),
        name="compress",
    )(proj, proj, w1bd, pe, w1, w2bd, kg0, bd128)


def _nsa_attn_kernel(*refs, tq, seq, nwb, kc, ncp, nslp, shifted):
    qn_ref, qr_ref, gt_ref, kcmp_ref, vcmp_ref, ksa_ref, vs_ref = refs[:7]
    kw_refs = refs[7:7 + nwb]
    vw_refs = refs[7 + nwb:7 + 2 * nwb]
    msel_ref, cs_ref, egate_ref = refs[7 + 2 * nwb:10 + 2 * nwb]
    o_ref = refs[10 + 2 * nwb]
    bias_scr, qaug_scr, m_scr, l_scr, acc_scr, ow_scr, cand_scr, oc_scr, slc_scr, e_scr = refs[11 + 2 * nwb:]
    nslot = NSA_HEADS
    i = pl.program_id(1)
    c_cmp, c_slc, c_win = cs_ref[0:1, 0:1], cs_ref[0:1, 1:2], cs_ref[0:1, 2:3]
    lane = lax.broadcasted_iota(jnp.int32, (1, LANES), 1)
    lo = lane < HEAD

    def stack(q_ref):
        sl = [q_ref[:, j * LANES:(j + 1) * LANES] for j in range(NSA_GROUP)]
        zero = jnp.zeros_like(sl[0])
        return jnp.concatenate([jnp.where(lo, s, zero) for s in sl] + [jnp.where(lo, zero, s) for s in sl], axis=0)

    tpos = i * tq + lax.broadcasted_iota(jnp.int32, (tq, 1), 0)
    tpos_st = i * tq + lax.broadcasted_iota(jnp.int32, (nslot * tq, 1), 0) % tq
    gates = gt_ref[...]

    qn_st = stack(qn_ref)
    if shifted:
        ccw = min(2 * LANES, ncp)
        n_vis = (i + 1) * (tq // CMP_STRIDE) - 1
        n_cch = (n_vis + ccw - 1) // ccw
        qaug_scr[:, 0:LANES] = qn_st
        l_scr[...] = jnp.zeros(l_scr.shape, F32)
        acc_scr[...] = jnp.zeros(acc_scr.shape, F32)

        def cmp_scores(ch):
            k0 = ch * ccw
            s = _dot_nt(qaug_scr[:, 0:LANES], kcmp_ref[0, 0, pl.ds(k0, ccw), :])
            ckey = k0 + lax.broadcasted_iota(jnp.int32, (1, ccw), 1)
            vis = (ckey * CMP_STRIDE + (CMP_LEN - 1)) <= tpos
            cbias = jnp.where(vis, -c_cmp, -SEL_BIG)
            for slot in range(nslot):
                rs = slice(slot * tq, (slot + 1) * tq)
                e = jnp.exp2(s[rs] + cbias)
                e_scr[rs, pl.ds(k0, ccw)] = e
                part = e[:, 0:LANES]
                for t in range(1, ccw // LANES):
                    part = part + e[:, t * LANES:(t + 1) * LANES]
                l_scr[rs, :] += part
            acc_scr[...] += _dot(e_scr[:, pl.ds(k0, ccw)], vcmp_ref[0, 0, pl.ds(k0, ccw), :])

        def for_visible_chunks(fn):
            def variant(count):
                for ch in range(count):
                    fn(ch)

            for count in range(1, ncp // ccw + 1):
                pl.when(n_cch == count)(functools.partial(variant, count))

        for_visible_chunks(cmp_scores)
        l_c = jnp.sum(l_scr[...], axis=-1, keepdims=True)
        rinv_c = 1.0 / jnp.where(l_c > 0.0, l_c, 1.0)
        oc_scr[...] = acc_scr[...] * rinv_c
        l_scr[...] = jnp.broadcast_to(rinv_c, l_scr.shape)
        slc_scr[...] = jnp.zeros(slc_scr.shape, F32)

        def cmp_importance(ch):
            k0 = ch * ccw
            for g in range(NSA_KV_HEADS):
                ps = None
                for r in range(NSA_GROUP):
                    rs = slice((g * NSA_GROUP + r) * tq, (g * NSA_GROUP + r + 1) * tq)
                    rinv_l = jnp.concatenate([l_scr[rs, :]] * (ccw // LANES), axis=1)
                    p = e_scr[rs, pl.ds(k0, ccw)] * rinv_l
                    ps = p if ps is None else ps + p
                slc_scr[g] += _dot_hi_l(ps, msel_ref[pl.ds(k0, ccw), :])

        for_visible_chunks(cmp_importance)
        slc_rows = [slc_scr[g] for g in range(NSA_KV_HEADS)]
    else:
        ckey = lax.broadcasted_iota(jnp.int32, (1, ncp), 1)
        cmask = (ckey * CMP_STRIDE + (CMP_LEN - 1)) <= tpos
        s_all = _dot_nt(qn_st, kcmp_ref[0, 0])
        psum = [None, None]
        e_parts, rinv_parts = [], []
        for slot in range(nslot):
            s = jnp.where(cmask, s_all[slot * tq:(slot + 1) * tq], NEG_INF)
            e = jnp.where(cmask, jnp.exp2(s - jnp.max(s, axis=-1, keepdims=True)), 0.0)
            l = jnp.sum(e, axis=-1, keepdims=True)
            rinv = 1.0 / jnp.where(l > 0.0, l, 1.0)
            e_parts.append(e.astype(MXU_DTYPE))
            rinv_parts.append(rinv)
            p = e * rinv
            g = slot // NSA_GROUP
            psum[g] = p if psum[g] is None else psum[g] + p
        oc_scr[...] = (_dot(jnp.concatenate(e_parts, axis=0), vcmp_ref[0, 0])
                       * jnp.concatenate(rinv_parts, axis=0))
        slc_rows = [_dot_hi_l(psum[g], msel_ref[...]) for g in range(NSA_KV_HEADS)]

    qr_st = stack(qr_ref)
    kwc = jnp.concatenate([r[...] for r in kw_refs], axis=0)
    vwc = jnp.concatenate([r[...] for r in vw_refs], axis=0)
    sw = _dot_nt(qr_st, kwc)
    if shifted:
        rr = lax.broadcasted_iota(jnp.int32, (tq, tq), 0)
        cc = lax.broadcasted_iota(jnp.int32, (tq, tq), 1)
        parts = []
        for jb in range(nwb):
            shift_b = c_win + jnp.where(i - (nwb - 1) + jb >= 0, 0.0, SEL_BIG)
            blk = jnp.exp2(sw[:, jb * tq:(jb + 1) * tq] - shift_b)
            if jb == 0 or jb == nwb - 1:
                vis = (cc > rr) if jb == 0 else (cc <= rr)
                blk = jnp.concatenate([jnp.where(vis, blk[s * tq:(s + 1) * tq], 0.0) for s in range(nslot)],
                                      axis=0)
            parts.append(blk)
        ew = jnp.concatenate(parts, axis=1)
    else:
        kpos_w = (i - (nwb - 1)) * tq + lax.broadcasted_iota(jnp.int32, (1, nwb * tq), 1)
        dist = tpos_st - kpos_w
        wmask = (dist >= 0) & (dist < WINDOW) & (kpos_w >= 0)
        sw = jnp.where(wmask, sw, NEG_INF)
        ew = jnp.exp2(sw - jnp.max(sw, axis=-1, keepdims=True))
    ow_scr[...] = _dot(ew, vwc) * (1.0 / jnp.sum(ew, axis=-1, keepdims=True))

    jrow = lax.broadcasted_iota(jnp.int32, (nslp, 1), 0)
    jrowf = jrow.astype(F32)
    cur_t = (i * tq + lax.broadcasted_iota(jnp.int32, (1, tq), 1)) // SLC_BLOCK
    forced = (jrow == 0) | (jrow == cur_t) | (jrow == cur_t - 1)
    causal = jrow <= cur_t
    shift_s = c_slc if shifted else 0.0
    n_free = N_SELECT - 3
    to_bias = lambda sel: ((sel - 1.0) * SEL_BIG - shift_s).T.astype(bias_scr.dtype)
    n_bad = None
    for g in range(NSA_KV_HEADS):
        slc = slc_rows[g].T
        cand = jnp.where(causal & jnp.logical_not(forced), slc, -1.0)
        cand_scr[g] = cand
        c = cand
        for _ in range(n_free):
            c = jnp.where(c == jnp.max(c, axis=0, keepdims=True), -2.0, c)
        picked = (c == -2.0) & (cand >= 0.0)
        n_picked = jnp.sum(jnp.where(picked, 1.0, 0.0), axis=0, keepdims=True)
        n_real = jnp.sum(jnp.where(cand >= 0.0, 1.0, 0.0), axis=0, keepdims=True)
        bad = jnp.where(n_picked == jnp.minimum(n_real, float(n_free)), 0.0, 1.0)
        n_bad = bad if n_bad is None else n_bad + bad
        bias_scr[g] = to_bias(jnp.where(forced | picked, 1.0, 0.0))

    @pl.when(jnp.max(n_bad) > 0.0)
    def _():
        for g in range(NSA_KV_HEADS):
            cand = cand_scr[g]
            sel = jnp.where(forced, 1.0, 0.0)
            for _ in range(n_free):
                mx = jnp.max(cand, axis=0, keepdims=True)
                first = jnp.min(jnp.where(cand == mx, jrowf, float(nslp)), axis=0, keepdims=True)
                hit = jrowf == first
                sel = jnp.where(hit, 1.0, sel)
                cand = jnp.where(hit, -2.0, cand)
            bias_scr[g] = to_bias(jnp.where(causal, sel, 0.0))

    qaug_scr[:, 0:LANES] = qr_st
    if not shifted:
        m_scr[...] = jnp.full(m_scr.shape, NEG_INF, F32)
    l_scr[...] = jnp.zeros(l_scr.shape, F32)
    acc_scr[...] = jnp.zeros(acc_scr.shape, F32)
    group_keys = LANES * SLC_BLOCK

    def load_bias(key0):
        @pl.when(key0 % group_keys == 0)
        def _():
            off = pl.multiple_of((key0 // group_keys) * LANES, LANES)
            b0 = bias_scr[0, :, pl.ds(off, LANES)]
            b1 = bias_scr[1, :, pl.ds(off, LANES)]
            qaug_scr[:, LANES:2 * LANES] = jnp.concatenate([b0] * NSA_GROUP + [b1] * NSA_GROUP, axis=0)

    def qk(key0, width):
        return _dot_nt(qaug_scr[...], ksa_ref[pl.ds(pl.multiple_of(key0, width), width), :])

    def scores(key0, width):
        load_bias(key0)
        return qk(key0, width)

    def accumulate(s, key0, width):
        k0 = pl.multiple_of(key0, width)
        if shifted:
            p = jnp.exp2(s)
            psum_l = p[:, 0:LANES]
            for t in range(1, width // LANES):
                psum_l = psum_l + p[:, t * LANES:(t + 1) * LANES]
            l_scr[...] += psum_l
            acc_scr[...] += _dot(p, vs_ref[pl.ds(k0, width), :])
        else:
            m_old = m_scr[...]
            m_new = jnp.maximum(m_old, jnp.max(s, axis=-1, keepdims=True))
            alpha = jnp.exp2(m_old - m_new)
            p = jnp.exp2(s - m_new[:, 0:1])
            l_scr[...] = alpha * l_scr[...] + jnp.sum(p, axis=-1, keepdims=True)
            acc_scr[...] = alpha * acc_scr[...] + _dot(p, vs_ref[pl.ds(k0, width), :])
            m_scr[...] = m_new

    per_trip = CHUNKS_PER_TRIP if (shifted and group_keys % (CHUNKS_PER_TRIP * kc) == 0) else 1
    pair = per_trip * kc
    n_pair = (i * tq) // pair if per_trip > 1 else 0
    n_full = (i * tq - n_pair * pair) // kc
    before_sub = n_pair * pair + n_full * kc
    n_sub = (i * tq - before_sub) // tq

    def body_pair(c, carry):
        key0 = c * pair
        load_bias(key0)
        for h in range(per_trip):
            accumulate(qk(key0 + h * kc, kc), key0 + h * kc, kc)
        return carry

    def body_full(c, carry):
        key0 = n_pair * pair + c * kc
        accumulate(scores(key0, kc), key0, kc)
        return carry

    lax.fori_loop(0, n_pair, body_pair, 0)
    lax.fori_loop(0, n_full, body_full, 0)

    def own_chunk():
        col = lax.broadcasted_iota(jnp.int32, (1, tq), 1)
        accumulate(jnp.where(col <= tpos_st - i * tq, qk(i * tq, tq), -SEL_BIG), i * tq, tq)

    load_bias(before_sub)
    if shifted:
        def tail(count):
            for j in range(count):
                accumulate(qk(before_sub + j * tq, tq), before_sub + j * tq, tq)
            own_chunk()

        for count in range(kc // tq):
            pl.when(n_sub == count)(functools.partial(tail, count))
    else:
        def body_sub(j, carry):
            key0 = before_sub + j * tq
            accumulate(qk(key0, tq), key0, tq)
            return carry

        lax.fori_loop(0, n_sub, body_sub, 0)
        own_chunk()
    if shifted:
        o_s = acc_scr[...] * (1.0 / jnp.sum(l_scr[...], axis=-1, keepdims=True))
    else:
        o_s = acc_scr[...] / l_scr[...]
    o_w = ow_scr[...]

    gate_e = _dot_hi2_l(gates, egate_ref[...])
    for j in range(NSA_GROUP):
        r0 = slice(j * tq, (j + 1) * tq)
        r1 = slice((NSA_GROUP + j) * tq, (NSA_GROUP + j + 1) * tq)
        out = None
        for br, o_b in enumerate((oc_scr[...], o_s, o_w)):
            lanes = slice(br * NSA_WIDTH + j * LANES, br * NSA_WIDTH + (j + 1) * LANES)
            term = gate_e[:, lanes] * jnp.where(lo, o_b[r0], o_b[r1])
            out = term if out is None else out + term
        o_ref[:, j * LANES:(j + 1) * LANES] = out.astype(o_ref.dtype)


def _gate_expander():
    mat = np.zeros((LANES, 3 * NSA_WIDTH), np.float32)
    for br in range(3):
        for j in range(NSA_GROUP):
            for g in range(NSA_KV_HEADS):
                head = g * NSA_GROUP + j
                c0 = br * NSA_WIDTH + j * LANES + g * HEAD
                mat[3 * head + br, c0:c0 + HEAD] = 1.0
    return jnp.asarray(mat, MXU_DTYPE)


def _nsa_attn(qn, qr, gt, kvc, ksa, vs, kw, vw, msel, cshift, batch, seq, shifted):
    tq, kc = (TQ if shifted else TQ_RUNMAX), KC
    nq = seq // tq
    nwb = WINDOW // tq + 1
    ncp = seq // CMP_STRIDE
    nslp = msel.shape[1]
    row = lambda w: pl.BlockSpec((tq, w), lambda b, i: (b * nq + i, 0))
    once = pl.Buffered(1)
    win = lambda jb: pl.BlockSpec((tq, LANES), lambda b, i: (b * nq + jnp.maximum(i - (nwb - 1) + jb, 0), 0))
    in_specs = ([row(NSA_WIDTH), row(NSA_WIDTH), row(LANES),
                 pl.BlockSpec((1, 1, ncp, LANES), lambda b, i: (0, b, 0, 0), pipeline_mode=once),
                 pl.BlockSpec((1, 1, ncp, LANES), lambda b, i: (1, b, 0, 0), pipeline_mode=once),
                 pl.BlockSpec((seq, 2 * LANES), lambda b, i: (b, 0), pipeline_mode=once),
                 pl.BlockSpec((seq, LANES), lambda b, i: (b, 0), pipeline_mode=once)]
                + [win(jb) for jb in range(nwb)] + [win(jb) for jb in range(nwb)]
                + [pl.BlockSpec((ncp, nslp), lambda b, i: (0, 0), pipeline_mode=once),
                   pl.BlockSpec((1, LANES), lambda b, i: (0, 0)),
                   pl.BlockSpec((LANES, 3 * NSA_WIDTH), lambda b, i: (0, 0), pipeline_mode=once)])
    return pl.pallas_call(
        functools.partial(_nsa_attn_kernel, tq=tq, seq=seq, nwb=nwb, kc=kc, ncp=ncp, nslp=nslp,
                          shifted=shifted),
        grid=(batch, nq),
        in_specs=in_specs,
        out_specs=row(NSA_WIDTH),
        out_shape=jax.ShapeDtypeStruct((batch * seq, NSA_WIDTH), MXU_DTYPE),
        scratch_shapes=[pltpu.VMEM((NSA_KV_HEADS, tq, nslp), MXU_DTYPE),
                        pltpu.VMEM((NSA_HEADS * tq, 2 * LANES), MXU_DTYPE),
                        pltpu.VMEM((NSA_HEADS * tq, LANES), F32),
                        pltpu.VMEM((NSA_HEADS * tq, LANES), F32),
                        pltpu.VMEM((NSA_HEADS * tq, LANES), F32),
                        pltpu.VMEM((NSA_HEADS * tq, LANES), F32),
                        pltpu.VMEM((NSA_KV_HEADS, nslp, tq), F32),
                        pltpu.VMEM((NSA_HEADS * tq, LANES), F32),
                        pltpu.VMEM((NSA_KV_HEADS, tq, nslp), F32),
                        pltpu.VMEM((NSA_HEADS * tq, ncp) if shifted else (SUBLANES, LANES), F32)],
        compiler_params=pltpu.CompilerParams(dimension_semantics=("parallel", "arbitrary"),
                                             vmem_limit_bytes=VMEM_LIMIT),
        name="nsa_attn" if shifted else "nsa_attn_runmax",
    )(qn, qr, gt, kvc, kvc, ksa, vs, *([kw] * nwb), *([vw] * nwb), msel, cshift, _gate_expander())


def _out_ffn_kernel(x_ref, og_ref, on_ref, c_ref, ch_ref, cw_ref, wo_ref, fg_ref, wgu_ref, wd_ref, o_ref,
                    *, tm, seq):
    i = pl.program_id(0)
    keep = jnp.where((i * tm) % seq == 0, 0.0, 1.0)
    rows = lax.broadcasted_iota(jnp.int32, (tm, 1), 0)
    u = c_ref[:, CONV_WIDTH:2 * CONV_WIDTH] * c_ref[:, 2 * CONV_WIDTH:3 * CONV_WIDTH]
    hu = ch_ref[:, CONV_WIDTH:2 * CONV_WIDTH] * ch_ref[:, 2 * CONV_WIDTH:3 * CONV_WIDTH] * keep
    w = cw_ref[...]
    conv = w[CONV_K - 1:CONV_K, :] * u
    for s in range(1, CONV_K):
        conv = conv + w[CONV_K - 1 - s:CONV_K - s, :] * _shift_rows(u, hu, s, rows)
    oc = c_ref[:, 0:CONV_WIDTH] * conv
    x1 = (x_ref[...] + _dot(og_ref[...], wo_ref[0:GDN_WIDTH, :])
          + _dot(on_ref[...], wo_ref[GDN_WIDTH:GDN_WIDTH + NSA_WIDTH, :])
          + _dot(oc, wo_ref[GDN_WIDTH + NSA_WIDTH:, :]))
    ms = jnp.mean(x1 * x1, axis=-1, keepdims=True)
    h2 = (x1 * lax.rsqrt(ms + EPS) * fg_ref[...]).astype(MXU_DTYPE)
    o_ref[...] = x1
    for c0 in range(0, D_FF, FF_CHUNK):
        gate = jnp.dot(h2, wgu_ref[:, c0:c0 + FF_CHUNK], preferred_element_type=F32)
        up = jnp.dot(h2, wgu_ref[:, D_FF + c0:D_FF + c0 + FF_CHUNK], preferred_element_type=F32)
        o_ref[...] += _dot(_silu(gate) * up, wd_ref[c0:c0 + FF_CHUNK, :])


def _out_ffn(x2, o_gdn, o_nsa, proj, conv_w, w_out, fgain, wgu, wd, seq):
    m = x2.shape[0]
    tm = TM_FFN
    hb = tm // SUBLANES
    full = lambda a, b: pl.BlockSpec((a, b), lambda i: (0, 0), pipeline_mode=pl.Buffered(1))
    return pl.pallas_call(
        functools.partial(_out_ffn_kernel, tm=tm, seq=seq),
        grid=(m // tm,),
        in_specs=[pl.BlockSpec((tm, D_MODEL), lambda i: (i, 0)),
                  pl.BlockSpec((tm, GDN_WIDTH), lambda i: (i, 0)),
                  pl.BlockSpec((tm, NSA_WIDTH), lambda i: (i, 0)),
                  pl.BlockSpec((tm, C_W), lambda i: (i, C_OFF // C_W)),
                  pl.BlockSpec((SUBLANES, C_W), lambda i: (jnp.maximum(i * hb - 1, 0), C_OFF // C_W)),
                  full(CONV_K, CONV_WIDTH), full(D_MODEL, D_MODEL), full(1, D_MODEL),
                  full(D_MODEL, 2 * D_FF), full(D_FF, D_MODEL)],
        out_specs=pl.BlockSpec((tm, D_MODEL), lambda i: (i, 0)),
        out_shape=jax.ShapeDtypeStruct((m, D_MODEL), F32),
        compiler_params=pltpu.CompilerParams(dimension_semantics=("parallel",),
                                             vmem_limit_bytes=VMEM_LIMIT),
        name="out_ffn",
    )(x2, o_gdn, o_nsa, proj, proj, conv_w, w_out, fgain, wgu, wd)


def _proj_column_map():
    offs = np.concatenate([[0], np.cumsum(IN_SIZES)])
    seg = lambda k: np.arange(offs[k], offs[k + 1])
    pad = lambda n: -np.ones(n, np.int64)
    nq = seg(6).reshape(NSA_HEADS, HEAD)[list(Q_SLOT_HEADS)].reshape(-1)
    cols = np.concatenate([
        seg(14), seg(15), seg(16),
        seg(7), seg(8),
        seg(0), seg(1), seg(2), seg(3),
        seg(4), seg(5), pad(GG_W - 2 * GDN_HEADS),
        nq, seg(9), seg(11), seg(10), seg(12), seg(13), pad(LANES - 3 * NSA_HEADS)])
    assert cols.shape[0] == W_COLS
    return cols


def _block_diag_ones(n):
    idx = np.arange(n) // HEAD
    return jnp.asarray(idx[:, None] == idx[None, :], MXU_DTYPE)


def _head_expander(first_lane):
    mat = np.zeros((LANES, GDN_WIDTH), np.float32)
    for h in range(GDN_HEADS):
        mat[first_lane + h, h * HEAD:(h + 1) * HEAD] = 1.0
    return jnp.asarray(mat, MXU_DTYPE)


def _compress_weights(w1, w2):
    nslab = 2 * NSA_KV_HEADS
    slab_kv = np.arange(nslab) // NSA_KV_HEADS
    eye = jnp.eye(nslab, dtype=MXU_DTYPE)
    w1r = w1.astype(MXU_DTYPE).reshape(2, 2, CMP_STRIDE, HEAD, HEAD)[slab_kv]
    w1bd = jnp.einsum("shtde,sS->tsdhSe", w1r, eye)
    w2bd = jnp.einsum("sde,sS->sdSe", w2.astype(MXU_DTYPE)[slab_kv], eye)
    return w1bd.reshape(CMP_STRIDE, P_W, 2 * P_W), w2bd.reshape(P_W, P_W)


def _rope_table(positions):
    half = ROT_DIM // 2
    inv = jnp.float32(ROPE_THETA) ** (-jnp.arange(0, ROT_DIM, 2, dtype=jnp.float32) / ROT_DIM)
    inv_l = jnp.concatenate([inv, inv, jnp.zeros((LANES - ROT_DIM,), F32)])
    ang = positions.astype(jnp.float32).reshape(-1)[:, None] * inv_l[None, :]
    lane = jnp.arange(LANES)[None, :]
    return jnp.where(lane < half, jnp.cos(ang), jnp.where(lane < ROT_DIM, jnp.sin(ang), 0.0))


def _rope_expanders():
    half = ROT_DIM // 2
    ex = np.zeros((3, LANES, LANES), np.float32)
    cpat = np.zeros((1, LANES), np.float32)
    for j in range(LANES):
        d = j % HEAD
        if d < ROT_DIM:
            ex[0, d % half, j] = 1.0
        else:
            cpat[0, j] = 1.0
        if d < half:
            ex[1, half + d, j] = -1.0
        elif d < ROT_DIM:
            ex[2, half + (d - half), j] = 1.0
    return jnp.asarray(ex, MXU_DTYPE), jnp.asarray(cpat, F32)


def _selection_matrix(ncp, nslp):
    ratio = SLC_BLOCK // CMP_STRIDE
    frac = np.minimum(CMP_LEN, SLC_BLOCK - CMP_STRIDE * np.arange(ratio)).astype(np.float64) / CMP_LEN
    mat = np.zeros((ncp, nslp), np.float32)
    c = np.arange(ncp)
    mat[c, c // ratio] = frac[c % ratio]
    nxt = c // ratio + 1
    ok = nxt < nslp
    mat[c[ok], nxt[ok]] += (1.0 - frac[c % ratio])[ok]
    return jnp.asarray(mat, MXU_DTYPE)


def kernel(x, positions, attn_norm, w_in, gdn_conv_w, gdn_a_log, gdn_dt_bias, gdn_norm, nsa_q_norm,
           nsa_k_norm, nsa_cmp_pe, nsa_cmp_w1, nsa_cmp_w2, conv_w, w_out, ffn_norm, w_gate_up, w_down):
    batch, seq, _ = x.shape
    depth = w_in.shape[0]
    m = batch * seq
    assert seq % max(TM_PROJ, TM_FFN, T_GDN, KC) == 0 and (seq // CMP_STRIDE) % LANES == 0
    nb = seq // CMP_STRIDE
    nslp = -(-(seq // SLC_BLOCK) // LANES) * LANES

    cols = _proj_column_map()
    take = jnp.asarray(np.maximum(cols, 0), jnp.int32)
    valid = jnp.asarray(cols >= 0)
    assert Q_SLOT_HEADS == tuple(g * NSA_GROUP + j for j in range(NSA_GROUP) for g in range(NSA_KV_HEADS))
    bd128, bd256, bd512 = _block_diag_ones(128), _block_diag_ones(256), _block_diag_ones(512)
    rope_t = _rope_table(positions)
    rope_ex, rope_cp = _rope_expanders()
    msel = _selection_matrix(nb, nslp)
    lane_pad = lambda v: jnp.zeros((1, LANES), F32).at[0, GDN_HEADS:2 * GDN_HEADS].set(v.astype(F32))

    x2 = x.reshape(m, D_MODEL)
    for l in range(depth):
        w_l = jnp.where(valid[None, :], jnp.take(w_in[l], take, axis=1), 0.0).astype(MXU_DTYPE)
        qg = jnp.tile(nsa_q_norm[l], NSA_HEADS).reshape(1, NSA_WIDTH)
        kg = jnp.tile(nsa_k_norm[l, 1:3], (1, NSA_KV_HEADS))
        proj, qn, qr, ksa, kw, vs, vw, gt = _in_proj(x2, attn_norm[l].reshape(1, D_MODEL), w_l, rope_t, rope_ex,
                                                     rope_cp, qg, kg, bd512, bd128, seq)

        o_gdn = _gdn(proj, gdn_conv_w[l], lane_pad(gdn_a_log[l]), lane_pad(gdn_dt_bias[l]),
                     jnp.tile(gdn_norm[l], GDN_HEADS).reshape(1, GDN_WIDTH), bd256,
                     _head_expander(0), _head_expander(GDN_HEADS), batch, seq)

        w1bd, w2bd = _compress_weights(nsa_cmp_w1[l], nsa_cmp_w2[l])
        kvc = _compress(proj, w1bd, nsa_cmp_pe[l].reshape(2, CMP_LEN * HEAD), nsa_cmp_w1[l].astype(MXU_DTYPE),
                        w2bd, jnp.tile(nsa_k_norm[l, 0], NSA_KV_HEADS).reshape(1, NSA_KV_WIDTH), bd128,
                        batch, seq)

        bound = (HEAD ** 0.5) * jnp.max(jnp.abs(nsa_q_norm[l])) * jnp.max(jnp.abs(nsa_k_norm[l]), axis=1)
        cshift = jnp.zeros((1, LANES), F32).at[0, 0:3].set((bound * LOG2E).astype(F32))
        attn_args = (qn, qr, gt, kvc, ksa, vs, kw, vw, msel, cshift)
        o_nsa = lax.cond(
            jnp.max(bound) <= MAX_SOFTMAX_SHIFT,
            lambda a: _nsa_attn(*a, batch, seq, True),
            lambda a: _nsa_attn(*a, batch, seq, False),
            attn_args)

        wo = w_out[l].astype(MXU_DTYPE)
        wo_nsa = wo[GDN_WIDTH:GDN_WIDTH + NSA_WIDTH].reshape(NSA_KV_HEADS, NSA_GROUP, HEAD, D_MODEL)
        wo = jnp.concatenate([wo[:GDN_WIDTH], wo_nsa.transpose(1, 0, 2, 3).reshape(NSA_WIDTH, D_MODEL),
                              wo[GDN_WIDTH + NSA_WIDTH:]], axis=0)
        x2 = _out_ffn(x2, o_gdn, o_nsa, proj, conv_w[l], wo,
                      ffn_norm[l].reshape(1, D_MODEL), w_gate_up[l].astype(MXU_DTYPE),
                      w_down[l].astype(MXU_DTYPE), seq)
    return x2.reshape(batch, seq, D_MODEL)
```
